```python
import math
import jax, jax.numpy as jnp
from jax import lax
import numpy as np

D_MODEL = 1024
BATCH = 4
SEQ = 4096
DEPTH = 1
DEC_BATCH = 32
DEC_SEQ = 4
PAST_LEN = 16384
PAGE_SIZE = 128

MIX_W = D_MODEL
ATT_W = MIX_W // 2
HG_W = MIX_W - ATT_W
ATT_DH = 64
ATT_H = ATT_W // ATT_DH
DIL_PATTERNS = ((128, 1), (512, 4), (2048, 16))
MAX_WINDOW = 2048
Q_BLOCK = 128
HG_DK = 128
HG_H = HG_W // HG_DK
HG_DV = HG_W // HG_H
HG_CHUNK = 64
N_MEM = 256
X_H = 4
X_DH = 128
X_W = X_H * X_DH
D_FF = 2816
CONV_W = 3
EPS = 1e-6
IN_COLS = 3 * ATT_W + 4 * HG_W
IN_SPLITS = (ATT_W, 2 * ATT_W, 3 * ATT_W, 3 * ATT_W + HG_W, 3 * ATT_W + 2 * HG_W, 3 * ATT_W + 3 * HG_W)

kernel_name = 'hymba_hgrn2_dilated_decoder_step'


def rmsnorm(x, g):
    x32 = x.astype(jnp.float32)
    y = x32 * lax.rsqrt(jnp.mean(x32 * x32, axis=-1, keepdims=True) + EPS)
    return (y * g.astype(jnp.float32)).astype(x.dtype)


def head_rmsnorm(a, g):
    B, T, H, Dh = a.shape
    a32 = a.astype(jnp.float32)
    y = a32 * lax.rsqrt(jnp.mean(a32 * a32, axis=-1, keepdims=True) + EPS)
    return y.reshape(B, T, H * Dh) * g.astype(jnp.float32)


def alibi_slopes():
    return jnp.exp2(-8.0 * jnp.arange(1, ATT_H + 1, dtype=jnp.float32) / ATT_H)


def dilated_block(q, k_all, v_all, q_idx):
    slopes = alibi_slopes()
    lses, outs = [], []
    for win, dil in DIL_PATTERNS:
        dist = jnp.arange(win // dil + 1, dtype=jnp.int32) * dil
        idx = q_idx[:, None] - dist[None, :]
        valid = idx >= 0
        idx = jnp.maximum(idx, 0)
        kg = k_all[:, idx]
        vg = v_all[:, idx]
        s = jnp.einsum('bqhd,bqjhd->bhqj', q, kg).astype(jnp.float32) * (ATT_DH ** -0.5)
        s = s - slopes[:, None, None] * dist.astype(jnp.float32)
        s = jnp.where(valid[None, None], s, -jnp.inf)
        m = jnp.max(s, axis=-1, keepdims=True)
        p = jnp.exp(s - m)
        den = jnp.sum(p, axis=-1, keepdims=True)
        o = jnp.einsum('bhqj,bqjhd->bhqd', p.astype(vg.dtype), vg).astype(jnp.float32) / den
        lses.append(m + jnp.log(den))
        outs.append(o)
    w = jax.nn.softmax(jnp.stack(lses), axis=0)
    out = jnp.sum(w * jnp.stack(outs), axis=0)
    return out.transpose(0, 2, 1, 3).astype(q.dtype)


def dilated_attention(q, k_all, v_all, q_idx):
    B, T, H, Dh = q.shape
    if T % Q_BLOCK == 0 and T > Q_BLOCK:
        nb = T // Q_BLOCK
        qb = q.reshape(B, nb, Q_BLOCK, H, Dh).transpose(1, 0, 2, 3, 4)
        ib = q_idx.reshape(nb, Q_BLOCK)
        ob = lax.map(lambda a: dilated_block(a[0], k_all, v_all, a[1]), (qb, ib))
        return ob.transpose(1, 0, 2, 3, 4).reshape(B, T, H, Dh)
    return dilated_block(q, k_all, v_all, q_idx)


def hgrn2_recurrence(q, k, v, logf, s0):
    B, T, H, DK = q.shape
    DV = v.shape[-1]
    C = math.gcd(T, HG_CHUNK)
    n = T // C

    def to_chunks(a):
        return a.reshape(B, n, C, H, a.shape[-1]).transpose(1, 0, 3, 2, 4)

    causal = jnp.tril(jnp.ones((C, C), dtype=bool))[None, None, :, :, None]

    def step(S, inp):
        qc, kc, vc, gc = inp
        b = jnp.cumsum(gc, axis=2)
        decay = jnp.exp(jnp.where(causal, b[:, :, :, None, :] - b[:, :, None, :, :], -jnp.inf))
        a = jnp.einsum('bhtk,bhsk,bhtsk->bhts', qc, kc, decay)
        o = jnp.einsum('bhts,bhsv->bhtv', a, vc) + jnp.einsum('bhtk,bhkv->bhtv', qc * jnp.exp(b), S)
        b_last = b[:, :, -1, :]
        S = jnp.exp(b_last)[..., None] * S + jnp.einsum('bhsk,bhsv->bhkv', kc * jnp.exp(b_last[:, :, None, :] - b), vc)
        return S, o

    s_fin, o = lax.scan(step, s0.astype(jnp.float32), (to_chunks(q), to_chunks(k), to_chunks(v), to_chunks(logf)))
    o = o.transpose(1, 0, 3, 2, 4).reshape(B, T, H, DV)
    return o, s_fin


def hgrn_lower_bound(lb_logits, layer):
    probs = jax.nn.softmax(lb_logits.astype(jnp.float32), axis=0)
    return jnp.cumsum(probs, axis=0)[layer]


def memory_kv(mem, norm_mem, w_ck, w_cv):
    B = mem.shape[0]
    m = rmsnorm(mem, norm_mem)
    return ((m @ w_ck).reshape(B, N_MEM, X_H, X_DH), (m @ w_cv).reshape(B, N_MEM, X_H, X_DH))


def decoder_layer(x, k_past, v_past, hg_s0, conv_buf, mem_k, mem_v, lb,
                  norm_mix, w_in, att_out_norm, hg_out_norm, w_out,
                  norm_cross, w_cq, w_co, norm_ffn, w_gate, w_up, conv_w, conv_b, w_down):
    B, T, _ = x.shape
    P = k_past.shape[1]
    f32 = jnp.float32
    h = rmsnorm(x, norm_mix)
    z = h @ w_in
    aq, ak, av, hq, hf, hi, hg = jnp.split(z, IN_SPLITS, axis=-1)
    aq = aq.reshape(B, T, ATT_H, ATT_DH)
    ak = ak.reshape(B, T, ATT_H, ATT_DH)
    av = av.reshape(B, T, ATT_H, ATT_DH)
    k_all = jnp.concatenate([k_past.astype(ak.dtype), ak], axis=1)
    v_all = jnp.concatenate([v_past.astype(av.dtype), av], axis=1)
    q_idx = P + jnp.arange(T, dtype=jnp.int32)
    att = dilated_attention(aq, k_all, v_all, q_idx)
    att = head_rmsnorm(att, att_out_norm).astype(x.dtype)
    f = lb + (1.0 - lb) * jax.nn.sigmoid(hf.astype(f32))
    qh = jax.nn.silu(hq.astype(f32)) * (HG_DK ** -0.5)
    o, s_new = hgrn2_recurrence(qh.reshape(B, T, HG_H, HG_DK), (1.0 - f).reshape(B, T, HG_H, HG_DK),
                                hi.astype(f32).reshape(B, T, HG_H, HG_DV), jnp.log(f).reshape(B, T, HG_H, HG_DK), hg_s0)
    o = (head_rmsnorm(o, hg_out_norm) * jax.nn.silu(hg.astype(f32))).astype(x.dtype)
    x = x + jnp.concatenate([att, o], axis=-1) @ w_out
    c = rmsnorm(x, norm_cross)
    cq = (c @ w_cq).reshape(B, T, X_H, X_DH)
    s = jnp.einsum('bthd,bmhd->bhtm', cq, mem_k.astype(cq.dtype)).astype(f32) * (X_DH ** -0.5)
    pr = jax.nn.softmax(s, axis=-1)
    co = jnp.einsum('bhtm,bmhd->bthd', pr.astype(x.dtype), mem_v.astype(x.dtype)).reshape(B, T, X_W)
    x = x + co @ w_co
    u = rmsnorm(x, norm_ffn)
    ug = u @ w_gate
    up = jnp.concatenate([conv_buf.astype(ug.dtype), ug], axis=1)
    conv = conv_b + sum(conv_w[j] * up[:, j:j + T] for j in range(CONV_W))
    x = x + (jax.nn.silu(conv) * (u @ w_up)) @ w_down
    return x, ak, av, s_new, up[:, -(CONV_W - 1):]


def setup_inputs(seed: int = 0) -> dict:
    key = jax.random.key(seed)
    ks = iter(jax.random.split(key, 40))
    nrm = lambda shape, scale=1.0: jax.random.normal(next(ks), shape, jnp.float32) * scale
    gain = lambda shape: 1.0 + 0.02 * jax.random.normal(next(ks), shape, jnp.float32)
    win_buf = min(MAX_WINDOW, PAST_LEN)
    return {
        'x_prompt': nrm((BATCH, SEQ, D_MODEL)),
        'x_sample': nrm((DEC_BATCH, DEC_SEQ, D_MODEL)),
        'cache_win_k': nrm((DEPTH, DEC_BATCH, win_buf, ATT_H, ATT_DH)),
        'cache_win_v': nrm((DEPTH, DEC_BATCH, win_buf, ATT_H, ATT_DH)),
        'state_hgrn': nrm((DEPTH, DEC_BATCH, HG_H, HG_DK, HG_DV), 0.1),
        'state_ffn_conv': nrm((DEPTH, DEC_BATCH, CONV_W - 1, D_FF)),
        'cache_mem_k': nrm((DEPTH, DEC_BATCH, N_MEM, X_H, X_DH)),
        'cache_mem_v': nrm((DEPTH, DEC_BATCH, N_MEM, X_H, X_DH)),
        'mem_prompt': nrm((BATCH, N_MEM, D_MODEL)),
        'hg_lb_logits': nrm((DEPTH + 1, HG_W)),
        'norm_mix': gain((DEPTH, D_MODEL)),
        'w_in': nrm((DEPTH, D_MODEL, IN_COLS), D_MODEL ** -0.5),
        'att_out_norm': gain((DEPTH, ATT_W)),
        'hg_out_norm': gain((DEPTH, HG_W)),
        'w_out': nrm((DEPTH, MIX_W, D_MODEL), MIX_W ** -0.5),
        'norm_cross': gain((DEPTH, D_MODEL)),
        'norm_mem': gain((DEPTH, D_MODEL)),
        'w_cq': nrm((DEPTH, D_MODEL, X_W), D_MODEL ** -0.5),
        'w_ck': nrm((DEPTH, D_MODEL, X_W), D_MODEL ** -0.5),
        'w_cv': nrm((DEPTH, D_MODEL, X_W), D_MODEL ** -0.5),
        'w_co': nrm((DEPTH, X_W, D_MODEL), X_W ** -0.5),
        'norm_ffn': gain((DEPTH, D_MODEL)),
        'w_gate': nrm((DEPTH, D_MODEL, D_FF), D_MODEL ** -0.5),
        'w_up': nrm((DEPTH, D_MODEL, D_FF), D_MODEL ** -0.5),
        'conv_w': nrm((DEPTH, CONV_W, D_FF), CONV_W ** -0.5),
        'conv_b': nrm((DEPTH, D_FF), 0.01),
        'w_down': nrm((DEPTH, D_FF, D_MODEL), D_FF ** -0.5),
        'norm_final': gain((D_MODEL,)),
    }


def reference(x_prompt, x_sample, cache_win_k, cache_win_v, state_hgrn, state_ffn_conv, cache_mem_k, cache_mem_v,
              mem_prompt, hg_lb_logits, norm_mix, w_in, att_out_norm, hg_out_norm, w_out, norm_cross, norm_mem,
              w_cq, w_ck, w_cv, w_co, norm_ffn, w_gate, w_up, conv_w, conv_b, w_down, norm_final):
    Bp, T, _ = x_prompt.shape
    keep = min(MAX_WINDOW, T)
    yp, ys = x_prompt, x_sample
    pk, pv, ps, pc, pmk, pmv = [], [], [], [], [], []
    sk, sv, ss, sc = [], [], [], []
    for l in range(DEPTH):
        lb = hgrn_lower_bound(hg_lb_logits, l)
        weights = (norm_mix[l], w_in[l], att_out_norm[l], hg_out_norm[l], w_out[l], norm_cross[l], w_cq[l], w_co[l],
                   norm_ffn[l], w_gate[l], w_up[l], conv_w[l], conv_b[l], w_down[l])
        mk, mv = memory_kv(mem_prompt, norm_mem[l], w_ck[l], w_cv[l])
        empty = jnp.zeros((Bp, 0, ATT_H, ATT_DH), x_prompt.dtype)
        yp, k_new, v_new, s_new, c_new = decoder_layer(
            yp, empty, empty, jnp.zeros((Bp, HG_H, HG_DK, HG_DV), jnp.float32),
            jnp.zeros((Bp, CONV_W - 1, D_FF), x_prompt.dtype), mk, mv, lb, *weights)
        pk.append(k_new[:, T - keep:])
        pv.append(v_new[:, T - keep:])
        ps.append(s_new)
        pc.append(c_new)
        pmk.append(mk)
        pmv.append(mv)
        ys, k_new, v_new, s_new, c_new = decoder_layer(
            ys, cache_win_k[l], cache_win_v[l], state_hgrn[l], state_ffn_conv[l], cache_mem_k[l], cache_mem_v[l],
            lb, *weights)
        sk.append(k_new)
        sv.append(v_new)
        ss.append(s_new)
        sc.append(c_new)
    yp = rmsnorm(yp, norm_final)
    ys = rmsnorm(ys, norm_final)
    return (yp, ys, jnp.stack(pk), jnp.stack(pv), jnp.stack(ps), jnp.stack(pc), jnp.stack(pmk), jnp.stack(pmv),
            jnp.stack(sk), jnp.stack(sv), jnp.stack(ss), jnp.stack(sc))
```

```python
import functools

import jax
import jax.numpy as jnp
from jax import lax
from jax.experimental import pallas as pl
from jax.experimental.pallas import tpu as pltpu

F32 = jnp.float32
BF16 = jnp.bfloat16
EPS = 1e-6
DIL_PATTERNS = ((128, 1), (512, 4), (2048, 16))
ATT_BLOCK = 128
LANES = 128
SUBLANES = 8
HG_CHUNK = 64
HG_SUB = 16
NEG = -1e30
HIGHEST = lax.Precision.HIGHEST
NT_DIMS = (((1,), (1,)), ((), ()))
TN_DIMS = (((0,), (0,)), ((), ()))
VMEM_LIMIT = 56 * 1024 * 1024


def _dot(a, b):
    return jnp.dot(a, b, preferred_element_type=F32)


def _dot_nt(a, b):
    return lax.dot_general(a, b, NT_DIMS, preferred_element_type=F32)


def _sigmoid(x):
    return 1.0 / (1.0 + jnp.exp(-x))


def _rms(x, g):
    return x * lax.rsqrt(jnp.mean(x * x, axis=-1, keepdims=True) + EPS) * g


def _const_spec(shape):
    nd = len(shape)
    return pl.BlockSpec(shape, lambda *_: (0,) * nd, pipeline_mode=pl.Buffered(1))


def _params(sem):
    return pltpu.CompilerParams(dimension_semantics=sem, vmem_limit_bytes=VMEM_LIMIT)


def _in_proj_kernel(x_ref, g_ref, w_ref, q_ref, k_ref, v_ref, kf_ref, vf_ref, hz_ref, *, att_w):
    h = _rms(x_ref[0], g_ref[...]).astype(BF16)

    def proj(lo, hi):
        return _dot(h, w_ref[:, lo:hi])

    q_ref[0] = proj(0, att_w).astype(BF16)
    zk = proj(att_w, 2 * att_w)
    k_ref[0] = zk.astype(BF16)
    kf_ref[0] = zk
    zv = proj(2 * att_w, 3 * att_w)
    v_ref[0] = zv.astype(BF16)
    vf_ref[0] = zv
    base = 3 * att_w
    step = 4 * LANES
    for c in range(0, hz_ref.shape[-1], step):
        hz_ref[0, :, c:c + step] = proj(base + c, base + c + step)


def _in_proj(x, g, w_bf, *, att_w, tm, keep):
    B, T, D = x.shape
    ncols = w_bf.shape[1]
    hzw = ncols - 3 * att_w
    n_t = T // tm
    first = n_t - keep // tm
    row = lambda b, i: (b, i, 0)
    keep_map = lambda b, i: (b, jnp.maximum(i - first, 0), 0)
    return pl.pallas_call(
        functools.partial(_in_proj_kernel, att_w=att_w),
        grid=(B, n_t),
        in_specs=[pl.BlockSpec((1, tm, D), row), _const_spec((1, D)), _const_spec((D, ncols))],
        out_specs=[pl.BlockSpec((1, tm, att_w), row)] * 3
        + [pl.BlockSpec((1, tm, att_w), keep_map)] * 2
        + [pl.BlockSpec((1, tm, hzw), row)],
        out_shape=[jax.ShapeDtypeStruct((B, T, att_w), BF16)] * 3
        + [jax.ShapeDtypeStruct((B, keep, att_w), F32)] * 2
        + [jax.ShapeDtypeStruct((B, T, hzw), F32)],
        compiler_params=_params(("parallel", "arbitrary")),
        name="in_proj",
    )(x, g, w_bf)


def _mem_kv_kernel(m_ref, g_ref, w_ref, kf_ref, vf_ref, kb_ref, vb_ref):
    h = _rms(m_ref[...], g_ref[...]).astype(BF16)
    xw = w_ref.shape[1] // 2
    k = _dot(h, w_ref[:, :xw])
    v = _dot(h, w_ref[:, xw:])
    kf_ref[...] = k
    vf_ref[...] = v
    kb_ref[...] = k.astype(BF16)
    vb_ref[...] = v.astype(BF16)


def _mem_kv(mem2d, g, w_ckv_bf, *, tm):
    n, D = mem2d.shape
    xw = w_ckv_bf.shape[1] // 2
    row = lambda i: (i, 0)
    return pl.pallas_call(
        _mem_kv_kernel,
        grid=(n // tm,),
        in_specs=[pl.BlockSpec((tm, D), row), _const_spec((1, D)), _const_spec((D, 2 * xw))],
        out_specs=[pl.BlockSpec((tm, xw), row)] * 4,
        out_shape=[jax.ShapeDtypeStruct((n, xw), F32)] * 2 + [jax.ShapeDtypeStruct((n, xw), BF16)] * 2,
        compiler_params=_params(("parallel",)),
        name="mem_kv",
    )(mem2d, g, w_ckv_bf)


def _dil_attn_kernel(q_ref, kc_ref, kp_ref, vc_ref, vp_ref, o_ref, lse_ref, *, dil, n_heads, dh):
    i = pl.program_id(2)
    blk = ATT_BLOCK
    row = lax.broadcasted_iota(jnp.int32, (blk, 2 * blk), 0)
    col = lax.broadcasted_iota(jnp.int32, (blk, 2 * blk), 1)
    delta = row + blk - col
    valid = (delta >= 0) & (delta <= blk) & ((col >= blk) | (i > 0))
    dist = (delta * dil).astype(F32)
    lane = lax.broadcasted_iota(jnp.int32, (blk, LANES), 1)
    heads_per_tile = LANES // dh
    scale = dh ** -0.5
    lse_tile = jnp.zeros((blk, LANES), F32)
    for t in range(n_heads // heads_per_tile):
        sl = slice(t * LANES, (t + 1) * LANES)
        q2 = q_ref[0, :, sl].astype(F32)
        kcat = jnp.concatenate([kp_ref[0, :, sl], kc_ref[0, :, sl]], axis=0)
        vcat = jnp.concatenate([vp_ref[0, :, sl], vc_ref[0, :, sl]], axis=0)
        o2 = jnp.zeros((blk, LANES), F32)
        for e in range(heads_per_tile):
            h = t * heads_per_tile + e
            slope = 2.0 ** (-8.0 * (h + 1) / n_heads)
            in_head = (lane // dh) == e
            qm = jnp.where(in_head, q2, 0.0).astype(BF16)
            s = _dot_nt(qm, kcat) * scale - slope * dist
            s = jnp.where(valid, s, NEG)
            m = jnp.max(s, axis=-1, keepdims=True)
            p = jnp.exp(s - m)
            den = jnp.sum(p, axis=-1, keepdims=True)
            pv = _dot(p.astype(BF16), vcat)
            o2 = jnp.where(in_head, pv / den, o2)
            lse_tile = jnp.where(lane == h, m + jnp.log(den), lse_tile)
        o_ref[0, :, sl] = o2.astype(BF16)
    lse_ref[0] = lse_tile


def _dil_attn(q, k, v, *, dil, n_heads, dh):
    B, T, W = q.shape
    L = T // dil
    blk = ATT_BLOCK
    view = lambda a: a.reshape(B, L, dil * a.shape[-1])
    cur = lambda b, r, i: (b, i, r)
    prev = lambda b, r, i: (b, jnp.maximum(i - 1, 0), r)
    o, lse = pl.pallas_call(
        functools.partial(_dil_attn_kernel, dil=dil, n_heads=n_heads, dh=dh),
        grid=(B, dil, L // blk),
        in_specs=[pl.BlockSpec((1, blk, W), cur), pl.BlockSpec((1, blk, W), cur), pl.BlockSpec((1, blk, W), prev),
                  pl.BlockSpec((1, blk, W), cur), pl.BlockSpec((1, blk, W), prev)],
        out_specs=[pl.BlockSpec((1, blk, W), cur), pl.BlockSpec((1, blk, LANES), cur)],
        out_shape=[jax.ShapeDtypeStruct((B, L, dil * W), BF16), jax.ShapeDtypeStruct((B, L, dil * LANES), F32)],
        compiler_params=_params(("parallel", "parallel", "arbitrary")),
        name=f"dil_attn_d{dil}",
    )(view(q), view(k), view(k), view(v), view(v))
    return o.reshape(B, T, W), lse.reshape(B, T, LANES)


def _dil_attn_step_kernel(q_ref, kn_ref, vn_ref, ck_ref, cv_ref, o_ref, *, n_heads, dh, n_q):
    past = ck_ref.shape[1]
    n_new = kn_ref.shape[1]
    W = q_ref.shape[-1]
    rows = n_q * n_heads
    sub = lax.broadcasted_iota(jnp.int32, (n_heads, W), 0)
    lane_head = lax.broadcasted_iota(jnp.int32, (n_heads, W), 1) // dh
    qbd = jnp.concatenate(
        [jnp.where(sub == lane_head, jnp.broadcast_to(q_ref[0, i:i + 1, :], (n_heads, W)), 0.0) for i in range(n_q)],
        axis=0).astype(BF16)
    scale = dh ** -0.5

    def scores(keys_bf, n_keys, first_pos):
        s = _dot_nt(qbd, keys_bf) * scale
        r = lax.broadcasted_iota(jnp.int32, (rows, n_keys), 0)
        c = lax.broadcasted_iota(jnp.int32, (rows, n_keys), 1)
        delta = past + r // n_heads - (first_pos + c)
        slope = jnp.exp2((-8.0 / n_heads) * ((r % n_heads) + 1).astype(F32))
        cnt = jnp.zeros((rows, n_keys), F32)
        for win, dil in DIL_PATTERNS:
            ok = (delta >= 0) & (delta <= win) & ((delta & (dil - 1)) == 0)
            cnt = cnt + ok.astype(F32)
        s = s - slope * delta.astype(F32)
        return jnp.where(cnt > 0.0, s, NEG), cnt

    s_c, cnt_c = scores(ck_ref[0].astype(BF16), past, 0)
    s_n, cnt_n = scores(kn_ref[0], n_new, past)
    m = jnp.maximum(jnp.max(s_c, axis=-1, keepdims=True), jnp.max(s_n, axis=-1, keepdims=True))
    p_c = jnp.exp(s_c - m) * cnt_c
    p_n = jnp.exp(s_n - m) * cnt_n
    den = jnp.sum(p_c, axis=-1, keepdims=True) + jnp.sum(p_n, axis=-1, keepdims=True)
    out = (_dot(p_c.astype(BF16), cv_ref[0].astype(BF16)) + _dot(p_n.astype(BF16), vn_ref[0])) / den
    r = lax.broadcasted_iota(jnp.int32, (rows, W), 0)
    c = lax.broadcasted_iota(jnp.int32, (rows, W), 1)
    out = jnp.where((r % n_heads) == (c // dh), out, 0.0)
    o_ref[0] = jnp.sum(out.reshape(n_q, n_heads, W), axis=1)


def _dil_attn_step(q, k_new_pad, v_new_pad, cache_k, cache_v, *, n_heads, dh):
    B, n_q, W = q.shape
    past = cache_k.shape[1]
    n_new = k_new_pad.shape[1]
    bmap = lambda b: (b, 0, 0)
    return pl.pallas_call(
        functools.partial(_dil_attn_step_kernel, n_heads=n_heads, dh=dh, n_q=n_q),
        grid=(B,),
        in_specs=[pl.BlockSpec((1, n_q, W), bmap), pl.BlockSpec((1, n_new, W), bmap), pl.BlockSpec((1, n_new, W), bmap),
                  pl.BlockSpec((1, past, W), bmap), pl.BlockSpec((1, past, W), bmap)],
        out_specs=pl.BlockSpec((1, n_q, W), bmap),
        out_shape=jax.ShapeDtypeStruct((B, n_q, W), F32),
        compiler_params=_params(("parallel",)),
        name="dil_attn_step",
    )(q, k_new_pad, v_new_pad, cache_k, cache_v)


def _lower_bound(lbl_ref, layer):
    logits = lbl_ref[...]
    e = jnp.exp(logits - jnp.max(logits, axis=0, keepdims=True))
    return jnp.sum(e[:layer + 1], axis=0, keepdims=True) / jnp.sum(e, axis=0, keepdims=True)


def _hgrn_kernel(hz_ref, lbl_ref, gn_ref, o_ref, s_ref, st_ref, b_sc, k_sc, *, n_heads, dk, layer):
    i = pl.program_id(1)
    hw = n_heads * dk
    tc = hz_ref.shape[1]
    ch, sub = HG_CHUNK, HG_SUB

    @pl.when(i == 0)
    def _():
        st_ref[...] = jnp.zeros_like(st_ref)

    lb = _lower_bound(lbl_ref, layer)
    r_c = lax.broadcasted_iota(jnp.int32, (ch, ch), 0)
    c_c = lax.broadcasted_iota(jnp.int32, (ch, ch), 1)
    tril = (r_c >= c_c).astype(F32)
    r_s = lax.broadcasted_iota(jnp.int32, (sub, ch), 0)
    c_s = lax.broadcasted_iota(jnp.int32, (sub, ch), 1)

    for c in range(tc // ch):
        rows = slice(c * ch, (c + 1) * ch)
        hq = hz_ref[0, rows, 0:hw]
        f = lb + (1.0 - lb) * _sigmoid(hz_ref[0, rows, hw:2 * hw])
        qh = hq * _sigmoid(hq) * dk ** -0.5
        b_sc[...] = jnp.dot(tril, jnp.log(f), precision=HIGHEST, preferred_element_type=F32)
        k_sc[...] = 1.0 - f
        for h in range(n_heads):
            hs = slice(h * dk, (h + 1) * dk)
            bh = b_sc[:, hs]
            kh = k_sc[:, hs]
            qhh = qh[:, hs]
            vh = hz_ref[0, rows, 2 * hw + h * dk:2 * hw + (h + 1) * dk].astype(BF16)
            st = st_ref[h]
            o = _dot_nt((qhh * jnp.exp(bh)).astype(BF16), st.astype(BF16))
            blocks = []
            for j in range(ch // sub):
                r0 = j * sub
                bj = bh[r0:r0 + sub]
                qj = qhh[r0:r0 + sub]
                a_j = jnp.zeros((sub, ch), F32)
                for s in range(sub):
                    e = jnp.exp(bj - b_sc[r0 + s:r0 + s + 1, hs])
                    a = jnp.sum(qj * e * k_sc[r0 + s:r0 + s + 1, hs], axis=-1, keepdims=True)
                    a_j = jnp.where((c_s == r0 + s) & (r_s >= s), a, a_j)
                if j > 0:
                    beta = b_sc[r0 - 1:r0, hs]
                    qt = qj * jnp.exp(bj - beta)
                    kt = kh * jnp.exp(jnp.minimum(beta - bh, 0.0))
                    a_j = jnp.where(c_s < r0, _dot_nt(qt.astype(BF16), kt.astype(BF16)), a_j)
                blocks.append(a_j)
            a_mat = jnp.concatenate(blocks, axis=0)
            o = o + _dot(a_mat.astype(BF16), vh)
            b_last = b_sc[ch - 1:ch, hs]
            khat = (kh * jnp.exp(b_last - bh)).astype(BF16)
            st_ref[h] = st * jnp.exp(b_last) + lax.dot_general(vh, khat, TN_DIMS, preferred_element_type=F32)
            hg = hz_ref[0, rows, 3 * hw + h * dk:3 * hw + (h + 1) * dk]
            o_ref[0, rows, hs] = (_rms(o, gn_ref[:, hs]) * (hg * _sigmoid(hg))).astype(BF16)

    @pl.when(i == pl.num_programs(1) - 1)
    def _():
        for h in range(n_heads):
            s_ref[0, h] = st_ref[h].T


def _hgrn(hz, lb_logits, gn, *, n_heads, dk, tc, layer):
    B, T, _ = hz.shape
    hw = n_heads * dk
    row = lambda b, i: (b, i, 0)
    return pl.pallas_call(
        functools.partial(_hgrn_kernel, n_heads=n_heads, dk=dk, layer=layer),
        grid=(B, T // tc),
        in_specs=[pl.BlockSpec((1, tc, 4 * hw), row), _const_spec(lb_logits.shape), _const_spec((1, hw))],
        out_specs=[pl.BlockSpec((1, tc, hw), row), pl.BlockSpec((1, n_heads, dk, dk), lambda b, i: (b, 0, 0, 0))],
        out_shape=[jax.ShapeDtypeStruct((B, T, hw), BF16), jax.ShapeDtypeStruct((B, n_heads, dk, dk), F32)],
        scratch_shapes=[pltpu.VMEM((n_heads, dk, dk), F32), pltpu.VMEM((HG_CHUNK, hw), F32),
                        pltpu.VMEM((HG_CHUNK, hw), F32)],
        compiler_params=_params(("parallel", "arbitrary")),
        name="hgrn",
    )(hz, lb_logits, gn)


def _hgrn_step_kernel(hz_ref, s0_ref, lbl_ref, gn_ref, o_ref, s_ref, *, n_heads, dk, layer):
    hw = n_heads * dk
    n_t = hz_ref.shape[1]
    lb = _lower_bound(lbl_ref, layer)
    q, k, v, b, gate = [], [], [], [], []
    acc = jnp.zeros((1, hw), F32)
    for t in range(n_t):
        hq = hz_ref[0, t:t + 1, 0:hw]
        f = lb + (1.0 - lb) * _sigmoid(hz_ref[0, t:t + 1, hw:2 * hw])
        acc = acc + jnp.log(f)
        q.append(hq * _sigmoid(hq) * dk ** -0.5)
        k.append(1.0 - f)
        v.append(hz_ref[0, t:t + 1, 2 * hw:3 * hw])
        b.append(acc)
        hg = hz_ref[0, t:t + 1, 3 * hw:4 * hw]
        gate.append(hg * _sigmoid(hg))
    rr = lax.broadcasted_iota(jnp.int32, (dk, dk), 0)
    cc = lax.broadcasted_iota(jnp.int32, (dk, dk), 1)

    def tile_of_rows(rows):
        tile = jnp.zeros((dk, dk), F32)
        for t, r in enumerate(rows):
            tile = jnp.where(rr == t, jnp.broadcast_to(r, (dk, dk)), tile)
        return tile

    for h in range(n_heads):
        hs = slice(h * dk, (h + 1) * dk)
        s0 = s0_ref[0, h]
        qe = tile_of_rows([q[t][:, hs] * jnp.exp(b[t][:, hs]) for t in range(n_t)])
        inter = jnp.dot(qe, s0, precision=HIGHEST, preferred_element_type=F32)
        for t in range(n_t):
            o = inter[t:t + 1]
            for s in range(t + 1):
                a = jnp.sum(q[t][:, hs] * jnp.exp(b[t][:, hs] - b[s][:, hs]) * k[s][:, hs], axis=-1, keepdims=True)
                o = o + a * v[s][:, hs]
            o_ref[0, t:t + 1, hs] = _rms(o, gn_ref[:, hs]) * gate[t][:, hs]
        b_last = b[n_t - 1][:, hs]
        khat = tile_of_rows([k[t][:, hs] * jnp.exp(b_last - b[t][:, hs]) for t in range(n_t)])
        vpad = tile_of_rows([v[t][:, hs] for t in range(n_t)])
        decay = jnp.where(rr == cc, jnp.broadcast_to(jnp.exp(b_last), (dk, dk)), 0.0)
        s_ref[0, h] = (jnp.dot(decay, s0, precision=HIGHEST, preferred_element_type=F32)
                       + jnp.dot(khat.T, vpad, precision=HIGHEST, preferred_element_type=F32))


def _hgrn_step(hz, s0, lb_logits, gn, *, layer):
    B, n_t, _ = hz.shape
    _, n_heads, dk, dv = s0.shape
    hw = n_heads * dk
    return pl.pallas_call(
        functools.partial(_hgrn_step_kernel, n_heads=n_heads, dk=dk, layer=layer),
        grid=(B,),
        in_specs=[pl.BlockSpec((1, n_t, 4 * hw), lambda b: (b, 0, 0)),
                  pl.BlockSpec((1, n_heads, dk, dv), lambda b: (b, 0, 0, 0)),
                  _const_spec(lb_logits.shape), _const_spec((1, hw))],
        out_specs=[pl.BlockSpec((1, n_t, hw), lambda b: (b, 0, 0)),
                   pl.BlockSpec((1, n_heads, dk, dv), lambda b: (b, 0, 0, 0))],
        out_shape=[jax.ShapeDtypeStruct((B, n_t, hw), F32), jax.ShapeDtypeStruct((B, n_heads, dk, dv), F32)],
        compiler_params=_params(("parallel",)),
        name="hgrn_step",
    )(hz, s0, lb_logits, gn)


def _split_dot(x, m_bf):
    hi = x.astype(BF16)
    lo = (x - hi.astype(F32)).astype(BF16)
    return _dot(hi, m_bf) + _dot(lo, m_bf)


def _att_head_norm(att, g, e_ref, et_ref, dh):
    mean_sq = _split_dot(att * att, et_ref[...]) * (1.0 / dh)
    inv = _split_dot(lax.rsqrt(mean_sq + EPS), e_ref[...])
    return att * inv * g


def _mix_out(x, att_n, ohg_bf, wo_ref):
    aw = att_n.shape[-1]
    return x + _dot(att_n.astype(BF16), wo_ref[0:aw, :]) + _dot(ohg_bf, wo_ref[aw:, :])


def _cross_attend(cq, mk, mv, n_heads, dh):
    outs = []
    for h in range(n_heads):
        sl = slice(h * dh, (h + 1) * dh)
        s = _dot_nt(cq[:, sl].astype(BF16), mk[:, sl]) * dh ** -0.5
        p = jnp.exp(s - jnp.max(s, axis=-1, keepdims=True))
        den = jnp.sum(p, axis=-1, keepdims=True)
        outs.append(_dot(p.astype(BF16), mv[:, sl]) / den)
    return jnp.concatenate(outs, axis=-1)


def _ffn(u_bf, taps_fn, wg_ref, wu_ref, wd_ref, cw_ref, cb_ref, ff_chunk):
    dff = wg_ref.shape[1]
    n_taps = cw_ref.shape[0]
    acc = jnp.zeros((u_bf.shape[0], wd_ref.shape[1]), F32)
    for n in range(dff // ff_chunk):
        cs = slice(n * ff_chunk, (n + 1) * ff_chunk)
        ug = _dot(u_bf, wg_ref[:, cs])
        taps = taps_fn(ug, cs)
        conv = cb_ref[:, cs]
        for j in range(n_taps):
            conv = conv + cw_ref[j:j + 1, cs] * taps[j]
        act = conv * _sigmoid(conv) * _dot(u_bf, wu_ref[:, cs])
        acc = acc + _dot(act.astype(BF16), wd_ref[cs, :])
    return acc


def _post_kernel(x_ref, o1_ref, o2_ref, o3_ref, l1_ref, l2_ref, l3_ref, ohg_ref, mk_ref, mv_ref, cinit_ref,
                 e_ref, et_ref, ga_ref, gc_ref, gf_ref, gl_ref, wo_ref, wq_ref, wc_ref, wg_ref, wu_ref, wd_ref,
                 cw_ref, cb_ref, y_ref, cst_ref, buf_ref, *, att_dh, x_heads, x_dh, ff_chunk):
    i = pl.program_id(1)
    tm = x_ref.shape[1]
    pad = SUBLANES

    @pl.when(i == 0)
    def _():
        buf_ref[0:pad, :] = cinit_ref[0]

    l1, l2, l3 = l1_ref[0], l2_ref[0], l3_ref[0]
    m = jnp.maximum(jnp.maximum(l1, l2), l3)
    e1, e2, e3 = jnp.exp(l1 - m), jnp.exp(l2 - m), jnp.exp(l3 - m)
    inv = 1.0 / (e1 + e2 + e3)
    att = (_split_dot(e1 * inv, e_ref[...]) * o1_ref[0].astype(F32)
           + _split_dot(e2 * inv, e_ref[...]) * o2_ref[0].astype(F32)
           + _split_dot(e3 * inv, e_ref[...]) * o3_ref[0].astype(F32))
    att_n = _att_head_norm(att, ga_ref[...], e_ref, et_ref, att_dh)
    x1 = _mix_out(x_ref[0], att_n, ohg_ref[0], wo_ref)

    cq = _dot(_rms(x1, gc_ref[...]).astype(BF16), wq_ref[...])
    co = _cross_attend(cq, mk_ref[0], mv_ref[0], x_heads, x_dh)
    x2 = x1 + _dot(co.astype(BF16), wc_ref[...])

    n_taps = cw_ref.shape[0]

    def taps_fn(ug, cs):
        buf_ref[pad:pad + tm, cs] = ug
        return [buf_ref[pad - (n_taps - 1 - j):pad - (n_taps - 1 - j) + tm, cs] for j in range(n_taps - 1)] + [ug]

    u_bf = _rms(x2, gf_ref[...]).astype(BF16)
    x3 = x2 + _ffn(u_bf, taps_fn, wg_ref, wu_ref, wd_ref, cw_ref, cb_ref, ff_chunk)
    y_ref[0] = _rms(x3, gl_ref[...])
    last = buf_ref[tm:tm + pad, :]
    buf_ref[0:pad, :] = last
    cst_ref[0] = last


def _post(x, o_pats, lse_pats, ohg, mk_bf, mv_bf, cinit, e_mat, et_mat, gains, weights, conv_w, conv_b,
          *, tm, att_dh, x_heads, ff_chunk):
    B, T, D = x.shape
    aw = o_pats[0].shape[-1]
    hw = ohg.shape[-1]
    n_mem, xw = mk_bf.shape[1:]
    dff = conv_w.shape[1]
    row = lambda b, i: (b, i, 0)
    bat = lambda b, i: (b, 0, 0)
    in_specs = ([pl.BlockSpec((1, tm, D), row)] + [pl.BlockSpec((1, tm, aw), row)] * 3
                + [pl.BlockSpec((1, tm, LANES), row)] * 3 + [pl.BlockSpec((1, tm, hw), row)]
                + [pl.BlockSpec((1, n_mem, xw), bat)] * 2 + [pl.BlockSpec((1, SUBLANES, dff), bat)]
                + [_const_spec(a.shape) for a in (e_mat, et_mat) + tuple(gains) + tuple(weights) + (conv_w, conv_b)])
    return pl.pallas_call(
        functools.partial(_post_kernel, att_dh=att_dh, x_heads=x_heads, x_dh=xw // x_heads, ff_chunk=ff_chunk),
        grid=(B, T // tm),
        in_specs=in_specs,
        out_specs=[pl.BlockSpec((1, tm, D), row), pl.BlockSpec((1, SUBLANES, dff), bat)],
        out_shape=[jax.ShapeDtypeStruct((B, T, D), F32), jax.ShapeDtypeStruct((B, SUBLANES, dff), F32)],
        scratch_shapes=[pltpu.VMEM((tm + SUBLANES, dff), F32)],
        compiler_params=_params(("parallel", "arbitrary")),
        name="post_mixer",
    )(x, *o_pats, *lse_pats, ohg, mk_bf, mv_bf, cinit, e_mat, et_mat, *gains, *weights, conv_w, conv_b)


def _step_mix_kernel(x_ref, att_ref, ohg_ref, e_ref, et_ref, ga_ref, gc_ref, wo_ref, wq_ref, x1_ref, cq_ref,
                     *, att_dh):
    att_n = _att_head_norm(att_ref[...], ga_ref[...], e_ref, et_ref, att_dh)
    x1 = _mix_out(x_ref[...], att_n, ohg_ref[...].astype(BF16), wo_ref)
    x1_ref[...] = x1
    cq_ref[...] = _dot(_rms(x1, gc_ref[...]).astype(BF16), wq_ref[...])


def _step_cross_kernel(cq_ref, mk_ref, mv_ref, co_ref, *, x_heads, x_dh):
    co_ref[0] = _cross_attend(cq_ref[0], mk_ref[0].astype(BF16), mv_ref[0].astype(BF16), x_heads, x_dh)


def _step_ffn_kernel(x1_ref, co_ref, prev_ref, gf_ref, gl_ref, wc_ref, wg_ref, wu_ref, wd_ref,
                     cw_ref, cb_ref, y_ref, ug_ref, buf_ref, *, ff_chunk, n_t):
    n = x1_ref.shape[0]
    pad = SUBLANES
    n_taps = cw_ref.shape[0]
    x2 = x1_ref[...] + _dot(co_ref[...].astype(BF16), wc_ref[...])
    t_of_row = lax.broadcasted_iota(jnp.int32, (n, ff_chunk), 0) % n_t

    def taps_fn(ug, cs):
        ug_ref[:, cs] = ug
        buf_ref[0:pad, cs] = jnp.zeros((pad, ff_chunk), F32)
        buf_ref[pad:pad + n, cs] = ug
        taps = []
        for j in range(n_taps - 1):
            shift = n_taps - 1 - j
            shifted = buf_ref[pad - shift:pad - shift + n, cs]
            taps.append(jnp.where(t_of_row < shift, prev_ref[j, :, cs], shifted))
        return taps + [ug]

    u_bf = _rms(x2, gf_ref[...]).astype(BF16)
    x3 = x2 + _ffn(u_bf, taps_fn, wg_ref, wu_ref, wd_ref, cw_ref, cb_ref, ff_chunk)
    y_ref[...] = _rms(x3, gl_ref[...])


def _single_call(kernel, args, out_shape, name, scratch_shapes=()):
    return pl.pallas_call(
        kernel,
        grid=(1,),
        in_specs=[_const_spec(a.shape) for a in args],
        out_specs=[pl.BlockSpec(o.shape, lambda *_, nd=len(o.shape): (0,) * nd) for o in out_shape],
        out_shape=out_shape,
        scratch_shapes=list(scratch_shapes),
        compiler_params=_params(("arbitrary",)),
        name=name,
    )(*args)


def _step_cross(cq, mem_k, mem_v, *, x_heads):
    B, n_q, xw = cq.shape
    n_mem = mem_k.shape[1]
    bmap = lambda b: (b, 0, 0)
    return pl.pallas_call(
        functools.partial(_step_cross_kernel, x_heads=x_heads, x_dh=xw // x_heads),
        grid=(B,),
        in_specs=[pl.BlockSpec((1, n_q, xw), bmap), pl.BlockSpec((1, n_mem, xw), bmap),
                  pl.BlockSpec((1, n_mem, xw), bmap)],
        out_specs=pl.BlockSpec((1, n_q, xw), bmap),
        out_shape=jax.ShapeDtypeStruct((B, n_q, xw), F32),
        compiler_params=_params(("parallel",)),
        name="step_cross",
    )(cq, mem_k, mem_v)


def kernel(x_prompt, x_sample, cache_win_k, cache_win_v, state_hgrn, state_ffn_conv, cache_mem_k, cache_mem_v,
           mem_prompt, hg_lb_logits, norm_mix, w_in, att_out_norm, hg_out_norm, w_out, norm_cross, norm_mem,
           w_cq, w_ck, w_cv, w_co, norm_ffn, w_gate, w_up, conv_w, conv_b, w_down, norm_final):
    Bp, T, D = x_prompt.shape
    Bs, Ts, _ = x_sample.shape
    depth, _, past, att_h, att_dh = cache_win_k.shape
    _, _, hg_h, hg_dk, hg_dv = state_hgrn.shape
    _, _, n_mem, x_h, x_dh = cache_mem_k.shape
    n_taps, dff = conv_w.shape[1:]
    att_w = att_h * att_dh
    hg_w = hg_h * hg_dk
    xw = x_h * x_dh
    keep = min(max(w for w, _ in DIL_PATTERNS), T)
    assert depth == 1 and hg_dk == hg_dv == LANES and x_dh == LANES and LANES % att_dh == 0
    assert all(w // d == ATT_BLOCK for w, d in DIL_PATTERNS) and past >= max(w for w, _ in DIL_PATTERNS)
    assert T % (ATT_BLOCK * max(d for _, d in DIL_PATTERNS)) == 0
    assert n_taps - 1 <= min(Ts, SUBLANES) and Ts <= SUBLANES
    layer = 0
    ff_chunk = 2 * LANES
    row2 = lambda a: a.reshape(1, -1)

    w_in_bf = w_in[layer].astype(BF16)
    w_ckv_bf = jnp.concatenate([w_ck[layer], w_cv[layer]], axis=1).astype(BF16)
    weights = tuple(w[layer].astype(BF16) for w in (w_out, w_cq, w_co, w_gate, w_up, w_down))
    g_mix, g_att, g_hg = row2(norm_mix[layer]), row2(att_out_norm[layer]), row2(hg_out_norm[layer])
    g_cross, g_mem, g_ffn, g_final = (row2(norm_cross[layer]), row2(norm_mem[layer]), row2(norm_ffn[layer]),
                                      row2(norm_final))
    cw, cb = conv_w[layer], row2(conv_b[layer])
    head_of_col = jnp.arange(att_w, dtype=jnp.int32) // att_dh
    e_mat = (jnp.arange(LANES, dtype=jnp.int32)[:, None] == head_of_col[None, :]).astype(BF16)
    et_mat = e_mat.T

    q, k, v, k_keep, v_keep, hz = _in_proj(x_prompt, g_mix, w_in_bf, att_w=att_w, tm=512, keep=keep)
    mk, mv, mk_bf, mv_bf = _mem_kv(mem_prompt.reshape(Bp * n_mem, D), g_mem, w_ckv_bf, tm=256)
    pats = [_dil_attn(q, k, v, dil=d, n_heads=att_h, dh=att_dh) for _, d in DIL_PATTERNS]
    ohg, s_prompt = _hgrn(hz, hg_lb_logits, g_hg, n_heads=hg_h, dk=hg_dk, tc=256, layer=layer)
    y_prompt, cst = _post(
        x_prompt, [p[0] for p in pats], [p[1] for p in pats], ohg,
        mk_bf.reshape(Bp, n_mem, xw), mv_bf.reshape(Bp, n_mem, xw), jnp.zeros((Bp, SUBLANES, dff), F32),
        e_mat, et_mat, (g_att, g_cross, g_ffn, g_final), weights, cw, cb,
        tm=256, att_dh=att_dh, x_heads=x_h, ff_chunk=ff_chunk)

    n_s = Bs * Ts
    qs, ks, vs, ks_f, vs_f, hzs = _in_proj(x_sample.reshape(1, n_s, D), g_mix, w_in_bf, att_w=att_w, tm=n_s,
                                           keep=n_s)
    pad_new = lambda a: jnp.pad(a.reshape(Bs, Ts, att_w), ((0, 0), (0, LANES - Ts), (0, 0)))
    att_s = _dil_attn_step(qs.reshape(Bs, Ts, att_w).astype(F32), pad_new(ks), pad_new(vs),
                           cache_win_k[layer].reshape(Bs, past, att_w), cache_win_v[layer].reshape(Bs, past, att_w),
                           n_heads=att_h, dh=att_dh)
    ohg_s, s_sample = _hgrn_step(hzs.reshape(Bs, Ts, 4 * hg_w), state_hgrn[layer], hg_lb_logits, g_hg, layer=layer)
    x1_s, cq_s = _single_call(
        functools.partial(_step_mix_kernel, att_dh=att_dh),
        (x_sample.reshape(n_s, D), att_s.reshape(n_s, att_w), ohg_s.reshape(n_s, hg_w), e_mat, et_mat, g_att,
         g_cross, weights[0], weights[1]),
        [jax.ShapeDtypeStruct((n_s, D), F32), jax.ShapeDtypeStruct((n_s, xw), F32)], "step_mix")
    cq_pad = jnp.pad(cq_s.reshape(Bs, Ts, xw), ((0, 0), (0, SUBLANES - Ts), (0, 0)))
    co_s = _step_cross(cq_pad, cache_mem_k[layer].reshape(Bs, n_mem, xw),
                       cache_mem_v[layer].reshape(Bs, n_mem, xw), x_heads=x_h)[:, :Ts]
    conv_state = state_ffn_conv[layer]
    t_idx = jnp.arange(Ts)
    prev = jnp.stack([jnp.take(conv_state, jnp.clip(j + t_idx, 0, n_taps - 2), axis=1).reshape(n_s, dff)
                      for j in range(n_taps - 1)])
    y_s, ug_s = _single_call(
        functools.partial(_step_ffn_kernel, ff_chunk=ff_chunk, n_t=Ts),
        (x1_s, co_s.reshape(n_s, xw), prev, g_ffn, g_final, weights[2], weights[3], weights[4], weights[5], cw, cb),
        [jax.ShapeDtypeStruct((n_s, D), F32), jax.ShapeDtypeStruct((n_s, dff), F32)], "step_ffn",
        scratch_shapes=[pltpu.VMEM((n_s + SUBLANES, dff), F32)])

    stack = lambda a: a[None]
    return (y_prompt, y_s.reshape(Bs, Ts, D),
            stack(k_keep.reshape(Bp, keep, att_h, att_dh)), stack(v_keep.reshape(Bp, keep, att_h, att_dh)),
            stack(s_prompt), stack(cst[:, SUBLANES - (n_taps - 1):]),
            stack(mk.reshape(Bp, n_mem, x_h, x_dh)), stack(mv.reshape(Bp, n_mem, x_h, x_dh)),
            stack(ks_f.reshape(Bs, Ts, att_h, att_dh)), stack(vs_f.reshape(Bs, Ts, att_h, att_dh)),
            stack(s_sample), stack(ug_s.reshape(Bs, Ts, dff)[:, Ts - (n_taps - 1):]))
```

```python
import functools

import jax
import jax.numpy as jnp
from jax import lax
from jax.experimental import pallas as pl
from jax.experimental.pallas import tpu as pltpu

F32 = jnp.float32
BF16 = jnp.bfloat16
EPS = 1e-6
DIL_PATTERNS = ((128, 1), (512, 4), (2048, 16))
ATT_BLOCK = 128
LANES = 128
SUBLANES = 8
HG_CHUNK = 64
HG_SUB = 16
NEG = -1e30
HIGHEST = lax.Precision.HIGHEST
NT_DIMS = (((1,), (1,)), ((), ()))
TN_DIMS = (((0,), (0,)), ((), ()))
VMEM_LIMIT = 56 * 1024 * 1024


def _dot(a, b):
    return jnp.dot(a, b, preferred_element_type=F32)


def _dot_nt(a, b):
    return lax.dot_general(a, b, NT_DIMS, preferred_element_type=F32)


def _sigmoid(x):
    return 1.0 / (1.0 + jnp.exp(-x))


def _rms(x, g):
    return x * lax.rsqrt(jnp.mean(x * x, axis=-1, keepdims=True) + EPS) * g


def _const_spec(shape):
    nd = len(shape)
    return pl.BlockSpec(shape, lambda *_: (0,) * nd, pipeline_mode=pl.Buffered(1))


def _params(sem):
    return pltpu.CompilerParams(dimension_semantics=sem, vmem_limit_bytes=VMEM_LIMIT)


def _in_proj_kernel(x_ref, g_ref, w_ref, q_ref, k_ref, v_ref, kf_ref, vf_ref, hz_ref, *, att_w, q_scale):
    h = _rms(x_ref[0], g_ref[...]).astype(BF16)

    def proj(lo, hi):
        return _dot(h, w_ref[:, lo:hi])

    q_ref[0] = proj(0, att_w) * q_scale
    zk = proj(att_w, 2 * att_w)
    k_ref[0] = zk
    kf_ref[0] = zk
    zv = proj(2 * att_w, 3 * att_w)
    v_ref[0] = zv
    vf_ref[0] = zv
    base = 3 * att_w
    step = 4 * LANES
    for c in range(0, hz_ref.shape[-1], step):
        hz_ref[0, :, c:c + step] = proj(base + c, base + c + step)


def _in_proj(x, g, w_bf, *, att_w, q_scale, tm, keep):
    B, T, D = x.shape
    ncols = w_bf.shape[1]
    hzw = ncols - 3 * att_w
    n_t = T // tm
    first = n_t - keep // tm
    row = lambda b, i: (b, i, 0)
    keep_map = lambda b, i: (b, jnp.maximum(i - first, 0), 0)
    return pl.pallas_call(
        functools.partial(_in_proj_kernel, att_w=att_w, q_scale=q_scale),
        grid=(B, n_t),
        in_specs=[pl.BlockSpec((1, tm, D), row), _const_spec((1, D)), _const_spec((D, ncols))],
        out_specs=[pl.BlockSpec((1, tm, att_w), row)] * 3
        + [pl.BlockSpec((1, tm, att_w), keep_map)] * 2
        + [pl.BlockSpec((1, tm, hzw), row)],
        out_shape=[jax.ShapeDtypeStruct((B, T, att_w), F32)] * 3
        + [jax.ShapeDtypeStruct((B, keep, att_w), F32)] * 2
        + [jax.ShapeDtypeStruct((B, T, hzw), F32)],
        compiler_params=_params(("parallel", "arbitrary")),
        name="in_proj",
    )(x, g, w_bf)


def _mem_kv_kernel(m_ref, g_ref, w_ref, kf_ref, vf_ref, kb_ref, vb_ref):
    h = _rms(m_ref[...], g_ref[...]).astype(BF16)
    xw = w_ref.shape[1] // 2
    k = _dot(h, w_ref[:, :xw])
    v = _dot(h, w_ref[:, xw:])
    kf_ref[...] = k
    vf_ref[...] = v
    kb_ref[...] = k.astype(BF16)
    vb_ref[...] = v.astype(BF16)


def _mem_kv(mem2d, g, w_ckv_bf, *, tm):
    n, D = mem2d.shape
    xw = w_ckv_bf.shape[1] // 2
    row = lambda i: (i, 0)
    return pl.pallas_call(
        _mem_kv_kernel,
        grid=(n // tm,),
        in_specs=[pl.BlockSpec((tm, D), row), _const_spec((1, D)), _const_spec((D, 2 * xw))],
        out_specs=[pl.BlockSpec((tm, xw), row)] * 4,
        out_shape=[jax.ShapeDtypeStruct((n, xw), F32)] * 2 + [jax.ShapeDtypeStruct((n, xw), BF16)] * 2,
        compiler_params=_params(("parallel",)),
        name="mem_kv",
    )(mem2d, g, w_ckv_bf)


def _dil_attn_kernel(q_ref, k_ref, v_ref, g_ref, o_ref, acc_sc, m_sc, den_sc, bias_sc, *, n_heads, dh, unroll):
    tile = pl.program_id(1)
    T = q_ref.shape[1]
    blk = ATT_BLOCK
    hpt = LANES // dh
    rows, ncol = hpt * blk, 2 * blk
    rr = lax.broadcasted_iota(jnp.int32, (rows, ncol), 0)
    cc = lax.broadcasted_iota(jnp.int32, (rows, ncol), 1)
    delta = (rr % blk) + blk - cc
    in_band = (delta >= 0) & (delta <= blk)
    slope = jnp.exp2((-8.0 / n_heads) * (tile * hpt + rr // blk + 1).astype(F32))
    lane_head = lax.broadcasted_iota(jnp.int32, (blk, LANES), 1) // dh
    n_pat = len(DIL_PATTERNS)

    def per_lane(x):
        out = jnp.broadcast_to(x[0:blk], (blk, LANES))
        for e in range(1, hpt):
            out = jnp.where(lane_head == e, jnp.broadcast_to(x[e * blk:(e + 1) * blk], (blk, LANES)), out)
        return out

    for p, (_, dil) in enumerate(DIL_PATTERNS):
        alibi = -slope * (delta * dil).astype(F32)
        bias_sc[0] = jnp.where(in_band & (cc >= blk), alibi, NEG)
        bias_sc[1] = jnp.where(in_band, alibi, NEG)
        nblk = T // (blk * dil)

        def body(n, carry, p=p, dil=dil, nblk=nblk):
            kp, vp = carry
            i = n % nblk
            start = n // nblk + i * (blk * dil)
            idx = pl.ds(start, blk, stride=dil) if dil > 1 else pl.ds(pl.multiple_of(start, blk), blk)
            q2 = q_ref[0, idx, :]
            kc = k_ref[0, idx, :].astype(BF16)
            vc = v_ref[0, idx, :].astype(BF16)
            qs = jnp.concatenate([jnp.where(lane_head == e, q2, 0.0) for e in range(hpt)], axis=0).astype(BF16)
            s = _dot_nt(qs, jnp.concatenate([kp, kc], axis=0)) + bias_sc[jnp.minimum(i, 1)]
            m = jnp.max(s, axis=-1, keepdims=True)
            pr = jnp.exp(s - m)
            den = jnp.sum(pr, axis=-1, keepdims=True)
            pv = _dot(pr.astype(BF16), jnp.concatenate([vp, vc], axis=0))
            m_t, den_t, acc_t = per_lane(m), per_lane(den), per_lane(pv)
            if p > 0:
                m_old = m_sc[idx, :]
                m_new = jnp.maximum(m_old, m_t)
                a_old, a_t = jnp.exp(m_old - m_new), jnp.exp(m_t - m_new)
                den_t = a_old * den_sc[idx, :] + a_t * den_t
                acc_t = a_old * acc_sc[idx, :] + a_t * acc_t
                m_t = m_new
            if p == n_pat - 1:
                acc_sc[idx, :] = acc_t / den_t
            else:
                m_sc[idx, :] = m_t
                den_sc[idx, :] = den_t
                acc_sc[idx, :] = acc_t
            return kc, vc

        zero = jnp.zeros((blk, LANES), BF16)
        lax.fori_loop(0, T // blk, body, (zero, zero), unroll=unroll)

    ch = 4 * blk
    lane_h = lax.broadcasted_iota(jnp.int32, (ch, LANES), 1) // dh
    for c in range(T // ch):
        a = acc_sc[c * ch:(c + 1) * ch, :]
        sq = a * a
        mean_sq = jnp.zeros((ch, LANES), F32)
        for e in range(hpt):
            se = jnp.sum(jnp.where(lane_h == e, sq, 0.0), axis=-1, keepdims=True) * (1.0 / dh)
            mean_sq = jnp.where(lane_h == e, jnp.broadcast_to(se, (ch, LANES)), mean_sq)
        o_ref[0, c * ch:(c + 1) * ch, :] = (a * lax.rsqrt(mean_sq + EPS) * g_ref[...]).astype(BF16)


def _dil_attn(q, k, v, g, *, n_heads, dh, unroll):
    B, T, W = q.shape
    blk = ATT_BLOCK
    seq = pl.BlockSpec((1, T, LANES), lambda b, t: (b, 0, t))
    return pl.pallas_call(
        functools.partial(_dil_attn_kernel, n_heads=n_heads, dh=dh, unroll=unroll),
        grid=(B, W // LANES),
        in_specs=[seq, seq, seq, pl.BlockSpec((1, LANES), lambda b, t: (0, t))],
        out_specs=seq,
        out_shape=jax.ShapeDtypeStruct((B, T, W), BF16),
        scratch_shapes=[pltpu.VMEM((T, LANES), F32)] * 3
        + [pltpu.VMEM((2, (LANES // dh) * blk, 2 * blk), F32)],
        compiler_params=_params(("parallel", "arbitrary")),
        name="dil_attn",
    )(q, k, v, g)


def _dil_attn_step_kernel(q_ref, kn_ref, vn_ref, ck_ref, cv_ref, g_ref, o_ref, *, n_heads, dh, n_q):
    past = ck_ref.shape[1]
    n_new = kn_ref.shape[1]
    W = q_ref.shape[-1]
    rows = n_q * n_heads
    sub = lax.broadcasted_iota(jnp.int32, (n_heads, W), 0)
    lane_head = lax.broadcasted_iota(jnp.int32, (n_heads, W), 1) // dh
    qbd = jnp.concatenate(
        [jnp.where(sub == lane_head, jnp.broadcast_to(q_ref[0, i:i + 1, :], (n_heads, W)), 0.0) for i in range(n_q)],
        axis=0).astype(BF16)

    def scores(keys_bf, n_keys, first_pos):
        s = _dot_nt(qbd, keys_bf)
        r = lax.broadcasted_iota(jnp.int32, (rows, n_keys), 0)
        c = lax.broadcasted_iota(jnp.int32, (rows, n_keys), 1)
        delta = past + r // n_heads - (first_pos + c)
        slope = jnp.exp2((-8.0 / n_heads) * ((r % n_heads) + 1).astype(F32))
        cnt = jnp.zeros((rows, n_keys), F32)
        for win, dil in DIL_PATTERNS:
            ok = (delta >= 0) & (delta <= win) & ((delta & (dil - 1)) == 0)
            cnt = cnt + ok.astype(F32)
        s = s - slope * delta.astype(F32)
        return jnp.where(cnt > 0.0, s, NEG), cnt

    s_c, cnt_c = scores(ck_ref[0].astype(BF16), past, 0)
    s_n, cnt_n = scores(kn_ref[0], n_new, past)
    m = jnp.maximum(jnp.max(s_c, axis=-1, keepdims=True), jnp.max(s_n, axis=-1, keepdims=True))
    p_c = jnp.exp(s_c - m) * cnt_c
    p_n = jnp.exp(s_n - m) * cnt_n
    den = jnp.sum(p_c, axis=-1, keepdims=True) + jnp.sum(p_n, axis=-1, keepdims=True)
    out = (_dot(p_c.astype(BF16), cv_ref[0].astype(BF16)) + _dot(p_n.astype(BF16), vn_ref[0])) / den
    r = lax.broadcasted_iota(jnp.int32, (rows, W), 0)
    c = lax.broadcasted_iota(jnp.int32, (rows, W), 1)
    out = jnp.where((r % n_heads) == (c // dh), out, 0.0)
    out = out * lax.rsqrt(jnp.sum(out * out, axis=-1, keepdims=True) * (1.0 / dh) + EPS)
    o_ref[0] = jnp.sum(out.reshape(n_q, n_heads, W), axis=1) * g_ref[...]


def _dil_attn_step(q, k_new_pad, v_new_pad, cache_k, cache_v, g, *, n_heads, dh):
    B, n_q, W = q.shape
    past = cache_k.shape[1]
    n_new = k_new_pad.shape[1]
    bmap = lambda b: (b, 0, 0)
    return pl.pallas_call(
        functools.partial(_dil_attn_step_kernel, n_heads=n_heads, dh=dh, n_q=n_q),
        grid=(B,),
        in_specs=[pl.BlockSpec((1, n_q, W), bmap), pl.BlockSpec((1, n_new, W), bmap), pl.BlockSpec((1, n_new, W), bmap),
                  pl.BlockSpec((1, past, W), bmap), pl.BlockSpec((1, past, W), bmap), _const_spec((1, W))],
        out_specs=pl.BlockSpec((1, n_q, W), bmap),
        out_shape=jax.ShapeDtypeStruct((B, n_q, W), F32),
        compiler_params=_params(("parallel",)),
        name="dil_attn_step",
    )(q, k_new_pad, v_new_pad, cache_k, cache_v, g)


def _lower_bound(lbl_ref, layer):
    logits = lbl_ref[...]
    e = jnp.exp(logits - jnp.max(logits, axis=0, keepdims=True))
    return jnp.sum(e[:layer + 1], axis=0, keepdims=True) / jnp.sum(e, axis=0, keepdims=True)


def _hgrn_kernel(hz_ref, lbl_ref, gn_ref, o_ref, s_ref, st_ref, b_sc, k_sc, *, n_heads, dk, layer):
    i = pl.program_id(1)
    hw = n_heads * dk
    tc = hz_ref.shape[1]
    ch, sub = HG_CHUNK, HG_SUB

    @pl.when(i == 0)
    def _():
        st_ref[...] = jnp.zeros_like(st_ref)

    lb = _lower_bound(lbl_ref, layer)
    r_c = lax.broadcasted_iota(jnp.int32, (ch, ch), 0)
    c_c = lax.broadcasted_iota(jnp.int32, (ch, ch), 1)
    tril = (r_c >= c_c).astype(F32)
    r_s = lax.broadcasted_iota(jnp.int32, (sub, ch), 0)
    c_s = lax.broadcasted_iota(jnp.int32, (sub, ch), 1)

    for c in range(tc // ch):
        rows = slice(c * ch, (c + 1) * ch)
        hq = hz_ref[0, rows, 0:hw]
        f = lb + (1.0 - lb) * _sigmoid(hz_ref[0, rows, hw:2 * hw])
        qh = hq * _sigmoid(hq) * dk ** -0.5
        b_sc[...] = jnp.dot(tril, jnp.log(f), precision=HIGHEST, preferred_element_type=F32)
        k_sc[...] = 1.0 - f
        for h in range(n_heads):
            hs = slice(h * dk, (h + 1) * dk)
            bh = b_sc[:, hs]
            kh = k_sc[:, hs]
            qhh = qh[:, hs]
            vh = hz_ref[0, rows, 2 * hw + h * dk:2 * hw + (h + 1) * dk].astype(BF16)
            st = st_ref[h]
            o = _dot_nt((qhh * jnp.exp(bh)).astype(BF16), st.astype(BF16))
            blocks = []
            for j in range(ch // sub):
                r0 = j * sub
                bj = bh[r0:r0 + sub]
                qj = qhh[r0:r0 + sub]
                a_j = jnp.zeros((sub, ch), F32)
                for s in range(sub):
                    e = jnp.exp(bj - b_sc[r0 + s:r0 + s + 1, hs])
                    a = jnp.sum(qj * e * k_sc[r0 + s:r0 + s + 1, hs], axis=-1, keepdims=True)
                    a_j = jnp.where((c_s == r0 + s) & (r_s >= s), a, a_j)
                if j > 0:
                    beta = b_sc[r0 - 1:r0, hs]
                    qt = qj * jnp.exp(bj - beta)
                    kt = kh * jnp.exp(jnp.minimum(beta - bh, 0.0))
                    a_j = jnp.where(c_s < r0, _dot_nt(qt.astype(BF16), kt.astype(BF16)), a_j)
                blocks.append(a_j)
            a_mat = jnp.concatenate(blocks, axis=0)
            o = o + _dot(a_mat.astype(BF16), vh)
            b_last = b_sc[ch - 1:ch, hs]
            khat = (kh * jnp.exp(b_last - bh)).astype(BF16)
            st_ref[h] = st * jnp.exp(b_last) + lax.dot_general(vh, khat, TN_DIMS, preferred_element_type=F32)
            hg = hz_ref[0, rows, 3 * hw + h * dk:3 * hw + (h + 1) * dk]
            o_ref[0, rows, hs] = (_rms(o, gn_ref[:, hs]) * (hg * _sigmoid(hg))).astype(BF16)

    @pl.when(i == pl.num_programs(1) - 1)
    def _():
        for h in range(n_heads):
            s_ref[0, h] = st_ref[h].T


def _hgrn(hz, lb_logits, gn, *, n_heads, dk, tc, layer):
    B, T, _ = hz.shape
    hw = n_heads * dk
    row = lambda b, i: (b, i, 0)
    return pl.pallas_call(
        functools.partial(_hgrn_kernel, n_heads=n_heads, dk=dk, layer=layer),
        grid=(B, T // tc),
        in_specs=[pl.BlockSpec((1, tc, 4 * hw), row), _const_spec(lb_logits.shape), _const_spec((1, hw))],
        out_specs=[pl.BlockSpec((1, tc, hw), row), pl.BlockSpec((1, n_heads, dk, dk), lambda b, i: (b, 0, 0, 0))],
        out_shape=[jax.ShapeDtypeStruct((B, T, hw), BF16), jax.ShapeDtypeStruct((B, n_heads, dk, dk), F32)],
        scratch_shapes=[pltpu.VMEM((n_heads, dk, dk), F32), pltpu.VMEM((HG_CHUNK, hw), F32),
                        pltpu.VMEM((HG_CHUNK, hw), F32)],
        compiler_params=_params(("parallel", "arbitrary")),
        name="hgrn",
    )(hz, lb_logits, gn)


def _hgrn_step_kernel(hz_ref, s0_ref, lbl_ref, gn_ref, o_ref, s_ref, *, n_heads, dk, layer):
    hw = n_heads * dk
    n_t = hz_ref.shape[1]
    lb = _lower_bound(lbl_ref, layer)
    q, k, v, b, gate = [], [], [], [], []
    acc = jnp.zeros((1, hw), F32)
    for t in range(n_t):
        hq = hz_ref[0, t:t + 1, 0:hw]
        f = lb + (1.0 - lb) * _sigmoid(hz_ref[0, t:t + 1, hw:2 * hw])
        acc = acc + jnp.log(f)
        q.append(hq * _sigmoid(hq) * dk ** -0.5)
        k.append(1.0 - f)
        v.append(hz_ref[0, t:t + 1, 2 * hw:3 * hw])
        b.append(acc)
        hg = hz_ref[0, t:t + 1, 3 * hw:4 * hw]
        gate.append(hg * _sigmoid(hg))
    rr = lax.broadcasted_iota(jnp.int32, (dk, dk), 0)
    cc = lax.broadcasted_iota(jnp.int32, (dk, dk), 1)

    def tile_of_rows(rows):
        tile = jnp.zeros((dk, dk), F32)
        for t, r in enumerate(rows):
            tile = jnp.where(rr == t, jnp.broadcast_to(r, (dk, dk)), tile)
        return tile

    for h in range(n_heads):
        hs = slice(h * dk, (h + 1) * dk)
        s0 = s0_ref[0, h]
        qe = tile_of_rows([q[t][:, hs] * jnp.exp(b[t][:, hs]) for t in range(n_t)])
        inter = jnp.dot(qe, s0, precision=HIGHEST, preferred_element_type=F32)
        for t in range(n_t):
            o = inter[t:t + 1]
            for s in range(t + 1):
                a = jnp.sum(q[t][:, hs] * jnp.exp(b[t][:, hs] - b[s][:, hs]) * k[s][:, hs], axis=-1, keepdims=True)
                o = o + a * v[s][:, hs]
            o_ref[0, t:t + 1, hs] = _rms(o, gn_ref[:, hs]) * gate[t][:, hs]
        b_last = b[n_t - 1][:, hs]
        khat = tile_of_rows([k[t][:, hs] * jnp.exp(b_last - b[t][:, hs]) for t in range(n_t)])
        vpad = tile_of_rows([v[t][:, hs] for t in range(n_t)])
        decay = jnp.where(rr == cc, jnp.broadcast_to(jnp.exp(b_last), (dk, dk)), 0.0)
        s_ref[0, h] = (jnp.dot(decay, s0, precision=HIGHEST, preferred_element_type=F32)
                       + jnp.dot(khat.T, vpad, precision=HIGHEST, preferred_element_type=F32))


def _hgrn_step(hz, s0, lb_logits, gn, *, layer):
    B, n_t, _ = hz.shape
    _, n_heads, dk, dv = s0.shape
    hw = n_heads * dk
    return pl.pallas_call(
        functools.partial(_hgrn_step_kernel, n_heads=n_heads, dk=dk, layer=layer),
        grid=(B,),
        in_specs=[pl.BlockSpec((1, n_t, 4 * hw), lambda b: (b, 0, 0)),
                  pl.BlockSpec((1, n_heads, dk, dv), lambda b: (b, 0, 0, 0)),
                  _const_spec(lb_logits.shape), _const_spec((1, hw))],
        out_specs=[pl.BlockSpec((1, n_t, hw), lambda b: (b, 0, 0)),
                   pl.BlockSpec((1, n_heads, dk, dv), lambda b: (b, 0, 0, 0))],
        out_shape=[jax.ShapeDtypeStruct((B, n_t, hw), F32), jax.ShapeDtypeStruct((B, n_heads, dk, dv), F32)],
        compiler_params=_params(("parallel",)),
        name="hgrn_step",
    )(hz, s0, lb_logits, gn)


def _mix_out(x, att_bf, ohg_bf, wo_ref):
    aw = att_bf.shape[-1]
    return x + _dot(att_bf, wo_ref[0:aw, :]) + _dot(ohg_bf, wo_ref[aw:, :])


def _cross_attend(cq, mk, mv, n_heads, dh):
    outs = []
    for h in range(n_heads):
        sl = slice(h * dh, (h + 1) * dh)
        s = _dot_nt((cq[:, sl] * dh ** -0.5).astype(BF16), mk[:, sl])
        p = jnp.exp(s - jnp.max(s, axis=-1, keepdims=True))
        den = jnp.sum(p, axis=-1, keepdims=True)
        outs.append(_dot(p.astype(BF16), mv[:, sl]) / den)
    return jnp.concatenate(outs, axis=-1)


def _ffn(u_bf, taps_fn, wg_ref, wu_ref, wd_ref, cw_ref, cb_ref, ff_chunk):
    dff = wg_ref.shape[1]
    n_taps = cw_ref.shape[0]
    acc = jnp.zeros((u_bf.shape[0], wd_ref.shape[1]), F32)
    for n in range(dff // ff_chunk):
        cs = slice(n * ff_chunk, (n + 1) * ff_chunk)
        ug = _dot(u_bf, wg_ref[:, cs])
        taps = taps_fn(ug, cs, n)
        conv = cb_ref[:, cs]
        for j in range(n_taps):
            conv = conv + cw_ref[j:j + 1, cs] * taps[j]
        act = conv * _sigmoid(conv) * _dot(u_bf, wu_ref[:, cs])
        acc = acc + _dot(act.astype(BF16), wd_ref[cs, :])
    return acc


def _post_kernel(x_ref, att_ref, ohg_ref, mk_ref, mv_ref, cinit_ref, gc_ref, gf_ref, gl_ref, wo_ref, wq_ref,
                 wc_ref, wg_ref, wu_ref, wd_ref, cw_ref, cb_ref, y_ref, cst_ref, buf_ref, carry_ref,
                 *, x_heads, x_dh, ff_chunk):
    i = pl.program_id(1)
    tm = x_ref.shape[1]
    pad = SUBLANES
    n_taps = cw_ref.shape[0]

    @pl.when(i == 0)
    def _():
        carry_ref[...] = cinit_ref[0]

    x1 = _mix_out(x_ref[0], att_ref[0], ohg_ref[0], wo_ref)
    cq = _dot(_rms(x1, gc_ref[...]).astype(BF16), wq_ref[...])
    co = _cross_attend(cq, mk_ref[0], mv_ref[0], x_heads, x_dh)
    x2 = x1 + _dot(co.astype(BF16), wc_ref[...])

    def taps_fn(ug, cs, n):
        buf = buf_ref.at[n % 2]
        buf[0:pad, :] = carry_ref[:, cs]
        buf[pad:pad + tm, :] = ug
        carry_ref[:, cs] = ug[tm - pad:tm]
        return [buf[pad - (n_taps - 1 - j):pad - (n_taps - 1 - j) + tm, :] for j in range(n_taps - 1)] + [ug]

    u_bf = _rms(x2, gf_ref[...]).astype(BF16)
    x3 = x2 + _ffn(u_bf, taps_fn, wg_ref, wu_ref, wd_ref, cw_ref, cb_ref, ff_chunk)
    y_ref[0] = _rms(x3, gl_ref[...])
    cst_ref[0] = carry_ref[...]


def _post(x, att, ohg, mk_bf, mv_bf, cinit, gains, weights, conv_w, conv_b, *, tm, x_heads, ff_chunk):
    B, T, D = x.shape
    aw, hw = att.shape[-1], ohg.shape[-1]
    n_mem, xw = mk_bf.shape[1:]
    dff = conv_w.shape[1]
    row = lambda b, i: (b, i, 0)
    bat = lambda b, i: (b, 0, 0)
    in_specs = ([pl.BlockSpec((1, tm, D), row), pl.BlockSpec((1, tm, aw), row), pl.BlockSpec((1, tm, hw), row)]
                + [pl.BlockSpec((1, n_mem, xw), bat)] * 2 + [pl.BlockSpec((1, SUBLANES, dff), bat)]
                + [_const_spec(a.shape) for a in tuple(gains) + tuple(weights) + (conv_w, conv_b)])
    return pl.pallas_call(
        functools.partial(_post_kernel, x_heads=x_heads, x_dh=xw // x_heads, ff_chunk=ff_chunk),
        grid=(B, T // tm),
        in_specs=in_specs,
        out_specs=[pl.BlockSpec((1, tm, D), row), pl.BlockSpec((1, SUBLANES, dff), bat)],
        out_shape=[jax.ShapeDtypeStruct((B, T, D), F32), jax.ShapeDtypeStruct((B, SUBLANES, dff), F32)],
        scratch_shapes=[pltpu.VMEM((2, tm + SUBLANES, ff_chunk), F32), pltpu.VMEM((SUBLANES, dff), F32)],
        compiler_params=_params(("parallel", "arbitrary")),
        name="post_mixer",
    )(x, att, ohg, mk_bf, mv_bf, cinit, *gains, *weights, conv_w, conv_b)


def _step_mix_kernel(x_ref, att_ref, ohg_ref, gc_ref, wo_ref, wq_ref, x1_ref, cq_ref):
    x1 = _mix_out(x_ref[...], att_ref[...].astype(BF16), ohg_ref[...].astype(BF16), wo_ref)
    x1_ref[...] = x1
    cq_ref[...] = _dot(_rms(x1, gc_ref[...]).astype(BF16), wq_ref[...])


def _step_cross_kernel(cq_ref, mk_ref, mv_ref, co_ref, *, x_heads, x_dh):
    co_ref[0] = _cross_attend(cq_ref[0], mk_ref[0].astype(BF16), mv_ref[0].astype(BF16), x_heads, x_dh)


def _step_ffn_kernel(x1_ref, co_ref, prev_ref, gf_ref, gl_ref, wc_ref, wg_ref, wu_ref, wd_ref,
                     cw_ref, cb_ref, y_ref, ug_ref, buf_ref, *, ff_chunk, n_t):
    n = x1_ref.shape[0]
    pad = SUBLANES
    n_taps = cw_ref.shape[0]
    x2 = x1_ref[...] + _dot(co_ref[...].astype(BF16), wc_ref[...])
    t_of_row = lax.broadcasted_iota(jnp.int32, (n, ff_chunk), 0) % n_t

    def taps_fn(ug, cs, _):
        ug_ref[:, cs] = ug
        buf_ref[0:pad, cs] = jnp.zeros((pad, ff_chunk), F32)
        buf_ref[pad:pad + n, cs] = ug
        taps = []
        for j in range(n_taps - 1):
            shift = n_taps - 1 - j
            shifted = buf_ref[pad - shift:pad - shift + n, cs]
            taps.append(jnp.where(t_of_row < shift, prev_ref[j, :, cs], shifted))
        return taps + [ug]

    u_bf = _rms(x2, gf_ref[...]).astype(BF16)
    x3 = x2 + _ffn(u_bf, taps_fn, wg_ref, wu_ref, wd_ref, cw_ref, cb_ref, ff_chunk)
    y_ref[...] = _rms(x3, gl_ref[...])


def _single_call(kernel, args, out_shape, name, scratch_shapes=()):
    return pl.pallas_call(
        kernel,
        grid=(1,),
        in_specs=[_const_spec(a.shape) for a in args],
        out_specs=[pl.BlockSpec(o.shape, lambda *_, nd=len(o.shape): (0,) * nd) for o in out_shape],
        out_shape=out_shape,
        scratch_shapes=list(scratch_shapes),
        compiler_params=_params(("arbitrary",)),
        name=name,
    )(*args)


def _step_cross(cq, mem_k, mem_v, *, x_heads):
    B, n_q, xw = cq.shape
    n_mem = mem_k.shape[1]
    bmap = lambda b: (b, 0, 0)
    return pl.pallas_call(
        functools.partial(_step_cross_kernel, x_heads=x_heads, x_dh=xw // x_heads),
        grid=(B,),
        in_specs=[pl.BlockSpec((1, n_q, xw), bmap), pl.BlockSpec((1, n_mem, xw), bmap),
                  pl.BlockSpec((1, n_mem, xw), bmap)],
        out_specs=pl.BlockSpec((1, n_q, xw), bmap),
        out_shape=jax.ShapeDtypeStruct((B, n_q, xw), F32),
        compiler_params=_params(("parallel",)),
        name="step_cross",
    )(cq, mem_k, mem_v)


def kernel(x_prompt, x_sample, cache_win_k, cache_win_v, state_hgrn, state_ffn_conv, cache_mem_k, cache_mem_v,
           mem_prompt, hg_lb_logits, norm_mix, w_in, att_out_norm, hg_out_norm, w_out, norm_cross, norm_mem,
           w_cq, w_ck, w_cv, w_co, norm_ffn, w_gate, w_up, conv_w, conv_b, w_down, norm_final):
    Bp, T, D = x_prompt.shape
    Bs, Ts, _ = x_sample.shape
    depth, _, past, att_h, att_dh = cache_win_k.shape
    _, _, hg_h, hg_dk, hg_dv = state_hgrn.shape
    _, _, n_mem, x_h, x_dh = cache_mem_k.shape
    n_taps, dff = conv_w.shape[1:]
    att_w = att_h * att_dh
    hg_w = hg_h * hg_dk
    xw = x_h * x_dh
    keep = min(max(w for w, _ in DIL_PATTERNS), T)
    assert depth == 1 and hg_dk == hg_dv == LANES and x_dh == LANES and LANES % att_dh == 0
    assert all(w // d == ATT_BLOCK for w, d in DIL_PATTERNS) and past >= max(w for w, _ in DIL_PATTERNS)
    assert T % (ATT_BLOCK * max(d for _, d in DIL_PATTERNS)) == 0
    assert n_taps - 1 <= min(Ts, SUBLANES) and Ts <= SUBLANES
    layer = 0
    ff_chunk = 2 * LANES
    q_scale = att_dh ** -0.5
    row2 = lambda a: a.reshape(1, -1)

    w_in_bf = w_in[layer].astype(BF16)
    w_ckv_bf = jnp.concatenate([w_ck[layer], w_cv[layer]], axis=1).astype(BF16)
    weights = tuple(w[layer].astype(BF16) for w in (w_out, w_cq, w_co, w_gate, w_up, w_down))
    g_mix, g_att, g_hg = row2(norm_mix[layer]), row2(att_out_norm[layer]), row2(hg_out_norm[layer])
    g_cross, g_mem, g_ffn, g_final = (row2(norm_cross[layer]), row2(norm_mem[layer]), row2(norm_ffn[layer]),
                                      row2(norm_final))
    cw, cb = conv_w[layer], row2(conv_b[layer])

    q, k, v, k_keep, v_keep, hz = _in_proj(x_prompt, g_mix, w_in_bf, att_w=att_w, q_scale=q_scale, tm=512, keep=keep)
    mk, mv, mk_bf, mv_bf = _mem_kv(mem_prompt.reshape(Bp * n_mem, D), g_mem, w_ckv_bf, tm=256)
    att = _dil_attn(q, k, v, g_att, n_heads=att_h, dh=att_dh, unroll=8)
    ohg, s_prompt = _hgrn(hz, hg_lb_logits, g_hg, n_heads=hg_h, dk=hg_dk, tc=256, layer=layer)
    y_prompt, cst = _post(
        x_prompt, att, ohg, mk_bf.reshape(Bp, n_mem, xw), mv_bf.reshape(Bp, n_mem, xw),
        jnp.zeros((Bp, SUBLANES, dff), F32), (g_cross, g_ffn, g_final), weights, cw, cb,
        tm=512, x_heads=x_h, ff_chunk=ff_chunk)

    n_s = Bs * Ts
    qs, ks, vs, ks_f, vs_f, hzs = _in_proj(x_sample.reshape(1, n_s, D), g_mix, w_in_bf, att_w=att_w, q_scale=q_scale,
                                           tm=n_s, keep=n_s)
    pad_new = lambda a: jnp.pad(a.reshape(Bs, Ts, att_w).astype(BF16), ((0, 0), (0, LANES - Ts), (0, 0)))
    att_s = _dil_attn_step(qs.reshape(Bs, Ts, att_w), pad_new(ks), pad_new(vs),
                           cache_win_k[layer].reshape(Bs, past, att_w), cache_win_v[layer].reshape(Bs, past, att_w),
                           g_att, n_heads=att_h, dh=att_dh)
    ohg_s, s_sample = _hgrn_step(hzs.reshape(Bs, Ts, 4 * hg_w), state_hgrn[layer], hg_lb_logits, g_hg, layer=layer)
    x1_s, cq_s = _single_call(
        _step_mix_kernel,
        (x_sample.reshape(n_s, D), att_s.reshape(n_s, att_w), ohg_s.reshape(n_s, hg_w), g_cross, weights[0],
         weights[1]),
        [jax.ShapeDtypeStruct((n_s, D), F32), jax.ShapeDtypeStruct((n_s, xw), F32)], "step_mix")
    cq_pad = jnp.pad(cq_s.reshape(Bs, Ts, xw), ((0, 0), (0, SUBLANES - Ts), (0, 0)))
    co_s = _step_cross(cq_pad, cache_mem_k[layer].reshape(Bs, n_mem, xw),
                       cache_mem_v[layer].reshape(Bs, n_mem, xw), x_heads=x_h)[:, :Ts]
    conv_state = state_ffn_conv[layer]
    t_idx = jnp.arange(Ts)
    prev = jnp.stack([jnp.take(conv_state, jnp.clip(j + t_idx, 0, n_taps - 2), axis=1).reshape(n_s, dff)
                      for j in range(n_taps - 1)])
    y_s, ug_s = _single_call(
        functools.partial(_step_ffn_kernel, ff_chunk=ff_chunk, n_t=Ts),
        (x1_s, co_s.reshape(n_s, xw), prev, g_ffn, g_final, weights[2], weights[3], weights[4], weights[5], cw, cb),
        [jax.ShapeDtypeStruct((n_s, D), F32), jax.ShapeDtypeStruct((n_s, dff), F32)], "step_ffn",
        scratch_shapes=[pltpu.VMEM((n_s + SUBLANES, dff), F32)])

    stack = lambda a: a[None]
    return (y_prompt, y_s.reshape(Bs, Ts, D),
            stack(k_keep.reshape(Bp, keep, att_h, att_dh)), stack(v_keep.reshape(Bp, keep, att_h, att_dh)),
            stack(s_prompt), stack(cst[:, SUBLANES - (n_taps - 1):]),
            stack(mk.reshape(Bp, n_mem, x_h, x_dh)), stack(mv.reshape(Bp, n_mem, x_h, x_dh)),
            stack(ks_f.reshape(Bs, Ts, att_h, att_dh)), stack(vs_f.reshape(Bs, Ts, att_h, att_dh)),
            stack(s_sample), stack(ug_s.reshape(Bs, Ts, dff)[:, Ts - (n_taps - 1):]))
```

```python
import functools

import jax
import jax.numpy as jnp
from jax import lax
from jax.experimental import pallas as pl
from jax.experimental.pallas import tpu as pltpu

F32 = jnp.float32
BF16 = jnp.bfloat16
EPS = 1e-6
DIL_PATTERNS = ((128, 1), (512, 4), (2048, 16))
ATT_BLOCK = 128
LANES = 128
SUBLANES = 8
HG_CHUNK = 64
HG_SUB = 16
NEG = -1e30
HIGHEST = lax.Precision.HIGHEST
NT_DIMS = (((1,), (1,)), ((), ()))
TN_DIMS = (((0,), (0,)), ((), ()))
VMEM_LIMIT = 56 * 1024 * 1024


def _dot(a, b):
    return jnp.dot(a, b, preferred_element_type=F32)


def _dot_nt(a, b):
    return lax.dot_general(a, b, NT_DIMS, preferred_element_type=F32)


def _sigmoid(x):
    return 1.0 / (1.0 + jnp.exp(-x))


def _rms(x, g):
    return x * lax.rsqrt(jnp.mean(x * x, axis=-1, keepdims=True) + EPS) * g


def _const_spec(shape):
    nd = len(shape)
    return pl.BlockSpec(shape, lambda *_: (0,) * nd, pipeline_mode=pl.Buffered(1))


def _params(sem):
    return pltpu.CompilerParams(dimension_semantics=sem, vmem_limit_bytes=VMEM_LIMIT)


def _in_proj_kernel(x_ref, g_ref, w_ref, q_ref, k_ref, v_ref, kf_ref, vf_ref, hz_ref, *, att_w, q_scale):
    h = _rms(x_ref[0], g_ref[...]).astype(BF16)

    def proj(lo, hi):
        return _dot(h, w_ref[:, lo:hi])

    q_ref[0] = proj(0, att_w) * q_scale
    zk = proj(att_w, 2 * att_w)
    k_ref[0] = zk
    kf_ref[0] = zk
    zv = proj(2 * att_w, 3 * att_w)
    v_ref[0] = zv
    vf_ref[0] = zv
    base = 3 * att_w
    step = 4 * LANES
    for c in range(0, hz_ref.shape[-1], step):
        hz_ref[0, :, c:c + step] = proj(base + c, base + c + step)


def _in_proj(x, g, w_bf, *, att_w, q_scale, tm, keep):
    B, T, D = x.shape
    ncols = w_bf.shape[1]
    hzw = ncols - 3 * att_w
    n_t = T // tm
    first = n_t - keep // tm
    row = lambda b, i: (b, i, 0)
    keep_map = lambda b, i: (b, jnp.maximum(i - first, 0), 0)
    return pl.pallas_call(
        functools.partial(_in_proj_kernel, att_w=att_w, q_scale=q_scale),
        grid=(B, n_t),
        in_specs=[pl.BlockSpec((1, tm, D), row), _const_spec((1, D)), _const_spec((D, ncols))],
        out_specs=[pl.BlockSpec((1, tm, att_w), row)] * 3
        + [pl.BlockSpec((1, tm, att_w), keep_map)] * 2
        + [pl.BlockSpec((1, tm, hzw), row)],
        out_shape=[jax.ShapeDtypeStruct((B, T, att_w), F32)] * 3
        + [jax.ShapeDtypeStruct((B, keep, att_w), F32)] * 2
        + [jax.ShapeDtypeStruct((B, T, hzw), F32)],
        compiler_params=_params(("parallel", "arbitrary")),
        name="in_proj",
    )(x, g, w_bf)


def _mem_kv_kernel(m_ref, g_ref, w_ref, kf_ref, vf_ref, kb_ref, vb_ref):
    h = _rms(m_ref[...], g_ref[...]).astype(BF16)
    xw = w_ref.shape[1] // 2
    k = _dot(h, w_ref[:, :xw])
    v = _dot(h, w_ref[:, xw:])
    kf_ref[...] = k
    vf_ref[...] = v
    kb_ref[...] = k.astype(BF16)
    vb_ref[...] = v.astype(BF16)


def _mem_kv(mem2d, g, w_ckv_bf, *, tm):
    n, D = mem2d.shape
    xw = w_ckv_bf.shape[1] // 2
    row = lambda i: (i, 0)
    return pl.pallas_call(
        _mem_kv_kernel,
        grid=(n // tm,),
        in_specs=[pl.BlockSpec((tm, D), row), _const_spec((1, D)), _const_spec((D, 2 * xw))],
        out_specs=[pl.BlockSpec((tm, xw), row)] * 4,
        out_shape=[jax.ShapeDtypeStruct((n, xw), F32)] * 2 + [jax.ShapeDtypeStruct((n, xw), BF16)] * 2,
        compiler_params=_params(("parallel",)),
        name="mem_kv",
    )(mem2d, g, w_ckv_bf)


def _dil_attn_kernel(q_ref, k_ref, v_ref, g_ref, o_ref, acc_sc, m_sc, den_sc, bias_sc, *, n_heads, dh, unroll):
    tile = pl.program_id(1)
    T = q_ref.shape[1]
    blk = ATT_BLOCK
    hpt = LANES // dh
    rows, ncol = hpt * blk, 2 * blk
    rr = lax.broadcasted_iota(jnp.int32, (rows, ncol), 0)
    cc = lax.broadcasted_iota(jnp.int32, (rows, ncol), 1)
    delta = (rr % blk) + blk - cc
    in_band = (delta >= 0) & (delta <= blk)
    slope = jnp.exp2((-8.0 / n_heads) * (tile * hpt + rr // blk + 1).astype(F32))
    lane_head = lax.broadcasted_iota(jnp.int32, (blk, LANES), 1) // dh
    n_pat = len(DIL_PATTERNS)

    def per_lane(x):
        out = jnp.broadcast_to(x[0:blk], (blk, LANES))
        for e in range(1, hpt):
            out = jnp.where(lane_head == e, jnp.broadcast_to(x[e * blk:(e + 1) * blk], (blk, LANES)), out)
        return out

    for p, (_, dil) in enumerate(DIL_PATTERNS):
        alibi = -slope * (delta * dil).astype(F32)
        bias_sc[0] = jnp.where(in_band & (cc >= blk), alibi, NEG)
        bias_sc[1] = jnp.where(in_band, alibi, NEG)
        nblk = T // (blk * dil)

        def body(n, carry, p=p, dil=dil, nblk=nblk):
            kp, vp = carry
            i = n % nblk
            start = n // nblk + i * (blk * dil)
            idx = pl.ds(start, blk, stride=dil) if dil > 1 else pl.ds(pl.multiple_of(start, blk), blk)
            q2 = q_ref[0, idx, :]
            kc = k_ref[0, idx, :].astype(BF16)
            vc = v_ref[0, idx, :].astype(BF16)
            qs = jnp.concatenate([jnp.where(lane_head == e, q2, 0.0) for e in range(hpt)], axis=0).astype(BF16)
            s = _dot_nt(qs, jnp.concatenate([kp, kc], axis=0)) + bias_sc[jnp.minimum(i, 1)]
            m = jnp.max(s, axis=-1, keepdims=True)
            pr = jnp.exp(s - m)
            den = jnp.sum(pr, axis=-1, keepdims=True)
            pv = _dot(pr.astype(BF16), jnp.concatenate([vp, vc], axis=0))
            m_t, den_t, acc_t = per_lane(m), per_lane(den), per_lane(pv)
            if p > 0:
                m_old = m_sc[idx, :]
                m_new = jnp.maximum(m_old, m_t)
                a_old, a_t = jnp.exp(m_old - m_new), jnp.exp(m_t - m_new)
                den_t = a_old * den_sc[idx, :] + a_t * den_t
                acc_t = a_old * acc_sc[idx, :] + a_t * acc_t
                m_t = m_new
            if p == n_pat - 1:
                acc_sc[idx, :] = acc_t / den_t
            else:
                m_sc[idx, :] = m_t
                den_sc[idx, :] = den_t
                acc_sc[idx, :] = acc_t
            return kc, vc

        zero = jnp.zeros((blk, LANES), BF16)
        lax.fori_loop(0, T // blk, body, (zero, zero), unroll=unroll)

    ch = 4 * blk
    lane_h = lax.broadcasted_iota(jnp.int32, (ch, LANES), 1) // dh
    for c in range(T // ch):
        a = acc_sc[c * ch:(c + 1) * ch, :]
        sq = a * a
        mean_sq = jnp.zeros((ch, LANES), F32)
        for e in range(hpt):
            se = jnp.sum(jnp.where(lane_h == e, sq, 0.0), axis=-1, keepdims=True) * (1.0 / dh)
            mean_sq = jnp.where(lane_h == e, jnp.broadcast_to(se, (ch, LANES)), mean_sq)
        o_ref[0, c * ch:(c + 1) * ch, :] = (a * lax.rsqrt(mean_sq + EPS) * g_ref[...]).astype(BF16)


def _dil_attn(q, k, v, g, *, n_heads, dh, unroll):
    B, T, W = q.shape
    blk = ATT_BLOCK
    seq = pl.BlockSpec((1, T, LANES), lambda b, t: (b, 0, t))
    return pl.pallas_call(
        functools.partial(_dil_attn_kernel, n_heads=n_heads, dh=dh, unroll=unroll),
        grid=(B, W // LANES),
        in_specs=[seq, seq, seq, pl.BlockSpec((1, LANES), lambda b, t: (0, t))],
        out_specs=seq,
        out_shape=jax.ShapeDtypeStruct((B, T, W), BF16),
        scratch_shapes=[pltpu.VMEM((T, LANES), F32)] * 3
        + [pltpu.VMEM((2, (LANES // dh) * blk, 2 * blk), F32)],
        compiler_params=_params(("parallel", "arbitrary")),
        name="dil_attn",
    )(q, k, v, g)


def _dil_attn_step_kernel(q_ref, kn_ref, vn_ref, pos_ref, fark_ref, reck_ref, farv_ref, recv_ref, g_ref, o_ref,
                          cnt_sc, bias_sc, *, n_heads, dh, n_q, past):
    n_rows = q_ref.shape[1]
    n_cols = pos_ref.shape[1]

    @pl.when(pl.program_id(0) == 0)
    def _():
        row = lax.broadcasted_iota(jnp.int32, (n_rows, n_cols), 0)
        col = lax.broadcasted_iota(jnp.int32, (n_rows, n_cols), 1)
        delta = past + row % n_q - pos_ref[...]
        same_head = (col % n_heads) == (row // n_q)
        cnt = jnp.zeros((n_rows, n_cols), F32)
        for win, dil in DIL_PATTERNS:
            cnt = cnt + (same_head & (delta >= 0) & (delta <= win) & ((delta & (dil - 1)) == 0)).astype(F32)
        slope = jnp.exp2((-8.0 / n_heads) * (row // n_q + 1).astype(F32))
        cnt_sc[...] = cnt
        bias_sc[...] = jnp.where(cnt > 0.0, -slope * delta.astype(F32), NEG)

    def flat(far_ref, rec_ref, new_ref):
        parts = [r[0].reshape(-1, dh) for r in (far_ref, rec_ref, new_ref)]
        return jnp.concatenate(parts, axis=0).astype(BF16)

    s = _dot_nt(q_ref[0].astype(BF16), flat(fark_ref, reck_ref, kn_ref)) + bias_sc[...]
    p = jnp.exp(s - jnp.max(s, axis=-1, keepdims=True)) * cnt_sc[...]
    o = _dot(p.astype(BF16), flat(farv_ref, recv_ref, vn_ref)) / jnp.sum(p, axis=-1, keepdims=True)
    o_ref[0] = _rms(o, g_ref[...])


def _dil_attn_step(q, k_new, v_new, cache_k, cache_v, g, *, n_q, recent):
    B, n_rows, dh = q.shape
    _, past, n_heads, _ = cache_k.shape
    n_new = k_new.shape[1]
    d_max = max(d for _, d in DIL_PATTERNS)
    assert past % recent == 0 and (past - recent) % d_max == 0 and n_q <= d_max
    n_far = (past - recent) // d_max
    pos = jnp.concatenate([(d_max * jnp.arange(n_far)[:, None] + jnp.arange(n_q)[None, :]).reshape(-1),
                           past - recent + jnp.arange(recent), past + jnp.arange(n_new)])
    pos = jnp.repeat(pos, n_heads).astype(jnp.int32)[None, :]
    far_view = lambda c: c.reshape(B, past // d_max, d_max, n_heads, dh)
    far = pl.BlockSpec((1, n_far, n_q, n_heads, dh), lambda b: (b, 0, 0, 0, 0))
    rec = pl.BlockSpec((1, recent, n_heads, dh), lambda b: (b, past // recent - 1, 0, 0))
    new = pl.BlockSpec((1, n_new, n_heads, dh), lambda b: (b, 0, 0, 0))
    rows = pl.BlockSpec((1, n_rows, dh), lambda b: (b, 0, 0))
    return pl.pallas_call(
        functools.partial(_dil_attn_step_kernel, n_heads=n_heads, dh=dh, n_q=n_q, past=past),
        grid=(B,),
        in_specs=[rows, new, new, _const_spec(pos.shape), far, rec, far, rec, _const_spec(g.shape)],
        out_specs=rows,
        out_shape=jax.ShapeDtypeStruct((B, n_rows, dh), F32),
        scratch_shapes=[pltpu.VMEM((n_rows, pos.shape[1]), F32)] * 2,
        compiler_params=_params(("arbitrary",)),
        name="dil_attn_step",
    )(q, k_new, v_new, pos, far_view(cache_k), cache_k, far_view(cache_v), cache_v, g)


def _lower_bound(lbl_ref, layer):
    logits = lbl_ref[...]
    e = jnp.exp(logits - jnp.max(logits, axis=0, keepdims=True))
    return jnp.sum(e[:layer + 1], axis=0, keepdims=True) / jnp.sum(e, axis=0, keepdims=True)


def _hgrn_kernel(hz_ref, lbl_ref, gn_ref, o_ref, s_ref, st_ref, b_sc, k_sc, *, n_heads, dk, layer):
    i = pl.program_id(1)
    hw = n_heads * dk
    tc = hz_ref.shape[1]
    ch, sub = HG_CHUNK, HG_SUB

    @pl.when(i == 0)
    def _():
        st_ref[...] = jnp.zeros_like(st_ref)

    lb = _lower_bound(lbl_ref, layer)
    r_c = lax.broadcasted_iota(jnp.int32, (ch, ch), 0)
    c_c = lax.broadcasted_iota(jnp.int32, (ch, ch), 1)
    tril = (r_c >= c_c).astype(F32)
    r_s = lax.broadcasted_iota(jnp.int32, (sub, ch), 0)
    c_s = lax.broadcasted_iota(jnp.int32, (sub, ch), 1)

    for c in range(tc // ch):
        rows = slice(c * ch, (c + 1) * ch)
        hq = hz_ref[0, rows, 0:hw]
        f = lb + (1.0 - lb) * _sigmoid(hz_ref[0, rows, hw:2 * hw])
        qh = hq * _sigmoid(hq) * dk ** -0.5
        b_sc[...] = jnp.dot(tril, jnp.log(f), precision=HIGHEST, preferred_element_type=F32)
        k_sc[...] = 1.0 - f
        for h in range(n_heads):
            hs = slice(h * dk, (h + 1) * dk)
            bh = b_sc[:, hs]
            kh = k_sc[:, hs]
            qhh = qh[:, hs]
            vh = hz_ref[0, rows, 2 * hw + h * dk:2 * hw + (h + 1) * dk].astype(BF16)
            st = st_ref[h]
            o = _dot_nt((qhh * jnp.exp(bh)).astype(BF16), st.astype(BF16))
            blocks = []
            for j in range(ch // sub):
                r0 = j * sub
                bj = bh[r0:r0 + sub]
                qj = qhh[r0:r0 + sub]
                a_j = jnp.zeros((sub, ch), F32)
                for s in range(sub):
                    e = jnp.exp(bj - b_sc[r0 + s:r0 + s + 1, hs])
                    a = jnp.sum(qj * e * k_sc[r0 + s:r0 + s + 1, hs], axis=-1, keepdims=True)
                    a_j = jnp.where((c_s == r0 + s) & (r_s >= s), a, a_j)
                if j > 0:
                    beta = b_sc[r0 - 1:r0, hs]
                    qt = qj * jnp.exp(bj - beta)
                    kt = kh * jnp.exp(jnp.minimum(beta - bh, 0.0))
                    a_j = jnp.where(c_s < r0, _dot_nt(qt.astype(BF16), kt.astype(BF16)), a_j)
                blocks.append(a_j)
            a_mat = jnp.concatenate(blocks, axis=0)
            o = o + _dot(a_mat.astype(BF16), vh)
            b_last = b_sc[ch - 1:ch, hs]
            khat = (kh * jnp.exp(b_last - bh)).astype(BF16)
            st_ref[h] = st * jnp.exp(b_last) + lax.dot_general(vh, khat, TN_DIMS, preferred_element_type=F32)
            hg = hz_ref[0, rows, 3 * hw + h * dk:3 * hw + (h + 1) * dk]
            o_ref[0, rows, hs] = (_rms(o, gn_ref[:, hs]) * (hg * _sigmoid(hg))).astype(BF16)

    @pl.when(i == pl.num_programs(1) - 1)
    def _():
        for h in range(n_heads):
            s_ref[0, h] = st_ref[h].T


def _hgrn(hz, lb_logits, gn, *, n_heads, dk, tc, layer):
    B, T, _ = hz.shape
    hw = n_heads * dk
    row = lambda b, i: (b, i, 0)
    return pl.pallas_call(
        functools.partial(_hgrn_kernel, n_heads=n_heads, dk=dk, layer=layer),
        grid=(B, T // tc),
        in_specs=[pl.BlockSpec((1, tc, 4 * hw), row), _const_spec(lb_logits.shape), _const_spec((1, hw))],
        out_specs=[pl.BlockSpec((1, tc, hw), row), pl.BlockSpec((1, n_heads, dk, dk), lambda b, i: (b, 0, 0, 0))],
        out_shape=[jax.ShapeDtypeStruct((B, T, hw), BF16), jax.ShapeDtypeStruct((B, n_heads, dk, dk), F32)],
        scratch_shapes=[pltpu.VMEM((n_heads, dk, dk), F32), pltpu.VMEM((HG_CHUNK, hw), F32),
                        pltpu.VMEM((HG_CHUNK, hw), F32)],
        compiler_params=_params(("parallel", "arbitrary")),
        name="hgrn",
    )(hz, lb_logits, gn)


def _hgrn_step_kernel(hz_ref, s0_ref, lbl_ref, gn_ref, o_ref, s_ref, *, n_heads, dk, layer):
    hw = n_heads * dk
    n_t = hz_ref.shape[1]
    lb = _lower_bound(lbl_ref, layer)
    q, k, v, b, gate = [], [], [], [], []
    acc = jnp.zeros((1, hw), F32)
    for t in range(n_t):
        hq = hz_ref[0, t:t + 1, 0:hw]
        f = lb + (1.0 - lb) * _sigmoid(hz_ref[0, t:t + 1, hw:2 * hw])
        acc = acc + jnp.log(f)
        q.append(hq * _sigmoid(hq) * dk ** -0.5)
        k.append(1.0 - f)
        v.append(hz_ref[0, t:t + 1, 2 * hw:3 * hw])
        b.append(acc)
        hg = hz_ref[0, t:t + 1, 3 * hw:4 * hw]
        gate.append(hg * _sigmoid(hg))
    rr = lax.broadcasted_iota(jnp.int32, (dk, dk), 0)
    cc = lax.broadcasted_iota(jnp.int32, (dk, dk), 1)

    def tile_of_rows(rows):
        tile = jnp.zeros((dk, dk), F32)
        for t, r in enumerate(rows):
            tile = jnp.where(rr == t, jnp.broadcast_to(r, (dk, dk)), tile)
        return tile

    for h in range(n_heads):
        hs = slice(h * dk, (h + 1) * dk)
        s0 = s0_ref[0, h]
        qe = tile_of_rows([q[t][:, hs] * jnp.exp(b[t][:, hs]) for t in range(n_t)])
        inter = jnp.dot(qe, s0, precision=HIGHEST, preferred_element_type=F32)
        for t in range(n_t):
            o = inter[t:t + 1]
            for s in range(t + 1):
                a = jnp.sum(q[t][:, hs] * jnp.exp(b[t][:, hs] - b[s][:, hs]) * k[s][:, hs], axis=-1, keepdims=True)
                o = o + a * v[s][:, hs]
            o_ref[0, t:t + 1, hs] = _rms(o, gn_ref[:, hs]) * gate[t][:, hs]
        b_last = b[n_t - 1][:, hs]
        khat = tile_of_rows([k[t][:, hs] * jnp.exp(b_last - b[t][:, hs]) for t in range(n_t)])
        vpad = tile_of_rows([v[t][:, hs] for t in range(n_t)])
        decay = jnp.where(rr == cc, jnp.broadcast_to(jnp.exp(b_last), (dk, dk)), 0.0)
        s_ref[0, h] = (jnp.dot(decay, s0, precision=HIGHEST, preferred_element_type=F32)
                       + jnp.dot(khat.T, vpad, precision=HIGHEST, preferred_element_type=F32))


def _hgrn_step(hz, s0, lb_logits, gn, *, layer):
    B, n_t, _ = hz.shape
    _, n_heads, dk, dv = s0.shape
    hw = n_heads * dk
    return pl.pallas_call(
        functools.partial(_hgrn_step_kernel, n_heads=n_heads, dk=dk, layer=layer),
        grid=(B,),
        in_specs=[pl.BlockSpec((1, n_t, 4 * hw), lambda b: (b, 0, 0)),
                  pl.BlockSpec((1, n_heads, dk, dv), lambda b: (b, 0, 0, 0)),
                  _const_spec(lb_logits.shape), _const_spec((1, hw))],
        out_specs=[pl.BlockSpec((1, n_t, hw), lambda b: (b, 0, 0)),
                   pl.BlockSpec((1, n_heads, dk, dv), lambda b: (b, 0, 0, 0))],
        out_shape=[jax.ShapeDtypeStruct((B, n_t, hw), F32), jax.ShapeDtypeStruct((B, n_heads, dk, dv), F32)],
        compiler_params=_params(("parallel",)),
        name="hgrn_step",
    )(hz, s0, lb_logits, gn)


def _mix_out(x, att_bf, ohg_bf, wo_ref):
    aw = att_bf.shape[-1]
    return x + _dot(att_bf, wo_ref[0:aw, :]) + _dot(ohg_bf, wo_ref[aw:, :])


def _cross_attend(cq, head_k, head_v, n_heads, dh):
    outs = []
    for h in range(n_heads):
        s = _dot_nt((cq[:, h * dh:(h + 1) * dh] * dh ** -0.5).astype(BF16), head_k(h))
        p = jnp.exp(s - jnp.max(s, axis=-1, keepdims=True))
        den = jnp.sum(p, axis=-1, keepdims=True)
        outs.append(_dot(p.astype(BF16), head_v(h)) / den)
    return jnp.concatenate(outs, axis=-1)


def _ffn(u_bf, taps_fn, wg_ref, wu_ref, wd_ref, cw_ref, cb_ref, ff_chunk):
    dff = wg_ref.shape[1]
    n_taps = cw_ref.shape[0]
    acc = jnp.zeros((u_bf.shape[0], wd_ref.shape[1]), F32)
    for n in range(dff // ff_chunk):
        cs = slice(n * ff_chunk, (n + 1) * ff_chunk)
        ug = _dot(u_bf, wg_ref[:, cs])
        taps = taps_fn(ug, cs, n)
        conv = cb_ref[:, cs]
        for j in range(n_taps):
            conv = conv + cw_ref[j:j + 1, cs] * taps[j]
        act = conv * _sigmoid(conv) * _dot(u_bf, wu_ref[:, cs])
        acc = acc + _dot(act.astype(BF16), wd_ref[cs, :])
    return acc


def _post_kernel(x_ref, att_ref, ohg_ref, mk_ref, mv_ref, cinit_ref, gc_ref, gf_ref, gl_ref, wo_ref, wq_ref,
                 wc_ref, wg_ref, wu_ref, wd_ref, cw_ref, cb_ref, y_ref, cst_ref, buf_ref, carry_ref,
                 *, x_heads, x_dh, ff_chunk):
    i = pl.program_id(1)
    tm = x_ref.shape[1]
    pad = SUBLANES
    n_taps = cw_ref.shape[0]

    @pl.when(i == 0)
    def _():
        carry_ref[...] = cinit_ref[0]

    x1 = _mix_out(x_ref[0], att_ref[0], ohg_ref[0], wo_ref)
    cq = _dot(_rms(x1, gc_ref[...]).astype(BF16), wq_ref[...])
    co = _cross_attend(cq, lambda h: mk_ref[0, :, h * x_dh:(h + 1) * x_dh],
                       lambda h: mv_ref[0, :, h * x_dh:(h + 1) * x_dh], x_heads, x_dh)
    x2 = x1 + _dot(co.astype(BF16), wc_ref[...])

    def taps_fn(ug, cs, n):
        buf = buf_ref.at[n % 2]
        buf[0:pad, :] = carry_ref[:, cs]
        buf[pad:pad + tm, :] = ug
        carry_ref[:, cs] = ug[tm - pad:tm]
        return [buf[pad - (n_taps - 1 - j):pad - (n_taps - 1 - j) + tm, :] for j in range(n_taps - 1)] + [ug]

    u_bf = _rms(x2, gf_ref[...]).astype(BF16)
    x3 = x2 + _ffn(u_bf, taps_fn, wg_ref, wu_ref, wd_ref, cw_ref, cb_ref, ff_chunk)
    y_ref[0] = _rms(x3, gl_ref[...])
    cst_ref[0] = carry_ref[...]


def _post(x, att, ohg, mk_bf, mv_bf, cinit, gains, weights, conv_w, conv_b, *, tm, x_heads, ff_chunk):
    B, T, D = x.shape
    aw, hw = att.shape[-1], ohg.shape[-1]
    n_mem, xw = mk_bf.shape[1:]
    dff = conv_w.shape[1]
    row = lambda b, i: (b, i, 0)
    bat = lambda b, i: (b, 0, 0)
    in_specs = ([pl.BlockSpec((1, tm, D), row), pl.BlockSpec((1, tm, aw), row), pl.BlockSpec((1, tm, hw), row)]
                + [pl.BlockSpec((1, n_mem, xw), bat)] * 2 + [pl.BlockSpec((1, SUBLANES, dff), bat)]
                + [_const_spec(a.shape) for a in tuple(gains) + tuple(weights) + (conv_w, conv_b)])
    return pl.pallas_call(
        functools.partial(_post_kernel, x_heads=x_heads, x_dh=xw // x_heads, ff_chunk=ff_chunk),
        grid=(B, T // tm),
        in_specs=in_specs,
        out_specs=[pl.BlockSpec((1, tm, D), row), pl.BlockSpec((1, SUBLANES, dff), bat)],
        out_shape=[jax.ShapeDtypeStruct((B, T, D), F32), jax.ShapeDtypeStruct((B, SUBLANES, dff), F32)],
        scratch_shapes=[pltpu.VMEM((2, tm + SUBLANES, ff_chunk), F32), pltpu.VMEM((SUBLANES, dff), F32)],
        compiler_params=_params(("parallel", "arbitrary")),
        name="post_mixer",
    )(x, att, ohg, mk_bf, mv_bf, cinit, *gains, *weights, conv_w, conv_b)


def _step_mix_kernel(x_ref, att_ref, ohg_ref, gc_ref, wo_ref, wq_ref, x1_ref, cq_ref):
    x1 = _mix_out(x_ref[...], att_ref[...].astype(BF16), ohg_ref[...].astype(BF16), wo_ref)
    x1_ref[...] = x1
    cq_ref[...] = _dot(_rms(x1, gc_ref[...]).astype(BF16), wq_ref[...])


def _step_cross_kernel(cq_ref, mk_ref, mv_ref, co_ref, *, x_heads, x_dh):
    co_ref[0] = _cross_attend(cq_ref[0], lambda h: mk_ref[0, :, h, :].astype(BF16),
                              lambda h: mv_ref[0, :, h, :].astype(BF16), x_heads, x_dh)


def _step_ffn_kernel(x1_ref, co_ref, prev_ref, gf_ref, gl_ref, wc_ref, wg_ref, wu_ref, wd_ref,
                     cw_ref, cb_ref, y_ref, ug_ref, buf_ref, *, ff_chunk, n_t):
    n = x1_ref.shape[0]
    pad = SUBLANES
    n_taps = cw_ref.shape[0]
    x2 = x1_ref[...] + _dot(co_ref[...].astype(BF16), wc_ref[...])
    t_of_row = lax.broadcasted_iota(jnp.int32, (n, ff_chunk), 0) % n_t

    def taps_fn(ug, cs, _):
        ug_ref[:, cs] = ug
        buf_ref[0:pad, cs] = jnp.zeros((pad, ff_chunk), F32)
        buf_ref[pad:pad + n, cs] = ug
        taps = []
        for j in range(n_taps - 1):
            shift = n_taps - 1 - j
            shifted = buf_ref[pad - shift:pad - shift + n, cs]
            taps.append(jnp.where(t_of_row < shift, prev_ref[j, :, cs], shifted))
        return taps + [ug]

    u_bf = _rms(x2, gf_ref[...]).astype(BF16)
    x3 = x2 + _ffn(u_bf, taps_fn, wg_ref, wu_ref, wd_ref, cw_ref, cb_ref, ff_chunk)
    y_ref[...] = _rms(x3, gl_ref[...])


def _single_call(kernel, args, out_shape, name, scratch_shapes=()):
    return pl.pallas_call(
        kernel,
        grid=(1,),
        in_specs=[_const_spec(a.shape) for a in args],
        out_specs=[pl.BlockSpec(o.shape, lambda *_, nd=len(o.shape): (0,) * nd) for o in out_shape],
        out_shape=out_shape,
        scratch_shapes=list(scratch_shapes),
        compiler_params=_params(("arbitrary",)),
        name=name,
    )(*args)


def _step_cross(cq, mem_k, mem_v):
    B, n_q, xw = cq.shape
    _, n_mem, x_heads, x_dh = mem_k.shape
    bmap = lambda b: (b, 0, 0)
    mem = pl.BlockSpec((1, n_mem, x_heads, x_dh), lambda b: (b, 0, 0, 0))
    return pl.pallas_call(
        functools.partial(_step_cross_kernel, x_heads=x_heads, x_dh=x_dh),
        grid=(B,),
        in_specs=[pl.BlockSpec((1, n_q, xw), bmap), mem, mem],
        out_specs=pl.BlockSpec((1, n_q, xw), bmap),
        out_shape=jax.ShapeDtypeStruct((B, n_q, xw), F32),
        compiler_params=_params(("parallel",)),
        name="step_cross",
    )(cq, mem_k, mem_v)


def kernel(x_prompt, x_sample, cache_win_k, cache_win_v, state_hgrn, state_ffn_conv, cache_mem_k, cache_mem_v,
           mem_prompt, hg_lb_logits, norm_mix, w_in, att_out_norm, hg_out_norm, w_out, norm_cross, norm_mem,
           w_cq, w_ck, w_cv, w_co, norm_ffn, w_gate, w_up, conv_w, conv_b, w_down, norm_final):
    Bp, T, D = x_prompt.shape
    Bs, Ts, _ = x_sample.shape
    depth, _, past, att_h, att_dh = cache_win_k.shape
    _, _, hg_h, hg_dk, hg_dv = state_hgrn.shape
    _, _, n_mem, x_h, x_dh = cache_mem_k.shape
    n_taps, dff = conv_w.shape[1:]
    att_w = att_h * att_dh
    hg_w = hg_h * hg_dk
    xw = x_h * x_dh
    keep = min(max(w for w, _ in DIL_PATTERNS), T)
    assert depth == 1 and hg_dk == hg_dv == LANES and x_dh == LANES and LANES % att_dh == 0
    assert all(w // d == ATT_BLOCK for w, d in DIL_PATTERNS) and past >= max(w for w, _ in DIL_PATTERNS)
    d_max = max(d for _, d in DIL_PATTERNS)
    assert T % (ATT_BLOCK * d_max) == 0 and past % d_max == 0 and past % (4 * ATT_BLOCK) == 0
    assert n_taps - 1 <= min(Ts, SUBLANES) and Ts <= SUBLANES
    layer = 0
    ff_chunk = 2 * LANES
    q_scale = att_dh ** -0.5
    row2 = lambda a: a.reshape(1, -1)

    w_in_bf = w_in[layer].astype(BF16)
    w_ckv_bf = jnp.concatenate([w_ck[layer], w_cv[layer]], axis=1).astype(BF16)
    weights = tuple(w[layer].astype(BF16) for w in (w_out, w_cq, w_co, w_gate, w_up, w_down))
    g_mix, g_att, g_hg = row2(norm_mix[layer]), row2(att_out_norm[layer]), row2(hg_out_norm[layer])
    g_cross, g_mem, g_ffn, g_final = (row2(norm_cross[layer]), row2(norm_mem[layer]), row2(norm_ffn[layer]),
                                      row2(norm_final))
    cw, cb = conv_w[layer], row2(conv_b[layer])

    q, k, v, k_keep, v_keep, hz = _in_proj(x_prompt, g_mix, w_in_bf, att_w=att_w, q_scale=q_scale, tm=512, keep=keep)
    mk, mv, mk_bf, mv_bf = _mem_kv(mem_prompt.reshape(Bp * n_mem, D), g_mem, w_ckv_bf, tm=256)
    att = _dil_attn(q, k, v, g_att, n_heads=att_h, dh=att_dh, unroll=8)
    ohg, s_prompt = _hgrn(hz, hg_lb_logits, g_hg, n_heads=hg_h, dk=hg_dk, tc=256, layer=layer)
    y_prompt, cst = _post(
        x_prompt, att, ohg, mk_bf.reshape(Bp, n_mem, xw), mv_bf.reshape(Bp, n_mem, xw),
        jnp.zeros((Bp, SUBLANES, dff), F32), (g_cross, g_ffn, g_final), weights, cw, cb,
        tm=512, x_heads=x_h, ff_chunk=ff_chunk)

    n_s = Bs * Ts
    qs, ks, vs, ks_f, vs_f, hzs = _in_proj(x_sample.reshape(1, n_s, D), g_mix, w_in_bf, att_w=att_w, q_scale=q_scale,
                                           tm=n_s, keep=n_s)
    per_head = lambda a: a.reshape(Bs, Ts, att_h, att_dh)
    pad_new = lambda a: jnp.pad(per_head(a), ((0, 0), (0, 2 * SUBLANES - Ts), (0, 0), (0, 0)))
    recent = max(w for w, d in DIL_PATTERNS if d < d_max)
    att_s = _dil_attn_step(per_head(qs).transpose(0, 2, 1, 3).reshape(Bs, att_h * Ts, att_dh), pad_new(ks),
                           pad_new(vs), cache_win_k[layer], cache_win_v[layer],
                           jnp.repeat(g_att.reshape(att_h, att_dh), Ts, axis=0), n_q=Ts, recent=recent)
    att_s = att_s.reshape(Bs, att_h, Ts, att_dh).transpose(0, 2, 1, 3)
    ohg_s, s_sample = _hgrn_step(hzs.reshape(Bs, Ts, 4 * hg_w), state_hgrn[layer], hg_lb_logits, g_hg, layer=layer)
    x1_s, cq_s = _single_call(
        _step_mix_kernel,
        (x_sample.reshape(n_s, D), att_s.reshape(n_s, att_w), ohg_s.reshape(n_s, hg_w), g_cross, weights[0],
         weights[1]),
        [jax.ShapeDtypeStruct((n_s, D), F32), jax.ShapeDtypeStruct((n_s, xw), F32)], "step_mix")
    cq_pad = jnp.pad(cq_s.reshape(Bs, Ts, xw), ((0, 0), (0, SUBLANES - Ts), (0, 0)))
    co_s = _step_cross(cq_pad, cache_mem_k[layer], cache_mem_v[layer])[:, :Ts]
    conv_state = state_ffn_conv[layer]
    t_idx = jnp.arange(Ts)
    prev = jnp.stack([jnp.take(conv_state, jnp.clip(j + t_idx, 0, n_taps - 2), axis=1).reshape(n_s, dff)
                      for j in range(n_taps - 1)])
    y_s, ug_s = _single_call(
        functools.partial(_step_ffn_kernel, ff_chunk=ff_chunk, n_t=Ts),
        (x1_s, co_s.reshape(n_s, xw), prev, g_ffn, g_final, weights[2], weights[3], weights[4], weights[5], cw, cb),
        [jax.ShapeDtypeStruct((n_s, D), F32), jax.ShapeDtypeStruct((n_s, dff), F32)], "step_ffn",
        scratch_shapes=[pltpu.VMEM((n_s + SUBLANES, dff), F32)])

    stack = lambda a: a[None]
    return (y_prompt, y_s.reshape(Bs, Ts, D),
            stack(k_keep.reshape(Bp, keep, att_h, att_dh)), stack(v_keep.reshape(Bp, keep, att_h, att_dh)),
            stack(s_prompt), stack(cst[:, SUBLANES - (n_taps - 1):]),
            stack(mk.reshape(Bp, n_mem, x_h, x_dh)), stack(mv.reshape(Bp, n_mem, x_h, x_dh)),
            stack(ks_f.reshape(Bs, Ts, att_h, att_dh)), stack(vs_f.reshape(Bs, Ts, att_h, att_dh)),
            stack(s_sample), stack(ug_s.reshape(Bs, Ts, dff)[:, Ts - (n_taps - 1):]))
```

```python
import functools

import jax
import jax.numpy as jnp
from jax import lax
from jax.experimental import pallas as pl
from jax.experimental.pallas import tpu as pltpu

F32 = jnp.float32
BF16 = jnp.bfloat16
EPS = 1e-6
DIL_PATTERNS = ((128, 1), (512, 4), (2048, 16))
ATT_BLOCK = 128
LANES = 128
SUBLANES = 8
HG_CHUNK = 64
HG_SUB = 16
NEG = -1e30
HIGHEST = lax.Precision.HIGHEST
NT_DIMS = (((1,), (1,)), ((), ()))
TN_DIMS = (((0,), (0,)), ((), ()))
VMEM_LIMIT = 56 * 1024 * 1024


def _dot(a, b):
    return jnp.dot(a, b, preferred_element_type=F32)


def _dot_nt(a, b):
    return lax.dot_general(a, b, NT_DIMS, preferred_element_type=F32)


def _sigmoid(x):
    return 1.0 / (1.0 + jnp.exp(-x))


def _rms(x, g):
    return x * lax.rsqrt(jnp.mean(x * x, axis=-1, keepdims=True) + EPS) * g


def _const_spec(shape):
    nd = len(shape)
    return pl.BlockSpec(shape, lambda *_: (0,) * nd, pipeline_mode=pl.Buffered(1))


def _params(sem):
    return pltpu.CompilerParams(dimension_semantics=sem, vmem_limit_bytes=VMEM_LIMIT)


def _in_proj_kernel(x_ref, g_ref, w_ref, q_ref, k_ref, v_ref, kf_ref, vf_ref, hz_ref, *, att_w, q_scale):
    h = _rms(x_ref[0], g_ref[...]).astype(BF16)

    def proj(lo, hi):
        return _dot(h, w_ref[:, lo:hi])

    q_ref[0] = proj(0, att_w) * q_scale
    zk = proj(att_w, 2 * att_w)
    k_ref[0] = zk
    kf_ref[0] = zk
    zv = proj(2 * att_w, 3 * att_w)
    v_ref[0] = zv
    vf_ref[0] = zv
    base = 3 * att_w
    step = 4 * LANES
    for c in range(0, hz_ref.shape[-1], step):
        hz_ref[0, :, c:c + step] = proj(base + c, base + c + step)


def _in_proj(x, g, w_bf, *, att_w, q_scale, tm, keep):
    B, T, D = x.shape
    ncols = w_bf.shape[1]
    hzw = ncols - 3 * att_w
    n_t = T // tm
    first = n_t - keep // tm
    row = lambda b, i: (b, i, 0)
    keep_map = lambda b, i: (b, jnp.maximum(i - first, 0), 0)
    return pl.pallas_call(
        functools.partial(_in_proj_kernel, att_w=att_w, q_scale=q_scale),
        grid=(B, n_t),
        in_specs=[pl.BlockSpec((1, tm, D), row), _const_spec((1, D)), _const_spec((D, ncols))],
        out_specs=[pl.BlockSpec((1, tm, att_w), row)] * 3
        + [pl.BlockSpec((1, tm, att_w), keep_map)] * 2
        + [pl.BlockSpec((1, tm, hzw), row)],
        out_shape=[jax.ShapeDtypeStruct((B, T, att_w), F32)] * 3
        + [jax.ShapeDtypeStruct((B, keep, att_w), F32)] * 2
        + [jax.ShapeDtypeStruct((B, T, hzw), F32)],
        compiler_params=_params(("parallel", "arbitrary")),
        name="in_proj",
    )(x, g, w_bf)


def _mem_kv_kernel(m_ref, g_ref, w_ref, kf_ref, vf_ref, kb_ref, vb_ref):
    h = _rms(m_ref[...], g_ref[...]).astype(BF16)
    xw = w_ref.shape[1] // 2
    k = _dot(h, w_ref[:, :xw])
    v = _dot(h, w_ref[:, xw:])
    kf_ref[...] = k
    vf_ref[...] = v
    kb_ref[...] = k.astype(BF16)
    vb_ref[...] = v.astype(BF16)


def _mem_kv(mem2d, g, w_ckv_bf, *, tm):
    n, D = mem2d.shape
    xw = w_ckv_bf.shape[1] // 2
    row = lambda i: (i, 0)
    return pl.pallas_call(
        _mem_kv_kernel,
        grid=(n // tm,),
        in_specs=[pl.BlockSpec((tm, D), row), _const_spec((1, D)), _const_spec((D, 2 * xw))],
        out_specs=[pl.BlockSpec((tm, xw), row)] * 4,
        out_shape=[jax.ShapeDtypeStruct((n, xw), F32)] * 2 + [jax.ShapeDtypeStruct((n, xw), BF16)] * 2,
        compiler_params=_params(("parallel",)),
        name="mem_kv",
    )(mem2d, g, w_ckv_bf)


def _dil_attn_kernel(q_ref, k_ref, v_ref, g_ref, o_ref, acc_sc, m_sc, den_sc, bias_sc, *, n_heads, dh, unroll):
    tile = pl.program_id(1)
    T = q_ref.shape[1]
    blk = ATT_BLOCK
    hpt = LANES // dh
    rows, ncol = hpt * blk, 2 * blk
    rr = lax.broadcasted_iota(jnp.int32, (rows, ncol), 0)
    cc = lax.broadcasted_iota(jnp.int32, (rows, ncol), 1)
    delta = (rr % blk) + blk - cc
    in_band = (delta >= 0) & (delta <= blk)
    slope = jnp.exp2((-8.0 / n_heads) * (tile * hpt + rr // blk + 1).astype(F32))
    lane_head = lax.broadcasted_iota(jnp.int32, (blk, LANES), 1) // dh
    n_pat = len(DIL_PATTERNS)

    def per_lane(x):
        out = jnp.broadcast_to(x[0:blk], (blk, LANES))
        for e in range(1, hpt):
            out = jnp.where(lane_head == e, jnp.broadcast_to(x[e * blk:(e + 1) * blk], (blk, LANES)), out)
        return out

    for p, (_, dil) in enumerate(DIL_PATTERNS):
        alibi = -slope * (delta * dil).astype(F32)
        bias_sc[0] = jnp.where(in_band & (cc >= blk), alibi, NEG)
        bias_sc[1] = jnp.where(in_band, alibi, NEG)
        nblk = T // (blk * dil)

        def body(n, carry, p=p, dil=dil, nblk=nblk):
            kp, vp = carry
            i = n % nblk
            start = n // nblk + i * (blk * dil)
            idx = pl.ds(start, blk, stride=dil) if dil > 1 else pl.ds(pl.multiple_of(start, blk), blk)
            q2 = q_ref[0, idx, :]
            kc = k_ref[0, idx, :].astype(BF16)
            vc = v_ref[0, idx, :].astype(BF16)
            qs = jnp.concatenate([jnp.where(lane_head == e, q2, 0.0) for e in range(hpt)], axis=0).astype(BF16)
            s = _dot_nt(qs, jnp.concatenate([kp, kc], axis=0)) + bias_sc[jnp.minimum(i, 1)]
            m = jnp.max(s, axis=-1, keepdims=True)
            pr = jnp.exp(s - m)
            den = jnp.sum(pr, axis=-1, keepdims=True)
            pv = _dot(pr.astype(BF16), jnp.concatenate([vp, vc], axis=0))
            m_t, den_t, acc_t = per_lane(m), per_lane(den), per_lane(pv)
            if p > 0:
                m_old = m_sc[idx, :]
                m_new = jnp.maximum(m_old, m_t)
                a_old, a_t = jnp.exp(m_old - m_new), jnp.exp(m_t - m_new)
                den_t = a_old * den_sc[idx, :] + a_t * den_t
                acc_t = a_old * acc_sc[idx, :] + a_t * acc_t
                m_t = m_new
            if p == n_pat - 1:
                acc_sc[idx, :] = acc_t / den_t
            else:
                m_sc[idx, :] = m_t
                den_sc[idx, :] = den_t
                acc_sc[idx, :] = acc_t
            return kc, vc

        zero = jnp.zeros((blk, LANES), BF16)
        lax.fori_loop(0, T // blk, body, (zero, zero), unroll=unroll)

    ch = 4 * blk
    lane_h = lax.broadcasted_iota(jnp.int32, (ch, LANES), 1) // dh
    for c in range(T // ch):
        a = acc_sc[c * ch:(c + 1) * ch, :]
        sq = a * a
        mean_sq = jnp.zeros((ch, LANES), F32)
        for e in range(hpt):
            se = jnp.sum(jnp.where(lane_h == e, sq, 0.0), axis=-1, keepdims=True) * (1.0 / dh)
            mean_sq = jnp.where(lane_h == e, jnp.broadcast_to(se, (ch, LANES)), mean_sq)
        o_ref[0, c * ch:(c + 1) * ch, :] = (a * lax.rsqrt(mean_sq + EPS) * g_ref[...]).astype(BF16)


def _dil_attn(q, k, v, g, *, n_heads, dh, unroll):
    B, T, W = q.shape
    blk = ATT_BLOCK
    seq = pl.BlockSpec((1, T, LANES), lambda b, t: (b, 0, t))
    return pl.pallas_call(
        functools.partial(_dil_attn_kernel, n_heads=n_heads, dh=dh, unroll=unroll),
        grid=(B, W // LANES),
        in_specs=[seq, seq, seq, pl.BlockSpec((1, LANES), lambda b, t: (0, t))],
        out_specs=seq,
        out_shape=jax.ShapeDtypeStruct((B, T, W), BF16),
        scratch_shapes=[pltpu.VMEM((T, LANES), F32)] * 3
        + [pltpu.VMEM((2, (LANES // dh) * blk, 2 * blk), F32)],
        compiler_params=_params(("parallel", "arbitrary")),
        name="dil_attn",
    )(q, k, v, g)


def _dil_attn_step_kernel(q_ref, kn_ref, vn_ref, kt_ref, vt_ref, g_ref, o_ref, cnt_sc, bias_sc,
                          *, n_heads, dh, n_q):
    past = kt_ref.shape[2]
    n_new = kn_ref.shape[1]
    W = q_ref.shape[-1]
    rows = n_q * n_heads
    n_cols = past + n_new

    @pl.when(pl.program_id(0) == 0)
    def _():
        r = lax.broadcasted_iota(jnp.int32, (rows, n_cols), 0)
        c = lax.broadcasted_iota(jnp.int32, (rows, n_cols), 1)
        delta = past + r // n_heads - c
        cnt = jnp.zeros((rows, n_cols), F32)
        for win, dil in DIL_PATTERNS:
            cnt = cnt + ((delta >= 0) & (delta <= win) & ((delta & (dil - 1)) == 0)).astype(F32)
        slope = jnp.exp2((-8.0 / n_heads) * ((r % n_heads) + 1).astype(F32))
        cnt_sc[...] = cnt
        bias_sc[...] = jnp.where(cnt > 0.0, -slope * delta.astype(F32), NEG)

    sub = lax.broadcasted_iota(jnp.int32, (n_heads, W), 0)
    lane_head = lax.broadcasted_iota(jnp.int32, (n_heads, W), 1) // dh
    qbd = jnp.concatenate(
        [jnp.where(sub == lane_head, jnp.broadcast_to(q_ref[0, i:i + 1, :], (n_heads, W)), 0.0) for i in range(n_q)],
        axis=0).astype(BF16)
    s = jnp.concatenate([_dot(qbd, kt_ref[0].astype(BF16)), _dot_nt(qbd, kn_ref[0])], axis=1) + bias_sc[...]
    p = jnp.exp(s - jnp.max(s, axis=-1, keepdims=True)) * cnt_sc[...]
    den = jnp.sum(p, axis=-1, keepdims=True)
    p = p.astype(BF16)
    out = (_dot_nt(p[:, :past], vt_ref[0].astype(BF16)) + _dot(p[:, past:], vn_ref[0])) / den
    r = lax.broadcasted_iota(jnp.int32, (rows, W), 0)
    c = lax.broadcasted_iota(jnp.int32, (rows, W), 1)
    out = jnp.where((r % n_heads) == (c // dh), out, 0.0)
    out = out * lax.rsqrt(jnp.sum(out * out, axis=-1, keepdims=True) * (1.0 / dh) + EPS)
    o_ref[0] = jnp.sum(out.reshape(n_q, n_heads, W), axis=1) * g_ref[...]


def _dil_attn_step(q, k_new_pad, v_new_pad, cache_kt, cache_vt, g, *, n_heads, dh):
    B, n_q, W = q.shape
    past = cache_kt.shape[2]
    n_new = k_new_pad.shape[1]
    bmap = lambda b: (b, 0, 0)
    return pl.pallas_call(
        functools.partial(_dil_attn_step_kernel, n_heads=n_heads, dh=dh, n_q=n_q),
        grid=(B,),
        in_specs=[pl.BlockSpec((1, n_q, W), bmap), pl.BlockSpec((1, n_new, W), bmap), pl.BlockSpec((1, n_new, W), bmap),
                  pl.BlockSpec((1, W, past), bmap), pl.BlockSpec((1, W, past), bmap), _const_spec((1, W))],
        out_specs=pl.BlockSpec((1, n_q, W), bmap),
        out_shape=jax.ShapeDtypeStruct((B, n_q, W), F32),
        scratch_shapes=[pltpu.VMEM((n_q * n_heads, past + n_new), F32)] * 2,
        compiler_params=_params(("arbitrary",)),
        name="dil_attn_step",
    )(q, k_new_pad, v_new_pad, cache_kt, cache_vt, g)


def _lower_bound(lbl_ref, layer):
    logits = lbl_ref[...]
    e = jnp.exp(logits - jnp.max(logits, axis=0, keepdims=True))
    return jnp.sum(e[:layer + 1], axis=0, keepdims=True) / jnp.sum(e, axis=0, keepdims=True)


def _hgrn_kernel(hz_ref, lbl_ref, gn_ref, o_ref, s_ref, st_ref, b_sc, k_sc, *, n_heads, dk, layer):
    i = pl.program_id(1)
    hw = n_heads * dk
    tc = hz_ref.shape[1]
    ch, sub = HG_CHUNK, HG_SUB

    @pl.when(i == 0)
    def _():
        st_ref[...] = jnp.zeros_like(st_ref)

    lb = _lower_bound(lbl_ref, layer)
    r_c = lax.broadcasted_iota(jnp.int32, (ch, ch), 0)
    c_c = lax.broadcasted_iota(jnp.int32, (ch, ch), 1)
    tril = (r_c >= c_c).astype(F32)
    r_s = lax.broadcasted_iota(jnp.int32, (sub, ch), 0)
    c_s = lax.broadcasted_iota(jnp.int32, (sub, ch), 1)

    for c in range(tc // ch):
        rows = slice(c * ch, (c + 1) * ch)
        hq = hz_ref[0, rows, 0:hw]
        f = lb + (1.0 - lb) * _sigmoid(hz_ref[0, rows, hw:2 * hw])
        qh = hq * _sigmoid(hq) * dk ** -0.5
        b_sc[...] = jnp.dot(tril, jnp.log(f), precision=HIGHEST, preferred_element_type=F32)
        k_sc[...] = 1.0 - f
        for h in range(n_heads):
            hs = slice(h * dk, (h + 1) * dk)
            bh = b_sc[:, hs]
            kh = k_sc[:, hs]
            qhh = qh[:, hs]
            vh = hz_ref[0, rows, 2 * hw + h * dk:2 * hw + (h + 1) * dk].astype(BF16)
            st = st_ref[h]
            o = _dot_nt((qhh * jnp.exp(bh)).astype(BF16), st.astype(BF16))
            blocks = []
            for j in range(ch // sub):
                r0 = j * sub
                bj = bh[r0:r0 + sub]
                qj = qhh[r0:r0 + sub]
                a_j = jnp.zeros((sub, ch), F32)
                for s in range(sub):
                    e = jnp.exp(bj - b_sc[r0 + s:r0 + s + 1, hs])
                    a = jnp.sum(qj * e * k_sc[r0 + s:r0 + s + 1, hs], axis=-1, keepdims=True)
                    a_j = jnp.where((c_s == r0 + s) & (r_s >= s), a, a_j)
                if j > 0:
                    beta = b_sc[r0 - 1:r0, hs]
                    qt = qj * jnp.exp(bj - beta)
                    kt = kh * jnp.exp(jnp.minimum(beta - bh, 0.0))
                    a_j = jnp.where(c_s < r0, _dot_nt(qt.astype(BF16), kt.astype(BF16)), a_j)
                blocks.append(a_j)
            a_mat = jnp.concatenate(blocks, axis=0)
            o = o + _dot(a_mat.astype(BF16), vh)
            b_last = b_sc[ch - 1:ch, hs]
            khat = (kh * jnp.exp(b_last - bh)).astype(BF16)
            st_ref[h] = st * jnp.exp(b_last) + lax.dot_general(vh, khat, TN_DIMS, preferred_element_type=F32)
            hg = hz_ref[0, rows, 3 * hw + h * dk:3 * hw + (h + 1) * dk]
            o_ref[0, rows, hs] = (_rms(o, gn_ref[:, hs]) * (hg * _sigmoid(hg))).astype(BF16)

    @pl.when(i == pl.num_programs(1) - 1)
    def _():
        for h in range(n_heads):
            s_ref[0, h] = st_ref[h].T


def _hgrn(hz, lb_logits, gn, *, n_heads, dk, tc, layer):
    B, T, _ = hz.shape
    hw = n_heads * dk
    row = lambda b, i: (b, i, 0)
    return pl.pallas_call(
        functools.partial(_hgrn_kernel, n_heads=n_heads, dk=dk, layer=layer),
        grid=(B, T // tc),
        in_specs=[pl.BlockSpec((1, tc, 4 * hw), row), _const_spec(lb_logits.shape), _const_spec((1, hw))],
        out_specs=[pl.BlockSpec((1, tc, hw), row), pl.BlockSpec((1, n_heads, dk, dk), lambda b, i: (b, 0, 0, 0))],
        out_shape=[jax.ShapeDtypeStruct((B, T, hw), BF16), jax.ShapeDtypeStruct((B, n_heads, dk, dk), F32)],
        scratch_shapes=[pltpu.VMEM((n_heads, dk, dk), F32), pltpu.VMEM((HG_CHUNK, hw), F32),
                        pltpu.VMEM((HG_CHUNK, hw), F32)],
        compiler_params=_params(("parallel", "arbitrary")),
        name="hgrn",
    )(hz, lb_logits, gn)


def _hgrn_step_kernel(hz_ref, s0_ref, lbl_ref, gn_ref, o_ref, s_ref, *, n_heads, dk, layer):
    hw = n_heads * dk
    n_t = hz_ref.shape[1]
    lb = _lower_bound(lbl_ref, layer)
    q, k, v, b, gate = [], [], [], [], []
    acc = jnp.zeros((1, hw), F32)
    for t in range(n_t):
        hq = hz_ref[0, t:t + 1, 0:hw]
        f = lb + (1.0 - lb) * _sigmoid(hz_ref[0, t:t + 1, hw:2 * hw])
        acc = acc + jnp.log(f)
        q.append(hq * _sigmoid(hq) * dk ** -0.5)
        k.append(1.0 - f)
        v.append(hz_ref[0, t:t + 1, 2 * hw:3 * hw])
        b.append(acc)
        hg = hz_ref[0, t:t + 1, 3 * hw:4 * hw]
        gate.append(hg * _sigmoid(hg))
    rr = lax.broadcasted_iota(jnp.int32, (dk, dk), 0)
    cc = lax.broadcasted_iota(jnp.int32, (dk, dk), 1)

    def tile_of_rows(rows):
        tile = jnp.zeros((dk, dk), F32)
        for t, r in enumerate(rows):
            tile = jnp.where(rr == t, jnp.broadcast_to(r, (dk, dk)), tile)
        return tile

    for h in range(n_heads):
        hs = slice(h * dk, (h + 1) * dk)
        s0 = s0_ref[0, h]
        qe = tile_of_rows([q[t][:, hs] * jnp.exp(b[t][:, hs]) for t in range(n_t)])
        inter = jnp.dot(qe, s0, precision=HIGHEST, preferred_element_type=F32)
        for t in range(n_t):
            o = inter[t:t + 1]
            for s in range(t + 1):
                a = jnp.sum(q[t][:, hs] * jnp.exp(b[t][:, hs] - b[s][:, hs]) * k[s][:, hs], axis=-1, keepdims=True)
                o = o + a * v[s][:, hs]
            o_ref[0, t:t + 1, hs] = _rms(o, gn_ref[:, hs]) * gate[t][:, hs]
        b_last = b[n_t - 1][:, hs]
        khat = tile_of_rows([k[t][:, hs] * jnp.exp(b_last - b[t][:, hs]) for t in range(n_t)])
        vpad = tile_of_rows([v[t][:, hs] for t in range(n_t)])
        decay = jnp.where(rr == cc, jnp.broadcast_to(jnp.exp(b_last), (dk, dk)), 0.0)
        s_ref[0, h] = (jnp.dot(decay, s0, precision=HIGHEST, preferred_element_type=F32)
                       + jnp.dot(khat.T, vpad, precision=HIGHEST, preferred_element_type=F32))


def _hgrn_step(hz, s0, lb_logits, gn, *, layer):
    B, n_t, _ = hz.shape
    _, n_heads, dk, dv = s0.shape
    hw = n_heads * dk
    return pl.pallas_call(
        functools.partial(_hgrn_step_kernel, n_heads=n_heads, dk=dk, layer=layer),
        grid=(B,),
        in_specs=[pl.BlockSpec((1, n_t, 4 * hw), lambda b: (b, 0, 0)),
                  pl.BlockSpec((1, n_heads, dk, dv), lambda b: (b, 0, 0, 0)),
                  _const_spec(lb_logits.shape), _const_spec((1, hw))],
        out_specs=[pl.BlockSpec((1, n_t, hw), lambda b: (b, 0, 0)),
                   pl.BlockSpec((1, n_heads, dk, dv), lambda b: (b, 0, 0, 0))],
        out_shape=[jax.ShapeDtypeStruct((B, n_t, hw), F32), jax.ShapeDtypeStruct((B, n_heads, dk, dv), F32)],
        compiler_params=_params(("parallel",)),
        name="hgrn_step",
    )(hz, s0, lb_logits, gn)


def _mix_out(x, att_bf, ohg_bf, wo_ref):
    aw = att_bf.shape[-1]
    return x + _dot(att_bf, wo_ref[0:aw, :]) + _dot(ohg_bf, wo_ref[aw:, :])


def _cross_attend(cq, head_k, head_v, n_heads, dh):
    outs = []
    for h in range(n_heads):
        s = _dot_nt((cq[:, h * dh:(h + 1) * dh] * dh ** -0.5).astype(BF16), head_k(h))
        p = jnp.exp(s - jnp.max(s, axis=-1, keepdims=True))
        den = jnp.sum(p, axis=-1, keepdims=True)
        outs.append(_dot(p.astype(BF16), head_v(h)) / den)
    return jnp.concatenate(outs, axis=-1)


def _ffn(u_bf, taps_fn, wg_ref, wu_ref, wd_ref, cw_ref, cb_ref, ff_chunk):
    dff = wg_ref.shape[1]
    n_taps = cw_ref.shape[0]
    acc = jnp.zeros((u_bf.shape[0], wd_ref.shape[1]), F32)
    for n in range(dff // ff_chunk):
        cs = slice(n * ff_chunk, (n + 1) * ff_chunk)
        ug = _dot(u_bf, wg_ref[:, cs])
        taps = taps_fn(ug, cs, n)
        conv = cb_ref[:, cs]
        for j in range(n_taps):
            conv = conv + cw_ref[j:j + 1, cs] * taps[j]
        act = conv * _sigmoid(conv) * _dot(u_bf, wu_ref[:, cs])
        acc = acc + _dot(act.astype(BF16), wd_ref[cs, :])
    return acc


def _post_kernel(x_ref, att_ref, ohg_ref, mk_ref, mv_ref, cinit_ref, gc_ref, gf_ref, gl_ref, wo_ref, wq_ref,
                 wc_ref, wg_ref, wu_ref, wd_ref, cw_ref, cb_ref, y_ref, cst_ref, buf_ref, carry_ref,
                 *, x_heads, x_dh, ff_chunk):
    i = pl.program_id(1)
    tm = x_ref.shape[1]
    pad = SUBLANES
    n_taps = cw_ref.shape[0]

    @pl.when(i == 0)
    def _():
        carry_ref[...] = cinit_ref[0]

    x1 = _mix_out(x_ref[0], att_ref[0], ohg_ref[0], wo_ref)
    cq = _dot(_rms(x1, gc_ref[...]).astype(BF16), wq_ref[...])
    co = _cross_attend(cq, lambda h: mk_ref[0, :, h * x_dh:(h + 1) * x_dh],
                       lambda h: mv_ref[0, :, h * x_dh:(h + 1) * x_dh], x_heads, x_dh)
    x2 = x1 + _dot(co.astype(BF16), wc_ref[...])

    def taps_fn(ug, cs, n):
        buf = buf_ref.at[n % 2]
        buf[0:pad, :] = carry_ref[:, cs]
        buf[pad:pad + tm, :] = ug
        carry_ref[:, cs] = ug[tm - pad:tm]
        return [buf[pad - (n_taps - 1 - j):pad - (n_taps - 1 - j) + tm, :] for j in range(n_taps - 1)] + [ug]

    u_bf = _rms(x2, gf_ref[...]).astype(BF16)
    x3 = x2 + _ffn(u_bf, taps_fn, wg_ref, wu_ref, wd_ref, cw_ref, cb_ref, ff_chunk)
    y_ref[0] = _rms(x3, gl_ref[...])
    cst_ref[0] = carry_ref[...]


def _post(x, att, ohg, mk_bf, mv_bf, cinit, gains, weights, conv_w, conv_b, *, tm, x_heads, ff_chunk):
    B, T, D = x.shape
    aw, hw = att.shape[-1], ohg.shape[-1]
    n_mem, xw = mk_bf.shape[1:]
    dff = conv_w.shape[1]
    row = lambda b, i: (b, i, 0)
    bat = lambda b, i: (b, 0, 0)
    in_specs = ([pl.BlockSpec((1, tm, D), row), pl.BlockSpec((1, tm, aw), row), pl.BlockSpec((1, tm, hw), row)]
                + [pl.BlockSpec((1, n_mem, xw), bat)] * 2 + [pl.BlockSpec((1, SUBLANES, dff), bat)]
                + [_const_spec(a.shape) for a in tuple(gains) + tuple(weights) + (conv_w, conv_b)])
    return pl.pallas_call(
        functools.partial(_post_kernel, x_heads=x_heads, x_dh=xw // x_heads, ff_chunk=ff_chunk),
        grid=(B, T // tm),
        in_specs=in_specs,
        out_specs=[pl.BlockSpec((1, tm, D), row), pl.BlockSpec((1, SUBLANES, dff), bat)],
        out_shape=[jax.ShapeDtypeStruct((B, T, D), F32), jax.ShapeDtypeStruct((B, SUBLANES, dff), F32)],
        scratch_shapes=[pltpu.VMEM((2, tm + SUBLANES, ff_chunk), F32), pltpu.VMEM((SUBLANES, dff), F32)],
        compiler_params=_params(("parallel", "arbitrary")),
        name="post_mixer",
    )(x, att, ohg, mk_bf, mv_bf, cinit, *gains, *weights, conv_w, conv_b)


def _step_mix_kernel(x_ref, att_ref, ohg_ref, gc_ref, wo_ref, wq_ref, x1_ref, cq_ref):
    x1 = _mix_out(x_ref[...], att_ref[...].astype(BF16), ohg_ref[...].astype(BF16), wo_ref)
    x1_ref[...] = x1
    cq_ref[...] = _dot(_rms(x1, gc_ref[...]).astype(BF16), wq_ref[...])


def _step_cross_kernel(cq_ref, mk_ref, mv_ref, co_ref, *, x_heads, x_dh):
    co_ref[0] = _cross_attend(cq_ref[0], lambda h: mk_ref[0, :, h, :].astype(BF16),
                              lambda h: mv_ref[0, :, h, :].astype(BF16), x_heads, x_dh)


def _step_ffn_kernel(x1_ref, co_ref, prev_ref, gf_ref, gl_ref, wc_ref, wg_ref, wu_ref, wd_ref,
                     cw_ref, cb_ref, y_ref, ug_ref, buf_ref, *, ff_chunk, n_t):
    n = x1_ref.shape[0]
    pad = SUBLANES
    n_taps = cw_ref.shape[0]
    x2 = x1_ref[...] + _dot(co_ref[...].astype(BF16), wc_ref[...])
    t_of_row = lax.broadcasted_iota(jnp.int32, (n, ff_chunk), 0) % n_t

    def taps_fn(ug, cs, _):
        ug_ref[:, cs] = ug
        buf_ref[0:pad, cs] = jnp.zeros((pad, ff_chunk), F32)
        buf_ref[pad:pad + n, cs] = ug
        taps = []
        for j in range(n_taps - 1):
            shift = n_taps - 1 - j
            shifted = buf_ref[pad - shift:pad - shift + n, cs]
            taps.append(jnp.where(t_of_row < shift, prev_ref[j, :, cs], shifted))
        return taps + [ug]

    u_bf = _rms(x2, gf_ref[...]).astype(BF16)
    x3 = x2 + _ffn(u_bf, taps_fn, wg_ref, wu_ref, wd_ref, cw_ref, cb_ref, ff_chunk)
    y_ref[...] = _rms(x3, gl_ref[...])


def _single_call(kernel, args, out_shape, name, scratch_shapes=()):
    return pl.pallas_call(
        kernel,
        grid=(1,),
        in_specs=[_const_spec(a.shape) for a in args],
        out_specs=[pl.BlockSpec(o.shape, lambda *_, nd=len(o.shape): (0,) * nd) for o in out_shape],
        out_shape=out_shape,
        scratch_shapes=list(scratch_shapes),
        compiler_params=_params(("arbitrary",)),
        name=name,
    )(*args)


def _step_cross(cq, mem_k, mem_v):
    B, n_q, xw = cq.shape
    _, n_mem, x_heads, x_dh = mem_k.shape
    bmap = lambda b: (b, 0, 0)
    mem = pl.BlockSpec((1, n_mem, x_heads, x_dh), lambda b: (b, 0, 0, 0))
    return pl.pallas_call(
        functools.partial(_step_cross_kernel, x_heads=x_heads, x_dh=x_dh),
        grid=(B,),
        in_specs=[pl.BlockSpec((1, n_q, xw), bmap), mem, mem],
        out_specs=pl.BlockSpec((1, n_q, xw), bmap),
        out_shape=jax.ShapeDtypeStruct((B, n_q, xw), F32),
        compiler_params=_params(("parallel",)),
        name="step_cross",
    )(cq, mem_k, mem_v)


def kernel(x_prompt, x_sample, cache_win_k, cache_win_v, state_hgrn, state_ffn_conv, cache_mem_k, cache_mem_v,
           mem_prompt, hg_lb_logits, norm_mix, w_in, att_out_norm, hg_out_norm, w_out, norm_cross, norm_mem,
           w_cq, w_ck, w_cv, w_co, norm_ffn, w_gate, w_up, conv_w, conv_b, w_down, norm_final):
    Bp, T, D = x_prompt.shape
    Bs, Ts, _ = x_sample.shape
    depth, _, past, att_h, att_dh = cache_win_k.shape
    _, _, hg_h, hg_dk, hg_dv = state_hgrn.shape
    _, _, n_mem, x_h, x_dh = cache_mem_k.shape
    n_taps, dff = conv_w.shape[1:]
    att_w = att_h * att_dh
    hg_w = hg_h * hg_dk
    xw = x_h * x_dh
    keep = min(max(w for w, _ in DIL_PATTERNS), T)
    assert depth == 1 and hg_dk == hg_dv == LANES and x_dh == LANES and LANES % att_dh == 0
    assert all(w // d == ATT_BLOCK for w, d in DIL_PATTERNS) and past >= max(w for w, _ in DIL_PATTERNS)
    d_max = max(d for _, d in DIL_PATTERNS)
    assert T % (ATT_BLOCK * d_max) == 0 and past % d_max == 0 and past % (4 * ATT_BLOCK) == 0
    assert n_taps - 1 <= min(Ts, SUBLANES) and Ts <= SUBLANES
    layer = 0
    ff_chunk = 2 * LANES
    q_scale = att_dh ** -0.5
    row2 = lambda a: a.reshape(1, -1)

    w_in_bf = w_in[layer].astype(BF16)
    w_ckv_bf = jnp.concatenate([w_ck[layer], w_cv[layer]], axis=1).astype(BF16)
    weights = tuple(w[layer].astype(BF16) for w in (w_out, w_cq, w_co, w_gate, w_up, w_down))
    g_mix, g_att, g_hg = row2(norm_mix[layer]), row2(att_out_norm[layer]), row2(hg_out_norm[layer])
    g_cross, g_mem, g_ffn, g_final = (row2(norm_cross[layer]), row2(norm_mem[layer]), row2(norm_ffn[layer]),
                                      row2(norm_final))
    cw, cb = conv_w[layer], row2(conv_b[layer])

    q, k, v, k_keep, v_keep, hz = _in_proj(x_prompt, g_mix, w_in_bf, att_w=att_w, q_scale=q_scale, tm=512, keep=keep)
    mk, mv, mk_bf, mv_bf = _mem_kv(mem_prompt.reshape(Bp * n_mem, D), g_mem, w_ckv_bf, tm=256)
    att = _dil_attn(q, k, v, g_att, n_heads=att_h, dh=att_dh, unroll=8)
    ohg, s_prompt = _hgrn(hz, hg_lb_logits, g_hg, n_heads=hg_h, dk=hg_dk, tc=256, layer=layer)
    y_prompt, cst = _post(
        x_prompt, att, ohg, mk_bf.reshape(Bp, n_mem, xw), mv_bf.reshape(Bp, n_mem, xw),
        jnp.zeros((Bp, SUBLANES, dff), F32), (g_cross, g_ffn, g_final), weights, cw, cb,
        tm=512, x_heads=x_h, ff_chunk=ff_chunk)

    n_s = Bs * Ts
    qs, ks, vs, ks_f, vs_f, hzs = _in_proj(x_sample.reshape(1, n_s, D), g_mix, w_in_bf, att_w=att_w, q_scale=q_scale,
                                           tm=n_s, keep=n_s)
    pad_new = lambda a: jnp.pad(a.reshape(Bs, Ts, att_w).astype(BF16), ((0, 0), (0, LANES - Ts), (0, 0)))
    feature_major = lambda c: jnp.transpose(c, (0, 2, 3, 1)).reshape(Bs, att_w, past)
    att_s = _dil_attn_step(qs.reshape(Bs, Ts, att_w), pad_new(ks), pad_new(vs), feature_major(cache_win_k[layer]),
                           feature_major(cache_win_v[layer]), g_att, n_heads=att_h, dh=att_dh)
    ohg_s, s_sample = _hgrn_step(hzs.reshape(Bs, Ts, 4 * hg_w), state_hgrn[layer], hg_lb_logits, g_hg, layer=layer)
    x1_s, cq_s = _single_call(
        _step_mix_kernel,
        (x_sample.reshape(n_s, D), att_s.reshape(n_s, att_w), ohg_s.reshape(n_s, hg_w), g_cross, weights[0],
         weights[1]),
        [jax.ShapeDtypeStruct((n_s, D), F32), jax.ShapeDtypeStruct((n_s, xw), F32)], "step_mix")
    cq_pad = jnp.pad(cq_s.reshape(Bs, Ts, xw), ((0, 0), (0, SUBLANES - Ts), (0, 0)))
    co_s = _step_cross(cq_pad, cache_mem_k[layer], cache_mem_v[layer])[:, :Ts]
    conv_state = state_ffn_conv[layer]
    t_idx = jnp.arange(Ts)
    prev = jnp.stack([jnp.take(conv_state, jnp.clip(j + t_idx, 0, n_taps - 2), axis=1).reshape(n_s, dff)
                      for j in range(n_taps - 1)])
    y_s, ug_s = _single_call(
        functools.partial(_step_ffn_kernel, ff_chunk=ff_chunk, n_t=Ts),
        (x1_s, co_s.reshape(n_s, xw), prev, g_ffn, g_final, weights[2], weights[3], weights[4], weights[5], cw, cb),
        [jax.ShapeDtypeStruct((n_s, D), F32), jax.ShapeDtypeStruct((n_s, dff), F32)], "step_ffn",
        scratch_shapes=[pltpu.VMEM((n_s + SUBLANES, dff), F32)])

    stack = lambda a: a[None]
    return (y_prompt, y_s.reshape(Bs, Ts, D),
            stack(k_keep.reshape(Bp, keep, att_h, att_dh)), stack(v_keep.reshape(Bp, keep, att_h, att_dh)),
            stack(s_prompt), stack(cst[:, SUBLANES - (n_taps - 1):]),
            stack(mk.reshape(Bp, n_mem, x_h, x_dh)), stack(mv.reshape(Bp, n_mem, x_h, x_dh)),
            stack(ks_f.reshape(Bs, Ts, att_h, att_dh)), stack(vs_f.reshape(Bs, Ts, att_h, att_dh)),
            stack(s_sample), stack(ug_s.reshape(Bs, Ts, dff)[:, Ts - (n_taps - 1):]))
```

```python
import functools

import jax
import jax.numpy as jnp
from jax import lax
from jax.experimental import pallas as pl
from jax.experimental.pallas import tpu as pltpu

F32 = jnp.float32
BF16 = jnp.bfloat16
EPS = 1e-6
DIL_PATTERNS = ((128, 1), (512, 4), (2048, 16))
ATT_BLOCK = 128
LANES = 128
SUBLANES = 8
HG_CHUNK = 64
HG_SUB = 8
NEG = -1e30
LOG2E = 1.4426950408889634
HIGHEST = lax.Precision.HIGHEST
NT_DIMS = (((1,), (1,)), ((), ()))
TN_DIMS = (((0,), (0,)), ((), ()))
VMEM_LIMIT = 56 * 1024 * 1024


def _dot(a, b):
    return jnp.dot(a, b, preferred_element_type=F32)


def _dot_nt(a, b):
    return lax.dot_general(a, b, NT_DIMS, preferred_element_type=F32)


def _sigmoid(x):
    return 1.0 / (1.0 + jnp.exp(-x))


def _rms(x, g):
    return x * lax.rsqrt(jnp.mean(x * x, axis=-1, keepdims=True) + EPS) * g


def _const_spec(shape):
    nd = len(shape)
    return pl.BlockSpec(shape, lambda *_: (0,) * nd, pipeline_mode=pl.Buffered(1))


def _params(sem):
    return pltpu.CompilerParams(dimension_semantics=sem, vmem_limit_bytes=VMEM_LIMIT)


def _in_proj_kernel(x_ref, g_ref, w_ref, q_ref, k_ref, v_ref, kf_ref, vf_ref, hz_ref, *, att_w, q_scale):
    h = _rms(x_ref[0], g_ref[...]).astype(BF16)

    def proj(lo, hi):
        return _dot(h, w_ref[:, lo:hi])

    q_ref[0] = proj(0, att_w) * q_scale
    zk = proj(att_w, 2 * att_w)
    k_ref[0] = zk
    kf_ref[0] = zk
    zv = proj(2 * att_w, 3 * att_w)
    v_ref[0] = zv
    vf_ref[0] = zv
    base = 3 * att_w
    step = 4 * LANES
    for c in range(0, hz_ref.shape[-1], step):
        hz_ref[0, :, c:c + step] = proj(base + c, base + c + step)


def _in_proj(x, g, w_bf, *, att_w, q_scale, tm, keep):
    B, T, D = x.shape
    ncols = w_bf.shape[1]
    hzw = ncols - 3 * att_w
    n_t = T // tm
    first = n_t - keep // tm
    row = lambda b, i: (b, i, 0)
    keep_map = lambda b, i: (b, jnp.maximum(i - first, 0), 0)
    return pl.pallas_call(
        functools.partial(_in_proj_kernel, att_w=att_w, q_scale=q_scale),
        grid=(B, n_t),
        in_specs=[pl.BlockSpec((1, tm, D), row), _const_spec((1, D)), _const_spec((D, ncols))],
        out_specs=[pl.BlockSpec((1, tm, att_w), row)] * 3
        + [pl.BlockSpec((1, tm, att_w), keep_map)] * 2
        + [pl.BlockSpec((1, tm, hzw), row)],
        out_shape=[jax.ShapeDtypeStruct((B, T, att_w), F32)] * 3
        + [jax.ShapeDtypeStruct((B, keep, att_w), F32)] * 2
        + [jax.ShapeDtypeStruct((B, T, hzw), F32)],
        compiler_params=_params(("parallel", "arbitrary")),
        name="in_proj",
    )(x, g, w_bf)


def _mem_kv_kernel(m_ref, g_ref, w_ref, kf_ref, vf_ref, kb_ref, vb_ref):
    h = _rms(m_ref[...], g_ref[...]).astype(BF16)
    xw = w_ref.shape[1] // 2
    k = _dot(h, w_ref[:, :xw])
    v = _dot(h, w_ref[:, xw:])
    kf_ref[...] = k
    vf_ref[...] = v
    kb_ref[...] = k.astype(BF16)
    vb_ref[...] = v.astype(BF16)


def _mem_kv(mem2d, g, w_ckv_bf, *, tm):
    n, D = mem2d.shape
    xw = w_ckv_bf.shape[1] // 2
    row = lambda i: (i, 0)
    return pl.pallas_call(
        _mem_kv_kernel,
        grid=(n // tm,),
        in_specs=[pl.BlockSpec((tm, D), row), _const_spec((1, D)), _const_spec((D, 2 * xw))],
        out_specs=[pl.BlockSpec((tm, xw), row)] * 4,
        out_shape=[jax.ShapeDtypeStruct((n, xw), F32)] * 2 + [jax.ShapeDtypeStruct((n, xw), BF16)] * 2,
        compiler_params=_params(("parallel",)),
        name="mem_kv",
    )(mem2d, g, w_ckv_bf)


def _dil_attn_kernel(q_ref, k_ref, v_ref, g_ref, o_ref, acc_sc, m_sc, den_sc, bias_sc, *, n_heads, dh, unroll):
    tile = pl.program_id(1)
    T = q_ref.shape[1]
    blk = ATT_BLOCK
    hpt = LANES // dh
    rows, ncol = hpt * blk, 2 * blk
    rr = lax.broadcasted_iota(jnp.int32, (rows, ncol), 0)
    cc = lax.broadcasted_iota(jnp.int32, (rows, ncol), 1)
    delta = (rr % blk) + blk - cc
    in_band = (delta >= 0) & (delta <= blk)
    slope = jnp.exp2((-8.0 / n_heads) * (tile * hpt + rr // blk + 1).astype(F32))
    lane_head = lax.broadcasted_iota(jnp.int32, (blk, LANES), 1) // dh
    n_pat = len(DIL_PATTERNS)

    def per_lane(x):
        out = jnp.broadcast_to(x[0:blk], (blk, LANES))
        for e in range(1, hpt):
            out = jnp.where(lane_head == e, jnp.broadcast_to(x[e * blk:(e + 1) * blk], (blk, LANES)), out)
        return out

    ones = jnp.ones((ncol, LANES), BF16)
    order = sorted(range(n_pat), key=lambda p: -DIL_PATTERNS[p][1])
    for step, p in enumerate(order):
        dil = DIL_PATTERNS[p][1]
        alibi = (-LOG2E) * slope * (delta * dil).astype(F32)
        bias_sc[0] = jnp.where(in_band & (cc >= blk), alibi, NEG)
        bias_sc[1] = jnp.where(in_band, alibi, NEG)
        nblk = T // (blk * dil)

        def body(n, carry, first=step == 0, last=step == n_pat - 1, dil=dil, nblk=nblk):
            kp, vp = carry
            i = n % nblk
            start = n // nblk + i * (blk * dil)
            idx = pl.ds(start, blk, stride=dil) if dil > 1 else pl.ds(pl.multiple_of(start, blk), blk)
            q2 = q_ref[0, idx, :] * LOG2E
            kc = k_ref[0, idx, :].astype(BF16)
            vc = v_ref[0, idx, :].astype(BF16)
            qs = jnp.concatenate([jnp.where(lane_head == e, q2, 0.0) for e in range(hpt)], axis=0).astype(BF16)
            s = _dot_nt(qs, jnp.concatenate([kp, kc], axis=0)) + bias_sc[jnp.minimum(i, 1)]
            m = jnp.max(s, axis=-1, keepdims=True)
            pr = jnp.exp2(s - m).astype(BF16)
            pv = _dot(pr, jnp.concatenate([jnp.concatenate([vp, vc], axis=0), ones], axis=1))
            m_t, acc_t, den_t = per_lane(m), per_lane(pv[:, :LANES]), per_lane(pv[:, LANES:])
            if not first:
                m_old = m_sc[idx, :]
                m_new = jnp.maximum(m_old, m_t)
                a_old, a_t = jnp.exp2(m_old - m_new), jnp.exp2(m_t - m_new)
                den_t = a_old * den_sc[idx, :] + a_t * den_t
                acc_t = a_old * acc_sc[idx, :] + a_t * acc_t
                m_t = m_new
            if last:
                acc_sc[idx, :] = acc_t / den_t
            else:
                m_sc[idx, :] = m_t
                den_sc[idx, :] = den_t
                acc_sc[idx, :] = acc_t
            return kc, vc

        zero = jnp.zeros((blk, LANES), BF16)
        lax.fori_loop(0, T // blk, body, (zero, zero), unroll=unroll)

    ch = 4 * blk
    lane_h = lax.broadcasted_iota(jnp.int32, (ch, LANES), 1) // dh
    for c in range(T // ch):
        a = acc_sc[c * ch:(c + 1) * ch, :]
        sq = a * a
        mean_sq = jnp.zeros((ch, LANES), F32)
        for e in range(hpt):
            se = jnp.sum(jnp.where(lane_h == e, sq, 0.0), axis=-1, keepdims=True) * (1.0 / dh)
            mean_sq = jnp.where(lane_h == e, jnp.broadcast_to(se, (ch, LANES)), mean_sq)
        o_ref[0, c * ch:(c + 1) * ch, :] = (a * lax.rsqrt(mean_sq + EPS) * g_ref[...]).astype(BF16)


def _dil_attn(q, k, v, g, *, n_heads, dh, unroll):
    B, T, W = q.shape
    blk = ATT_BLOCK
    seq = pl.BlockSpec((1, T, LANES), lambda b, t: (b, 0, t))
    return pl.pallas_call(
        functools.partial(_dil_attn_kernel, n_heads=n_heads, dh=dh, unroll=unroll),
        grid=(B, W // LANES),
        in_specs=[seq, seq, seq, pl.BlockSpec((1, LANES), lambda b, t: (0, t))],
        out_specs=seq,
        out_shape=jax.ShapeDtypeStruct((B, T, W), BF16),
        scratch_shapes=[pltpu.VMEM((T, LANES), F32)] * 3
        + [pltpu.VMEM((2, (LANES // dh) * blk, 2 * blk), F32)],
        compiler_params=_params(("parallel", "arbitrary")),
        name="dil_attn",
    )(q, k, v, g)


def _dil_attn_step_kernel(q_ref, kn_ref, vn_ref, kt_ref, vt_ref, g_ref, o_ref, cnt_sc, bias_sc,
                          *, n_heads, dh, n_q):
    past = kt_ref.shape[2]
    n_new = kn_ref.shape[1]
    W = q_ref.shape[-1]
    rows = n_q * n_heads
    n_cols = past + n_new

    @pl.when(pl.program_id(0) == 0)
    def _():
        r = lax.broadcasted_iota(jnp.int32, (rows, n_cols), 0)
        c = lax.broadcasted_iota(jnp.int32, (rows, n_cols), 1)
        delta = past + r // n_heads - c
        cnt = jnp.zeros((rows, n_cols), F32)
        for win, dil in DIL_PATTERNS:
            cnt = cnt + ((delta >= 0) & (delta <= win) & ((delta & (dil - 1)) == 0)).astype(F32)
        slope = jnp.exp2((-8.0 / n_heads) * ((r % n_heads) + 1).astype(F32))
        cnt_sc[...] = cnt
        bias_sc[...] = jnp.where(cnt > 0.0, -slope * delta.astype(F32), NEG)

    sub = lax.broadcasted_iota(jnp.int32, (n_heads, W), 0)
    lane_head = lax.broadcasted_iota(jnp.int32, (n_heads, W), 1) // dh
    qbd = jnp.concatenate(
        [jnp.where(sub == lane_head, jnp.broadcast_to(q_ref[0, i:i + 1, :], (n_heads, W)), 0.0) for i in range(n_q)],
        axis=0).astype(BF16)
    s = jnp.concatenate([_dot(qbd, kt_ref[0].astype(BF16)), _dot_nt(qbd, kn_ref[0])], axis=1) + bias_sc[...]
    p = jnp.exp(s - jnp.max(s, axis=-1, keepdims=True)) * cnt_sc[...]
    den = jnp.sum(p, axis=-1, keepdims=True)
    p = p.astype(BF16)
    out = (_dot_nt(p[:, :past], vt_ref[0].astype(BF16)) + _dot(p[:, past:], vn_ref[0])) / den
    r = lax.broadcasted_iota(jnp.int32, (rows, W), 0)
    c = lax.broadcasted_iota(jnp.int32, (rows, W), 1)
    out = jnp.where((r % n_heads) == (c // dh), out, 0.0)
    out = out * lax.rsqrt(jnp.sum(out * out, axis=-1, keepdims=True) * (1.0 / dh) + EPS)
    o_ref[0] = jnp.sum(out.reshape(n_q, n_heads, W), axis=1) * g_ref[...]


def _dil_attn_step(q, k_new_pad, v_new_pad, cache_kt, cache_vt, g, *, n_heads, dh):
    B, n_q, W = q.shape
    past = cache_kt.shape[2]
    n_new = k_new_pad.shape[1]
    bmap = lambda b: (b, 0, 0)
    return pl.pallas_call(
        functools.partial(_dil_attn_step_kernel, n_heads=n_heads, dh=dh, n_q=n_q),
        grid=(B,),
        in_specs=[pl.BlockSpec((1, n_q, W), bmap), pl.BlockSpec((1, n_new, W), bmap), pl.BlockSpec((1, n_new, W), bmap),
                  pl.BlockSpec((1, W, past), bmap), pl.BlockSpec((1, W, past), bmap), _const_spec((1, W))],
        out_specs=pl.BlockSpec((1, n_q, W), bmap),
        out_shape=jax.ShapeDtypeStruct((B, n_q, W), F32),
        scratch_shapes=[pltpu.VMEM((n_q * n_heads, past + n_new), F32)] * 2,
        compiler_params=_params(("arbitrary",)),
        name="dil_attn_step",
    )(q, k_new_pad, v_new_pad, cache_kt, cache_vt, g)


def _lower_bound(lbl_ref, layer):
    logits = lbl_ref[...]
    e = jnp.exp(logits - jnp.max(logits, axis=0, keepdims=True))
    return jnp.sum(e[:layer + 1], axis=0, keepdims=True) / jnp.sum(e, axis=0, keepdims=True)


def _cumsum_rows(tril_bf, x):
    x1 = x.astype(BF16)
    r1 = x - x1.astype(F32)
    x2 = r1.astype(BF16)
    x3 = (r1 - x2.astype(F32)).astype(BF16)
    return _dot(tril_bf, x1) + _dot(tril_bf, x2) + _dot(tril_bf, x3)


def _hgrn_kernel(hz_ref, lbl_ref, gn_ref, o_ref, s_ref, st_ref, b_sc, c_sc, *, n_heads, dk, layer):
    i = pl.program_id(1)
    hw = n_heads * dk
    tc = hz_ref.shape[1]
    ch, sub = HG_CHUNK, HG_SUB

    @pl.when(i == 0)
    def _():
        st_ref[...] = jnp.zeros_like(st_ref)

    lb = _lower_bound(lbl_ref, layer)
    r_c = lax.broadcasted_iota(jnp.int32, (ch, ch), 0)
    c_c = lax.broadcasted_iota(jnp.int32, (ch, ch), 1)
    tril = (r_c >= c_c).astype(BF16)
    r_s = lax.broadcasted_iota(jnp.int32, (sub, ch), 0)
    c_s = lax.broadcasted_iota(jnp.int32, (sub, ch), 1)

    for c in range(tc // ch):
        rows = slice(c * ch, (c + 1) * ch)
        hq = hz_ref[0, rows, 0:hw]
        f = lb + (1.0 - lb) * _sigmoid(hz_ref[0, rows, hw:2 * hw])
        qh = hq * _sigmoid(hq) * dk ** -0.5
        kk = 1.0 - f
        b = _cumsum_rows(tril, jnp.log(f) * LOG2E)
        b_rows, c_rows = b_sc.at[c], c_sc.at[c]
        b_rows[...] = b
        c_rows[...] = b - jnp.log(kk) * LOG2E
        for h in range(n_heads):
            hs = slice(h * dk, (h + 1) * dk)
            bh = b[:, hs]
            kh = kk[:, hs]
            qhh = qh[:, hs]
            vh = hz_ref[0, rows, 2 * hw + h * dk:2 * hw + (h + 1) * dk].astype(BF16)
            st = st_ref[h]
            o = _dot_nt((qhh * jnp.exp2(bh)).astype(BF16), st.astype(BF16))
            blocks = []
            for j in range(ch // sub):
                r0 = j * sub
                bj = bh[r0:r0 + sub]
                qj = qhh[r0:r0 + sub]
                a_j = jnp.zeros((sub, ch), F32)
                for s in range(sub):
                    a = jnp.sum(qj * jnp.exp2(bj - c_rows[r0 + s:r0 + s + 1, hs]), axis=-1, keepdims=True)
                    a_j = jnp.where(c_s == r0 + s, a, a_j)
                a_j = jnp.where(c_s - r0 <= r_s, a_j, 0.0)
                if j > 0:
                    beta = b_rows[r0 - 1:r0, hs]
                    qt = qj * jnp.exp2(bj - beta)
                    kt = kh[0:r0] * jnp.exp2(beta - bh[0:r0])
                    kt = jnp.concatenate([kt, jnp.zeros((ch - r0, dk), F32)], axis=0)
                    a_j = a_j + _dot_nt(qt.astype(BF16), kt.astype(BF16))
                blocks.append(a_j)
            a_mat = jnp.concatenate(blocks, axis=0)
            o = o + _dot(a_mat.astype(BF16), vh)
            b_last = b_rows[ch - 1:ch, hs]
            khat = (kh * jnp.exp2(b_last - bh)).astype(BF16)
            st_ref[h] = st * jnp.exp2(b_last) + lax.dot_general(vh, khat, TN_DIMS, preferred_element_type=F32)
            hg = hz_ref[0, rows, 3 * hw + h * dk:3 * hw + (h + 1) * dk]
            o_ref[0, rows, hs] = (_rms(o, gn_ref[:, hs]) * (hg * _sigmoid(hg))).astype(BF16)

    @pl.when(i == pl.num_programs(1) - 1)
    def _():
        for h in range(n_heads):
            s_ref[0, h] = st_ref[h].T


def _hgrn(hz, lb_logits, gn, *, n_heads, dk, tc, layer):
    B, T, _ = hz.shape
    hw = n_heads * dk
    row = lambda b, i: (b, i, 0)
    return pl.pallas_call(
        functools.partial(_hgrn_kernel, n_heads=n_heads, dk=dk, layer=layer),
        grid=(B, T // tc),
        in_specs=[pl.BlockSpec((1, tc, 4 * hw), row), _const_spec(lb_logits.shape), _const_spec((1, hw))],
        out_specs=[pl.BlockSpec((1, tc, hw), row), pl.BlockSpec((1, n_heads, dk, dk), lambda b, i: (b, 0, 0, 0))],
        out_shape=[jax.ShapeDtypeStruct((B, T, hw), BF16), jax.ShapeDtypeStruct((B, n_heads, dk, dk), F32)],
        scratch_shapes=[pltpu.VMEM((n_heads, dk, dk), F32)] + [pltpu.VMEM((tc // HG_CHUNK, HG_CHUNK, hw), F32)] * 2,
        compiler_params=_params(("parallel", "arbitrary")),
        name="hgrn",
    )(hz, lb_logits, gn)


def _hgrn_step_kernel(hz_ref, s0_ref, lbl_ref, gn_ref, o_ref, s_ref, *, n_heads, dk, layer):
    hw = n_heads * dk
    n_t = hz_ref.shape[1]
    lb = _lower_bound(lbl_ref, layer)
    q, k, v, b, gate = [], [], [], [], []
    acc = jnp.zeros((1, hw), F32)
    for t in range(n_t):
        hq = hz_ref[0, t:t + 1, 0:hw]
        f = lb + (1.0 - lb) * _sigmoid(hz_ref[0, t:t + 1, hw:2 * hw])
        acc = acc + jnp.log(f)
        q.append(hq * _sigmoid(hq) * dk ** -0.5)
        k.append(1.0 - f)
        v.append(hz_ref[0, t:t + 1, 2 * hw:3 * hw])
        b.append(acc)
        hg = hz_ref[0, t:t + 1, 3 * hw:4 * hw]
        gate.append(hg * _sigmoid(hg))
    rr = lax.broadcasted_iota(jnp.int32, (dk, dk), 0)
    cc = lax.broadcasted_iota(jnp.int32, (dk, dk), 1)

    def tile_of_rows(rows):
        tile = jnp.zeros((dk, dk), F32)
        for t, r in enumerate(rows):
            tile = jnp.where(rr == t, jnp.broadcast_to(r, (dk, dk)), tile)
        return tile

    for h in range(n_heads):
        hs = slice(h * dk, (h + 1) * dk)
        s0 = s0_ref[0, h]
        qe = tile_of_rows([q[t][:, hs] * jnp.exp(b[t][:, hs]) for t in range(n_t)])
        inter = jnp.dot(qe, s0, precision=HIGHEST, preferred_element_type=F32)
        for t in range(n_t):
            o = inter[t:t + 1]
            for s in range(t + 1):
                a = jnp.sum(q[t][:, hs] * jnp.exp(b[t][:, hs] - b[s][:, hs]) * k[s][:, hs], axis=-1, keepdims=True)
                o = o + a * v[s][:, hs]
            o_ref[0, t:t + 1, hs] = _rms(o, gn_ref[:, hs]) * gate[t][:, hs]
        b_last = b[n_t - 1][:, hs]
        khat = tile_of_rows([k[t][:, hs] * jnp.exp(b_last - b[t][:, hs]) for t in range(n_t)])
        vpad = tile_of_rows([v[t][:, hs] for t in range(n_t)])
        decay = jnp.where(rr == cc, jnp.broadcast_to(jnp.exp(b_last), (dk, dk)), 0.0)
        s_ref[0, h] = (jnp.dot(decay, s0, precision=HIGHEST, preferred_element_type=F32)
                       + jnp.dot(khat.T, vpad, precision=HIGHEST, preferred_element_type=F32))


def _hgrn_step(hz, s0, lb_logits, gn, *, layer):
    B, n_t, _ = hz.shape
    _, n_heads, dk, dv = s0.shape
    hw = n_heads * dk
    return pl.pallas_call(
        functools.partial(_hgrn_step_kernel, n_heads=n_heads, dk=dk, layer=layer),
        grid=(B,),
        in_specs=[pl.BlockSpec((1, n_t, 4 * hw), lambda b: (b, 0, 0)),
                  pl.BlockSpec((1, n_heads, dk, dv), lambda b: (b, 0, 0, 0)),
                  _const_spec(lb_logits.shape), _const_spec((1, hw))],
        out_specs=[pl.BlockSpec((1, n_t, hw), lambda b: (b, 0, 0)),
                   pl.BlockSpec((1, n_heads, dk, dv), lambda b: (b, 0, 0, 0))],
        out_shape=[jax.ShapeDtypeStruct((B, n_t, hw), F32), jax.ShapeDtypeStruct((B, n_heads, dk, dv), F32)],
        compiler_params=_params(("parallel",)),
        name="hgrn_step",
    )(hz, s0, lb_logits, gn)


def _mix_out(x, att_bf, ohg_bf, wo_ref):
    aw = att_bf.shape[-1]
    return x + _dot(att_bf, wo_ref[0:aw, :]) + _dot(ohg_bf, wo_ref[aw:, :])


def _cross_attend(cq, head_k, head_v, n_heads, dh):
    outs = []
    for h in range(n_heads):
        s = _dot_nt((cq[:, h * dh:(h + 1) * dh] * dh ** -0.5).astype(BF16), head_k(h))
        p = jnp.exp(s - jnp.max(s, axis=-1, keepdims=True))
        den = jnp.sum(p, axis=-1, keepdims=True)
        outs.append(_dot(p.astype(BF16), head_v(h)) / den)
    return jnp.concatenate(outs, axis=-1)


def _ffn(u_bf, taps_fn, wg_ref, wu_ref, wd_ref, cw_ref, cb_ref, ff_chunk):
    dff = wg_ref.shape[1]
    n_taps = cw_ref.shape[0]
    acc = jnp.zeros((u_bf.shape[0], wd_ref.shape[1]), F32)
    for n in range(dff // ff_chunk):
        cs = slice(n * ff_chunk, (n + 1) * ff_chunk)
        ug = _dot(u_bf, wg_ref[:, cs])
        taps = taps_fn(ug, cs, n)
        conv = cb_ref[:, cs]
        for j in range(n_taps):
            conv = conv + cw_ref[j:j + 1, cs] * taps[j]
        act = conv * _sigmoid(conv) * _dot(u_bf, wu_ref[:, cs])
        acc = acc + _dot(act.astype(BF16), wd_ref[cs, :])
    return acc


def _post_kernel(x_ref, att_ref, ohg_ref, mk_ref, mv_ref, cinit_ref, gc_ref, gf_ref, gl_ref, wo_ref, wq_ref,
                 wc_ref, wg_ref, wu_ref, wd_ref, cw_ref, cb_ref, y_ref, cst_ref, buf_ref, carry_ref,
                 *, x_heads, x_dh, ff_chunk, n_sub):
    i = pl.program_id(1)
    tm = x_ref.shape[1]
    pad = SUBLANES
    n_taps = cw_ref.shape[0]

    @pl.when(i == 0)
    def _():
        carry_ref[...] = cinit_ref[0]

    ts = tm // n_sub
    for t in range(n_sub):
        rows = slice(t * ts, (t + 1) * ts)
        x1 = _mix_out(x_ref[0, rows], att_ref[0, rows], ohg_ref[0, rows], wo_ref)
        cq = _dot(_rms(x1, gc_ref[...]).astype(BF16), wq_ref[...])
        co = _cross_attend(cq, lambda h: mk_ref[0, :, h * x_dh:(h + 1) * x_dh],
                           lambda h: mv_ref[0, :, h * x_dh:(h + 1) * x_dh], x_heads, x_dh)
        x2 = x1 + _dot(co.astype(BF16), wc_ref[...])

        def taps_fn(ug, cs, n, t=t):
            buf = buf_ref.at[t, n % 2]
            buf[0:pad, :] = carry_ref[:, cs]
            buf[pad:pad + ts, :] = ug
            carry_ref[:, cs] = ug[ts - pad:ts]
            return [buf[pad - (n_taps - 1 - j):pad - (n_taps - 1 - j) + ts, :] for j in range(n_taps - 1)] + [ug]

        u_bf = _rms(x2, gf_ref[...]).astype(BF16)
        x3 = x2 + _ffn(u_bf, taps_fn, wg_ref, wu_ref, wd_ref, cw_ref, cb_ref, ff_chunk)
        y_ref[0, rows] = _rms(x3, gl_ref[...])
    cst_ref[0] = carry_ref[...]


def _post(x, att, ohg, mk_bf, mv_bf, cinit, gains, weights, conv_w, conv_b, *, tm, n_sub, x_heads, ff_chunk):
    B, T, D = x.shape
    aw, hw = att.shape[-1], ohg.shape[-1]
    n_mem, xw = mk_bf.shape[1:]
    dff = conv_w.shape[1]
    row = lambda b, i: (b, i, 0)
    bat = lambda b, i: (b, 0, 0)
    in_specs = ([pl.BlockSpec((1, tm, D), row), pl.BlockSpec((1, tm, aw), row), pl.BlockSpec((1, tm, hw), row)]
                + [pl.BlockSpec((1, n_mem, xw), bat)] * 2 + [pl.BlockSpec((1, SUBLANES, dff), bat)]
                + [_const_spec(a.shape) for a in tuple(gains) + tuple(weights) + (conv_w, conv_b)])
    return pl.pallas_call(
        functools.partial(_post_kernel, x_heads=x_heads, x_dh=xw // x_heads, ff_chunk=ff_chunk, n_sub=n_sub),
        grid=(B, T // tm),
        in_specs=in_specs,
        out_specs=[pl.BlockSpec((1, tm, D), row), pl.BlockSpec((1, SUBLANES, dff), bat)],
        out_shape=[jax.ShapeDtypeStruct((B, T, D), F32), jax.ShapeDtypeStruct((B, SUBLANES, dff), F32)],
        scratch_shapes=[pltpu.VMEM((n_sub, 2, tm // n_sub + SUBLANES, ff_chunk), F32),
                        pltpu.VMEM((SUBLANES, dff), F32)],
        compiler_params=_params(("parallel", "arbitrary")),
        name="post_mixer",
    )(x, att, ohg, mk_bf, mv_bf, cinit, *gains, *weights, conv_w, conv_b)


def _step_mix_kernel(x_ref, att_ref, ohg_ref, gc_ref, wo_ref, wq_ref, x1_ref, cq_ref):
    x1 = _mix_out(x_ref[...], att_ref[...].astype(BF16), ohg_ref[...].astype(BF16), wo_ref)
    x1_ref[...] = x1
    cq_ref[...] = _dot(_rms(x1, gc_ref[...]).astype(BF16), wq_ref[...])


def _step_cross_kernel(cq_ref, mk_ref, mv_ref, co_ref, *, x_heads, x_dh):
    co_ref[0] = _cross_attend(cq_ref[0], lambda h: mk_ref[0, :, h, :].astype(BF16),
                              lambda h: mv_ref[0, :, h, :].astype(BF16), x_heads, x_dh)


def _step_ffn_kernel(x1_ref, co_ref, prev_ref, gf_ref, gl_ref, wc_ref, wg_ref, wu_ref, wd_ref,
                     cw_ref, cb_ref, y_ref, ug_ref, buf_ref, *, ff_chunk, n_t):
    n = x1_ref.shape[0]
    pad = SUBLANES
    n_taps = cw_ref.shape[0]
    x2 = x1_ref[...] + _dot(co_ref[...].astype(BF16), wc_ref[...])
    t_of_row = lax.broadcasted_iota(jnp.int32, (n, ff_chunk), 0) % n_t

    def taps_fn(ug, cs, _):
        ug_ref[:, cs] = ug
        buf_ref[0:pad, cs] = jnp.zeros((pad, ff_chunk), F32)
        buf_ref[pad:pad + n, cs] = ug
        taps = []
        for j in range(n_taps - 1):
            shift = n_taps - 1 - j
            shifted = buf_ref[pad - shift:pad - shift + n, cs]
            taps.append(jnp.where(t_of_row < shift, prev_ref[j, :, cs], shifted))
        return taps + [ug]

    u_bf = _rms(x2, gf_ref[...]).astype(BF16)
    x3 = x2 + _ffn(u_bf, taps_fn, wg_ref, wu_ref, wd_ref, cw_ref, cb_ref, ff_chunk)
    y_ref[...] = _rms(x3, gl_ref[...])


def _single_call(kernel, args, out_shape, name, scratch_shapes=()):
    return pl.pallas_call(
        kernel,
        grid=(1,),
        in_specs=[_const_spec(a.shape) for a in args],
        out_specs=[pl.BlockSpec(o.shape, lambda *_, nd=len(o.shape): (0,) * nd) for o in out_shape],
        out_shape=out_shape,
        scratch_shapes=list(scratch_shapes),
        compiler_params=_params(("arbitrary",)),
        name=name,
    )(*args)


def _step_cross(cq, mem_k, mem_v):
    B, n_q, xw = cq.shape
    _, n_mem, x_heads, x_dh = mem_k.shape
    bmap = lambda b: (b, 0, 0)
    mem = pl.BlockSpec((1, n_mem, x_heads, x_dh), lambda b: (b, 0, 0, 0))
    return pl.pallas_call(
        functools.partial(_step_cross_kernel, x_heads=x_heads, x_dh=x_dh),
        grid=(B,),
        in_specs=[pl.BlockSpec((1, n_q, xw), bmap), mem, mem],
        out_specs=pl.BlockSpec((1, n_q, xw), bmap),
        out_shape=jax.ShapeDtypeStruct((B, n_q, xw), F32),
        compiler_params=_params(("parallel",)),
        name="step_cross",
    )(cq, mem_k, mem_v)


def kernel(x_prompt, x_sample, cache_win_k, cache_win_v, state_hgrn, state_ffn_conv, cache_mem_k, cache_mem_v,
           mem_prompt, hg_lb_logits, norm_mix, w_in, att_out_norm, hg_out_norm, w_out, norm_cross, norm_mem,
           w_cq, w_ck, w_cv, w_co, norm_ffn, w_gate, w_up, conv_w, conv_b, w_down, norm_final):
    Bp, T, D = x_prompt.shape
    Bs, Ts, _ = x_sample.shape
    depth, _, past, att_h, att_dh = cache_win_k.shape
    _, _, hg_h, hg_dk, hg_dv = state_hgrn.shape
    _, _, n_mem, x_h, x_dh = cache_mem_k.shape
    n_taps, dff = conv_w.shape[1:]
    att_w = att_h * att_dh
    hg_w = hg_h * hg_dk
    xw = x_h * x_dh
    keep = min(max(w for w, _ in DIL_PATTERNS), T)
    assert depth == 1 and hg_dk == hg_dv == LANES and x_dh == LANES and LANES % att_dh == 0
    assert all(w // d == ATT_BLOCK for w, d in DIL_PATTERNS) and past >= max(w for w, _ in DIL_PATTERNS)
    d_max = max(d for _, d in DIL_PATTERNS)
    assert T % (ATT_BLOCK * d_max) == 0 and past % d_max == 0 and past % (4 * ATT_BLOCK) == 0
    assert n_taps - 1 <= min(Ts, SUBLANES) and Ts <= SUBLANES
    layer = 0
    ff_chunk = 2 * LANES
    q_scale = att_dh ** -0.5
    row2 = lambda a: a.reshape(1, -1)

    w_in_bf = w_in[layer].astype(BF16)
    w_ckv_bf = jnp.concatenate([w_ck[layer], w_cv[layer]], axis=1).astype(BF16)
    weights = tuple(w[layer].astype(BF16) for w in (w_out, w_cq, w_co, w_gate, w_up, w_down))
    g_mix, g_att, g_hg = row2(norm_mix[layer]), row2(att_out_norm[layer]), row2(hg_out_norm[layer])
    g_cross, g_mem, g_ffn, g_final = (row2(norm_cross[layer]), row2(norm_mem[layer]), row2(norm_ffn[layer]),
                                      row2(norm_final))
    cw, cb = conv_w[layer], row2(conv_b[layer])

    q, k, v, k_keep, v_keep, hz = _in_proj(x_prompt, g_mix, w_in_bf, att_w=att_w, q_scale=q_scale, tm=512, keep=keep)
    mk, mv, mk_bf, mv_bf = _mem_kv(mem_prompt.reshape(Bp * n_mem, D), g_mem, w_ckv_bf, tm=256)
    att = _dil_attn(q, k, v, g_att, n_heads=att_h, dh=att_dh, unroll=8)
    ohg, s_prompt = _hgrn(hz, hg_lb_logits, g_hg, n_heads=hg_h, dk=hg_dk, tc=256, layer=layer)
    y_prompt, cst = _post(
        x_prompt, att, ohg, mk_bf.reshape(Bp, n_mem, xw), mv_bf.reshape(Bp, n_mem, xw),
        jnp.zeros((Bp, SUBLANES, dff), F32), (g_cross, g_ffn, g_final), weights, cw, cb,
        tm=512, n_sub=1, x_heads=x_h, ff_chunk=ff_chunk)

    n_s = Bs * Ts
    qs, ks, vs, ks_f, vs_f, hzs = _in_proj(x_sample.reshape(1, n_s, D), g_mix, w_in_bf, att_w=att_w, q_scale=q_scale,
                                           tm=n_s, keep=n_s)
    pad_new = lambda a: jnp.pad(a.reshape(Bs, Ts, att_w).astype(BF16), ((0, 0), (0, LANES - Ts), (0, 0)))
    feature_major = lambda c: jnp.transpose(c, (0, 2, 3, 1)).reshape(Bs, att_w, past)
    att_s = _dil_attn_step(qs.reshape(Bs, Ts, att_w), pad_new(ks), pad_new(vs), feature_major(cache_win_k[layer]),
                           feature_major(cache_win_v[layer]), g_att, n_heads=att_h, dh=att_dh)
    ohg_s, s_sample = _hgrn_step(hzs.reshape(Bs, Ts, 4 * hg_w), state_hgrn[layer], hg_lb_logits, g_hg, layer=layer)
    x1_s, cq_s = _single_call(
        _step_mix_kernel,
        (x_sample.reshape(n_s, D), att_s.reshape(n_s, att_w), ohg_s.reshape(n_s, hg_w), g_cross, weights[0],
         weights[1]),
        [jax.ShapeDtypeStruct((n_s, D), F32), jax.ShapeDtypeStruct((n_s, xw), F32)], "step_mix")
    cq_pad = jnp.pad(cq_s.reshape(Bs, Ts, xw), ((0, 0), (0, SUBLANES - Ts), (0, 0)))
    co_s = _step_cross(cq_pad, cache_mem_k[layer], cache_mem_v[layer])[:, :Ts]
    conv_state = state_ffn_conv[layer]
    t_idx = jnp.arange(Ts)
    prev = jnp.stack([jnp.take(conv_state, jnp.clip(j + t_idx, 0, n_taps - 2), axis=1).reshape(n_s, dff)
                      for j in range(n_taps - 1)])
    y_s, ug_s = _single_call(
        functools.partial(_step_ffn_kernel, ff_chunk=ff_chunk, n_t=Ts),
        (x1_s, co_s.reshape(n_s, xw), prev, g_ffn, g_final, weights[2], weights[3], weights[4], weights[5], cw, cb),
        [jax.ShapeDtypeStruct((n_s, D), F32), jax.ShapeDtypeStruct((n_s, dff), F32)], "step_ffn",
        scratch_shapes=[pltpu.VMEM((n_s + SUBLANES, dff), F32)])

    stack = lambda a: a[None]
    return (y_prompt, y_s.reshape(Bs, Ts, D),
            stack(k_keep.reshape(Bp, keep, att_h, att_dh)), stack(v_keep.reshape(Bp, keep, att_h, att_dh)),
            stack(s_prompt), stack(cst[:, SUBLANES - (n_taps - 1):]),
            stack(mk.reshape(Bp, n_mem, x_h, x_dh)), stack(mv.reshape(Bp, n_mem, x_h, x_dh)),
            stack(ks_f.reshape(Bs, Ts, att_h, att_dh)), stack(vs_f.reshape(Bs, Ts, att_h, att_dh)),
            stack(s_sample), stack(ug_s.reshape(Bs, Ts, dff)[:, Ts - (n_taps - 1):]))
```

```python
import functools

import jax
import jax.numpy as jnp
from jax import lax
from jax.experimental import pallas as pl
from jax.experimental.pallas import tpu as pltpu

F32 = jnp.float32
BF16 = jnp.bfloat16
EPS = 1e-6
DIL_PATTERNS = ((128, 1), (512, 4), (2048, 16))
ATT_BLOCK = 128
LANES = 128
SUBLANES = 8
HG_CHUNK = 64
HG_SUB = 8
NEG = -1e30
LOG2E = 1.4426950408889634
HIGHEST = lax.Precision.HIGHEST
NT_DIMS = (((1,), (1,)), ((), ()))
TN_DIMS = (((0,), (0,)), ((), ()))
VMEM_LIMIT = 56 * 1024 * 1024


def _dot(a, b):
    return jnp.dot(a, b, preferred_element_type=F32)


def _dot_nt(a, b):
    return lax.dot_general(a, b, NT_DIMS, preferred_element_type=F32)


def _sigmoid(x):
    return 1.0 / (1.0 + jnp.exp(-x))


def _rms(x, g):
    return x * lax.rsqrt(jnp.mean(x * x, axis=-1, keepdims=True) + EPS) * g


def _const_spec(shape):
    nd = len(shape)
    return pl.BlockSpec(shape, lambda *_: (0,) * nd, pipeline_mode=pl.Buffered(1))


def _params(sem):
    return pltpu.CompilerParams(dimension_semantics=sem, vmem_limit_bytes=VMEM_LIMIT)


def _in_proj_kernel(x_ref, g_ref, w_ref, q_ref, k_ref, v_ref, kf_ref, vf_ref, hz_ref, *, att_w, q_scale):
    h = _rms(x_ref[0], g_ref[...]).astype(BF16)

    def proj(lo, hi):
        return _dot(h, w_ref[:, lo:hi])

    q_ref[0] = proj(0, att_w) * q_scale
    zk = proj(att_w, 2 * att_w)
    k_ref[0] = zk
    kf_ref[0] = zk
    zv = proj(2 * att_w, 3 * att_w)
    v_ref[0] = zv
    vf_ref[0] = zv
    base = 3 * att_w
    step = 4 * LANES
    for c in range(0, hz_ref.shape[-1], step):
        hz_ref[0, :, c:c + step] = proj(base + c, base + c + step)


def _in_proj(x, g, w_bf, *, att_w, q_scale, tm, keep):
    B, T, D = x.shape
    ncols = w_bf.shape[1]
    hzw = ncols - 3 * att_w
    n_t = T // tm
    first = n_t - keep // tm
    row = lambda b, i: (b, i, 0)
    keep_map = lambda b, i: (b, jnp.maximum(i - first, 0), 0)
    return pl.pallas_call(
        functools.partial(_in_proj_kernel, att_w=att_w, q_scale=q_scale),
        grid=(B, n_t),
        in_specs=[pl.BlockSpec((1, tm, D), row), _const_spec((1, D)), _const_spec((D, ncols))],
        out_specs=[pl.BlockSpec((1, tm, att_w), row)] * 3
        + [pl.BlockSpec((1, tm, att_w), keep_map)] * 2
        + [pl.BlockSpec((1, tm, hzw), row)],
        out_shape=[jax.ShapeDtypeStruct((B, T, att_w), F32)] * 3
        + [jax.ShapeDtypeStruct((B, keep, att_w), F32)] * 2
        + [jax.ShapeDtypeStruct((B, T, hzw), F32)],
        compiler_params=_params(("parallel", "arbitrary")),
        name="in_proj",
    )(x, g, w_bf)


def _mem_kv_kernel(m_ref, g_ref, w_ref, kf_ref, vf_ref, kb_ref, vb_ref):
    h = _rms(m_ref[...], g_ref[...]).astype(BF16)
    xw = w_ref.shape[1] // 2
    k = _dot(h, w_ref[:, :xw])
    v = _dot(h, w_ref[:, xw:])
    kf_ref[...] = k
    vf_ref[...] = v
    kb_ref[...] = k.astype(BF16)
    vb_ref[...] = v.astype(BF16)


def _mem_kv(mem2d, g, w_ckv_bf, *, tm):
    n, D = mem2d.shape
    xw = w_ckv_bf.shape[1] // 2
    row = lambda i: (i, 0)
    return pl.pallas_call(
        _mem_kv_kernel,
        grid=(n // tm,),
        in_specs=[pl.BlockSpec((tm, D), row), _const_spec((1, D)), _const_spec((D, 2 * xw))],
        out_specs=[pl.BlockSpec((tm, xw), row)] * 4,
        out_shape=[jax.ShapeDtypeStruct((n, xw), F32)] * 2 + [jax.ShapeDtypeStruct((n, xw), BF16)] * 2,
        compiler_params=_params(("parallel",)),
        name="mem_kv",
    )(mem2d, g, w_ckv_bf)


def _dil_attn_kernel(q_ref, k_ref, v_ref, g_ref, o_ref, acc_sc, m_sc, den_sc, bias_sc, *, n_heads, dh, unroll):
    tile = pl.program_id(1)
    T = q_ref.shape[1]
    blk = ATT_BLOCK
    hpt = LANES // dh
    rows, ncol = hpt * blk, 2 * blk
    rr = lax.broadcasted_iota(jnp.int32, (rows, ncol), 0)
    cc = lax.broadcasted_iota(jnp.int32, (rows, ncol), 1)
    delta = (rr % blk) + blk - cc
    in_band = (delta >= 0) & (delta <= blk)
    slope = jnp.exp2((-8.0 / n_heads) * (tile * hpt + rr // blk + 1).astype(F32))
    lane_head = lax.broadcasted_iota(jnp.int32, (blk, LANES), 1) // dh
    n_pat = len(DIL_PATTERNS)

    def per_lane(x):
        out = jnp.broadcast_to(x[0:blk], (blk, LANES))
        for e in range(1, hpt):
            out = jnp.where(lane_head == e, jnp.broadcast_to(x[e * blk:(e + 1) * blk], (blk, LANES)), out)
        return out

    ones = jnp.ones((ncol, LANES), BF16)
    order = sorted(range(n_pat), key=lambda p: -DIL_PATTERNS[p][1])
    for step, p in enumerate(order):
        dil = DIL_PATTERNS[p][1]
        alibi = (-LOG2E) * slope * (delta * dil).astype(F32)
        bias_sc[0] = jnp.where(in_band & (cc >= blk), alibi, NEG)
        bias_sc[1] = jnp.where(in_band, alibi, NEG)
        nblk = T // (blk * dil)

        def body(n, carry, first=step == 0, last=step == n_pat - 1, dil=dil, nblk=nblk):
            kp, vp = carry
            i = n % nblk
            start = n // nblk + i * (blk * dil)
            idx = pl.ds(start, blk, stride=dil) if dil > 1 else pl.ds(pl.multiple_of(start, blk), blk)
            q2 = q_ref[0, idx, :] * LOG2E
            kc = k_ref[0, idx, :].astype(BF16)
            vc = v_ref[0, idx, :].astype(BF16)
            qs = jnp.concatenate([jnp.where(lane_head == e, q2, 0.0) for e in range(hpt)], axis=0).astype(BF16)
            s = _dot_nt(qs, jnp.concatenate([kp, kc], axis=0)) + bias_sc[jnp.minimum(i, 1)]
            m = jnp.max(s, axis=-1, keepdims=True)
            pr = jnp.exp2(s - m).astype(BF16)
            pv = _dot(pr, jnp.concatenate([jnp.concatenate([vp, vc], axis=0), ones], axis=1))
            m_t, acc_t, den_t = per_lane(m), per_lane(pv[:, :LANES]), per_lane(pv[:, LANES:])
            if not first:
                m_old = m_sc[idx, :]
                m_new = jnp.maximum(m_old, m_t)
                a_old, a_t = jnp.exp2(m_old - m_new), jnp.exp2(m_t - m_new)
                den_t = a_old * den_sc[idx, :] + a_t * den_t
                acc_t = a_old * acc_sc[idx, :] + a_t * acc_t
                m_t = m_new
            if last:
                acc_sc[idx, :] = acc_t / den_t
            else:
                m_sc[idx, :] = m_t
                den_sc[idx, :] = den_t
                acc_sc[idx, :] = acc_t
            return kc, vc

        zero = jnp.zeros((blk, LANES), BF16)
        lax.fori_loop(0, T // blk, body, (zero, zero), unroll=unroll)

    ch = 4 * blk
    lane_h = lax.broadcasted_iota(jnp.int32, (ch, LANES), 1) // dh
    for c in range(T // ch):
        a = acc_sc[c * ch:(c + 1) * ch, :]
        sq = a * a
        mean_sq = jnp.zeros((ch, LANES), F32)
        for e in range(hpt):
            se = jnp.sum(jnp.where(lane_h == e, sq, 0.0), axis=-1, keepdims=True) * (1.0 / dh)
            mean_sq = jnp.where(lane_h == e, jnp.broadcast_to(se, (ch, LANES)), mean_sq)
        o_ref[0, c * ch:(c + 1) * ch, :] = (a * lax.rsqrt(mean_sq + EPS) * g_ref[...]).astype(BF16)


def _dil_attn(q, k, v, g, *, n_heads, dh, unroll):
    B, T, W = q.shape
    blk = ATT_BLOCK
    seq = pl.BlockSpec((1, T, LANES), lambda b, t: (b, 0, t))
    return pl.pallas_call(
        functools.partial(_dil_attn_kernel, n_heads=n_heads, dh=dh, unroll=unroll),
        grid=(B, W // LANES),
        in_specs=[seq, seq, seq, pl.BlockSpec((1, LANES), lambda b, t: (0, t))],
        out_specs=seq,
        out_shape=jax.ShapeDtypeStruct((B, T, W), BF16),
        scratch_shapes=[pltpu.VMEM((T, LANES), F32)] * 3
        + [pltpu.VMEM((2, (LANES // dh) * blk, 2 * blk), F32)],
        compiler_params=_params(("parallel", "arbitrary")),
        name="dil_attn",
    )(q, k, v, g)


def _dil_attn_step_kernel(q_ref, kn_ref, vn_ref, kt_ref, vt_ref, g_ref, o_ref, cnt_sc, bias_sc,
                          *, n_heads, dh, n_q):
    past = kt_ref.shape[2]
    n_new = kn_ref.shape[1]
    W = q_ref.shape[-1]
    rows = n_q * n_heads
    n_cols = past + n_new

    @pl.when(pl.program_id(0) == 0)
    def _():
        r = lax.broadcasted_iota(jnp.int32, (rows, n_cols), 0)
        c = lax.broadcasted_iota(jnp.int32, (rows, n_cols), 1)
        delta = past + r // n_heads - c
        cnt = jnp.zeros((rows, n_cols), F32)
        for win, dil in DIL_PATTERNS:
            cnt = cnt + ((delta >= 0) & (delta <= win) & ((delta & (dil - 1)) == 0)).astype(F32)
        slope = jnp.exp2((-8.0 / n_heads) * ((r % n_heads) + 1).astype(F32))
        cnt_sc[...] = cnt
        bias_sc[...] = jnp.where(cnt > 0.0, -slope * delta.astype(F32), NEG)

    sub = lax.broadcasted_iota(jnp.int32, (n_heads, W), 0)
    lane_head = lax.broadcasted_iota(jnp.int32, (n_heads, W), 1) // dh
    qbd = jnp.concatenate(
        [jnp.where(sub == lane_head, jnp.broadcast_to(q_ref[0, i:i + 1, :], (n_heads, W)), 0.0) for i in range(n_q)],
        axis=0).astype(BF16)
    s = jnp.concatenate([_dot(qbd, kt_ref[0].astype(BF16)), _dot_nt(qbd, kn_ref[0])], axis=1) + bias_sc[...]
    p = jnp.exp(s - jnp.max(s, axis=-1, keepdims=True)) * cnt_sc[...]
    den = jnp.sum(p, axis=-1, keepdims=True)
    p = p.astype(BF16)
    out = (_dot_nt(p[:, :past], vt_ref[0].astype(BF16)) + _dot(p[:, past:], vn_ref[0])) / den
    r = lax.broadcasted_iota(jnp.int32, (rows, W), 0)
    c = lax.broadcasted_iota(jnp.int32, (rows, W), 1)
    out = jnp.where((r % n_heads) == (c // dh), out, 0.0)
    out = out * lax.rsqrt(jnp.sum(out * out, axis=-1, keepdims=True) * (1.0 / dh) + EPS)
    o_ref[0] = jnp.sum(out.reshape(n_q, n_heads, W), axis=1) * g_ref[...]


def _dil_attn_step(q, k_new_pad, v_new_pad, cache_kt, cache_vt, g, *, n_heads, dh):
    B, n_q, W = q.shape
    past = cache_kt.shape[2]
    n_new = k_new_pad.shape[1]
    bmap = lambda b: (b, 0, 0)
    return pl.pallas_call(
        functools.partial(_dil_attn_step_kernel, n_heads=n_heads, dh=dh, n_q=n_q),
        grid=(B,),
        in_specs=[pl.BlockSpec((1, n_q, W), bmap), pl.BlockSpec((1, n_new, W), bmap), pl.BlockSpec((1, n_new, W), bmap),
                  pl.BlockSpec((1, W, past), bmap), pl.BlockSpec((1, W, past), bmap), _const_spec((1, W))],
        out_specs=pl.BlockSpec((1, n_q, W), bmap),
        out_shape=jax.ShapeDtypeStruct((B, n_q, W), F32),
        scratch_shapes=[pltpu.VMEM((n_q * n_heads, past + n_new), F32)] * 2,
        compiler_params=_params(("arbitrary",)),
        name="dil_attn_step",
    )(q, k_new_pad, v_new_pad, cache_kt, cache_vt, g)


def _lower_bound(lbl_ref, layer):
    logits = lbl_ref[...]
    e = jnp.exp(logits - jnp.max(logits, axis=0, keepdims=True))
    return jnp.sum(e[:layer + 1], axis=0, keepdims=True) / jnp.sum(e, axis=0, keepdims=True)


def _cumsum_rows(tril_bf, x):
    x1 = x.astype(BF16)
    r1 = x - x1.astype(F32)
    x2 = r1.astype(BF16)
    x3 = (r1 - x2.astype(F32)).astype(BF16)
    return _dot(tril_bf, x1) + _dot(tril_bf, x2) + _dot(tril_bf, x3)


def _hgrn_rows(hz_ref, o_ref, lb, gn_ref, st_ref, b_sc, c_sc, *, n_heads, dk):
    hw = n_heads * dk
    tc = hz_ref.shape[0]
    ch, sub = HG_CHUNK, HG_SUB
    r_c = lax.broadcasted_iota(jnp.int32, (ch, ch), 0)
    c_c = lax.broadcasted_iota(jnp.int32, (ch, ch), 1)
    tril = (r_c >= c_c).astype(BF16)
    r_s = lax.broadcasted_iota(jnp.int32, (sub, ch), 0)
    c_s = lax.broadcasted_iota(jnp.int32, (sub, ch), 1)

    for c in range(tc // ch):
        rows = slice(c * ch, (c + 1) * ch)
        hq = hz_ref[rows, 0:hw]
        f = lb + (1.0 - lb) * _sigmoid(hz_ref[rows, hw:2 * hw])
        qh = hq * _sigmoid(hq) * dk ** -0.5
        kk = 1.0 - f
        b = _cumsum_rows(tril, jnp.log(f) * LOG2E)
        b_rows, c_rows = b_sc.at[c], c_sc.at[c]
        b_rows[...] = b
        c_rows[...] = b - jnp.log(kk) * LOG2E
        for h in range(n_heads):
            hs = slice(h * dk, (h + 1) * dk)
            bh = b[:, hs]
            kh = kk[:, hs]
            qhh = qh[:, hs]
            vh = hz_ref[rows, 2 * hw + h * dk:2 * hw + (h + 1) * dk].astype(BF16)
            st = st_ref[h]
            o = _dot_nt((qhh * jnp.exp2(bh)).astype(BF16), st.astype(BF16))
            blocks = []
            for j in range(ch // sub):
                r0 = j * sub
                bj = bh[r0:r0 + sub]
                qj = qhh[r0:r0 + sub]
                a_j = jnp.zeros((sub, ch), F32)
                for s in range(sub):
                    a = jnp.sum(qj * jnp.exp2(bj - c_rows[r0 + s:r0 + s + 1, hs]), axis=-1, keepdims=True)
                    a_j = jnp.where(c_s == r0 + s, a, a_j)
                a_j = jnp.where(c_s - r0 <= r_s, a_j, 0.0)
                if j > 0:
                    beta = b_rows[r0 - 1:r0, hs]
                    qt = qj * jnp.exp2(bj - beta)
                    kt = kh[0:r0] * jnp.exp2(beta - bh[0:r0])
                    kt = jnp.concatenate([kt, jnp.zeros((ch - r0, dk), F32)], axis=0)
                    a_j = a_j + _dot_nt(qt.astype(BF16), kt.astype(BF16))
                blocks.append(a_j)
            a_mat = jnp.concatenate(blocks, axis=0)
            o = o + _dot(a_mat.astype(BF16), vh)
            b_last = b_rows[ch - 1:ch, hs]
            khat = (kh * jnp.exp2(b_last - bh)).astype(BF16)
            st_ref[h] = st * jnp.exp2(b_last) + lax.dot_general(vh, khat, TN_DIMS, preferred_element_type=F32)
            hg = hz_ref[rows, 3 * hw + h * dk:3 * hw + (h + 1) * dk]
            o_ref[rows, hs] = (_rms(o, gn_ref[:, hs]) * (hg * _sigmoid(hg))).astype(BF16)


def _in_proj_hgrn_kernel(x_ref, g_ref, w_ref, lbl_ref, gn_ref, q_ref, k_ref, v_ref, kf_ref, vf_ref, o_ref, s_ref,
                         hz_sc, st_ref, b_sc, c_sc, *, att_w, q_scale, n_heads, dk, layer, tiles_per_seq):
    g = pl.program_id(0)
    t_rec = jnp.maximum(g - 1, 0)

    @pl.when(g == 0)
    def _():
        hz_sc[...] = jnp.zeros_like(hz_sc)

    @pl.when(t_rec % tiles_per_seq == 0)
    def _():
        st_ref[...] = jnp.zeros_like(st_ref)

    lb = _lower_bound(lbl_ref, layer)

    def run(slot_w, slot_r):
        h = _rms(x_ref[...], g_ref[...]).astype(BF16)

        def proj(lo, hi):
            return _dot(h, w_ref[:, lo:hi])

        q_ref[...] = proj(0, att_w) * q_scale
        zk = proj(att_w, 2 * att_w)
        k_ref[...] = zk
        kf_ref[...] = zk
        zv = proj(2 * att_w, 3 * att_w)
        v_ref[...] = zv
        vf_ref[...] = zv
        base = 3 * att_w
        step = 4 * LANES
        for c in range(0, hz_sc.shape[-1], step):
            hz_sc[slot_w, :, c:c + step] = proj(base + c, base + c + step)
        _hgrn_rows(hz_sc.at[slot_r], o_ref, lb, gn_ref, st_ref, b_sc, c_sc, n_heads=n_heads, dk=dk)

    @pl.when(g % 2 == 0)
    def _():
        run(0, 1)

    @pl.when(g % 2 == 1)
    def _():
        run(1, 0)

    @pl.when((g > 0) & (t_rec % tiles_per_seq == tiles_per_seq - 1))
    def _():
        for h in range(n_heads):
            s_ref[0, h] = st_ref[h].T


def _in_proj_hgrn(x, g, w_bf, lb_logits, gn, *, att_w, q_scale, n_heads, dk, tm, keep, layer):
    B, T, D = x.shape
    hw = n_heads * dk
    n_t = T // tm
    n_tiles = B * n_t
    n_keep = keep // tm
    first = n_t - n_keep
    proj_tile = lambda s: jnp.minimum(s, n_tiles - 1)
    rec_tile = lambda s: jnp.maximum(s - 1, 0)
    rows = lambda s: (proj_tile(s), 0)
    keep_rows = lambda s: ((proj_tile(s) // n_t) * n_keep + jnp.maximum(proj_tile(s) % n_t - first, 0), 0)
    f32 = lambda n, w: jax.ShapeDtypeStruct((n, w), F32)
    return pl.pallas_call(
        functools.partial(_in_proj_hgrn_kernel, att_w=att_w, q_scale=q_scale, n_heads=n_heads, dk=dk, layer=layer,
                          tiles_per_seq=n_t),
        grid=(n_tiles + 1,),
        in_specs=[pl.BlockSpec((tm, D), rows), _const_spec((1, D)), _const_spec(w_bf.shape),
                  _const_spec(lb_logits.shape), _const_spec((1, hw))],
        out_specs=[pl.BlockSpec((tm, att_w), rows)] * 3 + [pl.BlockSpec((tm, att_w), keep_rows)] * 2
        + [pl.BlockSpec((tm, hw), lambda s: (rec_tile(s), 0)),
           pl.BlockSpec((1, n_heads, dk, dk), lambda s: (rec_tile(s) // n_t, 0, 0, 0))],
        out_shape=[f32(B * T, att_w)] * 3 + [f32(B * keep, att_w)] * 2
        + [jax.ShapeDtypeStruct((B * T, hw), BF16), jax.ShapeDtypeStruct((B, n_heads, dk, dk), F32)],
        scratch_shapes=[pltpu.VMEM((2, tm, 4 * hw), F32), pltpu.VMEM((n_heads, dk, dk), F32)]
        + [pltpu.VMEM((tm // HG_CHUNK, HG_CHUNK, hw), F32)] * 2,
        compiler_params=_params(("arbitrary",)),
        name="in_proj_hgrn",
    )(x.reshape(B * T, D), g, w_bf, lb_logits, gn)


def _hgrn_step_kernel(hz_ref, s0_ref, lbl_ref, gn_ref, o_ref, s_ref, *, n_heads, dk, layer):
    hw = n_heads * dk
    n_t = hz_ref.shape[1]
    lb = _lower_bound(lbl_ref, layer)
    q, k, v, b, gate = [], [], [], [], []
    acc = jnp.zeros((1, hw), F32)
    for t in range(n_t):
        hq = hz_ref[0, t:t + 1, 0:hw]
        f = lb + (1.0 - lb) * _sigmoid(hz_ref[0, t:t + 1, hw:2 * hw])
        acc = acc + jnp.log(f)
        q.append(hq * _sigmoid(hq) * dk ** -0.5)
        k.append(1.0 - f)
        v.append(hz_ref[0, t:t + 1, 2 * hw:3 * hw])
        b.append(acc)
        hg = hz_ref[0, t:t + 1, 3 * hw:4 * hw]
        gate.append(hg * _sigmoid(hg))
    rr = lax.broadcasted_iota(jnp.int32, (dk, dk), 0)
    cc = lax.broadcasted_iota(jnp.int32, (dk, dk), 1)

    def tile_of_rows(rows):
        tile = jnp.zeros((dk, dk), F32)
        for t, r in enumerate(rows):
            tile = jnp.where(rr == t, jnp.broadcast_to(r, (dk, dk)), tile)
        return tile

    for h in range(n_heads):
        hs = slice(h * dk, (h + 1) * dk)
        s0 = s0_ref[0, h]
        qe = tile_of_rows([q[t][:, hs] * jnp.exp(b[t][:, hs]) for t in range(n_t)])
        inter = jnp.dot(qe, s0, precision=HIGHEST, preferred_element_type=F32)
        for t in range(n_t):
            o = inter[t:t + 1]
            for s in range(t + 1):
                a = jnp.sum(q[t][:, hs] * jnp.exp(b[t][:, hs] - b[s][:, hs]) * k[s][:, hs], axis=-1, keepdims=True)
                o = o + a * v[s][:, hs]
            o_ref[0, t:t + 1, hs] = _rms(o, gn_ref[:, hs]) * gate[t][:, hs]
        b_last = b[n_t - 1][:, hs]
        khat = tile_of_rows([k[t][:, hs] * jnp.exp(b_last - b[t][:, hs]) for t in range(n_t)])
        vpad = tile_of_rows([v[t][:, hs] for t in range(n_t)])
        decay = jnp.where(rr == cc, jnp.broadcast_to(jnp.exp(b_last), (dk, dk)), 0.0)
        s_ref[0, h] = (jnp.dot(decay, s0, precision=HIGHEST, preferred_element_type=F32)
                       + jnp.dot(khat.T, vpad, precision=HIGHEST, preferred_element_type=F32))


def _hgrn_step(hz, s0, lb_logits, gn, *, layer):
    B, n_t, _ = hz.shape
    _, n_heads, dk, dv = s0.shape
    hw = n_heads * dk
    return pl.pallas_call(
        functools.partial(_hgrn_step_kernel, n_heads=n_heads, dk=dk, layer=layer),
        grid=(B,),
        in_specs=[pl.BlockSpec((1, n_t, 4 * hw), lambda b: (b, 0, 0)),
                  pl.BlockSpec((1, n_heads, dk, dv), lambda b: (b, 0, 0, 0)),
                  _const_spec(lb_logits.shape), _const_spec((1, hw))],
        out_specs=[pl.BlockSpec((1, n_t, hw), lambda b: (b, 0, 0)),
                   pl.BlockSpec((1, n_heads, dk, dv), lambda b: (b, 0, 0, 0))],
        out_shape=[jax.ShapeDtypeStruct((B, n_t, hw), F32), jax.ShapeDtypeStruct((B, n_heads, dk, dv), F32)],
        compiler_params=_params(("parallel",)),
        name="hgrn_step",
    )(hz, s0, lb_logits, gn)


def _mix_out(x, att_bf, ohg_bf, wo_ref):
    aw = att_bf.shape[-1]
    return x + _dot(att_bf, wo_ref[0:aw, :]) + _dot(ohg_bf, wo_ref[aw:, :])


def _cross_attend(cq, head_k, head_v, n_heads, dh):
    outs = []
    for h in range(n_heads):
        s = _dot_nt((cq[:, h * dh:(h + 1) * dh] * dh ** -0.5).astype(BF16), head_k(h))
        p = jnp.exp(s - jnp.max(s, axis=-1, keepdims=True))
        den = jnp.sum(p, axis=-1, keepdims=True)
        outs.append(_dot(p.astype(BF16), head_v(h)) / den)
    return jnp.concatenate(outs, axis=-1)


def _ffn(u_bf, taps_fn, wg_ref, wu_ref, wd_ref, cw_ref, cb_ref, ff_chunk):
    dff = wg_ref.shape[1]
    n_taps = cw_ref.shape[0]
    acc = jnp.zeros((u_bf.shape[0], wd_ref.shape[1]), F32)
    for n in range(dff // ff_chunk):
        cs = slice(n * ff_chunk, (n + 1) * ff_chunk)
        ug = _dot(u_bf, wg_ref[:, cs])
        taps = taps_fn(ug, cs, n)
        conv = cb_ref[:, cs]
        for j in range(n_taps):
            conv = conv + cw_ref[j:j + 1, cs] * taps[j]
        act = conv * _sigmoid(conv) * _dot(u_bf, wu_ref[:, cs])
        acc = acc + _dot(act.astype(BF16), wd_ref[cs, :])
    return acc


def _post_kernel(x_ref, att_ref, ohg_ref, mk_ref, mv_ref, cinit_ref, gc_ref, gf_ref, gl_ref, wo_ref, wq_ref,
                 wc_ref, wg_ref, wu_ref, wd_ref, cw_ref, cb_ref, y_ref, cst_ref, buf_ref, carry_ref,
                 *, x_heads, x_dh, ff_chunk, n_sub):
    i = pl.program_id(1)
    tm = x_ref.shape[1]
    pad = SUBLANES
    n_taps = cw_ref.shape[0]

    @pl.when(i == 0)
    def _():
        carry_ref[...] = cinit_ref[0]

    ts = tm // n_sub
    for t in range(n_sub):
        rows = slice(t * ts, (t + 1) * ts)
        x1 = _mix_out(x_ref[0, rows], att_ref[0, rows], ohg_ref[0, rows], wo_ref)
        cq = _dot(_rms(x1, gc_ref[...]).astype(BF16), wq_ref[...])
        co = _cross_attend(cq, lambda h: mk_ref[0, :, h * x_dh:(h + 1) * x_dh],
                           lambda h: mv_ref[0, :, h * x_dh:(h + 1) * x_dh], x_heads, x_dh)
        x2 = x1 + _dot(co.astype(BF16), wc_ref[...])

        def taps_fn(ug, cs, n, t=t):
            buf = buf_ref.at[t, n % 2]
            buf[0:pad, :] = carry_ref[:, cs]
            buf[pad:pad + ts, :] = ug
            carry_ref[:, cs] = ug[ts - pad:ts]
            return [buf[pad - (n_taps - 1 - j):pad - (n_taps - 1 - j) + ts, :] for j in range(n_taps - 1)] + [ug]

        u_bf = _rms(x2, gf_ref[...]).astype(BF16)
        x3 = x2 + _ffn(u_bf, taps_fn, wg_ref, wu_ref, wd_ref, cw_ref, cb_ref, ff_chunk)
        y_ref[0, rows] = _rms(x3, gl_ref[...])
    cst_ref[0] = carry_ref[...]


def _post(x, att, ohg, mk_bf, mv_bf, cinit, gains, weights, conv_w, conv_b, *, tm, n_sub, x_heads, ff_chunk):
    B, T, D = x.shape
    aw, hw = att.shape[-1], ohg.shape[-1]
    n_mem, xw = mk_bf.shape[1:]
    dff = conv_w.shape[1]
    row = lambda b, i: (b, i, 0)
    bat = lambda b, i: (b, 0, 0)
    in_specs = ([pl.BlockSpec((1, tm, D), row), pl.BlockSpec((1, tm, aw), row), pl.BlockSpec((1, tm, hw), row)]
                + [pl.BlockSpec((1, n_mem, xw), bat)] * 2 + [pl.BlockSpec((1, SUBLANES, dff), bat)]
                + [_const_spec(a.shape) for a in tuple(gains) + tuple(weights) + (conv_w, conv_b)])
    return pl.pallas_call(
        functools.partial(_post_kernel, x_heads=x_heads, x_dh=xw // x_heads, ff_chunk=ff_chunk, n_sub=n_sub),
        grid=(B, T // tm),
        in_specs=in_specs,
        out_specs=[pl.BlockSpec((1, tm, D), row), pl.BlockSpec((1, SUBLANES, dff), bat)],
        out_shape=[jax.ShapeDtypeStruct((B, T, D), F32), jax.ShapeDtypeStruct((B, SUBLANES, dff), F32)],
        scratch_shapes=[pltpu.VMEM((n_sub, 2, tm // n_sub + SUBLANES, ff_chunk), F32),
                        pltpu.VMEM((SUBLANES, dff), F32)],
        compiler_params=_params(("parallel", "arbitrary")),
        name="post_mixer",
    )(x, att, ohg, mk_bf, mv_bf, cinit, *gains, *weights, conv_w, conv_b)


def _step_mix_kernel(x_ref, att_ref, ohg_ref, gc_ref, wo_ref, wq_ref, x1_ref, cq_ref):
    x1 = _mix_out(x_ref[...], att_ref[...].astype(BF16), ohg_ref[...].astype(BF16), wo_ref)
    x1_ref[...] = x1
    cq_ref[...] = _dot(_rms(x1, gc_ref[...]).astype(BF16), wq_ref[...])


def _step_cross_kernel(cq_ref, mk_ref, mv_ref, co_ref, *, n_q):
    _, n_mem, n_heads, dh = mk_ref.shape
    mk = mk_ref[0].reshape(n_mem * n_heads, dh).astype(BF16)
    mv = mv_ref[0].reshape(n_mem * n_heads, dh).astype(BF16)
    s = _dot_nt((cq_ref[0] * dh ** -0.5).astype(BF16), mk)
    row = lax.broadcasted_iota(jnp.int32, s.shape, 0)
    col = lax.broadcasted_iota(jnp.int32, s.shape, 1)
    s = jnp.where((col % n_heads) == (row // n_q), s, NEG)
    p = jnp.exp(s - jnp.max(s, axis=-1, keepdims=True))
    co_ref[0] = _dot(p.astype(BF16), mv) / jnp.sum(p, axis=-1, keepdims=True)


def _step_ffn_kernel(x1_ref, co_ref, prev_ref, gf_ref, gl_ref, wc_ref, wg_ref, wu_ref, wd_ref,
                     cw_ref, cb_ref, y_ref, ug_ref, buf_ref, *, ff_chunk, n_t):
    n = x1_ref.shape[0]
    pad = SUBLANES
    n_taps = cw_ref.shape[0]
    x2 = x1_ref[...] + _dot(co_ref[...].astype(BF16), wc_ref[...])
    t_of_row = lax.broadcasted_iota(jnp.int32, (n, ff_chunk), 0) % n_t

    def taps_fn(ug, cs, _):
        ug_ref[:, cs] = ug
        buf_ref[0:pad, cs] = jnp.zeros((pad, ff_chunk), F32)
        buf_ref[pad:pad + n, cs] = ug
        taps = []
        for j in range(n_taps - 1):
            shift = n_taps - 1 - j
            shifted = buf_ref[pad - shift:pad - shift + n, cs]
            taps.append(jnp.where(t_of_row < shift, prev_ref[j, :, cs], shifted))
        return taps + [ug]

    u_bf = _rms(x2, gf_ref[...]).astype(BF16)
    x3 = x2 + _ffn(u_bf, taps_fn, wg_ref, wu_ref, wd_ref, cw_ref, cb_ref, ff_chunk)
    y_ref[...] = _rms(x3, gl_ref[...])


def _single_call(kernel, args, out_shape, name, scratch_shapes=()):
    return pl.pallas_call(
        kernel,
        grid=(1,),
        in_specs=[_const_spec(a.shape) for a in args],
        out_specs=[pl.BlockSpec(o.shape, lambda *_, nd=len(o.shape): (0,) * nd) for o in out_shape],
        out_shape=out_shape,
        scratch_shapes=list(scratch_shapes),
        compiler_params=_params(("arbitrary",)),
        name=name,
    )(*args)


def _step_cross(cq, mem_k, mem_v, *, n_q):
    B, n_rows, dh = cq.shape
    _, n_mem, x_heads, _ = mem_k.shape
    bmap = lambda b: (b, 0, 0)
    mem = pl.BlockSpec((1, n_mem, x_heads, dh), lambda b: (b, 0, 0, 0))
    return pl.pallas_call(
        functools.partial(_step_cross_kernel, n_q=n_q),
        grid=(B,),
        in_specs=[pl.BlockSpec((1, n_rows, dh), bmap), mem, mem],
        out_specs=pl.BlockSpec((1, n_rows, dh), bmap),
        out_shape=jax.ShapeDtypeStruct((B, n_rows, dh), F32),
        compiler_params=_params(("parallel",)),
        name="step_cross",
    )(cq, mem_k, mem_v)


def kernel(x_prompt, x_sample, cache_win_k, cache_win_v, state_hgrn, state_ffn_conv, cache_mem_k, cache_mem_v,
           mem_prompt, hg_lb_logits, norm_mix, w_in, att_out_norm, hg_out_norm, w_out, norm_cross, norm_mem,
           w_cq, w_ck, w_cv, w_co, norm_ffn, w_gate, w_up, conv_w, conv_b, w_down, norm_final):
    Bp, T, D = x_prompt.shape
    Bs, Ts, _ = x_sample.shape
    depth, _, past, att_h, att_dh = cache_win_k.shape
    _, _, hg_h, hg_dk, hg_dv = state_hgrn.shape
    _, _, n_mem, x_h, x_dh = cache_mem_k.shape
    n_taps, dff = conv_w.shape[1:]
    att_w = att_h * att_dh
    hg_w = hg_h * hg_dk
    xw = x_h * x_dh
    keep = min(max(w for w, _ in DIL_PATTERNS), T)
    assert depth == 1 and hg_dk == hg_dv == LANES and x_dh == LANES and LANES % att_dh == 0
    assert all(w // d == ATT_BLOCK for w, d in DIL_PATTERNS) and past >= max(w for w, _ in DIL_PATTERNS)
    d_max = max(d for _, d in DIL_PATTERNS)
    assert T % (ATT_BLOCK * d_max) == 0 and past % d_max == 0 and past % (4 * ATT_BLOCK) == 0
    assert n_taps - 1 <= min(Ts, SUBLANES) and Ts <= SUBLANES
    layer = 0
    ff_chunk = 2 * LANES
    q_scale = att_dh ** -0.5
    row2 = lambda a: a.reshape(1, -1)

    w_in_bf = w_in[layer].astype(BF16)
    w_ckv_bf = jnp.concatenate([w_ck[layer], w_cv[layer]], axis=1).astype(BF16)
    weights = tuple(w[layer].astype(BF16) for w in (w_out, w_cq, w_co, w_gate, w_up, w_down))
    g_mix, g_att, g_hg = row2(norm_mix[layer]), row2(att_out_norm[layer]), row2(hg_out_norm[layer])
    g_cross, g_mem, g_ffn, g_final = (row2(norm_cross[layer]), row2(norm_mem[layer]), row2(norm_ffn[layer]),
                                      row2(norm_final))
    cw, cb = conv_w[layer], row2(conv_b[layer])

    q, k, v, k_keep, v_keep, ohg, s_prompt = _in_proj_hgrn(
        x_prompt, g_mix, w_in_bf, hg_lb_logits, g_hg, att_w=att_w, q_scale=q_scale, n_heads=hg_h, dk=hg_dk,
        tm=512, keep=keep, layer=layer)
    q, k, v = (a.reshape(Bp, T, att_w) for a in (q, k, v))
    ohg = ohg.reshape(Bp, T, hg_w)
    mk, mv, mk_bf, mv_bf = _mem_kv(mem_prompt.reshape(Bp * n_mem, D), g_mem, w_ckv_bf, tm=256)
    att = _dil_attn(q, k, v, g_att, n_heads=att_h, dh=att_dh, unroll=8)
    y_prompt, cst = _post(
        x_prompt, att, ohg, mk_bf.reshape(Bp, n_mem, xw), mv_bf.reshape(Bp, n_mem, xw),
        jnp.zeros((Bp, SUBLANES, dff), F32), (g_cross, g_ffn, g_final), weights, cw, cb,
        tm=512, n_sub=1, x_heads=x_h, ff_chunk=ff_chunk)

    n_s = Bs * Ts
    qs, ks, vs, ks_f, vs_f, hzs = _in_proj(x_sample.reshape(1, n_s, D), g_mix, w_in_bf, att_w=att_w, q_scale=q_scale,
                                           tm=n_s, keep=n_s)
    pad_new = lambda a: jnp.pad(a.reshape(Bs, Ts, att_w).astype(BF16), ((0, 0), (0, LANES - Ts), (0, 0)))
    feature_major = lambda c: jnp.transpose(c, (0, 2, 3, 1)).reshape(Bs, att_w, past)
    att_s = _dil_attn_step(qs.reshape(Bs, Ts, att_w), pad_new(ks), pad_new(vs), feature_major(cache_win_k[layer]),
                           feature_major(cache_win_v[layer]), g_att, n_heads=att_h, dh=att_dh)
    ohg_s, s_sample = _hgrn_step(hzs.reshape(Bs, Ts, 4 * hg_w), state_hgrn[layer], hg_lb_logits, g_hg, layer=layer)
    x1_s, cq_s = _single_call(
        _step_mix_kernel,
        (x_sample.reshape(n_s, D), att_s.reshape(n_s, att_w), ohg_s.reshape(n_s, hg_w), g_cross, weights[0],
         weights[1]),
        [jax.ShapeDtypeStruct((n_s, D), F32), jax.ShapeDtypeStruct((n_s, xw), F32)], "step_mix")
    cq_heads = cq_s.reshape(Bs, Ts, x_h, x_dh).transpose(0, 2, 1, 3).reshape(Bs, x_h * Ts, x_dh)
    co_s = _step_cross(cq_heads, cache_mem_k[layer], cache_mem_v[layer], n_q=Ts)
    co_s = co_s.reshape(Bs, x_h, Ts, x_dh).transpose(0, 2, 1, 3)
    conv_state = state_ffn_conv[layer]
    t_idx = jnp.arange(Ts)
    prev = jnp.stack([jnp.take(conv_state, jnp.clip(j + t_idx, 0, n_taps - 2), axis=1).reshape(n_s, dff)
                      for j in range(n_taps - 1)])
    y_s, ug_s = _single_call(
        functools.partial(_step_ffn_kernel, ff_chunk=ff_chunk, n_t=Ts),
        (x1_s, co_s.reshape(n_s, xw), prev, g_ffn, g_final, weights[2], weights[3], weights[4], weights[5], cw, cb),
        [jax.ShapeDtypeStruct((n_s, D), F32), jax.ShapeDtypeStruct((n_s, dff), F32)], "step_ffn",
        scratch_shapes=[pltpu.VMEM((n_s + SUBLANES, dff), F32)])

    stack = lambda a: a[None]
    return (y_prompt, y_s.reshape(Bs, Ts, D),
            stack(k_keep.reshape(Bp, keep, att_h, att_dh)), stack(v_keep.reshape(Bp, keep, att_h, att_dh)),
            stack(s_prompt), stack(cst[:, SUBLANES - (n_taps - 1):]),
            stack(mk.reshape(Bp, n_mem, x_h, x_dh)), stack(mv.reshape(Bp, n_mem, x_h, x_dh)),
            stack(ks_f.reshape(Bs, Ts, att_h, att_dh)), stack(vs_f.reshape(Bs, Ts, att_h, att_dh)),
            stack(s_sample), stack(ug_s.reshape(Bs, Ts, dff)[:, Ts - (n_taps - 1):]))
```

```python
import functools

import jax
import jax.numpy as jnp
from jax import lax
from jax.experimental import pallas as pl
from jax.experimental.pallas import tpu as pltpu

F32 = jnp.float32
BF16 = jnp.bfloat16
EPS = 1e-6
DIL_PATTERNS = ((128, 1), (512, 4), (2048, 16))
ATT_BLOCK = 128
LANES = 128
SUBLANES = 8
HG_CHUNK = 64
HG_SUB = 8
NEG = -1e30
LOG2E = 1.4426950408889634
HIGHEST = lax.Precision.HIGHEST
NT_DIMS = (((1,), (1,)), ((), ()))
TN_DIMS = (((0,), (0,)), ((), ()))
VMEM_LIMIT = 56 * 1024 * 1024


def _dot(a, b):
    return jnp.dot(a, b, preferred_element_type=F32)


def _dot_nt(a, b):
    return lax.dot_general(a, b, NT_DIMS, preferred_element_type=F32)


def _sigmoid(x):
    return 1.0 / (1.0 + jnp.exp(-x))


def _rms(x, g):
    return x * lax.rsqrt(jnp.mean(x * x, axis=-1, keepdims=True) + EPS) * g


def _const_spec(shape):
    nd = len(shape)
    return pl.BlockSpec(shape, lambda *_: (0,) * nd, pipeline_mode=pl.Buffered(1))


def _params(sem):
    return pltpu.CompilerParams(dimension_semantics=sem, vmem_limit_bytes=VMEM_LIMIT)


def _in_proj_kernel(x_ref, g_ref, w_ref, q_ref, k_ref, v_ref, kf_ref, vf_ref, hz_ref, *, att_w, q_scale):
    h = _rms(x_ref[0], g_ref[...]).astype(BF16)

    def proj(lo, hi):
        return _dot(h, w_ref[:, lo:hi])

    q_ref[0] = proj(0, att_w) * q_scale
    zk = proj(att_w, 2 * att_w)
    k_ref[0] = zk
    kf_ref[0] = zk
    zv = proj(2 * att_w, 3 * att_w)
    v_ref[0] = zv
    vf_ref[0] = zv
    base = 3 * att_w
    step = 4 * LANES
    for c in range(0, hz_ref.shape[-1], step):
        hz_ref[0, :, c:c + step] = proj(base + c, base + c + step)


def _in_proj(x, g, w_bf, *, att_w, q_scale, tm, keep):
    B, T, D = x.shape
    ncols = w_bf.shape[1]
    hzw = ncols - 3 * att_w
    n_t = T // tm
    first = n_t - keep // tm
    row = lambda b, i: (b, i, 0)
    keep_map = lambda b, i: (b, jnp.maximum(i - first, 0), 0)
    return pl.pallas_call(
        functools.partial(_in_proj_kernel, att_w=att_w, q_scale=q_scale),
        grid=(B, n_t),
        in_specs=[pl.BlockSpec((1, tm, D), row), _const_spec((1, D)), _const_spec((D, ncols))],
        out_specs=[pl.BlockSpec((1, tm, att_w), row)] * 3
        + [pl.BlockSpec((1, tm, att_w), keep_map)] * 2
        + [pl.BlockSpec((1, tm, hzw), row)],
        out_shape=[jax.ShapeDtypeStruct((B, T, att_w), F32)] * 3
        + [jax.ShapeDtypeStruct((B, keep, att_w), F32)] * 2
        + [jax.ShapeDtypeStruct((B, T, hzw), F32)],
        compiler_params=_params(("parallel", "arbitrary")),
        name="in_proj",
    )(x, g, w_bf)


def _mem_kv_kernel(m_ref, g_ref, w_ref, kf_ref, vf_ref, kb_ref, vb_ref):
    h = _rms(m_ref[...], g_ref[...]).astype(BF16)
    xw = w_ref.shape[1] // 2
    k = _dot(h, w_ref[:, :xw])
    v = _dot(h, w_ref[:, xw:])
    kf_ref[...] = k
    vf_ref[...] = v
    kb_ref[...] = k.astype(BF16)
    vb_ref[...] = v.astype(BF16)


def _mem_kv(mem2d, g, w_ckv_bf, *, tm):
    n, D = mem2d.shape
    xw = w_ckv_bf.shape[1] // 2
    row = lambda i: (i, 0)
    return pl.pallas_call(
        _mem_kv_kernel,
        grid=(n // tm,),
        in_specs=[pl.BlockSpec((tm, D), row), _const_spec((1, D)), _const_spec((D, 2 * xw))],
        out_specs=[pl.BlockSpec((tm, xw), row)] * 4,
        out_shape=[jax.ShapeDtypeStruct((n, xw), F32)] * 2 + [jax.ShapeDtypeStruct((n, xw), BF16)] * 2,
        compiler_params=_params(("parallel",)),
        name="mem_kv",
    )(mem2d, g, w_ckv_bf)


def _dil_attn_kernel(q_ref, k_ref, v_ref, g_ref, o_ref, acc_sc, m_sc, den_sc, bias_sc, *, n_heads, dh, unroll):
    tile = pl.program_id(1)
    T = q_ref.shape[1]
    blk = ATT_BLOCK
    hpt = LANES // dh
    rows, ncol = hpt * blk, 2 * blk
    rr = lax.broadcasted_iota(jnp.int32, (rows, ncol), 0)
    cc = lax.broadcasted_iota(jnp.int32, (rows, ncol), 1)
    delta = (rr % blk) + blk - cc
    in_band = (delta >= 0) & (delta <= blk)
    slope = jnp.exp2((-8.0 / n_heads) * (tile * hpt + rr // blk + 1).astype(F32))
    lane_head = lax.broadcasted_iota(jnp.int32, (blk, LANES), 1) // dh
    n_pat = len(DIL_PATTERNS)

    def per_lane(x):
        out = jnp.broadcast_to(x[0:blk], (blk, LANES))
        for e in range(1, hpt):
            out = jnp.where(lane_head == e, jnp.broadcast_to(x[e * blk:(e + 1) * blk], (blk, LANES)), out)
        return out

    ones = jnp.ones((ncol, LANES), BF16)
    order = sorted(range(n_pat), key=lambda p: -DIL_PATTERNS[p][1])
    for step, p in enumerate(order):
        dil = DIL_PATTERNS[p][1]
        alibi = (-LOG2E) * slope * (delta * dil).astype(F32)
        bias_sc[0] = jnp.where(in_band & (cc >= blk), alibi, NEG)
        bias_sc[1] = jnp.where(in_band, alibi, NEG)
        nblk = T // (blk * dil)

        def body(n, carry, first=step == 0, last=step == n_pat - 1, dil=dil, nblk=nblk):
            kp, vp = carry
            i = n % nblk
            start = n // nblk + i * (blk * dil)
            idx = pl.ds(start, blk, stride=dil) if dil > 1 else pl.ds(pl.multiple_of(start, blk), blk)
            q2 = q_ref[0, idx, :] * LOG2E
            kc = k_ref[0, idx, :].astype(BF16)
            vc = v_ref[0, idx, :].astype(BF16)
            qs = jnp.concatenate([jnp.where(lane_head == e, q2, 0.0) for e in range(hpt)], axis=0).astype(BF16)
            s = _dot_nt(qs, jnp.concatenate([kp, kc], axis=0)) + bias_sc[jnp.minimum(i, 1)]
            m = jnp.max(s, axis=-1, keepdims=True)
            pr = jnp.exp2(s - m).astype(BF16)
            pv = _dot(pr, jnp.concatenate([jnp.concatenate([vp, vc], axis=0), ones], axis=1))
            m_t, acc_t, den_t = per_lane(m), per_lane(pv[:, :LANES]), per_lane(pv[:, LANES:])
            if not first:
                m_old = m_sc[idx, :]
                m_new = jnp.maximum(m_old, m_t)
                a_old, a_t = jnp.exp2(m_old - m_new), jnp.exp2(m_t - m_new)
                den_t = a_old * den_sc[idx, :] + a_t * den_t
                acc_t = a_old * acc_sc[idx, :] + a_t * acc_t
                m_t = m_new
            if last:
                acc_sc[idx, :] = acc_t / den_t
            else:
                m_sc[idx, :] = m_t
                den_sc[idx, :] = den_t
                acc_sc[idx, :] = acc_t
            return kc, vc

        zero = jnp.zeros((blk, LANES), BF16)
        lax.fori_loop(0, T // blk, body, (zero, zero), unroll=unroll)

    ch = 4 * blk
    same_head = (lax.broadcasted_iota(jnp.int32, (LANES, LANES), 0) // dh
                 == lax.broadcasted_iota(jnp.int32, (LANES, LANES), 1) // dh).astype(BF16)
    for c in range(T // ch):
        a = acc_sc[c * ch:(c + 1) * ch, :]
        mean_sq = _dot((a * a).astype(BF16), same_head) * (1.0 / dh)
        o_ref[0, c * ch:(c + 1) * ch, :] = (a * lax.rsqrt(mean_sq + EPS) * g_ref[...]).astype(BF16)


def _dil_attn(q, k, v, g, *, n_heads, dh, unroll):
    B, T, W = q.shape
    blk = ATT_BLOCK
    seq = pl.BlockSpec((1, T, LANES), lambda b, t: (b, 0, t))
    return pl.pallas_call(
        functools.partial(_dil_attn_kernel, n_heads=n_heads, dh=dh, unroll=unroll),
        grid=(B, W // LANES),
        in_specs=[seq, seq, seq, pl.BlockSpec((1, LANES), lambda b, t: (0, t))],
        out_specs=seq,
        out_shape=jax.ShapeDtypeStruct((B, T, W), BF16),
        scratch_shapes=[pltpu.VMEM((T, LANES), F32)] * 3
        + [pltpu.VMEM((2, (LANES // dh) * blk, 2 * blk), F32)],
        compiler_params=_params(("parallel", "arbitrary")),
        name="dil_attn",
    )(q, k, v, g)


def _dil_attn_step_kernel(q_ref, kn_ref, vn_ref, kt_ref, vt_ref, g_ref, o_ref, cnt_sc, bias_sc,
                          *, n_heads, dh, n_q):
    past = kt_ref.shape[2]
    n_new = kn_ref.shape[1]
    W = q_ref.shape[-1]
    rows = n_q * n_heads
    n_cols = past + n_new

    @pl.when(pl.program_id(0) == 0)
    def _():
        r = lax.broadcasted_iota(jnp.int32, (rows, n_cols), 0)
        c = lax.broadcasted_iota(jnp.int32, (rows, n_cols), 1)
        delta = past + r // n_heads - c
        cnt = jnp.zeros((rows, n_cols), F32)
        for win, dil in DIL_PATTERNS:
            cnt = cnt + ((delta >= 0) & (delta <= win) & ((delta & (dil - 1)) == 0)).astype(F32)
        slope = jnp.exp2((-8.0 / n_heads) * ((r % n_heads) + 1).astype(F32))
        cnt_sc[...] = cnt
        bias_sc[...] = jnp.where(cnt > 0.0, -slope * delta.astype(F32), NEG)

    sub = lax.broadcasted_iota(jnp.int32, (n_heads, W), 0)
    lane_head = lax.broadcasted_iota(jnp.int32, (n_heads, W), 1) // dh
    qbd = jnp.concatenate(
        [jnp.where(sub == lane_head, jnp.broadcast_to(q_ref[0, i:i + 1, :], (n_heads, W)), 0.0) for i in range(n_q)],
        axis=0).astype(BF16)
    s = jnp.concatenate([_dot(qbd, kt_ref[0].astype(BF16)), _dot_nt(qbd, kn_ref[0])], axis=1) + bias_sc[...]
    p = jnp.exp(s - jnp.max(s, axis=-1, keepdims=True)) * cnt_sc[...]
    den = jnp.sum(p, axis=-1, keepdims=True)
    p = p.astype(BF16)
    out = (_dot_nt(p[:, :past], vt_ref[0].astype(BF16)) + _dot(p[:, past:], vn_ref[0])) / den
    r = lax.broadcasted_iota(jnp.int32, (rows, W), 0)
    c = lax.broadcasted_iota(jnp.int32, (rows, W), 1)
    out = jnp.where((r % n_heads) == (c // dh), out, 0.0)
    out = out * lax.rsqrt(jnp.sum(out * out, axis=-1, keepdims=True) * (1.0 / dh) + EPS)
    o_ref[0] = jnp.sum(out.reshape(n_q, n_heads, W), axis=1) * g_ref[...]


def _dil_attn_step(q, k_new_pad, v_new_pad, cache_kt, cache_vt, g, *, n_heads, dh):
    B, n_q, W = q.shape
    past = cache_kt.shape[2]
    n_new = k_new_pad.shape[1]
    bmap = lambda b: (b, 0, 0)
    return pl.pallas_call(
        functools.partial(_dil_attn_step_kernel, n_heads=n_heads, dh=dh, n_q=n_q),
        grid=(B,),
        in_specs=[pl.BlockSpec((1, n_q, W), bmap), pl.BlockSpec((1, n_new, W), bmap), pl.BlockSpec((1, n_new, W), bmap),
                  pl.BlockSpec((1, W, past), bmap), pl.BlockSpec((1, W, past), bmap), _const_spec((1, W))],
        out_specs=pl.BlockSpec((1, n_q, W), bmap),
        out_shape=jax.ShapeDtypeStruct((B, n_q, W), F32),
        scratch_shapes=[pltpu.VMEM((n_q * n_heads, past + n_new), F32)] * 2,
        compiler_params=_params(("arbitrary",)),
        name="dil_attn_step",
    )(q, k_new_pad, v_new_pad, cache_kt, cache_vt, g)


def _lower_bound(lbl_ref, layer):
    logits = lbl_ref[...]
    e = jnp.exp(logits - jnp.max(logits, axis=0, keepdims=True))
    return jnp.sum(e[:layer + 1], axis=0, keepdims=True) / jnp.sum(e, axis=0, keepdims=True)


def _cumsum_rows(tril_bf, x):
    x1 = x.astype(BF16)
    r1 = x - x1.astype(F32)
    x2 = r1.astype(BF16)
    x3 = (r1 - x2.astype(F32)).astype(BF16)
    return _dot(tril_bf, x1) + _dot(tril_bf, x2) + _dot(tril_bf, x3)


def _hgrn_rows(hz_ref, o_ref, lb, gn_ref, st_ref, b_sc, c_sc, *, n_heads, dk):
    hw = n_heads * dk
    tc = hz_ref.shape[0]
    ch, sub = HG_CHUNK, HG_SUB
    r_c = lax.broadcasted_iota(jnp.int32, (ch, ch), 0)
    c_c = lax.broadcasted_iota(jnp.int32, (ch, ch), 1)
    tril = (r_c >= c_c).astype(BF16)
    r_s = lax.broadcasted_iota(jnp.int32, (sub, ch), 0)
    c_s = lax.broadcasted_iota(jnp.int32, (sub, ch), 1)

    for c in range(tc // ch):
        rows = slice(c * ch, (c + 1) * ch)
        hq = hz_ref[rows, 0:hw]
        f = lb + (1.0 - lb) * _sigmoid(hz_ref[rows, hw:2 * hw])
        qh = hq * _sigmoid(hq) * dk ** -0.5
        kk = 1.0 - f
        b = _cumsum_rows(tril, jnp.log(f) * LOG2E)
        b_rows, c_rows = b_sc.at[c], c_sc.at[c]
        b_rows[...] = b
        c_rows[...] = b - jnp.log(kk) * LOG2E
        for h in range(n_heads):
            hs = slice(h * dk, (h + 1) * dk)
            bh = b[:, hs]
            kh = kk[:, hs]
            qhh = qh[:, hs]
            vh = hz_ref[rows, 2 * hw + h * dk:2 * hw + (h + 1) * dk].astype(BF16)
            st = st_ref[h]
            o = _dot_nt((qhh * jnp.exp2(bh)).astype(BF16), st.astype(BF16))
            blocks = []
            for j in range(ch // sub):
                r0 = j * sub
                bj = bh[r0:r0 + sub]
                qj = qhh[r0:r0 + sub]
                a_j = jnp.zeros((sub, ch), F32)
                for s in range(sub):
                    a = jnp.sum(qj * jnp.exp2(bj - c_rows[r0 + s:r0 + s + 1, hs]), axis=-1, keepdims=True)
                    a_j = jnp.where(c_s == r0 + s, a, a_j)
                a_j = jnp.where(c_s - r0 <= r_s, a_j, 0.0)
                if j > 0:
                    beta = b_rows[r0 - 1:r0, hs]
                    qt = qj * jnp.exp2(bj - beta)
                    kt = kh[0:r0] * jnp.exp2(beta - bh[0:r0])
                    kt = jnp.concatenate([kt, jnp.zeros((ch - r0, dk), F32)], axis=0)
                    a_j = a_j + _dot_nt(qt.astype(BF16), kt.astype(BF16))
                blocks.append(a_j)
            a_mat = jnp.concatenate(blocks, axis=0)
            o = o + _dot(a_mat.astype(BF16), vh)
            b_last = b_rows[ch - 1:ch, hs]
            khat = (kh * jnp.exp2(b_last - bh)).astype(BF16)
            st_ref[h] = st * jnp.exp2(b_last) + lax.dot_general(vh, khat, TN_DIMS, preferred_element_type=F32)
            hg = hz_ref[rows, 3 * hw + h * dk:3 * hw + (h + 1) * dk]
            o_ref[rows, hs] = (_rms(o, gn_ref[:, hs]) * (hg * _sigmoid(hg))).astype(BF16)


def _in_proj_hgrn_kernel(x_ref, g_ref, w_ref, lbl_ref, gn_ref, q_ref, k_ref, v_ref, kf_ref, vf_ref, o_ref, s_ref,
                         hz_sc, st_ref, b_sc, c_sc, *, att_w, q_scale, n_heads, dk, layer, tiles_per_seq):
    g = pl.program_id(0)
    t_rec = jnp.maximum(g - 1, 0)

    @pl.when(g == 0)
    def _():
        hz_sc[...] = jnp.zeros_like(hz_sc)

    @pl.when(t_rec % tiles_per_seq == 0)
    def _():
        st_ref[...] = jnp.zeros_like(st_ref)

    lb = _lower_bound(lbl_ref, layer)

    def run(slot_w, slot_r):
        h = _rms(x_ref[...], g_ref[...]).astype(BF16)

        def proj(lo, hi):
            return _dot(h, w_ref[:, lo:hi])

        q_ref[...] = proj(0, att_w) * q_scale
        zk = proj(att_w, 2 * att_w)
        k_ref[...] = zk
        kf_ref[0] = zk
        zv = proj(2 * att_w, 3 * att_w)
        v_ref[...] = zv
        vf_ref[0] = zv
        base = 3 * att_w
        step = 4 * LANES
        for c in range(0, hz_sc.shape[-1], step):
            hz_sc[slot_w, :, c:c + step] = proj(base + c, base + c + step)
        _hgrn_rows(hz_sc.at[slot_r], o_ref, lb, gn_ref, st_ref, b_sc, c_sc, n_heads=n_heads, dk=dk)

    @pl.when(g % 2 == 0)
    def _():
        run(0, 1)

    @pl.when(g % 2 == 1)
    def _():
        run(1, 0)

    @pl.when((g > 0) & (t_rec % tiles_per_seq == tiles_per_seq - 1))
    def _():
        for h in range(n_heads):
            s_ref[0, h] = st_ref[h].T


def _in_proj_hgrn(x, g, w_bf, lb_logits, gn, *, att_w, q_scale, n_heads, dk, tm, keep, layer):
    B, T, D = x.shape
    hw = n_heads * dk
    n_t = T // tm
    n_tiles = B * n_t
    n_keep = keep // tm
    first = n_t - n_keep
    proj_tile = lambda s: jnp.minimum(s, n_tiles - 1)
    rec_tile = lambda s: jnp.maximum(s - 1, 0)
    rows = lambda s: (proj_tile(s), 0)
    keep_rows = lambda s: (proj_tile(s) // n_t, jnp.maximum(proj_tile(s) % n_t - first, 0), 0)
    f32 = lambda n, w: jax.ShapeDtypeStruct((n, w), F32)
    return pl.pallas_call(
        functools.partial(_in_proj_hgrn_kernel, att_w=att_w, q_scale=q_scale, n_heads=n_heads, dk=dk, layer=layer,
                          tiles_per_seq=n_t),
        grid=(n_tiles + 1,),
        in_specs=[pl.BlockSpec((tm, D), rows), _const_spec((1, D)), _const_spec(w_bf.shape),
                  _const_spec(lb_logits.shape), _const_spec((1, hw))],
        out_specs=[pl.BlockSpec((tm, att_w), rows)] * 3 + [pl.BlockSpec((1, tm, att_w), keep_rows)] * 2
        + [pl.BlockSpec((tm, hw), lambda s: (rec_tile(s), 0)),
           pl.BlockSpec((1, n_heads, dk, dk), lambda s: (rec_tile(s) // n_t, 0, 0, 0))],
        out_shape=[f32(B * T, att_w)] * 3 + [jax.ShapeDtypeStruct((B, keep, att_w), F32)] * 2
        + [jax.ShapeDtypeStruct((B * T, hw), BF16), jax.ShapeDtypeStruct((B, n_heads, dk, dk), F32)],
        scratch_shapes=[pltpu.VMEM((2, tm, 4 * hw), F32), pltpu.VMEM((n_heads, dk, dk), F32)]
        + [pltpu.VMEM((tm // HG_CHUNK, HG_CHUNK, hw), F32)] * 2,
        compiler_params=_params(("arbitrary",)),
        name="in_proj_hgrn",
    )(x.reshape(B * T, D), g, w_bf, lb_logits, gn)


def _hgrn_step_kernel(hz_ref, s0_ref, lbl_ref, gn_ref, o_ref, s_ref, *, n_heads, dk, layer):
    hw = n_heads * dk
    n_t = hz_ref.shape[1]
    lb = _lower_bound(lbl_ref, layer)
    q, k, v, b, gate = [], [], [], [], []
    acc = jnp.zeros((1, hw), F32)
    for t in range(n_t):
        hq = hz_ref[0, t:t + 1, 0:hw]
        f = lb + (1.0 - lb) * _sigmoid(hz_ref[0, t:t + 1, hw:2 * hw])
        acc = acc + jnp.log(f)
        q.append(hq * _sigmoid(hq) * dk ** -0.5)
        k.append(1.0 - f)
        v.append(hz_ref[0, t:t + 1, 2 * hw:3 * hw])
        b.append(acc)
        hg = hz_ref[0, t:t + 1, 3 * hw:4 * hw]
        gate.append(hg * _sigmoid(hg))
    rr = lax.broadcasted_iota(jnp.int32, (dk, dk), 0)
    cc = lax.broadcasted_iota(jnp.int32, (dk, dk), 1)

    def tile_of_rows(rows):
        tile = jnp.zeros((dk, dk), F32)
        for t, r in enumerate(rows):
            tile = jnp.where(rr == t, jnp.broadcast_to(r, (dk, dk)), tile)
        return tile

    for h in range(n_heads):
        hs = slice(h * dk, (h + 1) * dk)
        s0 = s0_ref[0, h]
        qe = tile_of_rows([q[t][:, hs] * jnp.exp(b[t][:, hs]) for t in range(n_t)])
        inter = jnp.dot(qe, s0, precision=HIGHEST, preferred_element_type=F32)
        for t in range(n_t):
            o = inter[t:t + 1]
            for s in range(t + 1):
                a = jnp.sum(q[t][:, hs] * jnp.exp(b[t][:, hs] - b[s][:, hs]) * k[s][:, hs], axis=-1, keepdims=True)
                o = o + a * v[s][:, hs]
            o_ref[0, t:t + 1, hs] = _rms(o, gn_ref[:, hs]) * gate[t][:, hs]
        b_last = b[n_t - 1][:, hs]
        khat = tile_of_rows([k[t][:, hs] * jnp.exp(b_last - b[t][:, hs]) for t in range(n_t)])
        vpad = tile_of_rows([v[t][:, hs] for t in range(n_t)])
        decay = jnp.where(rr == cc, jnp.broadcast_to(jnp.exp(b_last), (dk, dk)), 0.0)
        s_ref[0, h] = (jnp.dot(decay, s0, precision=HIGHEST, preferred_element_type=F32)
                       + jnp.dot(khat.T, vpad, precision=HIGHEST, preferred_element_type=F32))


def _hgrn_step(hz, s0, lb_logits, gn, *, layer):
    B, n_t, _ = hz.shape
    _, n_heads, dk, dv = s0.shape
    hw = n_heads * dk
    return pl.pallas_call(
        functools.partial(_hgrn_step_kernel, n_heads=n_heads, dk=dk, layer=layer),
        grid=(B,),
        in_specs=[pl.BlockSpec((1, n_t, 4 * hw), lambda b: (b, 0, 0)),
                  pl.BlockSpec((1, n_heads, dk, dv), lambda b: (b, 0, 0, 0)),
                  _const_spec(lb_logits.shape), _const_spec((1, hw))],
        out_specs=[pl.BlockSpec((1, n_t, hw), lambda b: (b, 0, 0)),
                   pl.BlockSpec((1, n_heads, dk, dv), lambda b: (b, 0, 0, 0))],
        out_shape=[jax.ShapeDtypeStruct((B, n_t, hw), F32), jax.ShapeDtypeStruct((B, n_heads, dk, dv), F32)],
        compiler_params=_params(("parallel",)),
        name="hgrn_step",
    )(hz, s0, lb_logits, gn)


def _mix_out(x, att_bf, ohg_bf, wo_ref):
    aw = att_bf.shape[-1]
    return x + _dot(att_bf, wo_ref[0:aw, :]) + _dot(ohg_bf, wo_ref[aw:, :])


def _cross_attend(cq, head_k, head_v, n_heads, dh):
    outs = []
    for h in range(n_heads):
        s = _dot_nt((cq[:, h * dh:(h + 1) * dh] * dh ** -0.5).astype(BF16), head_k(h))
        p = jnp.exp(s - jnp.max(s, axis=-1, keepdims=True))
        den = jnp.sum(p, axis=-1, keepdims=True)
        outs.append(_dot(p.astype(BF16), head_v(h)) / den)
    return jnp.concatenate(outs, axis=-1)


def _ffn(u_bf, taps_fn, wg_ref, wu_ref, wd_ref, cw_ref, cb_ref, ff_chunk):
    dff = wg_ref.shape[1]
    n_taps = cw_ref.shape[0]
    acc = jnp.zeros((u_bf.shape[0], wd_ref.shape[1]), F32)
    for n in range(dff // ff_chunk):
        cs = slice(n * ff_chunk, (n + 1) * ff_chunk)
        ug = _dot(u_bf, wg_ref[:, cs])
        taps = taps_fn(ug, cs, n)
        conv = cb_ref[:, cs]
        for j in range(n_taps):
            conv = conv + cw_ref[j:j + 1, cs] * taps[j]
        act = conv * _sigmoid(conv) * _dot(u_bf, wu_ref[:, cs])
        acc = acc + _dot(act.astype(BF16), wd_ref[cs, :])
    return acc


def _post_kernel(x_ref, att_ref, ohg_ref, mk_ref, mv_ref, cinit_ref, gc_ref, gf_ref, gl_ref, wo_ref, wq_ref,
                 wc_ref, wg_ref, wu_ref, wd_ref, cw_ref, cb_ref, y_ref, cst_ref, buf_ref, carry_ref,
                 *, x_heads, x_dh, ff_chunk, n_sub):
    i = pl.program_id(1)
    tm = x_ref.shape[1]
    pad = SUBLANES
    n_taps = cw_ref.shape[0]

    @pl.when(i == 0)
    def _():
        carry_ref[...] = cinit_ref[0]

    ts = tm // n_sub
    for t in range(n_sub):
        rows = slice(t * ts, (t + 1) * ts)
        x1 = _mix_out(x_ref[0, rows], att_ref[0, rows], ohg_ref[0, rows], wo_ref)
        cq = _dot(_rms(x1, gc_ref[...]).astype(BF16), wq_ref[...])
        co = _cross_attend(cq, lambda h: mk_ref[0, :, h * x_dh:(h + 1) * x_dh],
                           lambda h: mv_ref[0, :, h * x_dh:(h + 1) * x_dh], x_heads, x_dh)
        x2 = x1 + _dot(co.astype(BF16), wc_ref[...])

        def taps_fn(ug, cs, n, t=t):
            buf = buf_ref.at[t, n % 2]
            buf[0:pad, :] = carry_ref[:, cs]
            buf[pad:pad + ts, :] = ug
            carry_ref[:, cs] = ug[ts - pad:ts]
            return [buf[pad - (n_taps - 1 - j):pad - (n_taps - 1 - j) + ts, :] for j in range(n_taps - 1)] + [ug]

        u_bf = _rms(x2, gf_ref[...]).astype(BF16)
        x3 = x2 + _ffn(u_bf, taps_fn, wg_ref, wu_ref, wd_ref, cw_ref, cb_ref, ff_chunk)
        y_ref[0, rows] = _rms(x3, gl_ref[...])
    cst_ref[0] = carry_ref[...]


def _post(x, att, ohg, mk_bf, mv_bf, cinit, gains, weights, conv_w, conv_b, *, tm, n_sub, x_heads, ff_chunk):
    B, T, D = x.shape
    aw, hw = att.shape[-1], ohg.shape[-1]
    n_mem, xw = mk_bf.shape[1:]
    dff = conv_w.shape[1]
    row = lambda b, i: (b, i, 0)
    bat = lambda b, i: (b, 0, 0)
    in_specs = ([pl.BlockSpec((1, tm, D), row), pl.BlockSpec((1, tm, aw), row), pl.BlockSpec((1, tm, hw), row)]
                + [pl.BlockSpec((1, n_mem, xw), bat)] * 2 + [pl.BlockSpec((1, SUBLANES, dff), bat)]
                + [_const_spec(a.shape) for a in tuple(gains) + tuple(weights) + (conv_w, conv_b)])
    return pl.pallas_call(
        functools.partial(_post_kernel, x_heads=x_heads, x_dh=xw // x_heads, ff_chunk=ff_chunk, n_sub=n_sub),
        grid=(B, T // tm),
        in_specs=in_specs,
        out_specs=[pl.BlockSpec((1, tm, D), row), pl.BlockSpec((1, SUBLANES, dff), bat)],
        out_shape=[jax.ShapeDtypeStruct((B, T, D), F32), jax.ShapeDtypeStruct((B, SUBLANES, dff), F32)],
        scratch_shapes=[pltpu.VMEM((n_sub, 2, tm // n_sub + SUBLANES, ff_chunk), F32),
                        pltpu.VMEM((SUBLANES, dff), F32)],
        compiler_params=_params(("parallel", "arbitrary")),
        name="post_mixer",
    )(x, att, ohg, mk_bf, mv_bf, cinit, *gains, *weights, conv_w, conv_b)


def _step_mix_kernel(x_ref, att_ref, ohg_ref, gc_ref, wo_ref, wq_ref, x1_ref, cq_ref):
    x1 = _mix_out(x_ref[...], att_ref[...].astype(BF16), ohg_ref[...].astype(BF16), wo_ref)
    x1_ref[...] = x1
    cq_ref[...] = _dot(_rms(x1, gc_ref[...]).astype(BF16), wq_ref[...])


def _step_cross_kernel(cq_ref, mk_ref, mv_ref, co_ref, *, n_q):
    _, n_mem, n_heads, dh = mk_ref.shape
    mk = mk_ref[0].reshape(n_mem * n_heads, dh).astype(BF16)
    mv = mv_ref[0].reshape(n_mem * n_heads, dh).astype(BF16)
    s = _dot_nt((cq_ref[0] * dh ** -0.5).astype(BF16), mk)
    row = lax.broadcasted_iota(jnp.int32, s.shape, 0)
    col = lax.broadcasted_iota(jnp.int32, s.shape, 1)
    s = jnp.where((col % n_heads) == (row // n_q), s, NEG)
    p = jnp.exp(s - jnp.max(s, axis=-1, keepdims=True))
    co_ref[0] = _dot(p.astype(BF16), mv) / jnp.sum(p, axis=-1, keepdims=True)


def _step_ffn_kernel(x1_ref, co_ref, prev_ref, gf_ref, gl_ref, wc_ref, wg_ref, wu_ref, wd_ref,
                     cw_ref, cb_ref, y_ref, ug_ref, buf_ref, *, ff_chunk, n_t):
    n = x1_ref.shape[0]
    pad = SUBLANES
    n_taps = cw_ref.shape[0]
    x2 = x1_ref[...] + _dot(co_ref[...].astype(BF16), wc_ref[...])
    t_of_row = lax.broadcasted_iota(jnp.int32, (n, ff_chunk), 0) % n_t

    def taps_fn(ug, cs, _):
        ug_ref[:, cs] = ug
        buf_ref[0:pad, cs] = jnp.zeros((pad, ff_chunk), F32)
        buf_ref[pad:pad + n, cs] = ug
        taps = []
        for j in range(n_taps - 1):
            shift = n_taps - 1 - j
            shifted = buf_ref[pad - shift:pad - shift + n, cs]
            taps.append(jnp.where(t_of_row < shift, prev_ref[j, :, cs], shifted))
        return taps + [ug]

    u_bf = _rms(x2, gf_ref[...]).astype(BF16)
    x3 = x2 + _ffn(u_bf, taps_fn, wg_ref, wu_ref, wd_ref, cw_ref, cb_ref, ff_chunk)
    y_ref[...] = _rms(x3, gl_ref[...])


def _single_call(kernel, args, out_shape, name, scratch_shapes=()):
    return pl.pallas_call(
        kernel,
        grid=(1,),
        in_specs=[_const_spec(a.shape) for a in args],
        out_specs=[pl.BlockSpec(o.shape, lambda *_, nd=len(o.shape): (0,) * nd) for o in out_shape],
        out_shape=out_shape,
        scratch_shapes=list(scratch_shapes),
        compiler_params=_params(("arbitrary",)),
        name=name,
    )(*args)


def _step_cross(cq, mem_k, mem_v, *, n_q):
    B, n_rows, dh = cq.shape
    _, n_mem, x_heads, _ = mem_k.shape
    bmap = lambda b: (b, 0, 0)
    mem = pl.BlockSpec((1, n_mem, x_heads, dh), lambda b: (b, 0, 0, 0))
    return pl.pallas_call(
        functools.partial(_step_cross_kernel, n_q=n_q),
        grid=(B,),
        in_specs=[pl.BlockSpec((1, n_rows, dh), bmap), mem, mem],
        out_specs=pl.BlockSpec((1, n_rows, dh), bmap),
        out_shape=jax.ShapeDtypeStruct((B, n_rows, dh), F32),
        compiler_params=_params(("parallel",)),
        name="step_cross",
    )(cq, mem_k, mem_v)


def kernel(x_prompt, x_sample, cache_win_k, cache_win_v, state_hgrn, state_ffn_conv, cache_mem_k, cache_mem_v,
           mem_prompt, hg_lb_logits, norm_mix, w_in, att_out_norm, hg_out_norm, w_out, norm_cross, norm_mem,
           w_cq, w_ck, w_cv, w_co, norm_ffn, w_gate, w_up, conv_w, conv_b, w_down, norm_final):
    Bp, T, D = x_prompt.shape
    Bs, Ts, _ = x_sample.shape
    depth, _, past, att_h, att_dh = cache_win_k.shape
    _, _, hg_h, hg_dk, hg_dv = state_hgrn.shape
    _, _, n_mem, x_h, x_dh = cache_mem_k.shape
    n_taps, dff = conv_w.shape[1:]
    att_w = att_h * att_dh
    hg_w = hg_h * hg_dk
    xw = x_h * x_dh
    keep = min(max(w for w, _ in DIL_PATTERNS), T)
    assert depth == 1 and hg_dk == hg_dv == LANES and x_dh == LANES and LANES % att_dh == 0
    assert all(w // d == ATT_BLOCK for w, d in DIL_PATTERNS) and past >= max(w for w, _ in DIL_PATTERNS)
    d_max = max(d for _, d in DIL_PATTERNS)
    assert T % (ATT_BLOCK * d_max) == 0 and past % d_max == 0 and past % (4 * ATT_BLOCK) == 0
    assert n_taps - 1 <= min(Ts, SUBLANES) and Ts <= SUBLANES
    layer = 0
    ff_chunk = 2 * LANES
    q_scale = att_dh ** -0.5
    row2 = lambda a: a.reshape(1, -1)

    w_in_bf = w_in[layer].astype(BF16)
    w_ckv_bf = jnp.concatenate([w_ck[layer], w_cv[layer]], axis=1).astype(BF16)
    weights = tuple(w[layer].astype(BF16) for w in (w_out, w_cq, w_co, w_gate, w_up, w_down))
    g_mix, g_att, g_hg = row2(norm_mix[layer]), row2(att_out_norm[layer]), row2(hg_out_norm[layer])
    g_cross, g_mem, g_ffn, g_final = (row2(norm_cross[layer]), row2(norm_mem[layer]), row2(norm_ffn[layer]),
                                      row2(norm_final))
    cw, cb = conv_w[layer], row2(conv_b[layer])

    q, k, v, k_keep, v_keep, ohg, s_prompt = _in_proj_hgrn(
        x_prompt, g_mix, w_in_bf, hg_lb_logits, g_hg, att_w=att_w, q_scale=q_scale, n_heads=hg_h, dk=hg_dk,
        tm=512, keep=keep, layer=layer)
    q, k, v = (a.reshape(Bp, T, att_w) for a in (q, k, v))
    ohg = ohg.reshape(Bp, T, hg_w)
    mk, mv, mk_bf, mv_bf = _mem_kv(mem_prompt.reshape(Bp * n_mem, D), g_mem, w_ckv_bf, tm=256)
    att = _dil_attn(q, k, v, g_att, n_heads=att_h, dh=att_dh, unroll=16)
    y_prompt, cst = _post(
        x_prompt, att, ohg, mk_bf.reshape(Bp, n_mem, xw), mv_bf.reshape(Bp, n_mem, xw),
        jnp.zeros((Bp, SUBLANES, dff), F32), (g_cross, g_ffn, g_final), weights, cw, cb,
        tm=512, n_sub=1, x_heads=x_h, ff_chunk=ff_chunk)

    n_s = Bs * Ts
    qs, ks, vs, ks_f, vs_f, hzs = _in_proj(x_sample.reshape(1, n_s, D), g_mix, w_in_bf, att_w=att_w, q_scale=q_scale,
                                           tm=n_s, keep=n_s)
    pad_new = lambda a: jnp.pad(a.reshape(Bs, Ts, att_w).astype(BF16), ((0, 0), (0, LANES - Ts), (0, 0)))
    feature_major = lambda c: jnp.transpose(c, (0, 2, 3, 1)).reshape(Bs, att_w, past)
    att_s = _dil_attn_step(qs.reshape(Bs, Ts, att_w), pad_new(ks), pad_new(vs), feature_major(cache_win_k[layer]),
                           feature_major(cache_win_v[layer]), g_att, n_heads=att_h, dh=att_dh)
    ohg_s, s_sample = _hgrn_step(hzs.reshape(Bs, Ts, 4 * hg_w), state_hgrn[layer], hg_lb_logits, g_hg, layer=layer)
    x1_s, cq_s = _single_call(
        _step_mix_kernel,
        (x_sample.reshape(n_s, D), att_s.reshape(n_s, att_w), ohg_s.reshape(n_s, hg_w), g_cross, weights[0],
         weights[1]),
        [jax.ShapeDtypeStruct((n_s, D), F32), jax.ShapeDtypeStruct((n_s, xw), F32)], "step_mix")
    cq_heads = cq_s.reshape(Bs, Ts, x_h, x_dh).transpose(0, 2, 1, 3).reshape(Bs, x_h * Ts, x_dh)
    co_s = _step_cross(cq_heads, cache_mem_k[layer], cache_mem_v[layer], n_q=Ts)
    co_s = co_s.reshape(Bs, x_h, Ts, x_dh).transpose(0, 2, 1, 3)
    conv_state = state_ffn_conv[layer]
    t_idx = jnp.arange(Ts)
    prev = jnp.stack([jnp.take(conv_state, jnp.clip(j + t_idx, 0, n_taps - 2), axis=1).reshape(n_s, dff)
                      for j in range(n_taps - 1)])
    y_s, ug_s = _single_call(
        functools.partial(_step_ffn_kernel, ff_chunk=ff_chunk, n_t=Ts),
        (x1_s, co_s.reshape(n_s, xw), prev, g_ffn, g_final, weights[2], weights[3], weights[4], weights[5], cw, cb),
        [jax.ShapeDtypeStruct((n_s, D), F32), jax.ShapeDtypeStruct((n_s, dff), F32)], "step_ffn",
        scratch_shapes=[pltpu.VMEM((n_s + SUBLANES, dff), F32)])

    stack = lambda a: a[None]
    return (y_prompt, y_s.reshape(Bs, Ts, D),
            stack(k_keep.reshape(Bp, keep, att_h, att_dh)), stack(v_keep.reshape(Bp, keep, att_h, att_dh)),
            stack(s_prompt), stack(cst[:, SUBLANES - (n_taps - 1):]),
            stack(mk.reshape(Bp, n_mem, x_h, x_dh)), stack(mv.reshape(Bp, n_mem, x_h, x_dh)),
            stack(ks_f.reshape(Bs, Ts, att_h, att_dh)), stack(vs_f.reshape(Bs, Ts, att_h, att_dh)),
            stack(s_sample), stack(ug_s.reshape(Bs, Ts, dff)[:, Ts - (n_taps - 1):]))
```

```python
import functools

import jax
import jax.numpy as jnp
from jax import lax
from jax.experimental import pallas as pl
from jax.experimental.pallas import tpu as pltpu

F32 = jnp.float32
BF16 = jnp.bfloat16
EPS = 1e-6
DIL_PATTERNS = ((128, 1), (512, 4), (2048, 16))
ATT_BLOCK = 128
LANES = 128
SUBLANES = 8
HG_CHUNK = 64
HG_SUB = 8
NEG = -1e30
LOG2E = 1.4426950408889634
HIGHEST = lax.Precision.HIGHEST
NT_DIMS = (((1,), (1,)), ((), ()))
TN_DIMS = (((0,), (0,)), ((), ()))
VMEM_LIMIT = 56 * 1024 * 1024


def _dot(a, b):
    return jnp.dot(a, b, preferred_element_type=F32)


def _dot_nt(a, b):
    return lax.dot_general(a, b, NT_DIMS, preferred_element_type=F32)


def _sigmoid(x):
    return 1.0 / (1.0 + jnp.exp(-x))


def _rms(x, g):
    return x * lax.rsqrt(jnp.mean(x * x, axis=-1, keepdims=True) + EPS) * g


def _const_spec(shape):
    nd = len(shape)
    return pl.BlockSpec(shape, lambda *_: (0,) * nd, pipeline_mode=pl.Buffered(1))


def _params(sem):
    return pltpu.CompilerParams(dimension_semantics=sem, vmem_limit_bytes=VMEM_LIMIT)


def _in_proj_kernel(x_ref, g_ref, w_ref, q_ref, k_ref, v_ref, kf_ref, vf_ref, hz_ref, *, att_w, q_scale):
    h = _rms(x_ref[0], g_ref[...]).astype(BF16)

    def proj(lo, hi):
        return _dot(h, w_ref[:, lo:hi])

    q_ref[0] = proj(0, att_w) * q_scale
    zk = proj(att_w, 2 * att_w)
    k_ref[0] = zk
    kf_ref[0] = zk
    zv = proj(2 * att_w, 3 * att_w)
    v_ref[0] = zv
    vf_ref[0] = zv
    base = 3 * att_w
    step = 4 * LANES
    for c in range(0, hz_ref.shape[-1], step):
        hz_ref[0, :, c:c + step] = proj(base + c, base + c + step)


def _in_proj(x, g, w_bf, *, att_w, q_scale, tm, keep):
    B, T, D = x.shape
    ncols = w_bf.shape[1]
    hzw = ncols - 3 * att_w
    n_t = T // tm
    first = n_t - keep // tm
    row = lambda b, i: (b, i, 0)
    keep_map = lambda b, i: (b, jnp.maximum(i - first, 0), 0)
    return pl.pallas_call(
        functools.partial(_in_proj_kernel, att_w=att_w, q_scale=q_scale),
        grid=(B, n_t),
        in_specs=[pl.BlockSpec((1, tm, D), row), _const_spec((1, D)), _const_spec((D, ncols))],
        out_specs=[pl.BlockSpec((1, tm, att_w), row)] * 3
        + [pl.BlockSpec((1, tm, att_w), keep_map)] * 2
        + [pl.BlockSpec((1, tm, hzw), row)],
        out_shape=[jax.ShapeDtypeStruct((B, T, att_w), F32)] * 3
        + [jax.ShapeDtypeStruct((B, keep, att_w), F32)] * 2
        + [jax.ShapeDtypeStruct((B, T, hzw), F32)],
        compiler_params=_params(("parallel", "arbitrary")),
        name="in_proj",
    )(x, g, w_bf)


def _mem_kv_kernel(m_ref, g_ref, w_ref, kf_ref, vf_ref, kb_ref, vb_ref):
    h = _rms(m_ref[...], g_ref[...]).astype(BF16)
    xw = w_ref.shape[1] // 2
    k = _dot(h, w_ref[:, :xw])
    v = _dot(h, w_ref[:, xw:])
    kf_ref[...] = k
    vf_ref[...] = v
    kb_ref[...] = k.astype(BF16)
    vb_ref[...] = v.astype(BF16)


def _mem_kv(mem2d, g, w_ckv_bf, *, tm):
    n, D = mem2d.shape
    xw = w_ckv_bf.shape[1] // 2
    row = lambda i: (i, 0)
    return pl.pallas_call(
        _mem_kv_kernel,
        grid=(n // tm,),
        in_specs=[pl.BlockSpec((tm, D), row), _const_spec((1, D)), _const_spec((D, 2 * xw))],
        out_specs=[pl.BlockSpec((tm, xw), row)] * 4,
        out_shape=[jax.ShapeDtypeStruct((n, xw), F32)] * 2 + [jax.ShapeDtypeStruct((n, xw), BF16)] * 2,
        compiler_params=_params(("parallel",)),
        name="mem_kv",
    )(mem2d, g, w_ckv_bf)


def _dil_attn_kernel(q_ref, k_ref, v_ref, g_ref, o_ref, acc_sc, m_sc, den_sc, bias_sc, *, n_heads, dh, unroll):
    tile = pl.program_id(1)
    T = q_ref.shape[1]
    blk = ATT_BLOCK
    hpt = LANES // dh
    rows, ncol = hpt * blk, 2 * blk
    rr = lax.broadcasted_iota(jnp.int32, (rows, ncol), 0)
    cc = lax.broadcasted_iota(jnp.int32, (rows, ncol), 1)
    delta = (rr % blk) + blk - cc
    in_band = (delta >= 0) & (delta <= blk)
    slope = jnp.exp2((-8.0 / n_heads) * (tile * hpt + rr // blk + 1).astype(F32))
    lane_head = lax.broadcasted_iota(jnp.int32, (blk, LANES), 1) // dh
    n_pat = len(DIL_PATTERNS)

    def per_lane(x):
        out = jnp.broadcast_to(x[0:blk], (blk, LANES))
        for e in range(1, hpt):
            out = jnp.where(lane_head == e, jnp.broadcast_to(x[e * blk:(e + 1) * blk], (blk, LANES)), out)
        return out

    ones = jnp.ones((ncol, LANES), BF16)
    order = sorted(range(n_pat), key=lambda p: -DIL_PATTERNS[p][1])
    for step, p in enumerate(order):
        dil = DIL_PATTERNS[p][1]
        alibi = (-LOG2E) * slope * (delta * dil).astype(F32)
        bias_sc[0] = jnp.where(in_band & (cc >= blk), alibi, NEG)
        bias_sc[1] = jnp.where(in_band, alibi, NEG)
        nblk = T // (blk * dil)

        def body(n, carry, first=step == 0, last=step == n_pat - 1, dil=dil, nblk=nblk):
            kp, vp = carry
            i = n % nblk
            start = n // nblk + i * (blk * dil)
            idx = pl.ds(start, blk, stride=dil) if dil > 1 else pl.ds(pl.multiple_of(start, blk), blk)
            q2 = q_ref[0, idx, :] * LOG2E
            kc = k_ref[0, idx, :].astype(BF16)
            vc = v_ref[0, idx, :].astype(BF16)
            qs = jnp.concatenate([jnp.where(lane_head == e, q2, 0.0) for e in range(hpt)], axis=0).astype(BF16)
            s = _dot_nt(qs, jnp.concatenate([kp, kc], axis=0)) + bias_sc[jnp.minimum(i, 1)]
            m = jnp.max(s, axis=-1, keepdims=True)
            pr = jnp.exp2(s - m).astype(BF16)
            pv = _dot(pr, jnp.concatenate([jnp.concatenate([vp, vc], axis=0), ones], axis=1))
            m_t, acc_t, den_t = per_lane(m), per_lane(pv[:, :LANES]), per_lane(pv[:, LANES:])
            if not first:
                m_old = m_sc[idx, :]
                m_new = jnp.maximum(m_old, m_t)
                a_old, a_t = jnp.exp2(m_old - m_new), jnp.exp2(m_t - m_new)
                den_t = a_old * den_sc[idx, :] + a_t * den_t
                acc_t = a_old * acc_sc[idx, :] + a_t * acc_t
                m_t = m_new
            if last:
                acc_sc[idx, :] = acc_t / den_t
            else:
                m_sc[idx, :] = m_t
                den_sc[idx, :] = den_t
                acc_sc[idx, :] = acc_t
            return kc, vc

        zero = jnp.zeros((blk, LANES), BF16)
        lax.fori_loop(0, T // blk, body, (zero, zero), unroll=unroll)

    ch = 4 * blk
    same_head = (lax.broadcasted_iota(jnp.int32, (LANES, LANES), 0) // dh
                 == lax.broadcasted_iota(jnp.int32, (LANES, LANES), 1) // dh).astype(BF16)
    for c in range(T // ch):
        a = acc_sc[c * ch:(c + 1) * ch, :]
        mean_sq = _dot((a * a).astype(BF16), same_head) * (1.0 / dh)
        o_ref[0, c * ch:(c + 1) * ch, :] = (a * lax.rsqrt(mean_sq + EPS) * g_ref[...]).astype(BF16)


def _dil_attn(q, k, v, g, *, n_heads, dh, unroll):
    B, T, W = q.shape
    blk = ATT_BLOCK
    seq = pl.BlockSpec((1, T, LANES), lambda b, t: (b, 0, t))
    return pl.pallas_call(
        functools.partial(_dil_attn_kernel, n_heads=n_heads, dh=dh, unroll=unroll),
        grid=(B, W // LANES),
        in_specs=[seq, seq, seq, pl.BlockSpec((1, LANES), lambda b, t: (0, t))],
        out_specs=seq,
        out_shape=jax.ShapeDtypeStruct((B, T, W), BF16),
        scratch_shapes=[pltpu.VMEM((T, LANES), F32)] * 3
        + [pltpu.VMEM((2, (LANES // dh) * blk, 2 * blk), F32)],
        compiler_params=_params(("parallel", "arbitrary")),
        name="dil_attn",
    )(q, k, v, g)


def _dil_attn_step_kernel(q_ref, kn_ref, vn_ref, kt_ref, vt_ref, g_ref, o_ref, cnt_sc, bias_sc,
                          *, n_heads, dh, n_q):
    past = kt_ref.shape[2]
    n_new = kn_ref.shape[1]
    W = q_ref.shape[-1]
    rows = n_q * n_heads
    n_cols = past + n_new

    @pl.when(pl.program_id(0) == 0)
    def _():
        r = lax.broadcasted_iota(jnp.int32, (rows, n_cols), 0)
        c = lax.broadcasted_iota(jnp.int32, (rows, n_cols), 1)
        delta = past + r // n_heads - c
        cnt = jnp.zeros((rows, n_cols), F32)
        for win, dil in DIL_PATTERNS:
            cnt = cnt + ((delta >= 0) & (delta <= win) & ((delta & (dil - 1)) == 0)).astype(F32)
        slope = jnp.exp2((-8.0 / n_heads) * ((r % n_heads) + 1).astype(F32))
        cnt_sc[...] = cnt
        bias_sc[...] = jnp.where(cnt > 0.0, -slope * delta.astype(F32), NEG)

    sub = lax.broadcasted_iota(jnp.int32, (n_heads, W), 0)
    lane_head = lax.broadcasted_iota(jnp.int32, (n_heads, W), 1) // dh
    qbd = jnp.concatenate(
        [jnp.where(sub == lane_head, jnp.broadcast_to(q_ref[0, i:i + 1, :], (n_heads, W)), 0.0) for i in range(n_q)],
        axis=0).astype(BF16)
    s = jnp.concatenate([_dot(qbd, kt_ref[0].astype(BF16)), _dot_nt(qbd, kn_ref[0])], axis=1) + bias_sc[...]
    p = jnp.exp(s - jnp.max(s, axis=-1, keepdims=True)) * cnt_sc[...]
    den = jnp.sum(p, axis=-1, keepdims=True)
    p = p.astype(BF16)
    out = (_dot_nt(p[:, :past], vt_ref[0].astype(BF16)) + _dot(p[:, past:], vn_ref[0])) / den
    r = lax.broadcasted_iota(jnp.int32, (rows, W), 0)
    c = lax.broadcasted_iota(jnp.int32, (rows, W), 1)
    out = jnp.where((r % n_heads) == (c // dh), out, 0.0)
    out = out * lax.rsqrt(jnp.sum(out * out, axis=-1, keepdims=True) * (1.0 / dh) + EPS)
    o_ref[0] = jnp.sum(out.reshape(n_q, n_heads, W), axis=1) * g_ref[...]


def _lower_bound(lbl_ref, layer):
    logits = lbl_ref[...]
    e = jnp.exp(logits - jnp.max(logits, axis=0, keepdims=True))
    return jnp.sum(e[:layer + 1], axis=0, keepdims=True) / jnp.sum(e, axis=0, keepdims=True)


def _cumsum_rows(tril_bf, x):
    x1 = x.astype(BF16)
    r1 = x - x1.astype(F32)
    x2 = r1.astype(BF16)
    x3 = (r1 - x2.astype(F32)).astype(BF16)
    return _dot(tril_bf, x1) + _dot(tril_bf, x2) + _dot(tril_bf, x3)


def _hgrn_rows(hz_ref, o_ref, lb, gn_ref, st_ref, b_sc, c_sc, *, n_heads, dk):
    hw = n_heads * dk
    tc = hz_ref.shape[0]
    ch, sub = HG_CHUNK, HG_SUB
    r_c = lax.broadcasted_iota(jnp.int32, (ch, ch), 0)
    c_c = lax.broadcasted_iota(jnp.int32, (ch, ch), 1)
    tril = (r_c >= c_c).astype(BF16)
    r_s = lax.broadcasted_iota(jnp.int32, (sub, ch), 0)
    c_s = lax.broadcasted_iota(jnp.int32, (sub, ch), 1)

    local = [[None] * n_heads for _ in range(tc // ch)]
    for c in range(tc // ch):
        rows = slice(c * ch, (c + 1) * ch)
        hq = hz_ref[rows, 0:hw]
        f = lb + (1.0 - lb) * _sigmoid(hz_ref[rows, hw:2 * hw])
        qh = hq * _sigmoid(hq) * dk ** -0.5
        kk = 1.0 - f
        b = _cumsum_rows(tril, jnp.log(f) * LOG2E)
        b_rows, c_rows = b_sc.at[c], c_sc.at[c]
        b_rows[...] = b
        c_rows[...] = b - jnp.log(kk) * LOG2E
        for h in range(n_heads):
            hs = slice(h * dk, (h + 1) * dk)
            bh = b[:, hs]
            kh = kk[:, hs]
            qhh = qh[:, hs]
            vh = hz_ref[rows, 2 * hw + h * dk:2 * hw + (h + 1) * dk].astype(BF16)
            blocks = []
            for j in range(ch // sub):
                r0 = j * sub
                bj = bh[r0:r0 + sub]
                qj = qhh[r0:r0 + sub]
                a_j = jnp.zeros((sub, ch), F32)
                for s in range(sub):
                    a = jnp.sum(qj * jnp.exp2(bj - c_rows[r0 + s:r0 + s + 1, hs]), axis=-1, keepdims=True)
                    a_j = jnp.where(c_s == r0 + s, a, a_j)
                if j > 0:
                    beta = b_rows[r0 - 1:r0, hs]
                    qt = qj * jnp.exp2(bj - beta)
                    kt = kh[0:r0] * jnp.exp2(beta - bh[0:r0])
                    kt = jnp.concatenate([kt, jnp.zeros((ch - r0, dk), F32)], axis=0)
                    a_j = a_j + _dot_nt(qt.astype(BF16), kt.astype(BF16))
                blocks.append(jnp.where(c_s - r0 <= r_s, a_j, 0.0))
            b_last = b_rows[ch - 1:ch, hs]
            hg = hz_ref[rows, 3 * hw + h * dk:3 * hw + (h + 1) * dk]
            local[c][h] = dict(
                q=(qhh * jnp.exp2(bh)).astype(BF16), a=jnp.concatenate(blocks, axis=0).astype(BF16), v=vh,
                k=(kh * jnp.exp2(b_last - bh)).astype(BF16), decay=jnp.exp2(b_last), gate=hg * _sigmoid(hg))

    state = [st_ref[h] for h in range(n_heads)]
    for c in range(tc // ch):
        rows = slice(c * ch, (c + 1) * ch)
        for h in range(n_heads):
            hs = slice(h * dk, (h + 1) * dk)
            t = local[c][h]
            o = _dot_nt(t["q"], state[h].astype(BF16)) + _dot(t["a"], t["v"])
            state[h] = state[h] * t["decay"] + lax.dot_general(t["v"], t["k"], TN_DIMS, preferred_element_type=F32)
            o_ref[rows, hs] = (_rms(o, gn_ref[:, hs]) * t["gate"]).astype(BF16)
    for h in range(n_heads):
        st_ref[h] = state[h]


def _in_proj_hgrn_kernel(x_ref, g_ref, w_ref, lbl_ref, gn_ref, q_ref, k_ref, v_ref, kf_ref, vf_ref, o_ref, s_ref,
                         hz_sc, st_ref, b_sc, c_sc, *, att_w, q_scale, n_heads, dk, layer, tiles_per_seq):
    g = pl.program_id(0)
    t_rec = jnp.maximum(g - 1, 0)

    @pl.when(g == 0)
    def _():
        hz_sc[...] = jnp.zeros_like(hz_sc)

    @pl.when(t_rec % tiles_per_seq == 0)
    def _():
        st_ref[...] = jnp.zeros_like(st_ref)

    lb = _lower_bound(lbl_ref, layer)
    slot_w = g % 2
    hz = hz_sc.at[1 - slot_w]

    h = _rms(x_ref[...], g_ref[...]).astype(BF16)

    def proj(lo, hi):
        return _dot(h, w_ref[:, lo:hi])

    q_ref[...] = proj(0, att_w) * q_scale
    zk = proj(att_w, 2 * att_w)
    k_ref[...] = zk
    kf_ref[0] = zk
    zv = proj(2 * att_w, 3 * att_w)
    v_ref[...] = zv
    vf_ref[0] = zv
    base = 3 * att_w
    step = 4 * LANES
    for c in range(0, hz_sc.shape[-1], step):
        hz_sc[slot_w, :, c:c + step] = proj(base + c, base + c + step)

    _hgrn_rows(hz, o_ref, lb, gn_ref, st_ref, b_sc, c_sc, n_heads=n_heads, dk=dk)

    @pl.when((g > 0) & (t_rec % tiles_per_seq == tiles_per_seq - 1))
    def _():
        for h in range(n_heads):
            s_ref[0, h] = st_ref[h].T


def _in_proj_hgrn(x, g, w_bf, lb_logits, gn, *, att_w, q_scale, n_heads, dk, tm, keep, layer):
    B, T, D = x.shape
    hw = n_heads * dk
    n_t = T // tm
    n_tiles = B * n_t
    n_keep = keep // tm
    first = n_t - n_keep
    proj_tile = lambda s: jnp.minimum(s, n_tiles - 1)
    rec_tile = lambda s: jnp.maximum(s - 1, 0)
    rows = lambda s: (proj_tile(s), 0)
    keep_rows = lambda s: (proj_tile(s) // n_t, jnp.maximum(proj_tile(s) % n_t - first, 0), 0)
    f32 = lambda n, w: jax.ShapeDtypeStruct((n, w), F32)
    return pl.pallas_call(
        functools.partial(_in_proj_hgrn_kernel, att_w=att_w, q_scale=q_scale, n_heads=n_heads, dk=dk, layer=layer,
                          tiles_per_seq=n_t),
        grid=(n_tiles + 1,),
        in_specs=[pl.BlockSpec((tm, D), rows), _const_spec((1, D)), _const_spec(w_bf.shape),
                  _const_spec(lb_logits.shape), _const_spec((1, hw))],
        out_specs=[pl.BlockSpec((tm, att_w), rows)] * 3 + [pl.BlockSpec((1, tm, att_w), keep_rows)] * 2
        + [pl.BlockSpec((tm, hw), lambda s: (rec_tile(s), 0)),
           pl.BlockSpec((1, n_heads, dk, dk), lambda s: (rec_tile(s) // n_t, 0, 0, 0))],
        out_shape=[f32(B * T, att_w)] * 3 + [jax.ShapeDtypeStruct((B, keep, att_w), F32)] * 2
        + [jax.ShapeDtypeStruct((B * T, hw), BF16), jax.ShapeDtypeStruct((B, n_heads, dk, dk), F32)],
        scratch_shapes=[pltpu.VMEM((2, tm, 4 * hw), F32), pltpu.VMEM((n_heads, dk, dk), F32)]
        + [pltpu.VMEM((tm // HG_CHUNK, HG_CHUNK, hw), F32)] * 2,
        compiler_params=_params(("arbitrary",)),
        name="in_proj_hgrn",
    )(x.reshape(B * T, D), g, w_bf, lb_logits, gn)


def _hgrn_step_kernel(hz_ref, s0_ref, lbl_ref, gn_ref, o_ref, s_ref, *, n_heads, dk, layer):
    hw = n_heads * dk
    n_t = hz_ref.shape[1]
    lb = _lower_bound(lbl_ref, layer)
    q, k, v, b, gate = [], [], [], [], []
    acc = jnp.zeros((1, hw), F32)
    for t in range(n_t):
        hq = hz_ref[0, t:t + 1, 0:hw]
        f = lb + (1.0 - lb) * _sigmoid(hz_ref[0, t:t + 1, hw:2 * hw])
        acc = acc + jnp.log(f)
        q.append(hq * _sigmoid(hq) * dk ** -0.5)
        k.append(1.0 - f)
        v.append(hz_ref[0, t:t + 1, 2 * hw:3 * hw])
        b.append(acc)
        hg = hz_ref[0, t:t + 1, 3 * hw:4 * hw]
        gate.append(hg * _sigmoid(hg))
    rr = lax.broadcasted_iota(jnp.int32, (dk, dk), 0)
    cc = lax.broadcasted_iota(jnp.int32, (dk, dk), 1)

    def tile_of_rows(rows):
        tile = jnp.zeros((dk, dk), F32)
        for t, r in enumerate(rows):
            tile = jnp.where(rr == t, jnp.broadcast_to(r, (dk, dk)), tile)
        return tile

    for h in range(n_heads):
        hs = slice(h * dk, (h + 1) * dk)
        s0 = s0_ref[0, h]
        qe = tile_of_rows([q[t][:, hs] * jnp.exp(b[t][:, hs]) for t in range(n_t)])
        inter = jnp.dot(qe, s0, precision=HIGHEST, preferred_element_type=F32)
        for t in range(n_t):
            o = inter[t:t + 1]
            for s in range(t + 1):
                a = jnp.sum(q[t][:, hs] * jnp.exp(b[t][:, hs] - b[s][:, hs]) * k[s][:, hs], axis=-1, keepdims=True)
                o = o + a * v[s][:, hs]
            o_ref[0, t:t + 1, hs] = _rms(o, gn_ref[:, hs]) * gate[t][:, hs]
        b_last = b[n_t - 1][:, hs]
        khat = tile_of_rows([k[t][:, hs] * jnp.exp(b_last - b[t][:, hs]) for t in range(n_t)])
        vpad = tile_of_rows([v[t][:, hs] for t in range(n_t)])
        decay = jnp.where(rr == cc, jnp.broadcast_to(jnp.exp(b_last), (dk, dk)), 0.0)
        s_ref[0, h] = (jnp.dot(decay, s0, precision=HIGHEST, preferred_element_type=F32)
                       + jnp.dot(khat.T, vpad, precision=HIGHEST, preferred_element_type=F32))


def _step_mixers_kernel(q_ref, kn_ref, vn_ref, kt_ref, vt_ref, g_ref, hz_ref, s0_ref, lbl_ref, gn_ref,
                        o_ref, oh_ref, s_ref, cnt_sc, bias_sc, *, att_heads, dh, n_q, hg_heads, dk, layer):
    _dil_attn_step_kernel(q_ref, kn_ref, vn_ref, kt_ref, vt_ref, g_ref, o_ref, cnt_sc, bias_sc,
                          n_heads=att_heads, dh=dh, n_q=n_q)
    _hgrn_step_kernel(hz_ref, s0_ref, lbl_ref, gn_ref, oh_ref, s_ref, n_heads=hg_heads, dk=dk, layer=layer)


def _step_mixers(q, k_new_pad, v_new_pad, cache_kt, cache_vt, g_att, hz, s0, lb_logits, g_hg,
                 *, att_heads, dh, layer):
    B, n_q, W = q.shape
    past = cache_kt.shape[2]
    n_new = k_new_pad.shape[1]
    _, hg_heads, dk, dv = s0.shape
    hw = hg_heads * dk
    bmap = lambda b: (b, 0, 0)
    smap = lambda b: (b, 0, 0, 0)
    return pl.pallas_call(
        functools.partial(_step_mixers_kernel, att_heads=att_heads, dh=dh, n_q=n_q, hg_heads=hg_heads, dk=dk,
                          layer=layer),
        grid=(B,),
        in_specs=[pl.BlockSpec((1, n_q, W), bmap), pl.BlockSpec((1, n_new, W), bmap), pl.BlockSpec((1, n_new, W), bmap),
                  pl.BlockSpec((1, W, past), bmap), pl.BlockSpec((1, W, past), bmap), _const_spec((1, W)),
                  pl.BlockSpec((1, n_q, 4 * hw), bmap), pl.BlockSpec((1, hg_heads, dk, dv), smap),
                  _const_spec(lb_logits.shape), _const_spec((1, hw))],
        out_specs=[pl.BlockSpec((1, n_q, W), bmap), pl.BlockSpec((1, n_q, hw), bmap),
                   pl.BlockSpec((1, hg_heads, dk, dv), smap)],
        out_shape=[jax.ShapeDtypeStruct((B, n_q, W), F32), jax.ShapeDtypeStruct((B, n_q, hw), F32),
                   jax.ShapeDtypeStruct((B, hg_heads, dk, dv), F32)],
        scratch_shapes=[pltpu.VMEM((n_q * att_heads, past + n_new), F32)] * 2,
        compiler_params=_params(("arbitrary",)),
        name="step_mixers",
    )(q, k_new_pad, v_new_pad, cache_kt, cache_vt, g_att, hz, s0, lb_logits, g_hg)


def _mix_out(x, att_bf, ohg_bf, wo_ref):
    aw = att_bf.shape[-1]
    return x + _dot(att_bf, wo_ref[0:aw, :]) + _dot(ohg_bf, wo_ref[aw:, :])


def _cross_attend(cq, head_k, head_v, n_heads, dh):
    outs = []
    for h in range(n_heads):
        s = _dot_nt((cq[:, h * dh:(h + 1) * dh] * dh ** -0.5).astype(BF16), head_k(h))
        p = jnp.exp(s - jnp.max(s, axis=-1, keepdims=True))
        den = jnp.sum(p, axis=-1, keepdims=True)
        outs.append(_dot(p.astype(BF16), head_v(h)) / den)
    return jnp.concatenate(outs, axis=-1)


def _ffn(u_bf, taps_fn, wg_ref, wu_ref, wd_ref, cw_ref, cb_ref, ff_chunk):
    dff = wg_ref.shape[1]
    n_taps = cw_ref.shape[0]
    acc = jnp.zeros((u_bf.shape[0], wd_ref.shape[1]), F32)
    for n in range(dff // ff_chunk):
        cs = slice(n * ff_chunk, (n + 1) * ff_chunk)
        ug = _dot(u_bf, wg_ref[:, cs])
        taps = taps_fn(ug, cs, n)
        conv = cb_ref[:, cs]
        for j in range(n_taps):
            conv = conv + cw_ref[j:j + 1, cs] * taps[j]
        act = conv * _sigmoid(conv) * _dot(u_bf, wu_ref[:, cs])
        acc = acc + _dot(act.astype(BF16), wd_ref[cs, :])
    return acc


def _post_kernel(x_ref, att_ref, ohg_ref, mk_ref, mv_ref, cinit_ref, gc_ref, gf_ref, gl_ref, wo_ref, wq_ref,
                 wc_ref, wg_ref, wu_ref, wd_ref, cw_ref, cb_ref, y_ref, cst_ref, buf_ref, carry_ref,
                 *, x_heads, x_dh, ff_chunk, n_sub):
    i = pl.program_id(1)
    tm = x_ref.shape[1]
    pad = SUBLANES
    n_taps = cw_ref.shape[0]

    @pl.when(i == 0)
    def _():
        carry_ref[...] = cinit_ref[0]

    ts = tm // n_sub
    for t in range(n_sub):
        rows = slice(t * ts, (t + 1) * ts)
        x1 = _mix_out(x_ref[0, rows], att_ref[0, rows], ohg_ref[0, rows], wo_ref)
        cq = _dot(_rms(x1, gc_ref[...]).astype(BF16), wq_ref[...])
        co = _cross_attend(cq, lambda h: mk_ref[0, :, h * x_dh:(h + 1) * x_dh],
                           lambda h: mv_ref[0, :, h * x_dh:(h + 1) * x_dh], x_heads, x_dh)
        x2 = x1 + _dot(co.astype(BF16), wc_ref[...])

        def taps_fn(ug, cs, n, t=t):
            buf = buf_ref.at[t, n % 2]
            buf[0:pad, :] = carry_ref[:, cs]
            buf[pad:pad + ts, :] = ug
            carry_ref[:, cs] = ug[ts - pad:ts]
            return [buf[pad - (n_taps - 1 - j):pad - (n_taps - 1 - j) + ts, :] for j in range(n_taps - 1)] + [ug]

        u_bf = _rms(x2, gf_ref[...]).astype(BF16)
        x3 = x2 + _ffn(u_bf, taps_fn, wg_ref, wu_ref, wd_ref, cw_ref, cb_ref, ff_chunk)
        y_ref[0, rows] = _rms(x3, gl_ref[...])
    cst_ref[0] = carry_ref[...]


def _post(x, att, ohg, mk_bf, mv_bf, cinit, gains, weights, conv_w, conv_b, *, tm, n_sub, x_heads, ff_chunk):
    B, T, D = x.shape
    aw, hw = att.shape[-1], ohg.shape[-1]
    n_mem, xw = mk_bf.shape[1:]
    dff = conv_w.shape[1]
    row = lambda b, i: (b, i, 0)
    bat = lambda b, i: (b, 0, 0)
    in_specs = ([pl.BlockSpec((1, tm, D), row), pl.BlockSpec((1, tm, aw), row), pl.BlockSpec((1, tm, hw), row)]
                + [pl.BlockSpec((1, n_mem, xw), bat)] * 2 + [pl.BlockSpec((1, SUBLANES, dff), bat)]
                + [_const_spec(a.shape) for a in tuple(gains) + tuple(weights) + (conv_w, conv_b)])
    return pl.pallas_call(
        functools.partial(_post_kernel, x_heads=x_heads, x_dh=xw // x_heads, ff_chunk=ff_chunk, n_sub=n_sub),
        grid=(B, T // tm),
        in_specs=in_specs,
        out_specs=[pl.BlockSpec((1, tm, D), row), pl.BlockSpec((1, SUBLANES, dff), bat)],
        out_shape=[jax.ShapeDtypeStruct((B, T, D), F32), jax.ShapeDtypeStruct((B, SUBLANES, dff), F32)],
        scratch_shapes=[pltpu.VMEM((n_sub, 2, tm // n_sub + SUBLANES, ff_chunk), F32),
                        pltpu.VMEM((SUBLANES, dff), F32)],
        compiler_params=_params(("parallel", "arbitrary")),
        name="post_mixer",
    )(x, att, ohg, mk_bf, mv_bf, cinit, *gains, *weights, conv_w, conv_b)


def _step_mix_kernel(x_ref, att_ref, ohg_ref, gc_ref, wo_ref, wq_ref, x1_ref, cq_ref):
    x1 = _mix_out(x_ref[...], att_ref[...].astype(BF16), ohg_ref[...].astype(BF16), wo_ref)
    x1_ref[...] = x1
    cq_ref[...] = _dot(_rms(x1, gc_ref[...]).astype(BF16), wq_ref[...])


def _step_cross_kernel(cq_ref, mk_ref, mv_ref, co_ref, *, n_q):
    _, n_mem, n_heads, dh = mk_ref.shape
    mk = mk_ref[0].reshape(n_mem * n_heads, dh).astype(BF16)
    mv = mv_ref[0].reshape(n_mem * n_heads, dh).astype(BF16)
    s = _dot_nt((cq_ref[0] * dh ** -0.5).astype(BF16), mk)
    row = lax.broadcasted_iota(jnp.int32, s.shape, 0)
    col = lax.broadcasted_iota(jnp.int32, s.shape, 1)
    s = jnp.where((col % n_heads) == (row // n_q), s, NEG)
    p = jnp.exp(s - jnp.max(s, axis=-1, keepdims=True))
    co_ref[0] = _dot(p.astype(BF16), mv) / jnp.sum(p, axis=-1, keepdims=True)


def _step_ffn_kernel(x1_ref, co_ref, prev_ref, gf_ref, gl_ref, wc_ref, wg_ref, wu_ref, wd_ref,
                     cw_ref, cb_ref, y_ref, ug_ref, buf_ref, *, ff_chunk, n_t):
    n = x1_ref.shape[0]
    pad = SUBLANES
    n_taps = cw_ref.shape[0]
    x2 = x1_ref[...] + _dot(co_ref[...].astype(BF16), wc_ref[...])
    t_of_row = lax.broadcasted_iota(jnp.int32, (n, ff_chunk), 0) % n_t

    def taps_fn(ug, cs, _):
        ug_ref[:, cs] = ug
        buf_ref[0:pad, cs] = jnp.zeros((pad, ff_chunk), F32)
        buf_ref[pad:pad + n, cs] = ug
        taps = []
        for j in range(n_taps - 1):
            shift = n_taps - 1 - j
            shifted = buf_ref[pad - shift:pad - shift + n, cs]
            taps.append(jnp.where(t_of_row < shift, prev_ref[j, :, cs], shifted))
        return taps + [ug]

    u_bf = _rms(x2, gf_ref[...]).astype(BF16)
    x3 = x2 + _ffn(u_bf, taps_fn, wg_ref, wu_ref, wd_ref, cw_ref, cb_ref, ff_chunk)
    y_ref[...] = _rms(x3, gl_ref[...])


def _single_call(kernel, args, out_shape, name, scratch_shapes=()):
    return pl.pallas_call(
        kernel,
        grid=(1,),
        in_specs=[_const_spec(a.shape) for a in args],
        out_specs=[pl.BlockSpec(o.shape, lambda *_, nd=len(o.shape): (0,) * nd) for o in out_shape],
        out_shape=out_shape,
        scratch_shapes=list(scratch_shapes),
        compiler_params=_params(("arbitrary",)),
        name=name,
    )(*args)


def _step_cross(cq, mem_k, mem_v, *, n_q):
    B, n_rows, dh = cq.shape
    _, n_mem, x_heads, _ = mem_k.shape
    bmap = lambda b: (b, 0, 0)
    mem = pl.BlockSpec((1, n_mem, x_heads, dh), lambda b: (b, 0, 0, 0))
    return pl.pallas_call(
        functools.partial(_step_cross_kernel, n_q=n_q),
        grid=(B,),
        in_specs=[pl.BlockSpec((1, n_rows, dh), bmap), mem, mem],
        out_specs=pl.BlockSpec((1, n_rows, dh), bmap),
        out_shape=jax.ShapeDtypeStruct((B, n_rows, dh), F32),
        compiler_params=_params(("parallel",)),
        name="step_cross",
    )(cq, mem_k, mem_v)


def kernel(x_prompt, x_sample, cache_win_k, cache_win_v, state_hgrn, state_ffn_conv, cache_mem_k, cache_mem_v,
           mem_prompt, hg_lb_logits, norm_mix, w_in, att_out_norm, hg_out_norm, w_out, norm_cross, norm_mem,
           w_cq, w_ck, w_cv, w_co, norm_ffn, w_gate, w_up, conv_w, conv_b, w_down, norm_final):
    Bp, T, D = x_prompt.shape
    Bs, Ts, _ = x_sample.shape
    depth, _, past, att_h, att_dh = cache_win_k.shape
    _, _, hg_h, hg_dk, hg_dv = state_hgrn.shape
    _, _, n_mem, x_h, x_dh = cache_mem_k.shape
    n_taps, dff = conv_w.shape[1:]
    att_w = att_h * att_dh
    hg_w = hg_h * hg_dk
    xw = x_h * x_dh
    keep = min(max(w for w, _ in DIL_PATTERNS), T)
    assert depth == 1 and hg_dk == hg_dv == LANES and x_dh == LANES and LANES % att_dh == 0
    assert all(w // d == ATT_BLOCK for w, d in DIL_PATTERNS) and past >= max(w for w, _ in DIL_PATTERNS)
    d_max = max(d for _, d in DIL_PATTERNS)
    assert T % (ATT_BLOCK * d_max) == 0 and past % d_max == 0 and past % (4 * ATT_BLOCK) == 0
    assert n_taps - 1 <= min(Ts, SUBLANES) and Ts <= SUBLANES
    layer = 0
    ff_chunk = 2 * LANES
    q_scale = att_dh ** -0.5
    row2 = lambda a: a.reshape(1, -1)

    w_in_bf = w_in[layer].astype(BF16)
    w_ckv_bf = jnp.concatenate([w_ck[layer], w_cv[layer]], axis=1).astype(BF16)
    weights = tuple(w[layer].astype(BF16) for w in (w_out, w_cq, w_co, w_gate, w_up, w_down))
    g_mix, g_att, g_hg = row2(norm_mix[layer]), row2(att_out_norm[layer]), row2(hg_out_norm[layer])
    g_cross, g_mem, g_ffn, g_final = (row2(norm_cross[layer]), row2(norm_mem[layer]), row2(norm_ffn[layer]),
                                      row2(norm_final))
    cw, cb = conv_w[layer], row2(conv_b[layer])

    q, k, v, k_keep, v_keep, ohg, s_prompt = _in_proj_hgrn(
        x_prompt, g_mix, w_in_bf, hg_lb_logits, g_hg, att_w=att_w, q_scale=q_scale, n_heads=hg_h, dk=hg_dk,
        tm=512, keep=keep, layer=layer)
    q, k, v = (a.reshape(Bp, T, att_w) for a in (q, k, v))
    ohg = ohg.reshape(Bp, T, hg_w)
    mk, mv, mk_bf, mv_bf = _mem_kv(mem_prompt.reshape(Bp * n_mem, D), g_mem, w_ckv_bf, tm=256)
    att = _dil_attn(q, k, v, g_att, n_heads=att_h, dh=att_dh, unroll=16)
    y_prompt, cst = _post(
        x_prompt, att, ohg, mk_bf.reshape(Bp, n_mem, xw), mv_bf.reshape(Bp, n_mem, xw),
        jnp.zeros((Bp, SUBLANES, dff), F32), (g_cross, g_ffn, g_final), weights, cw, cb,
        tm=512, n_sub=1, x_heads=x_h, ff_chunk=ff_chunk)

    n_s = Bs * Ts
    qs, ks, vs, ks_f, vs_f, hzs = _in_proj(x_sample.reshape(1, n_s, D), g_mix, w_in_bf, att_w=att_w, q_scale=q_scale,
                                           tm=n_s, keep=n_s)
    pad_new = lambda a: jnp.pad(a.reshape(Bs, Ts, att_w).astype(BF16), ((0, 0), (0, LANES - Ts), (0, 0)))
    feature_major = lambda c: jnp.transpose(c, (0, 2, 3, 1)).reshape(Bs, att_w, past)
    att_s, ohg_s, s_sample = _step_mixers(
        qs.reshape(Bs, Ts, att_w), pad_new(ks), pad_new(vs), feature_major(cache_win_k[layer]),
        feature_major(cache_win_v[layer]), g_att, hzs.reshape(Bs, Ts, 4 * hg_w), state_hgrn[layer], hg_lb_logits,
        g_hg, att_heads=att_h, dh=att_dh, layer=layer)
    x1_s, cq_s = _single_call(
        _step_mix_kernel,
        (x_sample.reshape(n_s, D), att_s.reshape(n_s, att_w), ohg_s.reshape(n_s, hg_w), g_cross, weights[0],
         weights[1]),
        [jax.ShapeDtypeStruct((n_s, D), F32), jax.ShapeDtypeStruct((n_s, xw), F32)], "step_mix")
    cq_heads = cq_s.reshape(Bs, Ts, x_h, x_dh).transpose(0, 2, 1, 3).reshape(Bs, x_h * Ts, x_dh)
    co_s = _step_cross(cq_heads, cache_mem_k[layer], cache_mem_v[layer], n_q=Ts)
    co_s = co_s.reshape(Bs, x_h, Ts, x_dh).transpose(0, 2, 1, 3)
    conv_state = state_ffn_conv[layer]
    t_idx = jnp.arange(Ts)
    prev = jnp.stack([jnp.take(conv_state, jnp.clip(j + t_idx, 0, n_taps - 2), axis=1).reshape(n_s, dff)
                      for j in range(n_taps - 1)])
    y_s, ug_s = _single_call(
        functools.partial(_step_ffn_kernel, ff_chunk=ff_chunk, n_t=Ts),
        (x1_s, co_s.reshape(n_s, xw), prev, g_ffn, g_final, weights[2], weights[3], weights[4], weights[5], cw, cb),
        [jax.ShapeDtypeStruct((n_s, D), F32), jax.ShapeDtypeStruct((n_s, dff), F32)], "step_ffn",
        scratch_shapes=[pltpu.VMEM((n_s + SUBLANES, dff), F32)])

    stack = lambda a: a[None]
    return (y_prompt, y_s.reshape(Bs, Ts, D),
            stack(k_keep.reshape(Bp, keep, att_h, att_dh)), stack(v_keep.reshape(Bp, keep, att_h, att_dh)),
            stack(s_prompt), stack(cst[:, SUBLANES - (n_taps - 1):]),
            stack(mk.reshape(Bp, n_mem, x_h, x_dh)), stack(mv.reshape(Bp, n_mem, x_h, x_dh)),
            stack(ks_f.reshape(Bs, Ts, att_h, att_dh)), stack(vs_f.reshape(Bs, Ts, att_h, att_dh)),
            stack(s_sample), stack(ug_s.reshape(Bs, Ts, dff)[:, Ts - (n_taps - 1):]))
```

```python
import functools

import jax
import jax.numpy as jnp
from jax import lax
from jax.experimental import pallas as pl
from jax.experimental.pallas import tpu as pltpu

F32 = jnp.float32
BF16 = jnp.bfloat16
EPS = 1e-6
DIL_PATTERNS = ((128, 1), (512, 4), (2048, 16))
ATT_BLOCK = 128
LANES = 128
SUBLANES = 8
HG_CHUNK = 64
HG_SUB = 8
NEG = -1e30
LOG2E = 1.4426950408889634
HIGHEST = lax.Precision.HIGHEST
NT_DIMS = (((1,), (1,)), ((), ()))
TN_DIMS = (((0,), (0,)), ((), ()))
VMEM_LIMIT = 56 * 1024 * 1024


def _dot(a, b):
    return jnp.dot(a, b, preferred_element_type=F32)


def _dot_nt(a, b):
    return lax.dot_general(a, b, NT_DIMS, preferred_element_type=F32)


def _sigmoid(x):
    return 1.0 / (1.0 + jnp.exp(-x))


def _rms(x, g):
    return x * lax.rsqrt(jnp.mean(x * x, axis=-1, keepdims=True) + EPS) * g


def _const_spec(shape):
    nd = len(shape)
    return pl.BlockSpec(shape, lambda *_: (0,) * nd, pipeline_mode=pl.Buffered(1))


def _params(sem):
    return pltpu.CompilerParams(dimension_semantics=sem, vmem_limit_bytes=VMEM_LIMIT)


def _in_proj_kernel(x_ref, g_ref, w_ref, q_ref, k_ref, v_ref, kf_ref, vf_ref, hz_ref, *, att_w, q_scale):
    h = _rms(x_ref[0], g_ref[...]).astype(BF16)

    def proj(lo, hi):
        return _dot(h, w_ref[:, lo:hi])

    q_ref[0] = proj(0, att_w) * q_scale
    zk = proj(att_w, 2 * att_w)
    k_ref[0] = zk
    kf_ref[0] = zk
    zv = proj(2 * att_w, 3 * att_w)
    v_ref[0] = zv
    vf_ref[0] = zv
    base = 3 * att_w
    step = 4 * LANES
    for c in range(0, hz_ref.shape[-1], step):
        hz_ref[0, :, c:c + step] = proj(base + c, base + c + step)


def _in_proj(x, g, w_bf, *, att_w, q_scale, tm, keep):
    B, T, D = x.shape
    ncols = w_bf.shape[1]
    hzw = ncols - 3 * att_w
    n_t = T // tm
    first = n_t - keep // tm
    row = lambda b, i: (b, i, 0)
    keep_map = lambda b, i: (b, jnp.maximum(i - first, 0), 0)
    return pl.pallas_call(
        functools.partial(_in_proj_kernel, att_w=att_w, q_scale=q_scale),
        grid=(B, n_t),
        in_specs=[pl.BlockSpec((1, tm, D), row), _const_spec((1, D)), _const_spec((D, ncols))],
        out_specs=[pl.BlockSpec((1, tm, att_w), row)] * 3
        + [pl.BlockSpec((1, tm, att_w), keep_map)] * 2
        + [pl.BlockSpec((1, tm, hzw), row)],
        out_shape=[jax.ShapeDtypeStruct((B, T, att_w), F32)] * 3
        + [jax.ShapeDtypeStruct((B, keep, att_w), F32)] * 2
        + [jax.ShapeDtypeStruct((B, T, hzw), F32)],
        compiler_params=_params(("parallel", "arbitrary")),
        name="in_proj",
    )(x, g, w_bf)


def _mem_kv_kernel(m_ref, g_ref, w_ref, kf_ref, vf_ref, kb_ref, vb_ref):
    h = _rms(m_ref[...], g_ref[...]).astype(BF16)
    xw = w_ref.shape[1] // 2
    k = _dot(h, w_ref[:, :xw])
    v = _dot(h, w_ref[:, xw:])
    kf_ref[...] = k
    vf_ref[...] = v
    kb_ref[...] = k.astype(BF16)
    vb_ref[...] = v.astype(BF16)


def _mem_kv(mem2d, g, w_ckv_bf, *, tm):
    n, D = mem2d.shape
    xw = w_ckv_bf.shape[1] // 2
    row = lambda i: (i, 0)
    return pl.pallas_call(
        _mem_kv_kernel,
        grid=(n // tm,),
        in_specs=[pl.BlockSpec((tm, D), row), _const_spec((1, D)), _const_spec((D, 2 * xw))],
        out_specs=[pl.BlockSpec((tm, xw), row)] * 4,
        out_shape=[jax.ShapeDtypeStruct((n, xw), F32)] * 2 + [jax.ShapeDtypeStruct((n, xw), BF16)] * 2,
        compiler_params=_params(("parallel",)),
        name="mem_kv",
    )(mem2d, g, w_ckv_bf)


def _dil_attn_kernel(q_ref, k_ref, v_ref, g_ref, o_ref, acc_sc, m_sc, den_sc, bias_sc, *, n_heads, dh, unroll):
    tile = pl.program_id(1)
    T = q_ref.shape[1]
    blk = ATT_BLOCK
    hpt = LANES // dh
    rows, ncol = hpt * blk, 2 * blk
    rr = lax.broadcasted_iota(jnp.int32, (rows, ncol), 0)
    cc = lax.broadcasted_iota(jnp.int32, (rows, ncol), 1)
    delta = (rr % blk) + blk - cc
    in_band = (delta >= 0) & (delta <= blk)
    slope = jnp.exp2((-8.0 / n_heads) * (tile * hpt + rr // blk + 1).astype(F32))
    lane_head = lax.broadcasted_iota(jnp.int32, (blk, LANES), 1) // dh
    n_pat = len(DIL_PATTERNS)

    def per_lane(x):
        out = jnp.broadcast_to(x[0:blk], (blk, LANES))
        for e in range(1, hpt):
            out = jnp.where(lane_head == e, jnp.broadcast_to(x[e * blk:(e + 1) * blk], (blk, LANES)), out)
        return out

    ones = jnp.ones((ncol, LANES), BF16)
    order = sorted(range(n_pat), key=lambda p: -DIL_PATTERNS[p][1])
    for step, p in enumerate(order):
        dil = DIL_PATTERNS[p][1]
        alibi = (-LOG2E) * slope * (delta * dil).astype(F32)
        bias_sc[0] = jnp.where(in_band & (cc >= blk), alibi, NEG)
        bias_sc[1] = jnp.where(in_band, alibi, NEG)
        nblk = T // (blk * dil)

        def body(n, carry, first=step == 0, last=step == n_pat - 1, dil=dil, nblk=nblk):
            kp, vp = carry
            i = n % nblk
            start = n // nblk + i * (blk * dil)
            idx = pl.ds(start, blk, stride=dil) if dil > 1 else pl.ds(pl.multiple_of(start, blk), blk)
            q2 = q_ref[0, idx, :] * LOG2E
            kc = k_ref[0, idx, :].astype(BF16)
            vc = v_ref[0, idx, :].astype(BF16)
            qs = jnp.concatenate([jnp.where(lane_head == e, q2, 0.0) for e in range(hpt)], axis=0).astype(BF16)
            s = _dot_nt(qs, jnp.concatenate([kp, kc], axis=0)) + bias_sc[jnp.minimum(i, 1)]
            m = jnp.max(s, axis=-1, keepdims=True)
            pr = jnp.exp2(s - m).astype(BF16)
            pv = _dot(pr, jnp.concatenate([jnp.concatenate([vp, vc], axis=0), ones], axis=1))
            m_t, acc_t, den_t = per_lane(m), per_lane(pv[:, :LANES]), per_lane(pv[:, LANES:])
            if not first:
                m_old = m_sc[idx, :]
                m_new = jnp.maximum(m_old, m_t)
                a_old, a_t = jnp.exp2(m_old - m_new), jnp.exp2(m_t - m_new)
                den_t = a_old * den_sc[idx, :] + a_t * den_t
                acc_t = a_old * acc_sc[idx, :] + a_t * acc_t
                m_t = m_new
            if last:
                acc_sc[idx, :] = acc_t / den_t
            else:
                m_sc[idx, :] = m_t
                den_sc[idx, :] = den_t
                acc_sc[idx, :] = acc_t
            return kc, vc

        zero = jnp.zeros((blk, LANES), BF16)
        lax.fori_loop(0, T // blk, body, (zero, zero), unroll=unroll)

    ch = 4 * blk
    same_head = (lax.broadcasted_iota(jnp.int32, (LANES, LANES), 0) // dh
                 == lax.broadcasted_iota(jnp.int32, (LANES, LANES), 1) // dh).astype(BF16)
    for c in range(T // ch):
        a = acc_sc[c * ch:(c + 1) * ch, :]
        mean_sq = _dot((a * a).astype(BF16), same_head) * (1.0 / dh)
        o_ref[0, c * ch:(c + 1) * ch, :] = (a * lax.rsqrt(mean_sq + EPS) * g_ref[...]).astype(BF16)


def _dil_attn(q, k, v, g, *, n_heads, dh, unroll):
    B, T, W = q.shape
    blk = ATT_BLOCK
    seq = pl.BlockSpec((1, T, LANES), lambda b, t: (b, 0, t))
    return pl.pallas_call(
        functools.partial(_dil_attn_kernel, n_heads=n_heads, dh=dh, unroll=unroll),
        grid=(B, W // LANES),
        in_specs=[seq, seq, seq, pl.BlockSpec((1, LANES), lambda b, t: (0, t))],
        out_specs=seq,
        out_shape=jax.ShapeDtypeStruct((B, T, W), BF16),
        scratch_shapes=[pltpu.VMEM((T, LANES), F32)] * 3
        + [pltpu.VMEM((2, (LANES // dh) * blk, 2 * blk), F32)],
        compiler_params=_params(("parallel", "arbitrary")),
        name="dil_attn",
    )(q, k, v, g)


def _dil_attn_step_kernel(q_ref, kn_ref, vn_ref, kt_ref, vt_ref, g_ref, o_ref, cnt_sc, bias_sc,
                          *, n_heads, dh, n_q):
    past = kt_ref.shape[2]
    n_new = kn_ref.shape[1]
    W = q_ref.shape[-1]
    rows = n_q * n_heads
    n_cols = past + n_new

    @pl.when(pl.program_id(0) == 0)
    def _():
        r = lax.broadcasted_iota(jnp.int32, (rows, n_cols), 0)
        c = lax.broadcasted_iota(jnp.int32, (rows, n_cols), 1)
        delta = past + r // n_heads - c
        cnt = jnp.zeros((rows, n_cols), F32)
        for win, dil in DIL_PATTERNS:
            cnt = cnt + ((delta >= 0) & (delta <= win) & ((delta & (dil - 1)) == 0)).astype(F32)
        slope = jnp.exp2((-8.0 / n_heads) * ((r % n_heads) + 1).astype(F32))
        cnt_sc[...] = cnt
        bias_sc[...] = jnp.where(cnt > 0.0, -slope * delta.astype(F32), NEG)

    sub = lax.broadcasted_iota(jnp.int32, (n_heads, W), 0)
    lane_head = lax.broadcasted_iota(jnp.int32, (n_heads, W), 1) // dh
    qbd = jnp.concatenate(
        [jnp.where(sub == lane_head, jnp.broadcast_to(q_ref[0, i:i + 1, :], (n_heads, W)), 0.0) for i in range(n_q)],
        axis=0).astype(BF16)
    s = jnp.concatenate([_dot(qbd, kt_ref[0].astype(BF16)), _dot_nt(qbd, kn_ref[0])], axis=1) + bias_sc[...]
    p = jnp.exp(s - jnp.max(s, axis=-1, keepdims=True)) * cnt_sc[...]
    den = jnp.sum(p, axis=-1, keepdims=True)
    p = p.astype(BF16)
    out = (_dot_nt(p[:, :past], vt_ref[0].astype(BF16)) + _dot(p[:, past:], vn_ref[0])) / den
    r = lax.broadcasted_iota(jnp.int32, (rows, W), 0)
    c = lax.broadcasted_iota(jnp.int32, (rows, W), 1)
    out = jnp.where((r % n_heads) == (c // dh), out, 0.0)
    out = out * lax.rsqrt(jnp.sum(out * out, axis=-1, keepdims=True) * (1.0 / dh) + EPS)
    o_ref[0] = jnp.sum(out.reshape(n_q, n_heads, W), axis=1) * g_ref[...]


def _lower_bound(lbl_ref, layer):
    logits = lbl_ref[...]
    e = jnp.exp(logits - jnp.max(logits, axis=0, keepdims=True))
    return jnp.sum(e[:layer + 1], axis=0, keepdims=True) / jnp.sum(e, axis=0, keepdims=True)


def _cumsum_rows(tril_bf, x):
    x1 = x.astype(BF16)
    r1 = x - x1.astype(F32)
    x2 = r1.astype(BF16)
    x3 = (r1 - x2.astype(F32)).astype(BF16)
    return _dot(tril_bf, x1) + _dot(tril_bf, x2) + _dot(tril_bf, x3)


def _hgrn_rows(hz_ref, o_ref, lb, gn_ref, st_ref, b_sc, c_sc, *, n_heads, dk):
    hw = n_heads * dk
    tc = hz_ref.shape[0]
    ch, sub = HG_CHUNK, HG_SUB
    r_c = lax.broadcasted_iota(jnp.int32, (ch, ch), 0)
    c_c = lax.broadcasted_iota(jnp.int32, (ch, ch), 1)
    tril = (r_c >= c_c).astype(BF16)
    r_s = lax.broadcasted_iota(jnp.int32, (sub, ch), 0)
    c_s = lax.broadcasted_iota(jnp.int32, (sub, ch), 1)

    local = [[None] * n_heads for _ in range(tc // ch)]
    for c in range(tc // ch):
        rows = slice(c * ch, (c + 1) * ch)
        hq = hz_ref[rows, 0:hw]
        f = lb + (1.0 - lb) * _sigmoid(hz_ref[rows, hw:2 * hw])
        qh = hq * _sigmoid(hq) * dk ** -0.5
        kk = 1.0 - f
        b = _cumsum_rows(tril, jnp.log(f) * LOG2E)
        b_rows, c_rows = b_sc.at[c], c_sc.at[c]
        b_rows[...] = b
        c_rows[...] = b - jnp.log(kk) * LOG2E
        for h in range(n_heads):
            hs = slice(h * dk, (h + 1) * dk)
            bh = b[:, hs]
            kh = kk[:, hs]
            qhh = qh[:, hs]
            vh = hz_ref[rows, 2 * hw + h * dk:2 * hw + (h + 1) * dk].astype(BF16)
            blocks = []
            for j in range(ch // sub):
                r0 = j * sub
                bj = bh[r0:r0 + sub]
                qj = qhh[r0:r0 + sub]
                a_j = jnp.zeros((sub, ch), F32)
                for s in range(sub):
                    a = jnp.sum(qj * jnp.exp2(bj - c_rows[r0 + s:r0 + s + 1, hs]), axis=-1, keepdims=True)
                    a_j = jnp.where(c_s == r0 + s, a, a_j)
                if j > 0:
                    beta = b_rows[r0 - 1:r0, hs]
                    qt = qj * jnp.exp2(bj - beta)
                    kt = kh[0:r0] * jnp.exp2(beta - bh[0:r0])
                    kt = jnp.concatenate([kt, jnp.zeros((ch - r0, dk), F32)], axis=0)
                    a_j = a_j + _dot_nt(qt.astype(BF16), kt.astype(BF16))
                blocks.append(jnp.where(c_s - r0 <= r_s, a_j, 0.0))
            b_last = b_rows[ch - 1:ch, hs]
            hg = hz_ref[rows, 3 * hw + h * dk:3 * hw + (h + 1) * dk]
            local[c][h] = dict(
                q=(qhh * jnp.exp2(bh)).astype(BF16), a=jnp.concatenate(blocks, axis=0).astype(BF16), v=vh,
                k=(kh * jnp.exp2(b_last - bh)).astype(BF16), decay=jnp.exp2(b_last), gate=hg * _sigmoid(hg))

    state = [st_ref[h] for h in range(n_heads)]
    for c in range(tc // ch):
        rows = slice(c * ch, (c + 1) * ch)
        for h in range(n_heads):
            hs = slice(h * dk, (h + 1) * dk)
            t = local[c][h]
            o = _dot_nt(t["q"], state[h].astype(BF16)) + _dot(t["a"], t["v"])
            state[h] = state[h] * t["decay"] + lax.dot_general(t["v"], t["k"], TN_DIMS, preferred_element_type=F32)
            o_ref[rows, hs] = (_rms(o, gn_ref[:, hs]) * t["gate"]).astype(BF16)
    for h in range(n_heads):
        st_ref[h] = state[h]


def _in_proj_hgrn_kernel(x_ref, g_ref, w_ref, lbl_ref, gn_ref, q_ref, k_ref, v_ref, kf_ref, vf_ref, o_ref, s_ref,
                         hz_sc, st_ref, b_sc, c_sc, *, att_w, q_scale, n_heads, dk, layer, tiles_per_seq):
    g = pl.program_id(0)
    t_rec = jnp.maximum(g - 1, 0)

    @pl.when(g == 0)
    def _():
        hz_sc[...] = jnp.zeros_like(hz_sc)

    @pl.when(t_rec % tiles_per_seq == 0)
    def _():
        st_ref[...] = jnp.zeros_like(st_ref)

    lb = _lower_bound(lbl_ref, layer)
    slot_w = g % 2
    hz = hz_sc.at[1 - slot_w]

    h = _rms(x_ref[...], g_ref[...]).astype(BF16)

    def proj(lo, hi):
        return _dot(h, w_ref[:, lo:hi])

    q_ref[...] = proj(0, att_w) * q_scale
    zk = proj(att_w, 2 * att_w)
    k_ref[...] = zk
    kf_ref[0] = zk
    zv = proj(2 * att_w, 3 * att_w)
    v_ref[...] = zv
    vf_ref[0] = zv
    base = 3 * att_w
    step = 4 * LANES
    for c in range(0, hz_sc.shape[-1], step):
        hz_sc[slot_w, :, c:c + step] = proj(base + c, base + c + step)

    _hgrn_rows(hz, o_ref, lb, gn_ref, st_ref, b_sc, c_sc, n_heads=n_heads, dk=dk)

    @pl.when((g > 0) & (t_rec % tiles_per_seq == tiles_per_seq - 1))
    def _():
        for h in range(n_heads):
            s_ref[0, h] = st_ref[h].T


def _in_proj_hgrn(x, g, w_bf, lb_logits, gn, *, att_w, q_scale, n_heads, dk, tm, keep, layer):
    B, T, D = x.shape
    hw = n_heads * dk
    n_t = T // tm
    n_tiles = B * n_t
    n_keep = keep // tm
    first = n_t - n_keep
    proj_tile = lambda s: jnp.minimum(s, n_tiles - 1)
    rec_tile = lambda s: jnp.maximum(s - 1, 0)
    rows = lambda s: (proj_tile(s), 0)
    keep_rows = lambda s: (proj_tile(s) // n_t, jnp.maximum(proj_tile(s) % n_t - first, 0), 0)
    f32 = lambda n, w: jax.ShapeDtypeStruct((n, w), F32)
    return pl.pallas_call(
        functools.partial(_in_proj_hgrn_kernel, att_w=att_w, q_scale=q_scale, n_heads=n_heads, dk=dk, layer=layer,
                          tiles_per_seq=n_t),
        grid=(n_tiles + 1,),
        in_specs=[pl.BlockSpec((tm, D), rows), _const_spec((1, D)), _const_spec(w_bf.shape),
                  _const_spec(lb_logits.shape), _const_spec((1, hw))],
        out_specs=[pl.BlockSpec((tm, att_w), rows)] * 3 + [pl.BlockSpec((1, tm, att_w), keep_rows)] * 2
        + [pl.BlockSpec((tm, hw), lambda s: (rec_tile(s), 0)),
           pl.BlockSpec((1, n_heads, dk, dk), lambda s: (rec_tile(s) // n_t, 0, 0, 0))],
        out_shape=[f32(B * T, att_w)] * 3 + [jax.ShapeDtypeStruct((B, keep, att_w), F32)] * 2
        + [jax.ShapeDtypeStruct((B * T, hw), BF16), jax.ShapeDtypeStruct((B, n_heads, dk, dk), F32)],
        scratch_shapes=[pltpu.VMEM((2, tm, 4 * hw), F32), pltpu.VMEM((n_heads, dk, dk), F32)]
        + [pltpu.VMEM((tm // HG_CHUNK, HG_CHUNK, hw), F32)] * 2,
        compiler_params=_params(("arbitrary",)),
        name="in_proj_hgrn",
    )(x.reshape(B * T, D), g, w_bf, lb_logits, gn)


def _hgrn_step_kernel(hz_ref, s0_ref, lbl_ref, gn_ref, o_ref, s_ref, *, n_heads, dk, layer):
    hw = n_heads * dk
    n_t = hz_ref.shape[1]
    lb = _lower_bound(lbl_ref, layer)
    q, k, v, b, gate = [], [], [], [], []
    acc = jnp.zeros((1, hw), F32)
    for t in range(n_t):
        hq = hz_ref[0, t:t + 1, 0:hw]
        f = lb + (1.0 - lb) * _sigmoid(hz_ref[0, t:t + 1, hw:2 * hw])
        acc = acc + jnp.log(f)
        q.append(hq * _sigmoid(hq) * dk ** -0.5)
        k.append(1.0 - f)
        v.append(hz_ref[0, t:t + 1, 2 * hw:3 * hw])
        b.append(acc)
        hg = hz_ref[0, t:t + 1, 3 * hw:4 * hw]
        gate.append(hg * _sigmoid(hg))
    rr = lax.broadcasted_iota(jnp.int32, (dk, dk), 0)
    cc = lax.broadcasted_iota(jnp.int32, (dk, dk), 1)

    def tile_of_rows(rows):
        tile = jnp.zeros((dk, dk), F32)
        for t, r in enumerate(rows):
            tile = jnp.where(rr == t, jnp.broadcast_to(r, (dk, dk)), tile)
        return tile

    for h in range(n_heads):
        hs = slice(h * dk, (h + 1) * dk)
        s0 = s0_ref[0, h]
        qe = tile_of_rows([q[t][:, hs] * jnp.exp(b[t][:, hs]) for t in range(n_t)])
        inter = jnp.dot(qe, s0, precision=HIGHEST, preferred_element_type=F32)
        for t in range(n_t):
            o = inter[t:t + 1]
            for s in range(t + 1):
                a = jnp.sum(q[t][:, hs] * jnp.exp(b[t][:, hs] - b[s][:, hs]) * k[s][:, hs], axis=-1, keepdims=True)
                o = o + a * v[s][:, hs]
            o_ref[0, t:t + 1, hs] = _rms(o, gn_ref[:, hs]) * gate[t][:, hs]
        b_last = b[n_t - 1][:, hs]
        khat = tile_of_rows([k[t][:, hs] * jnp.exp(b_last - b[t][:, hs]) for t in range(n_t)])
        vpad = tile_of_rows([v[t][:, hs] for t in range(n_t)])
        decay = jnp.where(rr == cc, jnp.broadcast_to(jnp.exp(b_last), (dk, dk)), 0.0)
        s_ref[0, h] = (jnp.dot(decay, s0, precision=HIGHEST, preferred_element_type=F32)
                       + jnp.dot(khat.T, vpad, precision=HIGHEST, preferred_element_type=F32))


def _step_mixers_kernel(q_ref, kn_ref, vn_ref, kt_ref, vt_ref, g_ref, hz_ref, s0_ref, lbl_ref, gn_ref,
                        o_ref, oh_ref, s_ref, cnt_sc, bias_sc, *, att_heads, dh, n_q, hg_heads, dk, layer):
    _dil_attn_step_kernel(q_ref, kn_ref, vn_ref, kt_ref, vt_ref, g_ref, o_ref, cnt_sc, bias_sc,
                          n_heads=att_heads, dh=dh, n_q=n_q)
    _hgrn_step_kernel(hz_ref, s0_ref, lbl_ref, gn_ref, oh_ref, s_ref, n_heads=hg_heads, dk=dk, layer=layer)


def _step_mixers(q, k_new_pad, v_new_pad, cache_kt, cache_vt, g_att, hz, s0, lb_logits, g_hg,
                 *, att_heads, dh, layer):
    B, n_q, W = q.shape
    past = cache_kt.shape[2]
    n_new = k_new_pad.shape[1]
    _, hg_heads, dk, dv = s0.shape
    hw = hg_heads * dk
    bmap = lambda b: (b, 0, 0)
    smap = lambda b: (b, 0, 0, 0)
    return pl.pallas_call(
        functools.partial(_step_mixers_kernel, att_heads=att_heads, dh=dh, n_q=n_q, hg_heads=hg_heads, dk=dk,
                          layer=layer),
        grid=(B,),
        in_specs=[pl.BlockSpec((1, n_q, W), bmap), pl.BlockSpec((1, n_new, W), bmap), pl.BlockSpec((1, n_new, W), bmap),
                  pl.BlockSpec((1, W, past), bmap), pl.BlockSpec((1, W, past), bmap), _const_spec((1, W)),
                  pl.BlockSpec((1, n_q, 4 * hw), bmap), pl.BlockSpec((1, hg_heads, dk, dv), smap),
                  _const_spec(lb_logits.shape), _const_spec((1, hw))],
        out_specs=[pl.BlockSpec((1, n_q, W), bmap), pl.BlockSpec((1, n_q, hw), bmap),
                   pl.BlockSpec((1, hg_heads, dk, dv), smap)],
        out_shape=[jax.ShapeDtypeStruct((B, n_q, W), F32), jax.ShapeDtypeStruct((B, n_q, hw), F32),
                   jax.ShapeDtypeStruct((B, hg_heads, dk, dv), F32)],
        scratch_shapes=[pltpu.VMEM((n_q * att_heads, past + n_new), F32)] * 2,
        compiler_params=_params(("arbitrary",)),
        name="step_mixers",
    )(q, k_new_pad, v_new_pad, cache_kt, cache_vt, g_att, hz, s0, lb_logits, g_hg)


def _mix_out(x, att_bf, ohg_bf, wo_ref):
    aw = att_bf.shape[-1]
    return x + _dot(att_bf, wo_ref[0:aw, :]) + _dot(ohg_bf, wo_ref[aw:, :])


def _cross_attend(cq, head_k, head_v, n_heads, dh):
    heads = range(n_heads)
    scale = dh ** -0.5 * LOG2E
    scores = [_dot_nt((cq[:, h * dh:(h + 1) * dh] * scale).astype(BF16), head_k(h)) for h in heads]
    probs = [jnp.exp2(s - jnp.max(s, axis=-1, keepdims=True)).astype(BF16) for s in scores]
    ones = jnp.ones((probs[0].shape[1], dh), BF16)
    pvs = [_dot(probs[h], jnp.concatenate([head_v(h), ones], axis=1)) for h in heads]
    return jnp.concatenate([pv[:, :dh] / pv[:, dh:] for pv in pvs], axis=-1)


def _ffn(u_bf, taps_fn, wg_ref, wu_ref, wd_ref, cw_ref, cb_ref, ff_chunk):
    dff = wg_ref.shape[1]
    n_taps = cw_ref.shape[0]
    acc = jnp.zeros((u_bf.shape[0], wd_ref.shape[1]), F32)
    for n in range(dff // ff_chunk):
        cs = slice(n * ff_chunk, (n + 1) * ff_chunk)
        ug = _dot(u_bf, wg_ref[:, cs])
        taps = taps_fn(ug, cs, n)
        conv = cb_ref[:, cs] + cw_ref[n_taps - 1:n_taps, cs] * taps[n_taps - 1]
        for j in range(n_taps - 1):
            conv = conv + cw_ref[j:j + 1, cs] * taps[j]
        act = conv * _sigmoid(conv) * _dot(u_bf, wu_ref[:, cs])
        acc = acc + _dot(act.astype(BF16), wd_ref[cs, :])
    return acc


def _post_kernel(x_ref, att_ref, ohg_ref, mk_ref, mv_ref, cinit_ref, gc_ref, gf_ref, gl_ref, wo_ref, wq_ref,
                 wc_ref, wg_ref, wu_ref, wd_ref, cw_ref, cb_ref, y_ref, cst_ref, buf_ref, carry_ref,
                 *, x_heads, x_dh, ff_chunk, n_sub):
    i = pl.program_id(1)
    tm = x_ref.shape[1]
    pad = SUBLANES
    n_taps = cw_ref.shape[0]

    @pl.when(i == 0)
    def _():
        carry_ref[...] = cinit_ref[0]

    ts = tm // n_sub
    for t in range(n_sub):
        rows = slice(t * ts, (t + 1) * ts)
        x1 = _mix_out(x_ref[0, rows], att_ref[0, rows], ohg_ref[0, rows], wo_ref)
        cq = _dot(_rms(x1, gc_ref[...]).astype(BF16), wq_ref[...])
        co = _cross_attend(cq, lambda h: mk_ref[0, :, h * x_dh:(h + 1) * x_dh],
                           lambda h: mv_ref[0, :, h * x_dh:(h + 1) * x_dh], x_heads, x_dh)
        x2 = x1 + _dot(co.astype(BF16), wc_ref[...])

        def taps_fn(ug, cs, n, t=t):
            buf = buf_ref.at[t, n % 2]
            buf[0:pad, :] = carry_ref[:, cs]
            buf[pad:pad + ts, :] = ug
            carry_ref[:, cs] = ug[ts - pad:ts]
            return [buf[pad - (n_taps - 1 - j):pad - (n_taps - 1 - j) + ts, :] for j in range(n_taps - 1)] + [ug]

        u_bf = _rms(x2, gf_ref[...]).astype(BF16)
        x3 = x2 + _ffn(u_bf, taps_fn, wg_ref, wu_ref, wd_ref, cw_ref, cb_ref, ff_chunk)
        y_ref[0, rows] = _rms(x3, gl_ref[...])
    cst_ref[0] = carry_ref[...]


def _post(x, att, ohg, mk_bf, mv_bf, cinit, gains, weights, conv_w, conv_b, *, tm, n_sub, x_heads, ff_chunk):
    B, T, D = x.shape
    aw, hw = att.shape[-1], ohg.shape[-1]
    n_mem, xw = mk_bf.shape[1:]
    dff = conv_w.shape[1]
    row = lambda b, i: (b, i, 0)
    bat = lambda b, i: (b, 0, 0)
    in_specs = ([pl.BlockSpec((1, tm, D), row), pl.BlockSpec((1, tm, aw), row), pl.BlockSpec((1, tm, hw), row)]
                + [pl.BlockSpec((1, n_mem, xw), bat)] * 2 + [pl.BlockSpec((1, SUBLANES, dff), bat)]
                + [_const_spec(a.shape) for a in tuple(gains) + tuple(weights) + (conv_w, conv_b)])
    return pl.pallas_call(
        functools.partial(_post_kernel, x_heads=x_heads, x_dh=xw // x_heads, ff_chunk=ff_chunk, n_sub=n_sub),
        grid=(B, T // tm),
        in_specs=in_specs,
        out_specs=[pl.BlockSpec((1, tm, D), row), pl.BlockSpec((1, SUBLANES, dff), bat)],
        out_shape=[jax.ShapeDtypeStruct((B, T, D), F32), jax.ShapeDtypeStruct((B, SUBLANES, dff), F32)],
        scratch_shapes=[pltpu.VMEM((n_sub, 2, tm // n_sub + SUBLANES, ff_chunk), F32),
                        pltpu.VMEM((SUBLANES, dff), F32)],
        compiler_params=_params(("parallel", "arbitrary")),
        name="post_mixer",
    )(x, att, ohg, mk_bf, mv_bf, cinit, *gains, *weights, conv_w, conv_b)


def _step_mix_kernel(x_ref, att_ref, ohg_ref, gc_ref, wo_ref, wq_ref, x1_ref, cq_ref):
    x1 = _mix_out(x_ref[...], att_ref[...].astype(BF16), ohg_ref[...].astype(BF16), wo_ref)
    x1_ref[...] = x1
    cq_ref[...] = _dot(_rms(x1, gc_ref[...]).astype(BF16), wq_ref[...])


def _step_cross_kernel(cq_ref, mk_ref, mv_ref, co_ref, *, n_q):
    _, n_mem, n_heads, dh = mk_ref.shape
    mk = mk_ref[0].reshape(n_mem * n_heads, dh).astype(BF16)
    mv = mv_ref[0].reshape(n_mem * n_heads, dh).astype(BF16)
    s = _dot_nt((cq_ref[0] * dh ** -0.5).astype(BF16), mk)
    row = lax.broadcasted_iota(jnp.int32, s.shape, 0)
    col = lax.broadcasted_iota(jnp.int32, s.shape, 1)
    s = jnp.where((col % n_heads) == (row // n_q), s, NEG)
    p = jnp.exp(s - jnp.max(s, axis=-1, keepdims=True))
    co_ref[0] = _dot(p.astype(BF16), mv) / jnp.sum(p, axis=-1, keepdims=True)


def _step_ffn_kernel(x1_ref, co_ref, prev_ref, gf_ref, gl_ref, wc_ref, wg_ref, wu_ref, wd_ref,
                     cw_ref, cb_ref, y_ref, ug_ref, buf_ref, *, ff_chunk, n_t):
    n = x1_ref.shape[0]
    pad = SUBLANES
    n_taps = cw_ref.shape[0]
    x2 = x1_ref[...] + _dot(co_ref[...].astype(BF16), wc_ref[...])
    t_of_row = lax.broadcasted_iota(jnp.int32, (n, ff_chunk), 0) % n_t

    def taps_fn(ug, cs, _):
        ug_ref[:, cs] = ug
        buf_ref[0:pad, cs] = jnp.zeros((pad, ff_chunk), F32)
        buf_ref[pad:pad + n, cs] = ug
        taps = []
        for j in range(n_taps - 1):
            shift = n_taps - 1 - j
            shifted = buf_ref[pad - shift:pad - shift + n, cs]
            taps.append(jnp.where(t_of_row < shift, prev_ref[j, :, cs], shifted))
        return taps + [ug]

    u_bf = _rms(x2, gf_ref[...]).astype(BF16)
    x3 = x2 + _ffn(u_bf, taps_fn, wg_ref, wu_ref, wd_ref, cw_ref, cb_ref, ff_chunk)
    y_ref[...] = _rms(x3, gl_ref[...])


def _single_call(kernel, args, out_shape, name, scratch_shapes=()):
    return pl.pallas_call(
        kernel,
        grid=(1,),
        in_specs=[_const_spec(a.shape) for a in args],
        out_specs=[pl.BlockSpec(o.shape, lambda *_, nd=len(o.shape): (0,) * nd) for o in out_shape],
        out_shape=out_shape,
        scratch_shapes=list(scratch_shapes),
        compiler_params=_params(("arbitrary",)),
        name=name,
    )(*args)


def _step_cross(cq, mem_k, mem_v, *, n_q):
    B, n_rows, dh = cq.shape
    _, n_mem, x_heads, _ = mem_k.shape
    bmap = lambda b: (b, 0, 0)
    mem = pl.BlockSpec((1, n_mem, x_heads, dh), lambda b: (b, 0, 0, 0))
    return pl.pallas_call(
        functools.partial(_step_cross_kernel, n_q=n_q),
        grid=(B,),
        in_specs=[pl.BlockSpec((1, n_rows, dh), bmap), mem, mem],
        out_specs=pl.BlockSpec((1, n_rows, dh), bmap),
        out_shape=jax.ShapeDtypeStruct((B, n_rows, dh), F32),
        compiler_params=_params(("parallel",)),
        name="step_cross",
    )(cq, mem_k, mem_v)


def kernel(x_prompt, x_sample, cache_win_k, cache_win_v, state_hgrn, state_ffn_conv, cache_mem_k, cache_mem_v,
           mem_prompt, hg_lb_logits, norm_mix, w_in, att_out_norm, hg_out_norm, w_out, norm_cross, norm_mem,
           w_cq, w_ck, w_cv, w_co, norm_ffn, w_gate, w_up, conv_w, conv_b, w_down, norm_final):
    Bp, T, D = x_prompt.shape
    Bs, Ts, _ = x_sample.shape
    depth, _, past, att_h, att_dh = cache_win_k.shape
    _, _, hg_h, hg_dk, hg_dv = state_hgrn.shape
    _, _, n_mem, x_h, x_dh = cache_mem_k.shape
    n_taps, dff = conv_w.shape[1:]
    att_w = att_h * att_dh
    hg_w = hg_h * hg_dk
    xw = x_h * x_dh
    keep = min(max(w for w, _ in DIL_PATTERNS), T)
    assert depth == 1 and hg_dk == hg_dv == LANES and x_dh == LANES and LANES % att_dh == 0
    assert all(w // d == ATT_BLOCK for w, d in DIL_PATTERNS) and past >= max(w for w, _ in DIL_PATTERNS)
    d_max = max(d for _, d in DIL_PATTERNS)
    assert T % (ATT_BLOCK * d_max) == 0 and past % d_max == 0 and past % (4 * ATT_BLOCK) == 0
    assert n_taps - 1 <= min(Ts, SUBLANES) and Ts <= SUBLANES
    layer = 0
    ff_chunk = 2 * LANES
    q_scale = att_dh ** -0.5
    row2 = lambda a: a.reshape(1, -1)

    w_in_bf = w_in[layer].astype(BF16)
    w_ckv_bf = jnp.concatenate([w_ck[layer], w_cv[layer]], axis=1).astype(BF16)
    weights = tuple(w[layer].astype(BF16) for w in (w_out, w_cq, w_co, w_gate, w_up, w_down))
    g_mix, g_att, g_hg = row2(norm_mix[layer]), row2(att_out_norm[layer]), row2(hg_out_norm[layer])
    g_cross, g_mem, g_ffn, g_final = (row2(norm_cross[layer]), row2(norm_mem[layer]), row2(norm_ffn[layer]),
                                      row2(norm_final))
    cw, cb = conv_w[layer], row2(conv_b[layer])

    q, k, v, k_keep, v_keep, ohg, s_prompt = _in_proj_hgrn(
        x_prompt, g_mix, w_in_bf, hg_lb_logits, g_hg, att_w=att_w, q_scale=q_scale, n_heads=hg_h, dk=hg_dk,
        tm=512, keep=keep, layer=layer)
    q, k, v = (a.reshape(Bp, T, att_w) for a in (q, k, v))
    ohg = ohg.reshape(Bp, T, hg_w)
    mk, mv, mk_bf, mv_bf = _mem_kv(mem_prompt.reshape(Bp * n_mem, D), g_mem, w_ckv_bf, tm=256)
    att = _dil_attn(q, k, v, g_att, n_heads=att_h, dh=att_dh, unroll=16)
    y_prompt, cst = _post(
        x_prompt, att, ohg, mk_bf.reshape(Bp, n_mem, xw), mv_bf.reshape(Bp, n_mem, xw),
        jnp.zeros((Bp, SUBLANES, dff), F32), (g_cross, g_ffn, g_final), weights, cw, cb,
        tm=512, n_sub=1, x_heads=x_h, ff_chunk=ff_chunk)

    n_s = Bs * Ts
    qs, ks, vs, ks_f, vs_f, hzs = _in_proj(x_sample.reshape(1, n_s, D), g_mix, w_in_bf, att_w=att_w, q_scale=q_scale,
                                           tm=n_s, keep=n_s)
    pad_new = lambda a: jnp.pad(a.reshape(Bs, Ts, att_w).astype(BF16), ((0, 0), (0, LANES - Ts), (0, 0)))
    feature_major = lambda c: jnp.transpose(c, (0, 2, 3, 1)).reshape(Bs, att_w, past)
    att_s, ohg_s, s_sample = _step_mixers(
        qs.reshape(Bs, Ts, att_w), pad_new(ks), pad_new(vs), feature_major(cache_win_k[layer]),
        feature_major(cache_win_v[layer]), g_att, hzs.reshape(Bs, Ts, 4 * hg_w), state_hgrn[layer], hg_lb_logits,
        g_hg, att_heads=att_h, dh=att_dh, layer=layer)
    x1_s, cq_s = _single_call(
        _step_mix_kernel,
        (x_sample.reshape(n_s, D), att_s.reshape(n_s, att_w), ohg_s.reshape(n_s, hg_w), g_cross, weights[0],
         weights[1]),
        [jax.ShapeDtypeStruct((n_s, D), F32), jax.ShapeDtypeStruct((n_s, xw), F32)], "step_mix")
    cq_heads = cq_s.reshape(Bs, Ts, x_h, x_dh).transpose(0, 2, 1, 3).reshape(Bs, x_h * Ts, x_dh)
    co_s = _step_cross(cq_heads, cache_mem_k[layer], cache_mem_v[layer], n_q=Ts)
    co_s = co_s.reshape(Bs, x_h, Ts, x_dh).transpose(0, 2, 1, 3)
    conv_state = state_ffn_conv[layer]
    t_idx = jnp.arange(Ts)
    prev = jnp.stack([jnp.take(conv_state, jnp.clip(j + t_idx, 0, n_taps - 2), axis=1).reshape(n_s, dff)
                      for j in range(n_taps - 1)])
    y_s, ug_s = _single_call(
        functools.partial(_step_ffn_kernel, ff_chunk=ff_chunk, n_t=Ts),
        (x1_s, co_s.reshape(n_s, xw), prev, g_ffn, g_final, weights[2], weights[3], weights[4], weights[5], cw, cb),
        [jax.ShapeDtypeStruct((n_s, D), F32), jax.ShapeDtypeStruct((n_s, dff), F32)], "step_ffn",
        scratch_shapes=[pltpu.VMEM((n_s + SUBLANES, dff), F32)])

    stack = lambda a: a[None]
    return (y_prompt, y_s.reshape(Bs, Ts, D),
            stack(k_keep.reshape(Bp, keep, att_h, att_dh)), stack(v_keep.reshape(Bp, keep, att_h, att_dh)),
            stack(s_prompt), stack(cst[:, SUBLANES - (n_taps - 1):]),
            stack(mk.reshape(Bp, n_mem, x_h, x_dh)), stack(mv.reshape(Bp, n_mem, x_h, x_dh)),
            stack(ks_f.reshape(Bs, Ts, att_h, att_dh)), stack(vs_f.reshape(Bs, Ts, att_h, att_dh)),
            stack(s_sample), stack(ug_s.reshape(Bs, Ts, dff)[:, Ts - (n_taps - 1):]))
```

```python
import functools

import jax
import jax.numpy as jnp
from jax import lax
from jax.experimental import pallas as pl
from jax.experimental.pallas import tpu as pltpu

F32 = jnp.float32
BF16 = jnp.bfloat16
EPS = 1e-6
DIL_PATTERNS = ((128, 1), (512, 4), (2048, 16))
ATT_BLOCK = 128
LANES = 128
SUBLANES = 8
HG_CHUNK = 64
HG_SUB = 8
NEG = -1e30
LOG2E = 1.4426950408889634
HIGHEST = lax.Precision.HIGHEST
NT_DIMS = (((1,), (1,)), ((), ()))
TN_DIMS = (((0,), (0,)), ((), ()))
VMEM_LIMIT = 56 * 1024 * 1024


def _dot(a, b):
    return jnp.dot(a, b, preferred_element_type=F32)


def _dot_nt(a, b):
    return lax.dot_general(a, b, NT_DIMS, preferred_element_type=F32)


def _sigmoid(x):
    return 1.0 / (1.0 + jnp.exp(-x))


def _rms(x, g):
    return x * lax.rsqrt(jnp.mean(x * x, axis=-1, keepdims=True) + EPS) * g


def _const_spec(shape):
    nd = len(shape)
    return pl.BlockSpec(shape, lambda *_: (0,) * nd, pipeline_mode=pl.Buffered(1))


def _params(sem):
    return pltpu.CompilerParams(dimension_semantics=sem, vmem_limit_bytes=VMEM_LIMIT)


def _in_proj_kernel(x_ref, g_ref, w_ref, q_ref, k_ref, v_ref, kf_ref, vf_ref, hz_ref, *, att_w, q_scale):
    h = _rms(x_ref[0], g_ref[...]).astype(BF16)

    def proj(lo, hi):
        return _dot(h, w_ref[:, lo:hi])

    q_ref[0] = proj(0, att_w) * q_scale
    zk = proj(att_w, 2 * att_w)
    k_ref[0] = zk
    kf_ref[0] = zk
    zv = proj(2 * att_w, 3 * att_w)
    v_ref[0] = zv
    vf_ref[0] = zv
    base = 3 * att_w
    step = 4 * LANES
    for c in range(0, hz_ref.shape[-1], step):
        hz_ref[0, :, c:c + step] = proj(base + c, base + c + step)


def _in_proj(x, g, w_bf, *, att_w, q_scale, tm, keep):
    B, T, D = x.shape
    ncols = w_bf.shape[1]
    hzw = ncols - 3 * att_w
    n_t = T // tm
    first = n_t - keep // tm
    row = lambda b, i: (b, i, 0)
    keep_map = lambda b, i: (b, jnp.maximum(i - first, 0), 0)
    return pl.pallas_call(
        functools.partial(_in_proj_kernel, att_w=att_w, q_scale=q_scale),
        grid=(B, n_t),
        in_specs=[pl.BlockSpec((1, tm, D), row), _const_spec((1, D)), _const_spec((D, ncols))],
        out_specs=[pl.BlockSpec((1, tm, att_w), row)] * 3
        + [pl.BlockSpec((1, tm, att_w), keep_map)] * 2
        + [pl.BlockSpec((1, tm, hzw), row)],
        out_shape=[jax.ShapeDtypeStruct((B, T, att_w), F32)] * 3
        + [jax.ShapeDtypeStruct((B, keep, att_w), F32)] * 2
        + [jax.ShapeDtypeStruct((B, T, hzw), F32)],
        compiler_params=_params(("parallel", "arbitrary")),
        name="in_proj",
    )(x, g, w_bf)


def _mem_kv_kernel(m_ref, g_ref, w_ref, kf_ref, vf_ref, kb_ref, vb_ref):
    h = _rms(m_ref[...], g_ref[...]).astype(BF16)
    xw = w_ref.shape[1] // 2
    k = _dot(h, w_ref[:, :xw])
    v = _dot(h, w_ref[:, xw:])
    kf_ref[...] = k
    vf_ref[...] = v
    kb_ref[...] = k.astype(BF16)
    vb_ref[...] = v.astype(BF16)


def _mem_kv(mem2d, g, w_ckv_bf, *, tm):
    n, D = mem2d.shape
    xw = w_ckv_bf.shape[1] // 2
    row = lambda i: (i, 0)
    return pl.pallas_call(
        _mem_kv_kernel,
        grid=(n // tm,),
        in_specs=[pl.BlockSpec((tm, D), row), _const_spec((1, D)), _const_spec((D, 2 * xw))],
        out_specs=[pl.BlockSpec((tm, xw), row)] * 4,
        out_shape=[jax.ShapeDtypeStruct((n, xw), F32)] * 2 + [jax.ShapeDtypeStruct((n, xw), BF16)] * 2,
        compiler_params=_params(("parallel",)),
        name="mem_kv",
    )(mem2d, g, w_ckv_bf)


def _dil_attn_kernel(q_ref, k_ref, v_ref, g_ref, o_ref, acc_sc, m_sc, den_sc, bias_sc, *, n_heads, dh, unroll):
    tile = pl.program_id(1)
    T = q_ref.shape[1]
    blk = ATT_BLOCK
    hpt = LANES // dh
    rows, ncol = hpt * blk, 2 * blk
    rr = lax.broadcasted_iota(jnp.int32, (rows, ncol), 0)
    cc = lax.broadcasted_iota(jnp.int32, (rows, ncol), 1)
    delta = (rr % blk) + blk - cc
    in_band = (delta >= 0) & (delta <= blk)
    slope = jnp.exp2((-8.0 / n_heads) * (tile * hpt + rr // blk + 1).astype(F32))
    lane_head = lax.broadcasted_iota(jnp.int32, (blk, LANES), 1) // dh
    n_pat = len(DIL_PATTERNS)

    def per_lane(x):
        out = jnp.broadcast_to(x[0:blk], (blk, LANES))
        for e in range(1, hpt):
            out = jnp.where(lane_head == e, jnp.broadcast_to(x[e * blk:(e + 1) * blk], (blk, LANES)), out)
        return out

    ones = jnp.ones((ncol, LANES), BF16)
    order = sorted(range(n_pat), key=lambda p: -DIL_PATTERNS[p][1])
    for step, p in enumerate(order):
        dil = DIL_PATTERNS[p][1]
        alibi = (-LOG2E) * slope * (delta * dil).astype(F32)
        bias_sc[0] = jnp.where(in_band & (cc >= blk), alibi, NEG)
        bias_sc[1] = jnp.where(in_band, alibi, NEG)
        nblk = T // (blk * dil)

        def body(n, carry, first=step == 0, last=step == n_pat - 1, dil=dil, nblk=nblk):
            kp, vp = carry
            i = n % nblk
            start = n // nblk + i * (blk * dil)
            idx = pl.ds(start, blk, stride=dil) if dil > 1 else pl.ds(pl.multiple_of(start, blk), blk)
            q2 = q_ref[0, idx, :] * LOG2E
            kc = k_ref[0, idx, :].astype(BF16)
            vc = v_ref[0, idx, :].astype(BF16)
            qs = jnp.concatenate([jnp.where(lane_head == e, q2, 0.0) for e in range(hpt)], axis=0).astype(BF16)
            s = _dot_nt(qs, jnp.concatenate([kp, kc], axis=0)) + bias_sc[jnp.minimum(i, 1)]
            m = jnp.max(s, axis=-1, keepdims=True)
            pr = jnp.exp2(s - m).astype(BF16)
            pv = _dot(pr, jnp.concatenate([jnp.concatenate([vp, vc], axis=0), ones], axis=1))
            m_t, acc_t, den_t = per_lane(m), per_lane(pv[:, :LANES]), per_lane(pv[:, LANES:])
            if not first:
                m_old = m_sc[idx, :]
                m_new = jnp.maximum(m_old, m_t)
                a_old, a_t = jnp.exp2(m_old - m_new), jnp.exp2(m_t - m_new)
                den_t = a_old * den_sc[idx, :] + a_t * den_t
                acc_t = a_old * acc_sc[idx, :] + a_t * acc_t
                m_t = m_new
            if last:
                acc_sc[idx, :] = acc_t / den_t
            else:
                m_sc[idx, :] = m_t
                den_sc[idx, :] = den_t
                acc_sc[idx, :] = acc_t
            return kc, vc

        zero = jnp.zeros((blk, LANES), BF16)
        lax.fori_loop(0, T // blk, body, (zero, zero), unroll=unroll)

    ch = 4 * blk
    same_head = (lax.broadcasted_iota(jnp.int32, (LANES, LANES), 0) // dh
                 == lax.broadcasted_iota(jnp.int32, (LANES, LANES), 1) // dh).astype(BF16)
    for c in range(T // ch):
        a = acc_sc[c * ch:(c + 1) * ch, :]
        mean_sq = _dot((a * a).astype(BF16), same_head) * (1.0 / dh)
        o_ref[0, c * ch:(c + 1) * ch, :] = (a * lax.rsqrt(mean_sq + EPS) * g_ref[...]).astype(BF16)


def _dil_attn(q, k, v, g, *, n_heads, dh, unroll):
    B, T, W = q.shape
    blk = ATT_BLOCK
    seq = pl.BlockSpec((1, T, LANES), lambda b, t: (b, 0, t))
    return pl.pallas_call(
        functools.partial(_dil_attn_kernel, n_heads=n_heads, dh=dh, unroll=unroll),
        grid=(B, W // LANES),
        in_specs=[seq, seq, seq, pl.BlockSpec((1, LANES), lambda b, t: (0, t))],
        out_specs=seq,
        out_shape=jax.ShapeDtypeStruct((B, T, W), BF16),
        scratch_shapes=[pltpu.VMEM((T, LANES), F32)] * 3
        + [pltpu.VMEM((2, (LANES // dh) * blk, 2 * blk), F32)],
        compiler_params=_params(("parallel", "arbitrary")),
        name="dil_attn",
    )(q, k, v, g)


def _dil_attn_step_kernel(q_ref, kn_ref, vn_ref, kt_ref, vt_ref, g_ref, o_ref, cnt_sc, bias_sc,
                          *, n_heads, dh, n_q):
    past = kt_ref.shape[2]
    n_new = kn_ref.shape[1]
    W = q_ref.shape[-1]
    rows = n_q * n_heads
    n_cols = past + n_new

    @pl.when(pl.program_id(0) == 0)
    def _():
        r = lax.broadcasted_iota(jnp.int32, (rows, n_cols), 0)
        c = lax.broadcasted_iota(jnp.int32, (rows, n_cols), 1)
        delta = past + r // n_heads - c
        cnt = jnp.zeros((rows, n_cols), F32)
        for win, dil in DIL_PATTERNS:
            cnt = cnt + ((delta >= 0) & (delta <= win) & ((delta & (dil - 1)) == 0)).astype(F32)
        slope = jnp.exp2((-8.0 / n_heads) * ((r % n_heads) + 1).astype(F32))
        cnt_sc[...] = cnt
        bias_sc[...] = jnp.where(cnt > 0.0, -slope * delta.astype(F32), NEG)

    sub = lax.broadcasted_iota(jnp.int32, (n_heads, W), 0)
    lane_head = lax.broadcasted_iota(jnp.int32, (n_heads, W), 1) // dh
    qbd = jnp.concatenate(
        [jnp.where(sub == lane_head, jnp.broadcast_to(q_ref[0, i:i + 1, :], (n_heads, W)), 0.0) for i in range(n_q)],
        axis=0).astype(BF16)
    s = jnp.concatenate([_dot(qbd, kt_ref[0].astype(BF16)), _dot_nt(qbd, kn_ref[0])], axis=1) + bias_sc[...]
    p = jnp.exp(s - jnp.max(s, axis=-1, keepdims=True)) * cnt_sc[...]
    den = jnp.sum(p, axis=-1, keepdims=True)
    p = p.astype(BF16)
    out = (_dot_nt(p[:, :past], vt_ref[0].astype(BF16)) + _dot(p[:, past:], vn_ref[0])) / den
    r = lax.broadcasted_iota(jnp.int32, (rows, W), 0)
    c = lax.broadcasted_iota(jnp.int32, (rows, W), 1)
    out = jnp.where((r % n_heads) == (c // dh), out, 0.0)
    out = out * lax.rsqrt(jnp.sum(out * out, axis=-1, keepdims=True) * (1.0 / dh) + EPS)
    o_ref[0] = jnp.sum(out.reshape(n_q, n_heads, W), axis=1) * g_ref[...]


def _lower_bound(lbl_ref, layer):
    logits = lbl_ref[...]
    e = jnp.exp(logits - jnp.max(logits, axis=0, keepdims=True))
    return jnp.sum(e[:layer + 1], axis=0, keepdims=True) / jnp.sum(e, axis=0, keepdims=True)


def _cumsum_rows(tril_bf, x):
    x1 = x.astype(BF16)
    r1 = x - x1.astype(F32)
    x2 = r1.astype(BF16)
    x3 = (r1 - x2.astype(F32)).astype(BF16)
    return _dot(tril_bf, x1) + _dot(tril_bf, x2) + _dot(tril_bf, x3)


def _hgrn_rows(hz_ref, o_ref, lb, gn_ref, st_ref, b_sc, c_sc, *, n_heads, dk, between=()):
    hw = n_heads * dk
    tc = hz_ref.shape[0]
    ch, sub = HG_CHUNK, HG_SUB
    r_c = lax.broadcasted_iota(jnp.int32, (ch, ch), 0)
    c_c = lax.broadcasted_iota(jnp.int32, (ch, ch), 1)
    tril = (r_c >= c_c).astype(BF16)
    r_s = lax.broadcasted_iota(jnp.int32, (sub, ch), 0)
    c_s = lax.broadcasted_iota(jnp.int32, (sub, ch), 1)

    local = [[None] * n_heads for _ in range(tc // ch)]
    for c in range(tc // ch):
        rows = slice(c * ch, (c + 1) * ch)
        hq = hz_ref[rows, 0:hw]
        f = lb + (1.0 - lb) * _sigmoid(hz_ref[rows, hw:2 * hw])
        qh = hq * _sigmoid(hq) * dk ** -0.5
        kk = 1.0 - f
        b = _cumsum_rows(tril, jnp.log(f) * LOG2E)
        b_rows, c_rows = b_sc.at[c], c_sc.at[c]
        b_rows[...] = b
        c_rows[...] = b - jnp.log(kk) * LOG2E
        for h in range(n_heads):
            hs = slice(h * dk, (h + 1) * dk)
            bh = b[:, hs]
            kh = kk[:, hs]
            qhh = qh[:, hs]
            vh = hz_ref[rows, 2 * hw + h * dk:2 * hw + (h + 1) * dk].astype(BF16)
            blocks = []
            for j in range(ch // sub):
                r0 = j * sub
                bj = bh[r0:r0 + sub]
                qj = qhh[r0:r0 + sub]
                a_j = jnp.zeros((sub, ch), F32)
                for s in range(sub):
                    a = jnp.sum(qj * jnp.exp2(bj - c_rows[r0 + s:r0 + s + 1, hs]), axis=-1, keepdims=True)
                    a_j = jnp.where(c_s == r0 + s, a, a_j)
                if j > 0:
                    beta = b_rows[r0 - 1:r0, hs]
                    qt = qj * jnp.exp2(bj - beta)
                    kt = kh[0:r0] * jnp.exp2(beta - bh[0:r0])
                    kt = jnp.concatenate([kt, jnp.zeros((ch - r0, dk), F32)], axis=0)
                    a_j = a_j + _dot_nt(qt.astype(BF16), kt.astype(BF16))
                blocks.append(jnp.where(c_s - r0 <= r_s, a_j, 0.0))
            b_last = b_rows[ch - 1:ch, hs]
            hg = hz_ref[rows, 3 * hw + h * dk:3 * hw + (h + 1) * dk]
            local[c][h] = dict(
                q=(qhh * jnp.exp2(bh)).astype(BF16), a=jnp.concatenate(blocks, axis=0).astype(BF16), v=vh,
                k=(kh * jnp.exp2(b_last - bh)).astype(BF16), decay=jnp.exp2(b_last), gate=hg * _sigmoid(hg))
        if c < len(between):
            between[c]()

    state = [st_ref[h] for h in range(n_heads)]
    for c in range(tc // ch):
        rows = slice(c * ch, (c + 1) * ch)
        for h in range(n_heads):
            hs = slice(h * dk, (h + 1) * dk)
            t = local[c][h]
            o = _dot_nt(t["q"], state[h].astype(BF16)) + _dot(t["a"], t["v"])
            state[h] = state[h] * t["decay"] + lax.dot_general(t["v"], t["k"], TN_DIMS, preferred_element_type=F32)
            o_ref[rows, hs] = (_rms(o, gn_ref[:, hs]) * t["gate"]).astype(BF16)
    for h in range(n_heads):
        st_ref[h] = state[h]
    return min(len(between), tc // ch)


def _in_proj_hgrn_kernel(x_ref, g_ref, w_ref, lbl_ref, gn_ref, q_ref, k_ref, v_ref, kf_ref, vf_ref, o_ref, s_ref,
                         hz_sc, st_ref, b_sc, c_sc, *, att_w, q_scale, n_heads, dk, layer, tiles_per_seq):
    g = pl.program_id(0)
    t_rec = jnp.maximum(g - 1, 0)

    @pl.when(g == 0)
    def _():
        hz_sc[...] = jnp.zeros_like(hz_sc)

    @pl.when(t_rec % tiles_per_seq == 0)
    def _():
        st_ref[...] = jnp.zeros_like(st_ref)

    lb = _lower_bound(lbl_ref, layer)

    def run(slot_w, slot_r):
        h = _rms(x_ref[...], g_ref[...]).astype(BF16)

        def proj(lo, hi):
            return _dot(h, w_ref[:, lo:hi])

        def proj_q():
            q_ref[...] = proj(0, att_w) * q_scale

        def proj_k():
            zk = proj(att_w, 2 * att_w)
            k_ref[...] = zk
            kf_ref[0] = zk

        def proj_v():
            zv = proj(2 * att_w, 3 * att_w)
            v_ref[...] = zv
            vf_ref[0] = zv

        def proj_hz(c):
            def piece():
                hz_sc[slot_w, :, c:c + step] = proj(base + c, base + c + step)
            return piece

        base = 3 * att_w
        step = 4 * LANES
        pieces = [proj_q, proj_k, proj_v] + [proj_hz(c) for c in range(0, hz_sc.shape[-1], step)]
        done = _hgrn_rows(hz_sc.at[slot_r], o_ref, lb, gn_ref, st_ref, b_sc, c_sc, n_heads=n_heads, dk=dk,
                          between=pieces)
        for piece in pieces[done:]:
            piece()

    @pl.when(g % 2 == 0)
    def _():
        run(0, 1)

    @pl.when(g % 2 == 1)
    def _():
        run(1, 0)

    @pl.when((g > 0) & (t_rec % tiles_per_seq == tiles_per_seq - 1))
    def _():
        for h in range(n_heads):
            s_ref[0, h] = st_ref[h].T


def _in_proj_hgrn(x, g, w_bf, lb_logits, gn, *, att_w, q_scale, n_heads, dk, tm, keep, layer):
    B, T, D = x.shape
    hw = n_heads * dk
    n_t = T // tm
    n_tiles = B * n_t
    n_keep = keep // tm
    first = n_t - n_keep
    proj_tile = lambda s: jnp.minimum(s, n_tiles - 1)
    rec_tile = lambda s: jnp.maximum(s - 1, 0)
    rows = lambda s: (proj_tile(s), 0)
    keep_rows = lambda s: (proj_tile(s) // n_t, jnp.maximum(proj_tile(s) % n_t - first, 0), 0)
    f32 = lambda n, w: jax.ShapeDtypeStruct((n, w), F32)
    return pl.pallas_call(
        functools.partial(_in_proj_hgrn_kernel, att_w=att_w, q_scale=q_scale, n_heads=n_heads, dk=dk, layer=layer,
                          tiles_per_seq=n_t),
        grid=(n_tiles + 1,),
        in_specs=[pl.BlockSpec((tm, D), rows), _const_spec((1, D)), _const_spec(w_bf.shape),
                  _const_spec(lb_logits.shape), _const_spec((1, hw))],
        out_specs=[pl.BlockSpec((tm, att_w), rows)] * 3 + [pl.BlockSpec((1, tm, att_w), keep_rows)] * 2
        + [pl.BlockSpec((tm, hw), lambda s: (rec_tile(s), 0)),
           pl.BlockSpec((1, n_heads, dk, dk), lambda s: (rec_tile(s) // n_t, 0, 0, 0))],
        out_shape=[f32(B * T, att_w)] * 3 + [jax.ShapeDtypeStruct((B, keep, att_w), F32)] * 2
        + [jax.ShapeDtypeStruct((B * T, hw), BF16), jax.ShapeDtypeStruct((B, n_heads, dk, dk), F32)],
        scratch_shapes=[pltpu.VMEM((2, tm, 4 * hw), F32), pltpu.VMEM((n_heads, dk, dk), F32)]
        + [pltpu.VMEM((tm // HG_CHUNK, HG_CHUNK, hw), F32)] * 2,
        compiler_params=_params(("arbitrary",)),
        name="in_proj_hgrn",
    )(x.reshape(B * T, D), g, w_bf, lb_logits, gn)


def _hgrn_step_kernel(hz_ref, s0_ref, lbl_ref, gn_ref, o_ref, s_ref, *, n_heads, dk, layer):
    hw = n_heads * dk
    n_t = hz_ref.shape[1]
    lb = _lower_bound(lbl_ref, layer)
    q, k, v, b, gate = [], [], [], [], []
    acc = jnp.zeros((1, hw), F32)
    for t in range(n_t):
        hq = hz_ref[0, t:t + 1, 0:hw]
        f = lb + (1.0 - lb) * _sigmoid(hz_ref[0, t:t + 1, hw:2 * hw])
        acc = acc + jnp.log(f)
        q.append(hq * _sigmoid(hq) * dk ** -0.5)
        k.append(1.0 - f)
        v.append(hz_ref[0, t:t + 1, 2 * hw:3 * hw])
        b.append(acc)
        hg = hz_ref[0, t:t + 1, 3 * hw:4 * hw]
        gate.append(hg * _sigmoid(hg))
    rr = lax.broadcasted_iota(jnp.int32, (dk, dk), 0)
    cc = lax.broadcasted_iota(jnp.int32, (dk, dk), 1)

    def tile_of_rows(rows):
        tile = jnp.zeros((dk, dk), F32)
        for t, r in enumerate(rows):
            tile = jnp.where(rr == t, jnp.broadcast_to(r, (dk, dk)), tile)
        return tile

    for h in range(n_heads):
        hs = slice(h * dk, (h + 1) * dk)
        s0 = s0_ref[0, h]
        qe = tile_of_rows([q[t][:, hs] * jnp.exp(b[t][:, hs]) for t in range(n_t)])
        inter = jnp.dot(qe, s0, precision=HIGHEST, preferred_element_type=F32)
        for t in range(n_t):
            o = inter[t:t + 1]
            for s in range(t + 1):
                a = jnp.sum(q[t][:, hs] * jnp.exp(b[t][:, hs] - b[s][:, hs]) * k[s][:, hs], axis=-1, keepdims=True)
                o = o + a * v[s][:, hs]
            o_ref[0, t:t + 1, hs] = _rms(o, gn_ref[:, hs]) * gate[t][:, hs]
        b_last = b[n_t - 1][:, hs]
        khat = tile_of_rows([k[t][:, hs] * jnp.exp(b_last - b[t][:, hs]) for t in range(n_t)])
        vpad = tile_of_rows([v[t][:, hs] for t in range(n_t)])
        decay = jnp.where(rr == cc, jnp.broadcast_to(jnp.exp(b_last), (dk, dk)), 0.0)
        s_ref[0, h] = (jnp.dot(decay, s0, precision=HIGHEST, preferred_element_type=F32)
                       + jnp.dot(khat.T, vpad, precision=HIGHEST, preferred_element_type=F32))


def _step_mixers_kernel(q_ref, kn_ref, vn_ref, kt_ref, vt_ref, g_ref, hz_ref, s0_ref, lbl_ref, gn_ref,
                        o_ref, oh_ref, s_ref, cnt_sc, bias_sc, *, att_heads, dh, n_q, hg_heads, dk, layer):
    _dil_attn_step_kernel(q_ref, kn_ref, vn_ref, kt_ref, vt_ref, g_ref, o_ref, cnt_sc, bias_sc,
                          n_heads=att_heads, dh=dh, n_q=n_q)
    _hgrn_step_kernel(hz_ref, s0_ref, lbl_ref, gn_ref, oh_ref, s_ref, n_heads=hg_heads, dk=dk, layer=layer)


def _step_mixers(q, k_new_pad, v_new_pad, cache_kt, cache_vt, g_att, hz, s0, lb_logits, g_hg,
                 *, att_heads, dh, layer):
    B, n_q, W = q.shape
    past = cache_kt.shape[2]
    n_new = k_new_pad.shape[1]
    _, hg_heads, dk, dv = s0.shape
    hw = hg_heads * dk
    bmap = lambda b: (b, 0, 0)
    smap = lambda b: (b, 0, 0, 0)
    return pl.pallas_call(
        functools.partial(_step_mixers_kernel, att_heads=att_heads, dh=dh, n_q=n_q, hg_heads=hg_heads, dk=dk,
                          layer=layer),
        grid=(B,),
        in_specs=[pl.BlockSpec((1, n_q, W), bmap), pl.BlockSpec((1, n_new, W), bmap), pl.BlockSpec((1, n_new, W), bmap),
                  pl.BlockSpec((1, W, past), bmap), pl.BlockSpec((1, W, past), bmap), _const_spec((1, W)),
                  pl.BlockSpec((1, n_q, 4 * hw), bmap), pl.BlockSpec((1, hg_heads, dk, dv), smap),
                  _const_spec(lb_logits.shape), _const_spec((1, hw))],
        out_specs=[pl.BlockSpec((1, n_q, W), bmap), pl.BlockSpec((1, n_q, hw), bmap),
                   pl.BlockSpec((1, hg_heads, dk, dv), smap)],
        out_shape=[jax.ShapeDtypeStruct((B, n_q, W), F32), jax.ShapeDtypeStruct((B, n_q, hw), F32),
                   jax.ShapeDtypeStruct((B, hg_heads, dk, dv), F32)],
        scratch_shapes=[pltpu.VMEM((n_q * att_heads, past + n_new), F32)] * 2,
        compiler_params=_params(("arbitrary",)),
        name="step_mixers",
    )(q, k_new_pad, v_new_pad, cache_kt, cache_vt, g_att, hz, s0, lb_logits, g_hg)


def _mix_out(x, att_bf, ohg_bf, wo_ref):
    aw = att_bf.shape[-1]
    return x + _dot(att_bf, wo_ref[0:aw, :]) + _dot(ohg_bf, wo_ref[aw:, :])


def _cross_attend(cq, head_k, head_v, n_heads, dh):
    heads = range(n_heads)
    scale = dh ** -0.5 * LOG2E
    scores = [_dot_nt((cq[:, h * dh:(h + 1) * dh] * scale).astype(BF16), head_k(h)) for h in heads]
    probs = [jnp.exp2(s - jnp.max(s, axis=-1, keepdims=True)).astype(BF16) for s in scores]
    ones = jnp.ones((probs[0].shape[1], dh), BF16)
    pvs = [_dot(probs[h], jnp.concatenate([head_v(h), ones], axis=1)) for h in heads]
    return jnp.concatenate([pv[:, :dh] / pv[:, dh:] for pv in pvs], axis=-1)


def _ffn(u_bf, taps_fn, wg_ref, wu_ref, wd_ref, cw_ref, cb_ref, ff_chunk):
    dff = wg_ref.shape[1]
    n_taps = cw_ref.shape[0]
    acc = jnp.zeros((u_bf.shape[0], wd_ref.shape[1]), F32)
    for n in range(dff // ff_chunk):
        cs = slice(n * ff_chunk, (n + 1) * ff_chunk)
        ug = _dot(u_bf, wg_ref[:, cs])
        taps = taps_fn(ug, cs, n)
        conv = cb_ref[:, cs] + cw_ref[n_taps - 1:n_taps, cs] * taps[n_taps - 1]
        for j in range(n_taps - 1):
            conv = conv + cw_ref[j:j + 1, cs] * taps[j]
        act = conv * _sigmoid(conv) * _dot(u_bf, wu_ref[:, cs])
        acc = acc + _dot(act.astype(BF16), wd_ref[cs, :])
    return acc


def _post_kernel(x_ref, att_ref, ohg_ref, mk_ref, mv_ref, cinit_ref, gc_ref, gf_ref, gl_ref, wo_ref, wq_ref,
                 wc_ref, wg_ref, wu_ref, wd_ref, cw_ref, cb_ref, y_ref, cst_ref, buf_ref, carry_ref,
                 *, x_heads, x_dh, ff_chunk, n_sub):
    i = pl.program_id(1)
    tm = x_ref.shape[1]
    pad = SUBLANES
    n_taps = cw_ref.shape[0]

    @pl.when(i == 0)
    def _():
        carry_ref[...] = cinit_ref[0]

    ts = tm // n_sub
    for t in range(n_sub):
        rows = slice(t * ts, (t + 1) * ts)
        x1 = _mix_out(x_ref[0, rows], att_ref[0, rows], ohg_ref[0, rows], wo_ref)
        cq = _dot(_rms(x1, gc_ref[...]).astype(BF16), wq_ref[...])
        co = _cross_attend(cq, lambda h: mk_ref[0, :, h * x_dh:(h + 1) * x_dh],
                           lambda h: mv_ref[0, :, h * x_dh:(h + 1) * x_dh], x_heads, x_dh)
        x2 = x1 + _dot(co.astype(BF16), wc_ref[...])

        def taps_fn(ug, cs, n, t=t):
            buf = buf_ref.at[t, n % 2]
            buf[0:pad, :] = carry_ref[:, cs]
            buf[pad:pad + ts, :] = ug
            carry_ref[:, cs] = ug[ts - pad:ts]
            return [buf[pad - (n_taps - 1 - j):pad - (n_taps - 1 - j) + ts, :] for j in range(n_taps - 1)] + [ug]

        u_bf = _rms(x2, gf_ref[...]).astype(BF16)
        x3 = x2 + _ffn(u_bf, taps_fn, wg_ref, wu_ref, wd_ref, cw_ref, cb_ref, ff_chunk)
        y_ref[0, rows] = _rms(x3, gl_ref[...])
    cst_ref[0] = carry_ref[...]


def _post(x, att, ohg, mk_bf, mv_bf, cinit, gains, weights, conv_w, conv_b, *, tm, n_sub, x_heads, ff_chunk):
    B, T, D = x.shape
    aw, hw = att.shape[-1], ohg.shape[-1]
    n_mem, xw = mk_bf.shape[1:]
    dff = conv_w.shape[1]
    row = lambda b, i: (b, i, 0)
    bat = lambda b, i: (b, 0, 0)
    in_specs = ([pl.BlockSpec((1, tm, D), row), pl.BlockSpec((1, tm, aw), row), pl.BlockSpec((1, tm, hw), row)]
                + [pl.BlockSpec((1, n_mem, xw), bat)] * 2 + [pl.BlockSpec((1, SUBLANES, dff), bat)]
                + [_const_spec(a.shape) for a in tuple(gains) + tuple(weights) + (conv_w, conv_b)])
    return pl.pallas_call(
        functools.partial(_post_kernel, x_heads=x_heads, x_dh=xw // x_heads, ff_chunk=ff_chunk, n_sub=n_sub),
        grid=(B, T // tm),
        in_specs=in_specs,
        out_specs=[pl.BlockSpec((1, tm, D), row), pl.BlockSpec((1, SUBLANES, dff), bat)],
        out_shape=[jax.ShapeDtypeStruct((B, T, D), F32), jax.ShapeDtypeStruct((B, SUBLANES, dff), F32)],
        scratch_shapes=[pltpu.VMEM((n_sub, 2, tm // n_sub + SUBLANES, ff_chunk), F32),
                        pltpu.VMEM((SUBLANES, dff), F32)],
        compiler_params=_params(("parallel", "arbitrary")),
        name="post_mixer",
    )(x, att, ohg, mk_bf, mv_bf, cinit, *gains, *weights, conv_w, conv_b)


def _step_mix_kernel(x_ref, att_ref, ohg_ref, gc_ref, wo_ref, wq_ref, x1_ref, cq_ref):
    x1 = _mix_out(x_ref[...], att_ref[...].astype(BF16), ohg_ref[...].astype(BF16), wo_ref)
    x1_ref[...] = x1
    cq_ref[...] = _dot(_rms(x1, gc_ref[...]).astype(BF16), wq_ref[...])


def _step_cross_kernel(cq_ref, mk_ref, mv_ref, co_ref, *, n_q):
    _, n_mem, n_heads, dh = mk_ref.shape
    mk = mk_ref[0].reshape(n_mem * n_heads, dh).astype(BF16)
    mv = mv_ref[0].reshape(n_mem * n_heads, dh).astype(BF16)
    s = _dot_nt((cq_ref[0] * dh ** -0.5).astype(BF16), mk)
    row = lax.broadcasted_iota(jnp.int32, s.shape, 0)
    col = lax.broadcasted_iota(jnp.int32, s.shape, 1)
    s = jnp.where((col % n_heads) == (row // n_q), s, NEG)
    p = jnp.exp(s - jnp.max(s, axis=-1, keepdims=True))
    co_ref[0] = _dot(p.astype(BF16), mv) / jnp.sum(p, axis=-1, keepdims=True)


def _step_ffn_kernel(x1_ref, co_ref, prev_ref, gf_ref, gl_ref, wc_ref, wg_ref, wu_ref, wd_ref,
                     cw_ref, cb_ref, y_ref, ug_ref, buf_ref, *, ff_chunk, n_t):
    n = x1_ref.shape[0]
    pad = SUBLANES
    n_taps = cw_ref.shape[0]
    x2 = x1_ref[...] + _dot(co_ref[...].astype(BF16), wc_ref[...])
    t_of_row = lax.broadcasted_iota(jnp.int32, (n, ff_chunk), 0) % n_t

    def taps_fn(ug, cs, _):
        ug_ref[:, cs] = ug
        buf_ref[0:pad, cs] = jnp.zeros((pad, ff_chunk), F32)
        buf_ref[pad:pad + n, cs] = ug
        taps = []
        for j in range(n_taps - 1):
            shift = n_taps - 1 - j
            shifted = buf_ref[pad - shift:pad - shift + n, cs]
            taps.append(jnp.where(t_of_row < shift, prev_ref[j, :, cs], shifted))
        return taps + [ug]

    u_bf = _rms(x2, gf_ref[...]).astype(BF16)
    x3 = x2 + _ffn(u_bf, taps_fn, wg_ref, wu_ref, wd_ref, cw_ref, cb_ref, ff_chunk)
    y_ref[...] = _rms(x3, gl_ref[...])


def _single_call(kernel, args, out_shape, name, scratch_shapes=()):
    return pl.pallas_call(
        kernel,
        grid=(1,),
        in_specs=[_const_spec(a.shape) for a in args],
        out_specs=[pl.BlockSpec(o.shape, lambda *_, nd=len(o.shape): (0,) * nd) for o in out_shape],
        out_shape=out_shape,
        scratch_shapes=list(scratch_shapes),
        compiler_params=_params(("arbitrary",)),
        name=name,
    )(*args)


def _step_cross(cq, mem_k, mem_v, *, n_q):
    B, n_rows, dh = cq.shape
    _, n_mem, x_heads, _ = mem_k.shape
    bmap = lambda b: (b, 0, 0)
    mem = pl.BlockSpec((1, n_mem, x_heads, dh), lambda b: (b, 0, 0, 0))
    return pl.pallas_call(
        functools.partial(_step_cross_kernel, n_q=n_q),
        grid=(B,),
        in_specs=[pl.BlockSpec((1, n_rows, dh), bmap), mem, mem],
        out_specs=pl.BlockSpec((1, n_rows, dh), bmap),
        out_shape=jax.ShapeDtypeStruct((B, n_rows, dh), F32),
        compiler_params=_params(("parallel",)),
        name="step_cross",
    )(cq, mem_k, mem_v)


def kernel(x_prompt, x_sample, cache_win_k, cache_win_v, state_hgrn, state_ffn_conv, cache_mem_k, cache_mem_v,
           mem_prompt, hg_lb_logits, norm_mix, w_in, att_out_norm, hg_out_norm, w_out, norm_cross, norm_mem,
           w_cq, w_ck, w_cv, w_co, norm_ffn, w_gate, w_up, conv_w, conv_b, w_down, norm_final):
    Bp, T, D = x_prompt.shape
    Bs, Ts, _ = x_sample.shape
    depth, _, past, att_h, att_dh = cache_win_k.shape
    _, _, hg_h, hg_dk, hg_dv = state_hgrn.shape
    _, _, n_mem, x_h, x_dh = cache_mem_k.shape
    n_taps, dff = conv_w.shape[1:]
    att_w = att_h * att_dh
    hg_w = hg_h * hg_dk
    xw = x_h * x_dh
    keep = min(max(w for w, _ in DIL_PATTERNS), T)
    assert depth == 1 and hg_dk == hg_dv == LANES and x_dh == LANES and LANES % att_dh == 0
    assert all(w // d == ATT_BLOCK for w, d in DIL_PATTERNS) and past >= max(w for w, _ in DIL_PATTERNS)
    d_max = max(d for _, d in DIL_PATTERNS)
    assert T % (ATT_BLOCK * d_max) == 0 and past % d_max == 0 and past % (4 * ATT_BLOCK) == 0
    assert n_taps - 1 <= min(Ts, SUBLANES) and Ts <= SUBLANES
    layer = 0
    ff_chunk = 2 * LANES
    q_scale = att_dh ** -0.5
    row2 = lambda a: a.reshape(1, -1)

    w_in_bf = w_in[layer].astype(BF16)
    w_ckv_bf = jnp.concatenate([w_ck[layer], w_cv[layer]], axis=1).astype(BF16)
    weights = tuple(w[layer].astype(BF16) for w in (w_out, w_cq, w_co, w_gate, w_up, w_down))
    g_mix, g_att, g_hg = row2(norm_mix[layer]), row2(att_out_norm[layer]), row2(hg_out_norm[layer])
    g_cross, g_mem, g_ffn, g_final = (row2(norm_cross[layer]), row2(norm_mem[layer]), row2(norm_ffn[layer]),
                                      row2(norm_final))
    cw, cb = conv_w[layer], row2(conv_b[layer])

    q, k, v, k_keep, v_keep, ohg, s_prompt = _in_proj_hgrn(
        x_prompt, g_mix, w_in_bf, hg_lb_logits, g_hg, att_w=att_w, q_scale=q_scale, n_heads=hg_h, dk=hg_dk,
        tm=512, keep=keep, layer=layer)
    q, k, v = (a.reshape(Bp, T, att_w) for a in (q, k, v))
    ohg = ohg.reshape(Bp, T, hg_w)
    mk, mv, mk_bf, mv_bf = _mem_kv(mem_prompt.reshape(Bp * n_mem, D), g_mem, w_ckv_bf, tm=256)
    att = _dil_attn(q, k, v, g_att, n_heads=att_h, dh=att_dh, unroll=16)
    y_prompt, cst = _post(
        x_prompt, att, ohg, mk_bf.reshape(Bp, n_mem, xw), mv_bf.reshape(Bp, n_mem, xw),
        jnp.zeros((Bp, SUBLANES, dff), F32), (g_cross, g_ffn, g_final), weights, cw, cb,
        tm=512, n_sub=1, x_heads=x_h, ff_chunk=ff_chunk)

    n_s = Bs * Ts
    qs, ks, vs, ks_f, vs_f, hzs = _in_proj(x_sample.reshape(1, n_s, D), g_mix, w_in_bf, att_w=att_w, q_scale=q_scale,
                                           tm=n_s, keep=n_s)
    pad_new = lambda a: jnp.pad(a.reshape(Bs, Ts, att_w).astype(BF16), ((0, 0), (0, LANES - Ts), (0, 0)))
    feature_major = lambda c: jnp.transpose(c, (0, 2, 3, 1)).reshape(Bs, att_w, past)
    att_s, ohg_s, s_sample = _step_mixers(
        qs.reshape(Bs, Ts, att_w), pad_new(ks), pad_new(vs), feature_major(cache_win_k[layer]),
        feature_major(cache_win_v[layer]), g_att, hzs.reshape(Bs, Ts, 4 * hg_w), state_hgrn[layer], hg_lb_logits,
        g_hg, att_heads=att_h, dh=att_dh, layer=layer)
    x1_s, cq_s = _single_call(
        _step_mix_kernel,
        (x_sample.reshape(n_s, D), att_s.reshape(n_s, att_w), ohg_s.reshape(n_s, hg_w), g_cross, weights[0],
         weights[1]),
        [jax.ShapeDtypeStruct((n_s, D), F32), jax.ShapeDtypeStruct((n_s, xw), F32)], "step_mix")
    cq_heads = cq_s.reshape(Bs, Ts, x_h, x_dh).transpose(0, 2, 1, 3).reshape(Bs, x_h * Ts, x_dh)
    co_s = _step_cross(cq_heads, cache_mem_k[layer], cache_mem_v[layer], n_q=Ts)
    co_s = co_s.reshape(Bs, x_h, Ts, x_dh).transpose(0, 2, 1, 3)
    conv_state = state_ffn_conv[layer]
    t_idx = jnp.arange(Ts)
    prev = jnp.stack([jnp.take(conv_state, jnp.clip(j + t_idx, 0, n_taps - 2), axis=1).reshape(n_s, dff)
                      for j in range(n_taps - 1)])
    y_s, ug_s = _single_call(
        functools.partial(_step_ffn_kernel, ff_chunk=ff_chunk, n_t=Ts),
        (x1_s, co_s.reshape(n_s, xw), prev, g_ffn, g_final, weights[2], weights[3], weights[4], weights[5], cw, cb),
        [jax.ShapeDtypeStruct((n_s, D), F32), jax.ShapeDtypeStruct((n_s, dff), F32)], "step_ffn",
        scratch_shapes=[pltpu.VMEM((n_s + SUBLANES, dff), F32)])

    stack = lambda a: a[None]
    return (y_prompt, y_s.reshape(Bs, Ts, D),
            stack(k_keep.reshape(Bp, keep, att_h, att_dh)), stack(v_keep.reshape(Bp, keep, att_h, att_dh)),
            stack(s_prompt), stack(cst[:, SUBLANES - (n_taps - 1):]),
            stack(mk.reshape(Bp, n_mem, x_h, x_dh)), stack(mv.reshape(Bp, n_mem, x_h, x_dh)),
            stack(ks_f.reshape(Bs, Ts, att_h, att_dh)), stack(vs_f.reshape(Bs, Ts, att_h, att_dh)),
            stack(s_sample), stack(ug_s.reshape(Bs, Ts, dff)[:, Ts - (n_taps - 1):]))
```

```python
import functools

import jax
import jax.numpy as jnp
from jax import lax
from jax.experimental import pallas as pl
from jax.experimental.pallas import tpu as pltpu

F32 = jnp.float32
BF16 = jnp.bfloat16
EPS = 1e-6
DIL_PATTERNS = ((128, 1), (512, 4), (2048, 16))
ATT_BLOCK = 128
LANES = 128
SUBLANES = 8
HG_CHUNK = 64
HG_SUB = 8
NEG = -1e30
LOG2E = 1.4426950408889634
HIGHEST = lax.Precision.HIGHEST
NT_DIMS = (((1,), (1,)), ((), ()))
TN_DIMS = (((0,), (0,)), ((), ()))
VMEM_LIMIT = 56 * 1024 * 1024


def _dot(a, b):
    return jnp.dot(a, b, preferred_element_type=F32)


def _dot_nt(a, b):
    return lax.dot_general(a, b, NT_DIMS, preferred_element_type=F32)


def _sigmoid(x):
    return 1.0 / (1.0 + jnp.exp(-x))


def _rms(x, g):
    return x * lax.rsqrt(jnp.mean(x * x, axis=-1, keepdims=True) + EPS) * g


def _const_spec(shape):
    nd = len(shape)
    return pl.BlockSpec(shape, lambda *_: (0,) * nd, pipeline_mode=pl.Buffered(1))


def _params(sem):
    return pltpu.CompilerParams(dimension_semantics=sem, vmem_limit_bytes=VMEM_LIMIT)


def _in_proj_kernel(x_ref, g_ref, w_ref, q_ref, k_ref, v_ref, kf_ref, vf_ref, hz_ref, *, att_w, q_scale):
    h = _rms(x_ref[0], g_ref[...]).astype(BF16)

    def proj(lo, hi):
        return _dot(h, w_ref[:, lo:hi])

    q_ref[0] = proj(0, att_w) * q_scale
    zk = proj(att_w, 2 * att_w)
    k_ref[0] = zk
    kf_ref[0] = zk
    zv = proj(2 * att_w, 3 * att_w)
    v_ref[0] = zv
    vf_ref[0] = zv
    base = 3 * att_w
    step = 4 * LANES
    for c in range(0, hz_ref.shape[-1], step):
        hz_ref[0, :, c:c + step] = proj(base + c, base + c + step)


def _in_proj(x, g, w_bf, *, att_w, q_scale, tm, keep):
    B, T, D = x.shape
    ncols = w_bf.shape[1]
    hzw = ncols - 3 * att_w
    n_t = T // tm
    first = n_t - keep // tm
    row = lambda b, i: (b, i, 0)
    keep_map = lambda b, i: (b, jnp.maximum(i - first, 0), 0)
    return pl.pallas_call(
        functools.partial(_in_proj_kernel, att_w=att_w, q_scale=q_scale),
        grid=(B, n_t),
        in_specs=[pl.BlockSpec((1, tm, D), row), _const_spec((1, D)), _const_spec((D, ncols))],
        out_specs=[pl.BlockSpec((1, tm, att_w), row)] * 3
        + [pl.BlockSpec((1, tm, att_w), keep_map)] * 2
        + [pl.BlockSpec((1, tm, hzw), row)],
        out_shape=[jax.ShapeDtypeStruct((B, T, att_w), F32)] * 3
        + [jax.ShapeDtypeStruct((B, keep, att_w), F32)] * 2
        + [jax.ShapeDtypeStruct((B, T, hzw), F32)],
        compiler_params=_params(("parallel", "arbitrary")),
        name="in_proj",
    )(x, g, w_bf)


def _mem_kv_kernel(m_ref, g_ref, w_ref, kf_ref, vf_ref, kb_ref, vb_ref):
    h = _rms(m_ref[...], g_ref[...]).astype(BF16)
    xw = w_ref.shape[1] // 2
    k = _dot(h, w_ref[:, :xw])
    v = _dot(h, w_ref[:, xw:])
    kf_ref[...] = k
    vf_ref[...] = v
    kb_ref[...] = k.astype(BF16)
    vb_ref[...] = v.astype(BF16)


def _mem_kv(mem2d, g, w_ckv_bf, *, tm):
    n, D = mem2d.shape
    xw = w_ckv_bf.shape[1] // 2
    row = lambda i: (i, 0)
    return pl.pallas_call(
        _mem_kv_kernel,
        grid=(n // tm,),
        in_specs=[pl.BlockSpec((tm, D), row), _const_spec((1, D)), _const_spec((D, 2 * xw))],
        out_specs=[pl.BlockSpec((tm, xw), row)] * 4,
        out_shape=[jax.ShapeDtypeStruct((n, xw), F32)] * 2 + [jax.ShapeDtypeStruct((n, xw), BF16)] * 2,
        compiler_params=_params(("parallel",)),
        name="mem_kv",
    )(mem2d, g, w_ckv_bf)


def _dil_attn_kernel(q_ref, k_ref, v_ref, g_ref, o_ref, acc_sc, m_sc, den_sc, bias_sc, *, n_heads, dh, unroll):
    tile = pl.program_id(1)
    T = q_ref.shape[1]
    blk = ATT_BLOCK
    hpt = LANES // dh
    rows, ncol = hpt * blk, 2 * blk
    rr = lax.broadcasted_iota(jnp.int32, (rows, ncol), 0)
    cc = lax.broadcasted_iota(jnp.int32, (rows, ncol), 1)
    delta = (rr % blk) + blk - cc
    in_band = (delta >= 0) & (delta <= blk)
    slope = jnp.exp2((-8.0 / n_heads) * (tile * hpt + rr // blk + 1).astype(F32))
    lane_head = lax.broadcasted_iota(jnp.int32, (blk, LANES), 1) // dh
    n_pat = len(DIL_PATTERNS)

    def per_lane(x):
        out = jnp.broadcast_to(x[0:blk], (blk, LANES))
        for e in range(1, hpt):
            out = jnp.where(lane_head == e, jnp.broadcast_to(x[e * blk:(e + 1) * blk], (blk, LANES)), out)
        return out

    ones = jnp.ones((ncol, LANES), BF16)
    order = sorted(range(n_pat), key=lambda p: -DIL_PATTERNS[p][1])
    for step, p in enumerate(order):
        dil = DIL_PATTERNS[p][1]
        alibi = (-LOG2E) * slope * (delta * dil).astype(F32)
        bias_sc[0] = jnp.where(in_band & (cc >= blk), alibi, NEG)
        bias_sc[1] = jnp.where(in_band, alibi, NEG)
        nblk = T // (blk * dil)

        def body(n, carry, first=step == 0, last=step == n_pat - 1, dil=dil, nblk=nblk):
            kp, vp = carry
            i = n % nblk
            start = n // nblk + i * (blk * dil)
            idx = pl.ds(start, blk, stride=dil) if dil > 1 else pl.ds(pl.multiple_of(start, blk), blk)
            q2 = q_ref[0, idx, :] * LOG2E
            kc = k_ref[0, idx, :].astype(BF16)
            vc = v_ref[0, idx, :].astype(BF16)
            qs = jnp.concatenate([jnp.where(lane_head == e, q2, 0.0) for e in range(hpt)], axis=0).astype(BF16)
            s = _dot_nt(qs, jnp.concatenate([kp, kc], axis=0)) + bias_sc[jnp.minimum(i, 1)]
            m = jnp.max(s, axis=-1, keepdims=True)
            pr = jnp.exp2(s - m).astype(BF16)
            pv = _dot(pr, jnp.concatenate([jnp.concatenate([vp, vc], axis=0), ones], axis=1))
            m_t, acc_t, den_t = per_lane(m), per_lane(pv[:, :LANES]), per_lane(pv[:, LANES:])
            if not first:
                m_old = m_sc[idx, :]
                m_new = jnp.maximum(m_old, m_t)
                a_old, a_t = jnp.exp2(m_old - m_new), jnp.exp2(m_t - m_new)
                den_t = a_old * den_sc[idx, :] + a_t * den_t
                acc_t = a_old * acc_sc[idx, :] + a_t * acc_t
                m_t = m_new
            if last:
                acc_sc[idx, :] = acc_t / den_t
            else:
                m_sc[idx, :] = m_t
                den_sc[idx, :] = den_t
                acc_sc[idx, :] = acc_t
            return kc, vc

        zero = jnp.zeros((blk, LANES), BF16)
        lax.fori_loop(0, T // blk, body, (zero, zero), unroll=unroll)

    ch = 4 * blk
    same_head = (lax.broadcasted_iota(jnp.int32, (LANES, LANES), 0) // dh
                 == lax.broadcasted_iota(jnp.int32, (LANES, LANES), 1) // dh).astype(BF16)
    for c in range(T // ch):
        a = acc_sc[c * ch:(c + 1) * ch, :]
        mean_sq = _dot((a * a).astype(BF16), same_head) * (1.0 / dh)
        o_ref[0, c * ch:(c + 1) * ch, :] = (a * lax.rsqrt(mean_sq + EPS) * g_ref[...]).astype(BF16)


def _dil_attn(q, k, v, g, *, n_heads, dh, unroll):
    B, T, W = q.shape
    blk = ATT_BLOCK
    seq = pl.BlockSpec((1, T, LANES), lambda b, t: (b, 0, t))
    return pl.pallas_call(
        functools.partial(_dil_attn_kernel, n_heads=n_heads, dh=dh, unroll=unroll),
        grid=(B, W // LANES),
        in_specs=[seq, seq, seq, pl.BlockSpec((1, LANES), lambda b, t: (0, t))],
        out_specs=seq,
        out_shape=jax.ShapeDtypeStruct((B, T, W), BF16),
        scratch_shapes=[pltpu.VMEM((T, LANES), F32)] * 3
        + [pltpu.VMEM((2, (LANES // dh) * blk, 2 * blk), F32)],
        compiler_params=_params(("parallel", "arbitrary")),
        name="dil_attn",
    )(q, k, v, g)


def _dil_attn_step_kernel(q_ref, kn_ref, vn_ref, kt_ref, vt_ref, g_ref, o_ref, cnt_sc, bias_sc,
                          *, n_heads, dh, n_q):
    past = kt_ref.shape[2]
    n_new = LANES
    W = q_ref.shape[-1]
    rows = n_q * n_heads

    def new_rows(ref):
        sub8 = lax.broadcasted_iota(jnp.int32, (SUBLANES, W), 0)
        tile = jnp.zeros((SUBLANES, W), F32)
        for i in range(n_q):
            tile = jnp.where(sub8 == i, jnp.broadcast_to(ref[0, i:i + 1, :], (SUBLANES, W)), tile)
        return jnp.concatenate([tile, jnp.zeros((n_new - SUBLANES, W), F32)], axis=0).astype(BF16)
    n_cols = past + n_new

    @pl.when(pl.program_id(0) == 0)
    def _():
        r = lax.broadcasted_iota(jnp.int32, (rows, n_cols), 0)
        c = lax.broadcasted_iota(jnp.int32, (rows, n_cols), 1)
        delta = past + r // n_heads - c
        cnt = jnp.zeros((rows, n_cols), F32)
        for win, dil in DIL_PATTERNS:
            cnt = cnt + ((delta >= 0) & (delta <= win) & ((delta & (dil - 1)) == 0)).astype(F32)
        slope = jnp.exp2((-8.0 / n_heads) * ((r % n_heads) + 1).astype(F32))
        cnt_sc[...] = cnt
        bias_sc[...] = jnp.where(cnt > 0.0, -slope * delta.astype(F32), NEG)

    sub = lax.broadcasted_iota(jnp.int32, (n_heads, W), 0)
    lane_head = lax.broadcasted_iota(jnp.int32, (n_heads, W), 1) // dh
    qbd = jnp.concatenate(
        [jnp.where(sub == lane_head, jnp.broadcast_to(q_ref[0, i:i + 1, :], (n_heads, W)), 0.0) for i in range(n_q)],
        axis=0).astype(BF16)
    s = jnp.concatenate([_dot(qbd, kt_ref[0].astype(BF16)), _dot_nt(qbd, new_rows(kn_ref))], axis=1) + bias_sc[...]
    p = jnp.exp(s - jnp.max(s, axis=-1, keepdims=True)) * cnt_sc[...]
    den = jnp.sum(p, axis=-1, keepdims=True)
    p = p.astype(BF16)
    out = (_dot_nt(p[:, :past], vt_ref[0].astype(BF16)) + _dot(p[:, past:], new_rows(vn_ref))) / den
    r = lax.broadcasted_iota(jnp.int32, (rows, W), 0)
    c = lax.broadcasted_iota(jnp.int32, (rows, W), 1)
    out = jnp.where((r % n_heads) == (c // dh), out, 0.0)
    out = out * lax.rsqrt(jnp.sum(out * out, axis=-1, keepdims=True) * (1.0 / dh) + EPS)
    o_ref[0] = jnp.sum(out.reshape(n_q, n_heads, W), axis=1) * g_ref[...]


def _lower_bound(lbl_ref, layer):
    logits = lbl_ref[...]
    e = jnp.exp(logits - jnp.max(logits, axis=0, keepdims=True))
    return jnp.sum(e[:layer + 1], axis=0, keepdims=True) / jnp.sum(e, axis=0, keepdims=True)


def _cumsum_rows(tril_bf, x):
    x1 = x.astype(BF16)
    r1 = x - x1.astype(F32)
    x2 = r1.astype(BF16)
    x3 = (r1 - x2.astype(F32)).astype(BF16)
    return _dot(tril_bf, x1) + _dot(tril_bf, x2) + _dot(tril_bf, x3)


def _hgrn_rows(hz_ref, o_ref, lb, gn_ref, st_ref, b_sc, c_sc, *, n_heads, dk):
    hw = n_heads * dk
    tc = hz_ref.shape[0]
    ch, sub = HG_CHUNK, HG_SUB
    r_c = lax.broadcasted_iota(jnp.int32, (ch, ch), 0)
    c_c = lax.broadcasted_iota(jnp.int32, (ch, ch), 1)
    tril = (r_c >= c_c).astype(BF16)
    r_s = lax.broadcasted_iota(jnp.int32, (sub, ch), 0)
    c_s = lax.broadcasted_iota(jnp.int32, (sub, ch), 1)

    local = [[None] * n_heads for _ in range(tc // ch)]
    for c in range(tc // ch):
        rows = slice(c * ch, (c + 1) * ch)
        hq = hz_ref[rows, 0:hw]
        f = lb + (1.0 - lb) * _sigmoid(hz_ref[rows, hw:2 * hw])
        qh = hq * _sigmoid(hq) * dk ** -0.5
        kk = 1.0 - f
        b = _cumsum_rows(tril, jnp.log(f) * LOG2E)
        b_rows, c_rows = b_sc.at[c], c_sc.at[c]
        b_rows[...] = b
        c_rows[...] = b - jnp.log(kk) * LOG2E
        for h in range(n_heads):
            hs = slice(h * dk, (h + 1) * dk)
            bh = b[:, hs]
            kh = kk[:, hs]
            qhh = qh[:, hs]
            vh = hz_ref[rows, 2 * hw + h * dk:2 * hw + (h + 1) * dk].astype(BF16)
            blocks = []
            for j in range(ch // sub):
                r0 = j * sub
                bj = bh[r0:r0 + sub]
                qj = qhh[r0:r0 + sub]
                a_j = jnp.zeros((sub, ch), F32)
                for s in range(sub):
                    a = jnp.sum(qj * jnp.exp2(bj - c_rows[r0 + s:r0 + s + 1, hs]), axis=-1, keepdims=True)
                    a_j = jnp.where(c_s == r0 + s, a, a_j)
                if j > 0:
                    beta = b_rows[r0 - 1:r0, hs]
                    qt = qj * jnp.exp2(bj - beta)
                    kt = kh[0:r0] * jnp.exp2(beta - bh[0:r0])
                    kt = jnp.concatenate([kt, jnp.zeros((ch - r0, dk), F32)], axis=0)
                    a_j = a_j + _dot_nt(qt.astype(BF16), kt.astype(BF16))
                blocks.append(jnp.where(c_s - r0 <= r_s, a_j, 0.0))
            b_last = b_rows[ch - 1:ch, hs]
            hg = hz_ref[rows, 3 * hw + h * dk:3 * hw + (h + 1) * dk]
            local[c][h] = dict(
                q=(qhh * jnp.exp2(bh)).astype(BF16), a=jnp.concatenate(blocks, axis=0).astype(BF16), v=vh,
                k=(kh * jnp.exp2(b_last - bh)).astype(BF16), decay=jnp.exp2(b_last), gate=hg * _sigmoid(hg))

    state = [st_ref[h] for h in range(n_heads)]
    for c in range(tc // ch):
        rows = slice(c * ch, (c + 1) * ch)
        for h in range(n_heads):
            hs = slice(h * dk, (h + 1) * dk)
            t = local[c][h]
            o = _dot_nt(t["q"], state[h].astype(BF16)) + _dot(t["a"], t["v"])
            state[h] = state[h] * t["decay"] + lax.dot_general(t["v"], t["k"], TN_DIMS, preferred_element_type=F32)
            o_ref[rows, hs] = (_rms(o, gn_ref[:, hs]) * t["gate"]).astype(BF16)
    for h in range(n_heads):
        st_ref[h] = state[h]


def _in_proj_hgrn_kernel(x_ref, g_ref, w_ref, lbl_ref, gn_ref, q_ref, k_ref, v_ref, kf_ref, vf_ref, o_ref, s_ref,
                         hz_sc, st_ref, b_sc, c_sc, *, att_w, q_scale, n_heads, dk, layer, tiles_per_seq):
    g = pl.program_id(0)
    t_rec = jnp.maximum(g - 1, 0)

    @pl.when(g == 0)
    def _():
        hz_sc[...] = jnp.zeros_like(hz_sc)

    @pl.when(t_rec % tiles_per_seq == 0)
    def _():
        st_ref[...] = jnp.zeros_like(st_ref)

    lb = _lower_bound(lbl_ref, layer)
    slot_w = g % 2
    hz = hz_sc.at[1 - slot_w]

    h = _rms(x_ref[...], g_ref[...]).astype(BF16)

    def proj(lo, hi):
        return _dot(h, w_ref[:, lo:hi])

    q_ref[...] = proj(0, att_w) * q_scale
    zk = proj(att_w, 2 * att_w)
    k_ref[...] = zk
    kf_ref[0] = zk
    zv = proj(2 * att_w, 3 * att_w)
    v_ref[...] = zv
    vf_ref[0] = zv
    base = 3 * att_w
    step = 4 * LANES
    for c in range(0, hz_sc.shape[-1], step):
        hz_sc[slot_w, :, c:c + step] = proj(base + c, base + c + step)

    _hgrn_rows(hz, o_ref, lb, gn_ref, st_ref, b_sc, c_sc, n_heads=n_heads, dk=dk)

    @pl.when((g > 0) & (t_rec % tiles_per_seq == tiles_per_seq - 1))
    def _():
        for h in range(n_heads):
            s_ref[0, h] = st_ref[h].T


def _in_proj_hgrn(x, g, w_bf, lb_logits, gn, *, att_w, q_scale, n_heads, dk, tm, keep, layer):
    B, T, D = x.shape
    hw = n_heads * dk
    n_t = T // tm
    n_tiles = B * n_t
    n_keep = keep // tm
    first = n_t - n_keep
    proj_tile = lambda s: jnp.minimum(s, n_tiles - 1)
    rec_tile = lambda s: jnp.maximum(s - 1, 0)
    rows = lambda s: (proj_tile(s), 0)
    keep_rows = lambda s: (proj_tile(s) // n_t, jnp.maximum(proj_tile(s) % n_t - first, 0), 0)
    f32 = lambda n, w: jax.ShapeDtypeStruct((n, w), F32)
    return pl.pallas_call(
        functools.partial(_in_proj_hgrn_kernel, att_w=att_w, q_scale=q_scale, n_heads=n_heads, dk=dk, layer=layer,
                          tiles_per_seq=n_t),
        grid=(n_tiles + 1,),
        in_specs=[pl.BlockSpec((tm, D), rows), _const_spec((1, D)), _const_spec(w_bf.shape),
                  _const_spec(lb_logits.shape), _const_spec((1, hw))],
        out_specs=[pl.BlockSpec((tm, att_w), rows)] * 3 + [pl.BlockSpec((1, tm, att_w), keep_rows)] * 2
        + [pl.BlockSpec((tm, hw), lambda s: (rec_tile(s), 0)),
           pl.BlockSpec((1, n_heads, dk, dk), lambda s: (rec_tile(s) // n_t, 0, 0, 0))],
        out_shape=[f32(B * T, att_w)] * 3 + [jax.ShapeDtypeStruct((B, keep, att_w), F32)] * 2
        + [jax.ShapeDtypeStruct((B * T, hw), BF16), jax.ShapeDtypeStruct((B, n_heads, dk, dk), F32)],
        scratch_shapes=[pltpu.VMEM((2, tm, 4 * hw), F32), pltpu.VMEM((n_heads, dk, dk), F32)]
        + [pltpu.VMEM((tm // HG_CHUNK, HG_CHUNK, hw), F32)] * 2,
        compiler_params=_params(("arbitrary",)),
        name="in_proj_hgrn",
    )(x.reshape(B * T, D), g, w_bf, lb_logits, gn)


def _hgrn_step_kernel(hz_ref, s0_ref, lbl_ref, gn_ref, o_ref, s_ref, *, n_heads, dk, layer):
    hw = n_heads * dk
    n_t = hz_ref.shape[1]
    lb = _lower_bound(lbl_ref, layer)
    q, k, v, b, gate = [], [], [], [], []
    acc = jnp.zeros((1, hw), F32)
    for t in range(n_t):
        hq = hz_ref[0, t:t + 1, 0:hw]
        f = lb + (1.0 - lb) * _sigmoid(hz_ref[0, t:t + 1, hw:2 * hw])
        acc = acc + jnp.log(f)
        q.append(hq * _sigmoid(hq) * dk ** -0.5)
        k.append(1.0 - f)
        v.append(hz_ref[0, t:t + 1, 2 * hw:3 * hw])
        b.append(acc)
        hg = hz_ref[0, t:t + 1, 3 * hw:4 * hw]
        gate.append(hg * _sigmoid(hg))
    rr = lax.broadcasted_iota(jnp.int32, (dk, dk), 0)
    cc = lax.broadcasted_iota(jnp.int32, (dk, dk), 1)

    def tile_of_rows(rows):
        tile = jnp.zeros((dk, dk), F32)
        for t, r in enumerate(rows):
            tile = jnp.where(rr == t, jnp.broadcast_to(r, (dk, dk)), tile)
        return tile

    for h in range(n_heads):
        hs = slice(h * dk, (h + 1) * dk)
        s0 = s0_ref[0, h]
        qe = tile_of_rows([q[t][:, hs] * jnp.exp(b[t][:, hs]) for t in range(n_t)])
        inter = jnp.dot(qe, s0, precision=HIGHEST, preferred_element_type=F32)
        for t in range(n_t):
            o = inter[t:t + 1]
            for s in range(t + 1):
                a = jnp.sum(q[t][:, hs] * jnp.exp(b[t][:, hs] - b[s][:, hs]) * k[s][:, hs], axis=-1, keepdims=True)
                o = o + a * v[s][:, hs]
            o_ref[0, t:t + 1, hs] = _rms(o, gn_ref[:, hs]) * gate[t][:, hs]
        b_last = b[n_t - 1][:, hs]
        khat = tile_of_rows([k[t][:, hs] * jnp.exp(b_last - b[t][:, hs]) for t in range(n_t)])
        vpad = tile_of_rows([v[t][:, hs] for t in range(n_t)])
        decay = jnp.where(rr == cc, jnp.broadcast_to(jnp.exp(b_last), (dk, dk)), 0.0)
        s_ref[0, h] = (jnp.dot(decay, s0, precision=HIGHEST, preferred_element_type=F32)
                       + jnp.dot(khat.T, vpad, precision=HIGHEST, preferred_element_type=F32))


def _step_mixers_kernel(q_ref, kn_ref, vn_ref, kt_ref, vt_ref, g_ref, hz_ref, s0_ref, lbl_ref, gn_ref,
                        o_ref, oh_ref, s_ref, cnt_sc, bias_sc, *, att_heads, dh, n_q, hg_heads, dk, layer):
    _dil_attn_step_kernel(q_ref, kn_ref, vn_ref, kt_ref, vt_ref, g_ref, o_ref, cnt_sc, bias_sc,
                          n_heads=att_heads, dh=dh, n_q=n_q)
    _hgrn_step_kernel(hz_ref, s0_ref, lbl_ref, gn_ref, oh_ref, s_ref, n_heads=hg_heads, dk=dk, layer=layer)


def _step_mixers(q, k_new, v_new, cache_kt, cache_vt, g_att, hz, s0, lb_logits, g_hg,
                 *, att_heads, dh, layer):
    B, n_q, W = q.shape
    past = cache_kt.shape[2]
    _, hg_heads, dk, dv = s0.shape
    hw = hg_heads * dk
    bmap = lambda b: (b, 0, 0)
    smap = lambda b: (b, 0, 0, 0)
    return pl.pallas_call(
        functools.partial(_step_mixers_kernel, att_heads=att_heads, dh=dh, n_q=n_q, hg_heads=hg_heads, dk=dk,
                          layer=layer),
        grid=(B,),
        in_specs=[pl.BlockSpec((1, n_q, W), bmap)] * 3
        + [pl.BlockSpec((1, W, past), bmap), pl.BlockSpec((1, W, past), bmap), _const_spec((1, W)),
                  pl.BlockSpec((1, n_q, 4 * hw), bmap), pl.BlockSpec((1, hg_heads, dk, dv), smap),
                  _const_spec(lb_logits.shape), _const_spec((1, hw))],
        out_specs=[pl.BlockSpec((1, n_q, W), bmap), pl.BlockSpec((1, n_q, hw), bmap),
                   pl.BlockSpec((1, hg_heads, dk, dv), smap)],
        out_shape=[jax.ShapeDtypeStruct((B, n_q, W), F32), jax.ShapeDtypeStruct((B, n_q, hw), F32),
                   jax.ShapeDtypeStruct((B, hg_heads, dk, dv), F32)],
        scratch_shapes=[pltpu.VMEM((n_q * att_heads, past + LANES), F32)] * 2,
        compiler_params=_params(("arbitrary",)),
        name="step_mixers",
    )(q, k_new, v_new, cache_kt, cache_vt, g_att, hz, s0, lb_logits, g_hg)


def _mix_out(x, att_bf, ohg_bf, wo_ref):
    aw = att_bf.shape[-1]
    return x + _dot(att_bf, wo_ref[0:aw, :]) + _dot(ohg_bf, wo_ref[aw:, :])


def _cross_attend(cq, head_k, head_v, n_heads, dh):
    heads = range(n_heads)
    scale = dh ** -0.5 * LOG2E
    scores = [_dot_nt((cq[:, h * dh:(h + 1) * dh] * scale).astype(BF16), head_k(h)) for h in heads]
    probs = [jnp.exp2(s - jnp.max(s, axis=-1, keepdims=True)).astype(BF16) for s in scores]
    ones = jnp.ones((probs[0].shape[1], dh), BF16)
    pvs = [_dot(probs[h], jnp.concatenate([head_v(h), ones], axis=1)) for h in heads]
    return jnp.concatenate([pv[:, :dh] / pv[:, dh:] for pv in pvs], axis=-1)


def _ffn(u_bf, taps_fn, wg_ref, wu_ref, wd_ref, cw_ref, cb_ref, ff_chunk):
    dff = wg_ref.shape[1]
    n_taps = cw_ref.shape[0]
    acc = jnp.zeros((u_bf.shape[0], wd_ref.shape[1]), F32)
    for n in range(dff // ff_chunk):
        cs = slice(n * ff_chunk, (n + 1) * ff_chunk)
        ug = _dot(u_bf, wg_ref[:, cs])
        taps = taps_fn(ug, cs, n)
        conv = cb_ref[:, cs] + cw_ref[n_taps - 1:n_taps, cs] * taps[n_taps - 1]
        for j in range(n_taps - 1):
            conv = conv + cw_ref[j:j + 1, cs] * taps[j]
        act = conv * _sigmoid(conv) * _dot(u_bf, wu_ref[:, cs])
        acc = acc + _dot(act.astype(BF16), wd_ref[cs, :])
    return acc


def _post_kernel(x_ref, att_ref, ohg_ref, mk_ref, mv_ref, cinit_ref, gc_ref, gf_ref, gl_ref, wo_ref, wq_ref,
                 wc_ref, wg_ref, wu_ref, wd_ref, cw_ref, cb_ref, y_ref, cst_ref, buf_ref, carry_ref,
                 *, x_heads, x_dh, ff_chunk, n_sub):
    i = pl.program_id(1)
    tm = x_ref.shape[1]
    pad = SUBLANES
    n_taps = cw_ref.shape[0]

    @pl.when(i == 0)
    def _():
        carry_ref[...] = cinit_ref[0]

    ts = tm // n_sub
    for t in range(n_sub):
        rows = slice(t * ts, (t + 1) * ts)
        x1 = _mix_out(x_ref[0, rows], att_ref[0, rows], ohg_ref[0, rows], wo_ref)
        cq = _dot(_rms(x1, gc_ref[...]).astype(BF16), wq_ref[...])
        co = _cross_attend(cq, lambda h: mk_ref[0, :, h * x_dh:(h + 1) * x_dh],
                           lambda h: mv_ref[0, :, h * x_dh:(h + 1) * x_dh], x_heads, x_dh)
        x2 = x1 + _dot(co.astype(BF16), wc_ref[...])

        def taps_fn(ug, cs, n, t=t):
            buf = buf_ref.at[t, n % buf_ref.shape[1]]
            buf[0:pad, :] = carry_ref[:, cs]
            buf[pad:pad + ts, :] = ug
            carry_ref[:, cs] = ug[ts - pad:ts]
            return [buf[pad - (n_taps - 1 - j):pad - (n_taps - 1 - j) + ts, :] for j in range(n_taps - 1)] + [ug]

        u_bf = _rms(x2, gf_ref[...]).astype(BF16)
        x3 = x2 + _ffn(u_bf, taps_fn, wg_ref, wu_ref, wd_ref, cw_ref, cb_ref, ff_chunk)
        y_ref[0, rows] = _rms(x3, gl_ref[...])
    cst_ref[0] = carry_ref[...]


def _post(x, att, ohg, mk_bf, mv_bf, cinit, gains, weights, conv_w, conv_b, *, tm, n_sub, x_heads, ff_chunk):
    B, T, D = x.shape
    aw, hw = att.shape[-1], ohg.shape[-1]
    n_mem, xw = mk_bf.shape[1:]
    dff = conv_w.shape[1]
    row = lambda b, i: (b, i, 0)
    bat = lambda b, i: (b, 0, 0)
    in_specs = ([pl.BlockSpec((1, tm, D), row), pl.BlockSpec((1, tm, aw), row), pl.BlockSpec((1, tm, hw), row)]
                + [pl.BlockSpec((1, n_mem, xw), bat)] * 2 + [pl.BlockSpec((1, SUBLANES, dff), bat)]
                + [_const_spec(a.shape) for a in tuple(gains) + tuple(weights) + (conv_w, conv_b)])
    return pl.pallas_call(
        functools.partial(_post_kernel, x_heads=x_heads, x_dh=xw // x_heads, ff_chunk=ff_chunk, n_sub=n_sub),
        grid=(B, T // tm),
        in_specs=in_specs,
        out_specs=[pl.BlockSpec((1, tm, D), row), pl.BlockSpec((1, SUBLANES, dff), bat)],
        out_shape=[jax.ShapeDtypeStruct((B, T, D), F32), jax.ShapeDtypeStruct((B, SUBLANES, dff), F32)],
        scratch_shapes=[pltpu.VMEM((n_sub, min(2, dff // ff_chunk), tm // n_sub + SUBLANES, ff_chunk), F32),
                        pltpu.VMEM((SUBLANES, dff), F32)],
        compiler_params=_params(("parallel", "arbitrary")),
        name="post_mixer",
    )(x, att, ohg, mk_bf, mv_bf, cinit, *gains, *weights, conv_w, conv_b)


def _step_mix_kernel(x_ref, att_ref, ohg_ref, gc_ref, wo_ref, wq_ref, x1_ref, cq_ref):
    x1 = _mix_out(x_ref[...], att_ref[...].astype(BF16), ohg_ref[...].astype(BF16), wo_ref)
    x1_ref[...] = x1
    cq_ref[...] = _dot(_rms(x1, gc_ref[...]).astype(BF16), wq_ref[...])


def _step_cross_kernel(cq_ref, mk_ref, mv_ref, co_ref, *, n_q):
    n_b, n_mem, n_heads, dh = mk_ref.shape
    row = lax.broadcasted_iota(jnp.int32, (cq_ref.shape[1], n_mem * n_heads), 0)
    col = lax.broadcasted_iota(jnp.int32, (cq_ref.shape[1], n_mem * n_heads), 1)
    own_head = (col % n_heads) == (row // n_q)
    for b in range(n_b):
        mk = mk_ref[b].reshape(n_mem * n_heads, dh).astype(BF16)
        mv = mv_ref[b].reshape(n_mem * n_heads, dh).astype(BF16)
        s = jnp.where(own_head, _dot_nt((cq_ref[b] * dh ** -0.5).astype(BF16), mk), NEG)
        p = jnp.exp(s - jnp.max(s, axis=-1, keepdims=True))
        co_ref[b] = _dot(p.astype(BF16), mv) / jnp.sum(p, axis=-1, keepdims=True)


def _step_ffn_kernel(x1_ref, co_ref, prev_ref, gf_ref, gl_ref, wc_ref, wg_ref, wu_ref, wd_ref,
                     cw_ref, cb_ref, y_ref, ug_ref, buf_ref, *, ff_chunk, n_t):
    n = x1_ref.shape[0]
    pad = SUBLANES
    n_taps = cw_ref.shape[0]
    x2 = x1_ref[...] + _dot(co_ref[...].astype(BF16), wc_ref[...])
    t_of_row = lax.broadcasted_iota(jnp.int32, (n, ff_chunk), 0) % n_t

    def taps_fn(ug, cs, _):
        ug_ref[:, cs] = ug
        buf_ref[0:pad, cs] = jnp.zeros((pad, ff_chunk), F32)
        buf_ref[pad:pad + n, cs] = ug
        taps = []
        for j in range(n_taps - 1):
            shift = n_taps - 1 - j
            shifted = buf_ref[pad - shift:pad - shift + n, cs]
            taps.append(jnp.where(t_of_row < shift, prev_ref[j, :, cs], shifted))
        return taps + [ug]

    u_bf = _rms(x2, gf_ref[...]).astype(BF16)
    x3 = x2 + _ffn(u_bf, taps_fn, wg_ref, wu_ref, wd_ref, cw_ref, cb_ref, ff_chunk)
    y_ref[...] = _rms(x3, gl_ref[...])


def _single_call(kernel, args, out_shape, name, scratch_shapes=()):
    return pl.pallas_call(
        kernel,
        grid=(1,),
        in_specs=[_const_spec(a.shape) for a in args],
        out_specs=[pl.BlockSpec(o.shape, lambda *_, nd=len(o.shape): (0,) * nd) for o in out_shape],
        out_shape=out_shape,
        scratch_shapes=list(scratch_shapes),
        compiler_params=_params(("arbitrary",)),
        name=name,
    )(*args)


def _step_cross(cq, mem_k, mem_v, *, n_q):
    B, n_rows, dh = cq.shape
    _, n_mem, x_heads, _ = mem_k.shape
    n_b = 4 if B % 4 == 0 else 1
    bmap = lambda b: (b, 0, 0)
    mem = pl.BlockSpec((n_b, n_mem, x_heads, dh), lambda b: (b, 0, 0, 0))
    return pl.pallas_call(
        functools.partial(_step_cross_kernel, n_q=n_q),
        grid=(B // n_b,),
        in_specs=[pl.BlockSpec((n_b, n_rows, dh), bmap), mem, mem],
        out_specs=pl.BlockSpec((n_b, n_rows, dh), bmap),
        out_shape=jax.ShapeDtypeStruct((B, n_rows, dh), F32),
        compiler_params=_params(("parallel",)),
        name="step_cross",
    )(cq, mem_k, mem_v)


def kernel(x_prompt, x_sample, cache_win_k, cache_win_v, state_hgrn, state_ffn_conv, cache_mem_k, cache_mem_v,
           mem_prompt, hg_lb_logits, norm_mix, w_in, att_out_norm, hg_out_norm, w_out, norm_cross, norm_mem,
           w_cq, w_ck, w_cv, w_co, norm_ffn, w_gate, w_up, conv_w, conv_b, w_down, norm_final):
    Bp, T, D = x_prompt.shape
    Bs, Ts, _ = x_sample.shape
    depth, _, past, att_h, att_dh = cache_win_k.shape
    _, _, hg_h, hg_dk, hg_dv = state_hgrn.shape
    _, _, n_mem, x_h, x_dh = cache_mem_k.shape
    n_taps, dff = conv_w.shape[1:]
    att_w = att_h * att_dh
    hg_w = hg_h * hg_dk
    xw = x_h * x_dh
    keep = min(max(w for w, _ in DIL_PATTERNS), T)
    assert depth == 1 and hg_dk == hg_dv == LANES and x_dh == LANES and LANES % att_dh == 0
    assert all(w // d == ATT_BLOCK for w, d in DIL_PATTERNS) and past >= max(w for w, _ in DIL_PATTERNS)
    d_max = max(d for _, d in DIL_PATTERNS)
    assert T % (ATT_BLOCK * d_max) == 0 and past % d_max == 0 and past % (4 * ATT_BLOCK) == 0
    assert n_taps - 1 <= min(Ts, SUBLANES) and Ts <= SUBLANES
    layer = 0
    ff_chunk = dff
    q_scale = att_dh ** -0.5
    row2 = lambda a: a.reshape(1, -1)

    w_in_bf = w_in[layer].astype(BF16)
    w_ckv_bf = jnp.concatenate([w_ck[layer], w_cv[layer]], axis=1).astype(BF16)
    weights = tuple(w[layer].astype(BF16) for w in (w_out, w_cq, w_co, w_gate, w_up, w_down))
    g_mix, g_att, g_hg = row2(norm_mix[layer]), row2(att_out_norm[layer]), row2(hg_out_norm[layer])
    g_cross, g_mem, g_ffn, g_final = (row2(norm_cross[layer]), row2(norm_mem[layer]), row2(norm_ffn[layer]),
                                      row2(norm_final))
    cw, cb = conv_w[layer], row2(conv_b[layer])

    q, k, v, k_keep, v_keep, ohg, s_prompt = _in_proj_hgrn(
        x_prompt, g_mix, w_in_bf, hg_lb_logits, g_hg, att_w=att_w, q_scale=q_scale, n_heads=hg_h, dk=hg_dk,
        tm=512, keep=keep, layer=layer)
    q, k, v = (a.reshape(Bp, T, att_w) for a in (q, k, v))
    ohg = ohg.reshape(Bp, T, hg_w)
    mk, mv, mk_bf, mv_bf = _mem_kv(mem_prompt.reshape(Bp * n_mem, D), g_mem, w_ckv_bf, tm=256)
    att = _dil_attn(q, k, v, g_att, n_heads=att_h, dh=att_dh, unroll=16)
    y_prompt, cst = _post(
        x_prompt, att, ohg, mk_bf.reshape(Bp, n_mem, xw), mv_bf.reshape(Bp, n_mem, xw),
        jnp.zeros((Bp, SUBLANES, dff), F32), (g_cross, g_ffn, g_final), weights, cw, cb,
        tm=512, n_sub=1, x_heads=x_h, ff_chunk=ff_chunk)

    n_s = Bs * Ts
    qs, ks, vs, ks_f, vs_f, hzs = _in_proj(x_sample.reshape(1, n_s, D), g_mix, w_in_bf, att_w=att_w, q_scale=q_scale,
                                           tm=n_s, keep=n_s)
    per_seq = lambda a: a.reshape(Bs, Ts, att_w)
    feature_major = lambda c: jnp.transpose(c, (0, 2, 3, 1)).reshape(Bs, att_w, past)
    att_s, ohg_s, s_sample = _step_mixers(
        per_seq(qs), per_seq(ks), per_seq(vs), feature_major(cache_win_k[layer]),
        feature_major(cache_win_v[layer]), g_att, hzs.reshape(Bs, Ts, 4 * hg_w), state_hgrn[layer], hg_lb_logits,
        g_hg, att_heads=att_h, dh=att_dh, layer=layer)
    x1_s, cq_s = _single_call(
        _step_mix_kernel,
        (x_sample.reshape(n_s, D), att_s.reshape(n_s, att_w), ohg_s.reshape(n_s, hg_w), g_cross, weights[0],
         weights[1]),
        [jax.ShapeDtypeStruct((n_s, D), F32), jax.ShapeDtypeStruct((n_s, xw), F32)], "step_mix")
    cq_heads = cq_s.reshape(Bs, Ts, x_h, x_dh).transpose(0, 2, 1, 3).reshape(Bs, x_h * Ts, x_dh)
    co_s = _step_cross(cq_heads, cache_mem_k[layer], cache_mem_v[layer], n_q=Ts)
    co_s = co_s.reshape(Bs, x_h, Ts, x_dh).transpose(0, 2, 1, 3)
    conv_state = state_ffn_conv[layer]
    t_idx = jnp.arange(Ts)
    prev = jnp.stack([jnp.take(conv_state, jnp.clip(j + t_idx, 0, n_taps - 2), axis=1).reshape(n_s, dff)
                      for j in range(n_taps - 1)])
    y_s, ug_s = _single_call(
        functools.partial(_step_ffn_kernel, ff_chunk=ff_chunk, n_t=Ts),
        (x1_s, co_s.reshape(n_s, xw), prev, g_ffn, g_final, weights[2], weights[3], weights[4], weights[5], cw, cb),
        [jax.ShapeDtypeStruct((n_s, D), F32), jax.ShapeDtypeStruct((n_s, dff), F32)], "step_ffn",
        scratch_shapes=[pltpu.VMEM((n_s + SUBLANES, dff), F32)])

    stack = lambda a: a[None]
    return (y_prompt, y_s.reshape(Bs, Ts, D),
            stack(k_keep.reshape(Bp, keep, att_h, att_dh)), stack(v_keep.reshape(Bp, keep, att_h, att_dh)),
            stack(s_prompt), stack(cst[:, SUBLANES - (n_taps - 1):]),
            stack(mk.reshape(Bp, n_mem, x_h, x_dh)), stack(mv.reshape(Bp, n_mem, x_h, x_dh)),
            stack(ks_f.reshape(Bs, Ts, att_h, att_dh)), stack(vs_f.reshape(Bs, Ts, att_h, att_dh)),
            stack(s_sample), stack(ug_s.reshape(Bs, Ts, dff)[:, Ts - (n_taps - 1):]))
```

```python
import functools

import jax
import jax.numpy as jnp
from jax import lax
from jax.experimental import pallas as pl
from jax.experimental.pallas import tpu as pltpu

F32 = jnp.float32
BF16 = jnp.bfloat16
EPS = 1e-6
DIL_PATTERNS = ((128, 1), (512, 4), (2048, 16))
ATT_BLOCK = 128
LANES = 128
SUBLANES = 8
HG_CHUNK = 64
HG_SUB = 8
NEG = -1e30
LOG2E = 1.4426950408889634
HIGHEST = lax.Precision.HIGHEST
NT_DIMS = (((1,), (1,)), ((), ()))
TN_DIMS = (((0,), (0,)), ((), ()))
VMEM_LIMIT = 56 * 1024 * 1024


def _dot(a, b):
    return jnp.dot(a, b, preferred_element_type=F32)


def _dot_nt(a, b):
    return lax.dot_general(a, b, NT_DIMS, preferred_element_type=F32)


def _sigmoid(x):
    return 1.0 / (1.0 + jnp.exp(-x))


def _rms(x, g):
    return x * lax.rsqrt(jnp.mean(x * x, axis=-1, keepdims=True) + EPS) * g


def _const_spec(shape):
    nd = len(shape)
    return pl.BlockSpec(shape, lambda *_: (0,) * nd, pipeline_mode=pl.Buffered(1))


def _params(sem):
    return pltpu.CompilerParams(dimension_semantics=sem, vmem_limit_bytes=VMEM_LIMIT)


def _in_proj_kernel(x_ref, g_ref, w_ref, q_ref, k_ref, v_ref, kf_ref, vf_ref, hz_ref, *, att_w, q_scale):
    h = _rms(x_ref[0], g_ref[...]).astype(BF16)

    def proj(lo, hi):
        return _dot(h, w_ref[:, lo:hi])

    q_ref[0] = proj(0, att_w) * q_scale
    zk = proj(att_w, 2 * att_w)
    k_ref[0] = zk
    kf_ref[0] = zk
    zv = proj(2 * att_w, 3 * att_w)
    v_ref[0] = zv
    vf_ref[0] = zv
    base = 3 * att_w
    step = 4 * LANES
    for c in range(0, hz_ref.shape[-1], step):
        hz_ref[0, :, c:c + step] = proj(base + c, base + c + step)


def _in_proj(x, g, w_bf, *, att_w, q_scale, tm, keep):
    B, T, D = x.shape
    ncols = w_bf.shape[1]
    hzw = ncols - 3 * att_w
    n_t = T // tm
    first = n_t - keep // tm
    row = lambda b, i: (b, i, 0)
    keep_map = lambda b, i: (b, jnp.maximum(i - first, 0), 0)
    return pl.pallas_call(
        functools.partial(_in_proj_kernel, att_w=att_w, q_scale=q_scale),
        grid=(B, n_t),
        in_specs=[pl.BlockSpec((1, tm, D), row), _const_spec((1, D)), _const_spec((D, ncols))],
        out_specs=[pl.BlockSpec((1, tm, att_w), row)] * 3
        + [pl.BlockSpec((1, tm, att_w), keep_map)] * 2
        + [pl.BlockSpec((1, tm, hzw), row)],
        out_shape=[jax.ShapeDtypeStruct((B, T, att_w), F32)] * 3
        + [jax.ShapeDtypeStruct((B, keep, att_w), F32)] * 2
        + [jax.ShapeDtypeStruct((B, T, hzw), F32)],
        compiler_params=_params(("parallel", "arbitrary")),
        name="in_proj",
    )(x, g, w_bf)


def _mem_kv_kernel(m_ref, g_ref, w_ref, kf_ref, vf_ref, kb_ref, vb_ref):
    h = _rms(m_ref[...], g_ref[...]).astype(BF16)
    xw = w_ref.shape[1] // 2
    k = _dot(h, w_ref[:, :xw])
    v = _dot(h, w_ref[:, xw:])
    kf_ref[...] = k
    vf_ref[...] = v
    kb_ref[...] = k.astype(BF16)
    vb_ref[...] = v.astype(BF16)


def _mem_kv(mem2d, g, w_ckv_bf, *, tm):
    n, D = mem2d.shape
    xw = w_ckv_bf.shape[1] // 2
    row = lambda i: (i, 0)
    return pl.pallas_call(
        _mem_kv_kernel,
        grid=(n // tm,),
        in_specs=[pl.BlockSpec((tm, D), row), _const_spec((1, D)), _const_spec((D, 2 * xw))],
        out_specs=[pl.BlockSpec((tm, xw), row)] * 4,
        out_shape=[jax.ShapeDtypeStruct((n, xw), F32)] * 2 + [jax.ShapeDtypeStruct((n, xw), BF16)] * 2,
        compiler_params=_params(("parallel",)),
        name="mem_kv",
    )(mem2d, g, w_ckv_bf)


def _dil_attn_kernel(q_ref, k_ref, v_ref, g_ref, o_ref, acc_sc, m_sc, den_sc, bias_sc, *, n_heads, dh, unroll):
    tile = pl.program_id(1)
    T = q_ref.shape[1]
    blk = ATT_BLOCK
    hpt = LANES // dh
    rows, ncol = hpt * blk, 2 * blk
    rr = lax.broadcasted_iota(jnp.int32, (rows, ncol), 0)
    cc = lax.broadcasted_iota(jnp.int32, (rows, ncol), 1)
    delta = (rr % blk) + blk - cc
    in_band = (delta >= 0) & (delta <= blk)
    slope = jnp.exp2((-8.0 / n_heads) * (tile * hpt + rr // blk + 1).astype(F32))
    lane_head = lax.broadcasted_iota(jnp.int32, (blk, LANES), 1) // dh
    n_pat = len(DIL_PATTERNS)

    def per_lane(x):
        out = jnp.broadcast_to(x[0:blk], (blk, LANES))
        for e in range(1, hpt):
            out = jnp.where(lane_head == e, jnp.broadcast_to(x[e * blk:(e + 1) * blk], (blk, LANES)), out)
        return out

    ones = jnp.ones((ncol, LANES), BF16)
    order = sorted(range(n_pat), key=lambda p: -DIL_PATTERNS[p][1])
    for step, p in enumerate(order):
        dil = DIL_PATTERNS[p][1]
        alibi = (-LOG2E) * slope * (delta * dil).astype(F32)
        bias_sc[0] = jnp.where(in_band & (cc >= blk), alibi, NEG)
        bias_sc[1] = jnp.where(in_band, alibi, NEG)
        nblk = T // (blk * dil)

        def body(n, carry, first=step == 0, last=step == n_pat - 1, dil=dil, nblk=nblk):
            kp, vp = carry
            i = n % nblk
            start = n // nblk + i * (blk * dil)
            idx = pl.ds(start, blk, stride=dil) if dil > 1 else pl.ds(pl.multiple_of(start, blk), blk)
            q2 = q_ref[0, idx, :] * LOG2E
            kc = k_ref[0, idx, :].astype(BF16)
            vc = v_ref[0, idx, :].astype(BF16)
            qs = jnp.concatenate([jnp.where(lane_head == e, q2, 0.0) for e in range(hpt)], axis=0).astype(BF16)
            s = _dot_nt(qs, jnp.concatenate([kp, kc], axis=0)) + bias_sc[jnp.minimum(i, 1)]
            m = jnp.max(s, axis=-1, keepdims=True)
            pr = jnp.exp2(s - m).astype(BF16)
            pv = _dot(pr, jnp.concatenate([jnp.concatenate([vp, vc], axis=0), ones], axis=1))
            m_t, acc_t, den_t = per_lane(m), per_lane(pv[:, :LANES]), per_lane(pv[:, LANES:])
            if not first:
                m_old = m_sc[idx, :]
                m_new = jnp.maximum(m_old, m_t)
                a_old, a_t = jnp.exp2(m_old - m_new), jnp.exp2(m_t - m_new)
                den_t = a_old * den_sc[idx, :] + a_t * den_t
                acc_t = a_old * acc_sc[idx, :] + a_t * acc_t
                m_t = m_new
            if last:
                acc_sc[idx, :] = acc_t / den_t
            else:
                m_sc[idx, :] = m_t
                den_sc[idx, :] = den_t
                acc_sc[idx, :] = acc_t
            return kc, vc

        zero = jnp.zeros((blk, LANES), BF16)
        lax.fori_loop(0, T // blk, body, (zero, zero), unroll=unroll)

    ch = 4 * blk
    same_head = (lax.broadcasted_iota(jnp.int32, (LANES, LANES), 0) // dh
                 == lax.broadcasted_iota(jnp.int32, (LANES, LANES), 1) // dh).astype(BF16)
    for c in range(T // ch):
        a = acc_sc[c * ch:(c + 1) * ch, :]
        mean_sq = _dot((a * a).astype(BF16), same_head) * (1.0 / dh)
        o_ref[0, c * ch:(c + 1) * ch, :] = (a * lax.rsqrt(mean_sq + EPS) * g_ref[...]).astype(BF16)


def _dil_attn(q, k, v, g, *, n_heads, dh, unroll):
    B, T, W = q.shape
    blk = ATT_BLOCK
    seq = pl.BlockSpec((1, T, LANES), lambda b, t: (b, 0, t))
    return pl.pallas_call(
        functools.partial(_dil_attn_kernel, n_heads=n_heads, dh=dh, unroll=unroll),
        grid=(B, W // LANES),
        in_specs=[seq, seq, seq, pl.BlockSpec((1, LANES), lambda b, t: (0, t))],
        out_specs=seq,
        out_shape=jax.ShapeDtypeStruct((B, T, W), BF16),
        scratch_shapes=[pltpu.VMEM((T, LANES), F32)] * 3
        + [pltpu.VMEM((2, (LANES // dh) * blk, 2 * blk), F32)],
        compiler_params=_params(("parallel", "arbitrary")),
        name="dil_attn",
    )(q, k, v, g)


def _dil_attn_step_kernel(q_ref, kn_ref, vn_ref, kt_ref, vt_ref, g_ref, o_ref, cnt_sc, bias_sc,
                          *, n_heads, dh, n_q):
    past = kt_ref.shape[2]
    n_new = LANES
    W = q_ref.shape[-1]
    rows = n_q * n_heads

    def new_rows(ref):
        sub8 = lax.broadcasted_iota(jnp.int32, (SUBLANES, W), 0)
        tile = jnp.zeros((SUBLANES, W), F32)
        for i in range(n_q):
            tile = jnp.where(sub8 == i, jnp.broadcast_to(ref[0, i:i + 1, :], (SUBLANES, W)), tile)
        return jnp.concatenate([tile, jnp.zeros((n_new - SUBLANES, W), F32)], axis=0).astype(BF16)
    n_cols = past + n_new

    @pl.when(pl.program_id(0) == 0)
    def _():
        r = lax.broadcasted_iota(jnp.int32, (rows, n_cols), 0)
        c = lax.broadcasted_iota(jnp.int32, (rows, n_cols), 1)
        delta = past + r // n_heads - c
        cnt = jnp.zeros((rows, n_cols), F32)
        for win, dil in DIL_PATTERNS:
            cnt = cnt + ((delta >= 0) & (delta <= win) & ((delta & (dil - 1)) == 0)).astype(F32)
        slope = jnp.exp2((-8.0 / n_heads) * ((r % n_heads) + 1).astype(F32))
        cnt_sc[...] = cnt
        bias_sc[...] = jnp.where(cnt > 0.0, -slope * delta.astype(F32), NEG)

    sub = lax.broadcasted_iota(jnp.int32, (n_heads, W), 0)
    lane_head = lax.broadcasted_iota(jnp.int32, (n_heads, W), 1) // dh
    qbd = jnp.concatenate(
        [jnp.where(sub == lane_head, jnp.broadcast_to(q_ref[0, i:i + 1, :], (n_heads, W)), 0.0) for i in range(n_q)],
        axis=0).astype(BF16)
    s = jnp.concatenate([_dot(qbd, kt_ref[0].astype(BF16)), _dot_nt(qbd, new_rows(kn_ref))], axis=1) + bias_sc[...]
    p = jnp.exp(s - jnp.max(s, axis=-1, keepdims=True)) * cnt_sc[...]
    den = jnp.sum(p, axis=-1, keepdims=True)
    p = p.astype(BF16)
    out = (_dot_nt(p[:, :past], vt_ref[0].astype(BF16)) + _dot(p[:, past:], new_rows(vn_ref))) / den
    r = lax.broadcasted_iota(jnp.int32, (rows, W), 0)
    c = lax.broadcasted_iota(jnp.int32, (rows, W), 1)
    out = jnp.where((r % n_heads) == (c // dh), out, 0.0)
    out = out * lax.rsqrt(jnp.sum(out * out, axis=-1, keepdims=True) * (1.0 / dh) + EPS)
    o_ref[0] = jnp.sum(out.reshape(n_q, n_heads, W), axis=1) * g_ref[...]


def _lower_bound(lbl_ref, layer):
    logits = lbl_ref[...]
    e = jnp.exp(logits - jnp.max(logits, axis=0, keepdims=True))
    return jnp.sum(e[:layer + 1], axis=0, keepdims=True) / jnp.sum(e, axis=0, keepdims=True)


def _cumsum_rows(tril_bf, x):
    x1 = x.astype(BF16)
    r1 = x - x1.astype(F32)
    x2 = r1.astype(BF16)
    x3 = (r1 - x2.astype(F32)).astype(BF16)
    return _dot(tril_bf, x1) + _dot(tril_bf, x2) + _dot(tril_bf, x3)


def _hgrn_rows(hz_ref, o_ref, lb, gn_ref, st_ref, b_sc, c_sc, *, n_heads, dk):
    hw = n_heads * dk
    tc = hz_ref.shape[0]
    ch, sub = HG_CHUNK, HG_SUB
    r_c = lax.broadcasted_iota(jnp.int32, (ch, ch), 0)
    c_c = lax.broadcasted_iota(jnp.int32, (ch, ch), 1)
    tril = (r_c >= c_c).astype(BF16)
    r_s = lax.broadcasted_iota(jnp.int32, (sub, ch), 0)
    c_s = lax.broadcasted_iota(jnp.int32, (sub, ch), 1)

    local = [[None] * n_heads for _ in range(tc // ch)]
    for c in range(tc // ch):
        rows = slice(c * ch, (c + 1) * ch)
        hq = hz_ref[rows, 0:hw]
        f = lb + (1.0 - lb) * _sigmoid(hz_ref[rows, hw:2 * hw])
        qh = hq * _sigmoid(hq) * dk ** -0.5
        kk = 1.0 - f
        b = _cumsum_rows(tril, jnp.log(f) * LOG2E)
        b_rows, c_rows = b_sc.at[c], c_sc.at[c]
        b_rows[...] = b
        c_rows[...] = b - jnp.log(kk) * LOG2E
        for h in range(n_heads):
            hs = slice(h * dk, (h + 1) * dk)
            bh = b[:, hs]
            kh = kk[:, hs]
            qhh = qh[:, hs]
            vh = hz_ref[rows, 2 * hw + h * dk:2 * hw + (h + 1) * dk].astype(BF16)
            blocks = []
            for j in range(ch // sub):
                r0 = j * sub
                bj = bh[r0:r0 + sub]
                qj = qhh[r0:r0 + sub]
                a_j = jnp.zeros((sub, ch), F32)
                for s in range(sub):
                    a = jnp.sum(qj * jnp.exp2(bj - c_rows[r0 + s:r0 + s + 1, hs]), axis=-1, keepdims=True)
                    a_j = jnp.where(c_s == r0 + s, a, a_j)
                if j > 0:
                    beta = b_rows[r0 - 1:r0, hs]
                    qt = qj * jnp.exp2(bj - beta)
                    kt = kh[0:r0] * jnp.exp2(beta - bh[0:r0])
                    kt = jnp.concatenate([kt, jnp.zeros((ch - r0, dk), F32)], axis=0)
                    a_j = a_j + _dot_nt(qt.astype(BF16), kt.astype(BF16))
                blocks.append(jnp.where(c_s - r0 <= r_s, a_j, 0.0))
            b_last = b_rows[ch - 1:ch, hs]
            hg = hz_ref[rows, 3 * hw + h * dk:3 * hw + (h + 1) * dk]
            local[c][h] = dict(
                q=(qhh * jnp.exp2(bh)).astype(BF16), a=jnp.concatenate(blocks, axis=0).astype(BF16), v=vh,
                k=(kh * jnp.exp2(b_last - bh)).astype(BF16), decay=jnp.exp2(b_last), gate=hg * _sigmoid(hg))

    state = [st_ref[h] for h in range(n_heads)]
    for c in range(tc // ch):
        rows = slice(c * ch, (c + 1) * ch)
        for h in range(n_heads):
            hs = slice(h * dk, (h + 1) * dk)
            t = local[c][h]
            o = _dot_nt(t["q"], state[h].astype(BF16)) + _dot(t["a"], t["v"])
            state[h] = state[h] * t["decay"] + lax.dot_general(t["v"], t["k"], TN_DIMS, preferred_element_type=F32)
            o_ref[rows, hs] = (_rms(o, gn_ref[:, hs]) * t["gate"]).astype(BF16)
    for h in range(n_heads):
        st_ref[h] = state[h]


def _in_proj_hgrn_kernel(x_ref, g_ref, w_ref, lbl_ref, gn_ref, q_ref, k_ref, v_ref, kf_ref, vf_ref, o_ref, s_ref,
                         hz_sc, st_ref, b_sc, c_sc, *, att_w, q_scale, n_heads, dk, layer, tiles_per_seq):
    g = pl.program_id(0)
    t_rec = jnp.maximum(g - 1, 0)

    @pl.when(g == 0)
    def _():
        hz_sc[...] = jnp.zeros_like(hz_sc)

    @pl.when(t_rec % tiles_per_seq == 0)
    def _():
        st_ref[...] = jnp.zeros_like(st_ref)

    lb = _lower_bound(lbl_ref, layer)
    slot_w = g % 2
    hz = hz_sc.at[1 - slot_w]

    h = _rms(x_ref[...], g_ref[...]).astype(BF16)

    def proj(lo, hi):
        return _dot(h, w_ref[:, lo:hi])

    q_ref[...] = proj(0, att_w) * q_scale
    zk = proj(att_w, 2 * att_w)
    k_ref[...] = zk
    kf_ref[0] = zk
    zv = proj(2 * att_w, 3 * att_w)
    v_ref[...] = zv
    vf_ref[0] = zv
    base = 3 * att_w
    step = 4 * LANES
    for c in range(0, hz_sc.shape[-1], step):
        hz_sc[slot_w, :, c:c + step] = proj(base + c, base + c + step)

    _hgrn_rows(hz, o_ref, lb, gn_ref, st_ref, b_sc, c_sc, n_heads=n_heads, dk=dk)

    @pl.when((g > 0) & (t_rec % tiles_per_seq == tiles_per_seq - 1))
    def _():
        for h in range(n_heads):
            s_ref[0, h] = st_ref[h].T


def _in_proj_hgrn(x, g, w_bf, lb_logits, gn, *, att_w, q_scale, n_heads, dk, tm, keep, layer):
    B, T, D = x.shape
    hw = n_heads * dk
    n_t = T // tm
    n_tiles = B * n_t
    n_keep = keep // tm
    first = n_t - n_keep
    proj_tile = lambda s: jnp.minimum(s, n_tiles - 1)
    rec_tile = lambda s: jnp.maximum(s - 1, 0)
    rows = lambda s: (proj_tile(s), 0)
    keep_rows = lambda s: (proj_tile(s) // n_t, jnp.maximum(proj_tile(s) % n_t - first, 0), 0)
    f32 = lambda n, w: jax.ShapeDtypeStruct((n, w), F32)
    return pl.pallas_call(
        functools.partial(_in_proj_hgrn_kernel, att_w=att_w, q_scale=q_scale, n_heads=n_heads, dk=dk, layer=layer,
                          tiles_per_seq=n_t),
        grid=(n_tiles + 1,),
        in_specs=[pl.BlockSpec((tm, D), rows), _const_spec((1, D)), _const_spec(w_bf.shape),
                  _const_spec(lb_logits.shape), _const_spec((1, hw))],
        out_specs=[pl.BlockSpec((tm, att_w), rows)] * 3 + [pl.BlockSpec((1, tm, att_w), keep_rows)] * 2
        + [pl.BlockSpec((tm, hw), lambda s: (rec_tile(s), 0)),
           pl.BlockSpec((1, n_heads, dk, dk), lambda s: (rec_tile(s) // n_t, 0, 0, 0))],
        out_shape=[f32(B * T, att_w)] * 3 + [jax.ShapeDtypeStruct((B, keep, att_w), F32)] * 2
        + [jax.ShapeDtypeStruct((B * T, hw), BF16), jax.ShapeDtypeStruct((B, n_heads, dk, dk), F32)],
        scratch_shapes=[pltpu.VMEM((2, tm, 4 * hw), F32), pltpu.VMEM((n_heads, dk, dk), F32)]
        + [pltpu.VMEM((tm // HG_CHUNK, HG_CHUNK, hw), F32)] * 2,
        compiler_params=_params(("arbitrary",)),
        name="in_proj_hgrn",
    )(x.reshape(B * T, D), g, w_bf, lb_logits, gn)


def _hgrn_step_kernel(hz_ref, s0_ref, lbl_ref, gn_ref, o_ref, s_ref, *, n_heads, dk, layer):
    hw = n_heads * dk
    n_t = hz_ref.shape[1]
    lb = _lower_bound(lbl_ref, layer)
    q, k, v, b, gate = [], [], [], [], []
    acc = jnp.zeros((1, hw), F32)
    for t in range(n_t):
        hq = hz_ref[0, t:t + 1, 0:hw]
        f = lb + (1.0 - lb) * _sigmoid(hz_ref[0, t:t + 1, hw:2 * hw])
        acc = acc + jnp.log(f)
        q.append(hq * _sigmoid(hq) * dk ** -0.5)
        k.append(1.0 - f)
        v.append(hz_ref[0, t:t + 1, 2 * hw:3 * hw])
        b.append(acc)
        hg = hz_ref[0, t:t + 1, 3 * hw:4 * hw]
        gate.append(hg * _sigmoid(hg))
    rr = lax.broadcasted_iota(jnp.int32, (dk, dk), 0)
    cc = lax.broadcasted_iota(jnp.int32, (dk, dk), 1)

    def tile_of_rows(rows):
        tile = jnp.zeros((dk, dk), F32)
        for t, r in enumerate(rows):
            tile = jnp.where(rr == t, jnp.broadcast_to(r, (dk, dk)), tile)
        return tile

    for h in range(n_heads):
        hs = slice(h * dk, (h + 1) * dk)
        s0 = s0_ref[0, h]
        qe = tile_of_rows([q[t][:, hs] * jnp.exp(b[t][:, hs]) for t in range(n_t)])
        inter = jnp.dot(qe, s0, precision=HIGHEST, preferred_element_type=F32)
        for t in range(n_t):
            o = inter[t:t + 1]
            for s in range(t + 1):
                a = jnp.sum(q[t][:, hs] * jnp.exp(b[t][:, hs] - b[s][:, hs]) * k[s][:, hs], axis=-1, keepdims=True)
                o = o + a * v[s][:, hs]
            o_ref[0, t:t + 1, hs] = _rms(o, gn_ref[:, hs]) * gate[t][:, hs]
        b_last = b[n_t - 1][:, hs]
        khat = tile_of_rows([k[t][:, hs] * jnp.exp(b_last - b[t][:, hs]) for t in range(n_t)])
        vpad = tile_of_rows([v[t][:, hs] for t in range(n_t)])
        decay = jnp.where(rr == cc, jnp.broadcast_to(jnp.exp(b_last), (dk, dk)), 0.0)
        s_ref[0, h] = (jnp.dot(decay, s0, precision=HIGHEST, preferred_element_type=F32)
                       + jnp.dot(khat.T, vpad, precision=HIGHEST, preferred_element_type=F32))


def _step_mixers_kernel(q_ref, kn_ref, vn_ref, kt_ref, vt_ref, g_ref, hz_ref, s0_ref, lbl_ref, gn_ref,
                        o_ref, oh_ref, s_ref, cnt_sc, bias_sc, *, att_heads, dh, n_q, hg_heads, dk, layer):
    _dil_attn_step_kernel(q_ref, kn_ref, vn_ref, kt_ref, vt_ref, g_ref, o_ref, cnt_sc, bias_sc,
                          n_heads=att_heads, dh=dh, n_q=n_q)
    _hgrn_step_kernel(hz_ref, s0_ref, lbl_ref, gn_ref, oh_ref, s_ref, n_heads=hg_heads, dk=dk, layer=layer)


def _step_mixers(q, k_new, v_new, cache_kt, cache_vt, g_att, hz, s0, lb_logits, g_hg,
                 *, att_heads, dh, layer):
    B, n_q, W = q.shape
    past = cache_kt.shape[2]
    _, hg_heads, dk, dv = s0.shape
    hw = hg_heads * dk
    bmap = lambda b: (b, 0, 0)
    smap = lambda b: (b, 0, 0, 0)
    return pl.pallas_call(
        functools.partial(_step_mixers_kernel, att_heads=att_heads, dh=dh, n_q=n_q, hg_heads=hg_heads, dk=dk,
                          layer=layer),
        grid=(B,),
        in_specs=[pl.BlockSpec((1, n_q, W), bmap)] * 3
        + [pl.BlockSpec((1, W, past), bmap), pl.BlockSpec((1, W, past), bmap), _const_spec((1, W)),
                  pl.BlockSpec((1, n_q, 4 * hw), bmap), pl.BlockSpec((1, hg_heads, dk, dv), smap),
                  _const_spec(lb_logits.shape), _const_spec((1, hw))],
        out_specs=[pl.BlockSpec((1, n_q, W), bmap), pl.BlockSpec((1, n_q, hw), bmap),
                   pl.BlockSpec((1, hg_heads, dk, dv), smap)],
        out_shape=[jax.ShapeDtypeStruct((B, n_q, W), F32), jax.ShapeDtypeStruct((B, n_q, hw), F32),
                   jax.ShapeDtypeStruct((B, hg_heads, dk, dv), F32)],
        scratch_shapes=[pltpu.VMEM((n_q * att_heads, past + LANES), F32)] * 2,
        compiler_params=_params(("arbitrary",)),
        name="step_mixers",
    )(q, k_new, v_new, cache_kt, cache_vt, g_att, hz, s0, lb_logits, g_hg)


def _mix_out(x, att_bf, ohg_bf, wo_ref):
    aw = att_bf.shape[-1]
    return x + _dot(att_bf, wo_ref[0:aw, :]) + _dot(ohg_bf, wo_ref[aw:, :])


def _cross_attend(cq, head_k, head_v, n_heads, dh):
    heads = range(n_heads)
    scale = dh ** -0.5 * LOG2E
    scores = [_dot_nt((cq[:, h * dh:(h + 1) * dh] * scale).astype(BF16), head_k(h)) for h in heads]
    probs = [jnp.exp2(s - jnp.max(s, axis=-1, keepdims=True)).astype(BF16) for s in scores]
    ones = jnp.ones((probs[0].shape[1], dh), BF16)
    pvs = [_dot(probs[h], jnp.concatenate([head_v(h), ones], axis=1)) for h in heads]
    return jnp.concatenate([pv[:, :dh] / pv[:, dh:] for pv in pvs], axis=-1)


def _ffn(u_bf, taps_fn, wg_ref, wu_ref, wd_ref, cw_ref, cb_ref, ff_chunk):
    dff = wg_ref.shape[1]
    n_taps = cw_ref.shape[0]
    acc = jnp.zeros((u_bf.shape[0], wd_ref.shape[1]), F32)
    for n in range(dff // ff_chunk):
        cs = slice(n * ff_chunk, (n + 1) * ff_chunk)
        ug = _dot(u_bf, wg_ref[:, cs])
        taps = taps_fn(ug, cs, n)
        conv = cb_ref[:, cs] + cw_ref[n_taps - 1:n_taps, cs] * taps[n_taps - 1]
        for j in range(n_taps - 1):
            conv = conv + cw_ref[j:j + 1, cs] * taps[j]
        act = conv * _sigmoid(conv) * _dot(u_bf, wu_ref[:, cs])
        acc = acc + _dot(act.astype(BF16), wd_ref[cs, :])
    return acc


def _post_kernel(x_ref, att_ref, ohg_ref, mk_ref, mv_ref, cinit_ref, gc_ref, gf_ref, gl_ref, wo_ref, wq_ref,
                 wc_ref, wg_ref, wu_ref, wd_ref, cw_ref, cb_ref, y_ref, cst_ref, buf_ref, carry_ref,
                 *, x_heads, x_dh, ff_chunk, n_sub):
    i = pl.program_id(1)
    tm = x_ref.shape[1]
    pad = SUBLANES
    n_taps = cw_ref.shape[0]

    @pl.when(i == 0)
    def _():
        carry_ref[...] = cinit_ref[0]

    ts = tm // n_sub
    for t in range(n_sub):
        rows = slice(t * ts, (t + 1) * ts)
        x1 = _mix_out(x_ref[0, rows], att_ref[0, rows], ohg_ref[0, rows], wo_ref)
        cq = _dot(_rms(x1, gc_ref[...]).astype(BF16), wq_ref[...])
        co = _cross_attend(cq, lambda h: mk_ref[0, :, h * x_dh:(h + 1) * x_dh],
                           lambda h: mv_ref[0, :, h * x_dh:(h + 1) * x_dh], x_heads, x_dh)
        x2 = x1 + _dot(co.astype(BF16), wc_ref[...])

        def taps_fn(ug, cs, n, t=t):
            buf = buf_ref.at[t, n % buf_ref.shape[1]]
            buf[0:pad, :] = carry_ref[:, cs]
            buf[pad:pad + ts, :] = ug
            carry_ref[:, cs] = ug[ts - pad:ts]
            return [buf[pad - (n_taps - 1 - j):pad - (n_taps - 1 - j) + ts, :] for j in range(n_taps - 1)] + [ug]

        u_bf = _rms(x2, gf_ref[...]).astype(BF16)
        x3 = x2 + _ffn(u_bf, taps_fn, wg_ref, wu_ref, wd_ref, cw_ref, cb_ref, ff_chunk)
        y_ref[0, rows] = _rms(x3, gl_ref[...])
    cst_ref[0] = carry_ref[...]


def _post(x, att, ohg, mk_bf, mv_bf, cinit, gains, weights, conv_w, conv_b, *, tm, n_sub, x_heads, ff_chunk):
    B, T, D = x.shape
    aw, hw = att.shape[-1], ohg.shape[-1]
    n_mem, xw = mk_bf.shape[1:]
    dff = conv_w.shape[1]
    row = lambda b, i: (b, i, 0)
    bat = lambda b, i: (b, 0, 0)
    in_specs = ([pl.BlockSpec((1, tm, D), row), pl.BlockSpec((1, tm, aw), row), pl.BlockSpec((1, tm, hw), row)]
                + [pl.BlockSpec((1, n_mem, xw), bat)] * 2 + [pl.BlockSpec((1, SUBLANES, dff), bat)]
                + [_const_spec(a.shape) for a in tuple(gains) + tuple(weights) + (conv_w, conv_b)])
    return pl.pallas_call(
        functools.partial(_post_kernel, x_heads=x_heads, x_dh=xw // x_heads, ff_chunk=ff_chunk, n_sub=n_sub),
        grid=(B, T // tm),
        in_specs=in_specs,
        out_specs=[pl.BlockSpec((1, tm, D), row), pl.BlockSpec((1, SUBLANES, dff), bat)],
        out_shape=[jax.ShapeDtypeStruct((B, T, D), F32), jax.ShapeDtypeStruct((B, SUBLANES, dff), F32)],
        scratch_shapes=[pltpu.VMEM((n_sub, min(2, dff // ff_chunk), tm // n_sub + SUBLANES, ff_chunk), F32),
                        pltpu.VMEM((SUBLANES, dff), F32)],
        compiler_params=_params(("parallel", "arbitrary")),
        name="post_mixer",
    )(x, att, ohg, mk_bf, mv_bf, cinit, *gains, *weights, conv_w, conv_b)


def _step_mix_kernel(x_ref, att_ref, ohg_ref, gc_ref, wo_ref, wq_ref, x1_ref, cq_ref):
    x1 = _mix_out(x_ref[...], att_ref[...].astype(BF16), ohg_ref[...].astype(BF16), wo_ref)
    x1_ref[...] = x1
    cq_ref[...] = _dot(_rms(x1, gc_ref[...]).astype(BF16), wq_ref[...])


def _step_cross_kernel(cq_ref, mk_ref, mv_ref, co_ref, *, n_q):
    n_b, n_mem, n_heads, dh = mk_ref.shape
    row = lax.broadcasted_iota(jnp.int32, (cq_ref.shape[1], n_mem * n_heads), 0)
    col = lax.broadcasted_iota(jnp.int32, (cq_ref.shape[1], n_mem * n_heads), 1)
    own_head = (col % n_heads) == (row // n_q)
    for b in range(n_b):
        mk = mk_ref[b].reshape(n_mem * n_heads, dh).astype(BF16)
        mv = mv_ref[b].reshape(n_mem * n_heads, dh).astype(BF16)
        s = jnp.where(own_head, _dot_nt((cq_ref[b] * dh ** -0.5).astype(BF16), mk), NEG)
        p = jnp.exp(s - jnp.max(s, axis=-1, keepdims=True))
        co_ref[b] = _dot(p.astype(BF16), mv) / jnp.sum(p, axis=-1, keepdims=True)


def _step_ffn_kernel(x1_ref, co_ref, prev_ref, gf_ref, gl_ref, wc_ref, wg_ref, wu_ref, wd_ref,
                     cw_ref, cb_ref, y_ref, ug_ref, buf_ref, *, ff_chunk, n_t):
    n = x1_ref.shape[0]
    pad = SUBLANES
    n_taps = cw_ref.shape[0]
    x2 = x1_ref[...] + _dot(co_ref[...].astype(BF16), wc_ref[...])
    t_of_row = lax.broadcasted_iota(jnp.int32, (n, ff_chunk), 0) % n_t

    def taps_fn(ug, cs, _):
        ug_ref[:, cs] = ug
        buf_ref[0:pad, cs] = jnp.zeros((pad, ff_chunk), F32)
        buf_ref[pad:pad + n, cs] = ug
        taps = []
        for j in range(n_taps - 1):
            shift = n_taps - 1 - j
            shifted = buf_ref[pad - shift:pad - shift + n, cs]
            taps.append(jnp.where(t_of_row < shift, prev_ref[j, :, cs], shifted))
        return taps + [ug]

    u_bf = _rms(x2, gf_ref[...]).astype(BF16)
    x3 = x2 + _ffn(u_bf, taps_fn, wg_ref, wu_ref, wd_ref, cw_ref, cb_ref, ff_chunk)
    y_ref[...] = _rms(x3, gl_ref[...])


def _single_call(kernel, args, out_shape, name, scratch_shapes=()):
    return pl.pallas_call(
        kernel,
        grid=(1,),
        in_specs=[_const_spec(a.shape) for a in args],
        out_specs=[pl.BlockSpec(o.shape, lambda *_, nd=len(o.shape): (0,) * nd) for o in out_shape],
        out_shape=out_shape,
        scratch_shapes=list(scratch_shapes),
        compiler_params=_params(("arbitrary",)),
        name=name,
    )(*args)


def _step_cross(cq, mem_k, mem_v, *, n_q):
    B, n_rows, dh = cq.shape
    _, n_mem, x_heads, _ = mem_k.shape
    n_b = 4 if B % 4 == 0 else 1
    bmap = lambda b: (b, 0, 0)
    mem = pl.BlockSpec((n_b, n_mem, x_heads, dh), lambda b: (b, 0, 0, 0))
    return pl.pallas_call(
        functools.partial(_step_cross_kernel, n_q=n_q),
        grid=(B // n_b,),
        in_specs=[pl.BlockSpec((n_b, n_rows, dh), bmap), mem, mem],
        out_specs=pl.BlockSpec((n_b, n_rows, dh), bmap),
        out_shape=jax.ShapeDtypeStruct((B, n_rows, dh), F32),
        compiler_params=_params(("parallel",)),
        name="step_cross",
    )(cq, mem_k, mem_v)


def kernel(x_prompt, x_sample, cache_win_k, cache_win_v, state_hgrn, state_ffn_conv, cache_mem_k, cache_mem_v,
           mem_prompt, hg_lb_logits, norm_mix, w_in, att_out_norm, hg_out_norm, w_out, norm_cross, norm_mem,
           w_cq, w_ck, w_cv, w_co, norm_ffn, w_gate, w_up, conv_w, conv_b, w_down, norm_final):
    Bp, T, D = x_prompt.shape
    Bs, Ts, _ = x_sample.shape
    depth, _, past, att_h, att_dh = cache_win_k.shape
    _, _, hg_h, hg_dk, hg_dv = state_hgrn.shape
    _, _, n_mem, x_h, x_dh = cache_mem_k.shape
    n_taps, dff = conv_w.shape[1:]
    att_w = att_h * att_dh
    hg_w = hg_h * hg_dk
    xw = x_h * x_dh
    keep = min(max(w for w, _ in DIL_PATTERNS), T)
    assert depth == 1 and hg_dk == hg_dv == LANES and x_dh == LANES and LANES % att_dh == 0
    assert all(w // d == ATT_BLOCK for w, d in DIL_PATTERNS) and past >= max(w for w, _ in DIL_PATTERNS)
    d_max = max(d for _, d in DIL_PATTERNS)
    assert T % (ATT_BLOCK * d_max) == 0 and past % d_max == 0 and past % (4 * ATT_BLOCK) == 0
    assert n_taps - 1 <= min(Ts, SUBLANES) and Ts <= SUBLANES
    layer = 0
    ff_chunk = dff
    q_scale = att_dh ** -0.5
    row2 = lambda a: a.reshape(1, -1)

    w_in_bf = w_in[layer].astype(BF16)
    w_ckv_bf = jnp.concatenate([w_ck[layer], w_cv[layer]], axis=1).astype(BF16)
    weights = tuple(w[layer].astype(BF16) for w in (w_out, w_cq, w_co, w_gate, w_up, w_down))
    g_mix, g_att, g_hg = row2(norm_mix[layer]), row2(att_out_norm[layer]), row2(hg_out_norm[layer])
    g_cross, g_mem, g_ffn, g_final = (row2(norm_cross[layer]), row2(norm_mem[layer]), row2(norm_ffn[layer]),
                                      row2(norm_final))
    cw, cb = conv_w[layer], row2(conv_b[layer])

    q, k, v, k_keep, v_keep, ohg, s_prompt = _in_proj_hgrn(
        x_prompt, g_mix, w_in_bf, hg_lb_logits, g_hg, att_w=att_w, q_scale=q_scale, n_heads=hg_h, dk=hg_dk,
        tm=512, keep=keep, layer=layer)
    q, k, v = (a.reshape(Bp, T, att_w) for a in (q, k, v))
    ohg = ohg.reshape(Bp, T, hg_w)
    mk, mv, mk_bf, mv_bf = _mem_kv(mem_prompt.reshape(Bp * n_mem, D), g_mem, w_ckv_bf, tm=256)
    att = _dil_attn(q, k, v, g_att, n_heads=att_h, dh=att_dh, unroll=32)
    y_prompt, cst = _post(
        x_prompt, att, ohg, mk_bf.reshape(Bp, n_mem, xw), mv_bf.reshape(Bp, n_mem, xw),
        jnp.zeros((Bp, SUBLANES, dff), F32), (g_cross, g_ffn, g_final), weights, cw, cb,
        tm=512, n_sub=1, x_heads=x_h, ff_chunk=ff_chunk)

    n_s = Bs * Ts
    qs, ks, vs, ks_f, vs_f, hzs = _in_proj(x_sample.reshape(1, n_s, D), g_mix, w_in_bf, att_w=att_w, q_scale=q_scale,
                                           tm=n_s, keep=n_s)
    per_seq = lambda a: a.reshape(Bs, Ts, att_w)
    feature_major = lambda c: jnp.transpose(c, (0, 2, 3, 1)).reshape(Bs, att_w, past)
    att_s, ohg_s, s_sample = _step_mixers(
        per_seq(qs), per_seq(ks), per_seq(vs), feature_major(cache_win_k[layer]),
        feature_major(cache_win_v[layer]), g_att, hzs.reshape(Bs, Ts, 4 * hg_w), state_hgrn[layer], hg_lb_logits,
        g_hg, att_heads=att_h, dh=att_dh, layer=layer)
    x1_s, cq_s = _single_call(
        _step_mix_kernel,
        (x_sample.reshape(n_s, D), att_s.reshape(n_s, att_w), ohg_s.reshape(n_s, hg_w), g_cross, weights[0],
         weights[1]),
        [jax.ShapeDtypeStruct((n_s, D), F32), jax.ShapeDtypeStruct((n_s, xw), F32)], "step_mix")
    cq_heads = cq_s.reshape(Bs, Ts, x_h, x_dh).transpose(0, 2, 1, 3).reshape(Bs, x_h * Ts, x_dh)
    co_s = _step_cross(cq_heads, cache_mem_k[layer], cache_mem_v[layer], n_q=Ts)
    co_s = co_s.reshape(Bs, x_h, Ts, x_dh).transpose(0, 2, 1, 3)
    conv_state = state_ffn_conv[layer]
    t_idx = jnp.arange(Ts)
    prev = jnp.stack([jnp.take(conv_state, jnp.clip(j + t_idx, 0, n_taps - 2), axis=1).reshape(n_s, dff)
                      for j in range(n_taps - 1)])
    y_s, ug_s = _single_call(
        functools.partial(_step_ffn_kernel, ff_chunk=ff_chunk, n_t=Ts),
        (x1_s, co_s.reshape(n_s, xw), prev, g_ffn, g_final, weights[2], weights[3], weights[4], weights[5], cw, cb),
        [jax.ShapeDtypeStruct((n_s, D), F32), jax.ShapeDtypeStruct((n_s, dff), F32)], "step_ffn",
        scratch_shapes=[pltpu.VMEM((n_s + SUBLANES, dff), F32)])

    stack = lambda a: a[None]
    return (y_prompt, y_s.reshape(Bs, Ts, D),
            stack(k_keep.reshape(Bp, keep, att_h, att_dh)), stack(v_keep.reshape(Bp, keep, att_h, att_dh)),
            stack(s_prompt), stack(cst[:, SUBLANES - (n_taps - 1):]),
            stack(mk.reshape(Bp, n_mem, x_h, x_dh)), stack(mv.reshape(Bp, n_mem, x_h, x_dh)),
            stack(ks_f.reshape(Bs, Ts, att_h, att_dh)), stack(vs_f.reshape(Bs, Ts, att_h, att_dh)),
            stack(s_sample), stack(ug_s.reshape(Bs, Ts, dff)[:, Ts - (n_taps - 1):]))
```

```python
import functools

import jax
import jax.numpy as jnp
from jax import lax
from jax.experimental import pallas as pl
from jax.experimental.pallas import tpu as pltpu

F32 = jnp.float32
BF16 = jnp.bfloat16
EPS = 1e-6
DIL_PATTERNS = ((128, 1), (512, 4), (2048, 16))
ATT_BLOCK = 128
LANES = 128
SUBLANES = 8
HG_CHUNK = 64
HG_SUB = 8
NEG = -1e30
LOG2E = 1.4426950408889634
HIGHEST = lax.Precision.HIGHEST
NT_DIMS = (((1,), (1,)), ((), ()))
TN_DIMS = (((0,), (0,)), ((), ()))
VMEM_LIMIT = 56 * 1024 * 1024


def _dot(a, b):
    return jnp.dot(a, b, preferred_element_type=F32)


def _dot_nt(a, b):
    return lax.dot_general(a, b, NT_DIMS, preferred_element_type=F32)


def _sigmoid(x):
    return 1.0 / (1.0 + jnp.exp(-x))


def _rms(x, g):
    return x * lax.rsqrt(jnp.mean(x * x, axis=-1, keepdims=True) + EPS) * g


def _const_spec(shape):
    nd = len(shape)
    return pl.BlockSpec(shape, lambda *_: (0,) * nd, pipeline_mode=pl.Buffered(1))


def _params(sem):
    return pltpu.CompilerParams(dimension_semantics=sem, vmem_limit_bytes=VMEM_LIMIT)


def _in_proj_kernel(x_ref, g_ref, w_ref, q_ref, k_ref, v_ref, kf_ref, vf_ref, hz_ref, *, att_w, q_scale):
    h = _rms(x_ref[0], g_ref[...]).astype(BF16)

    def proj(lo, hi):
        return _dot(h, w_ref[:, lo:hi])

    q_ref[0] = proj(0, att_w) * q_scale
    zk = proj(att_w, 2 * att_w)
    k_ref[0] = zk
    kf_ref[0] = zk
    zv = proj(2 * att_w, 3 * att_w)
    v_ref[0] = zv
    vf_ref[0] = zv
    base = 3 * att_w
    step = 4 * LANES
    for c in range(0, hz_ref.shape[-1], step):
        hz_ref[0, :, c:c + step] = proj(base + c, base + c + step)


def _in_proj(x, g, w_bf, *, att_w, q_scale, tm, keep):
    B, T, D = x.shape
    ncols = w_bf.shape[1]
    hzw = ncols - 3 * att_w
    n_t = T // tm
    first = n_t - keep // tm
    row = lambda b, i: (b, i, 0)
    keep_map = lambda b, i: (b, jnp.maximum(i - first, 0), 0)
    return pl.pallas_call(
        functools.partial(_in_proj_kernel, att_w=att_w, q_scale=q_scale),
        grid=(B, n_t),
        in_specs=[pl.BlockSpec((1, tm, D), row), _const_spec((1, D)), _const_spec((D, ncols))],
        out_specs=[pl.BlockSpec((1, tm, att_w), row)] * 3
        + [pl.BlockSpec((1, tm, att_w), keep_map)] * 2
        + [pl.BlockSpec((1, tm, hzw), row)],
        out_shape=[jax.ShapeDtypeStruct((B, T, att_w), F32)] * 3
        + [jax.ShapeDtypeStruct((B, keep, att_w), F32)] * 2
        + [jax.ShapeDtypeStruct((B, T, hzw), F32)],
        compiler_params=_params(("parallel", "arbitrary")),
        name="in_proj",
    )(x, g, w_bf)


def _mem_kv_kernel(m_ref, g_ref, w_ref, kf_ref, vf_ref, kb_ref, vb_ref):
    h = _rms(m_ref[...], g_ref[...]).astype(BF16)
    xw = w_ref.shape[1] // 2
    k = _dot(h, w_ref[:, :xw])
    v = _dot(h, w_ref[:, xw:])
    kf_ref[...] = k
    vf_ref[...] = v
    kb_ref[...] = k.astype(BF16)
    vb_ref[...] = v.astype(BF16)


def _mem_kv(mem2d, g, w_ckv_bf, *, tm):
    n, D = mem2d.shape
    xw = w_ckv_bf.shape[1] // 2
    row = lambda i: (i, 0)
    return pl.pallas_call(
        _mem_kv_kernel,
        grid=(n // tm,),
        in_specs=[pl.BlockSpec((tm, D), row), _const_spec((1, D)), _const_spec((D, 2 * xw))],
        out_specs=[pl.BlockSpec((tm, xw), row)] * 4,
        out_shape=[jax.ShapeDtypeStruct((n, xw), F32)] * 2 + [jax.ShapeDtypeStruct((n, xw), BF16)] * 2,
        compiler_params=_params(("parallel",)),
        name="mem_kv",
    )(mem2d, g, w_ckv_bf)


def _dil_attn_kernel(q_ref, k_ref, v_ref, g_ref, o_ref, acc_sc, m_sc, den_sc, bias_sc, *, n_heads, dh):
    tile = pl.program_id(1)
    T = q_ref.shape[1]
    blk = ATT_BLOCK
    hpt = LANES // dh
    rows, ncol = hpt * blk, 2 * blk
    rr = lax.broadcasted_iota(jnp.int32, (rows, ncol), 0)
    cc = lax.broadcasted_iota(jnp.int32, (rows, ncol), 1)
    delta = (rr % blk) + blk - cc
    in_band = (delta >= 0) & (delta <= blk)
    slope = jnp.exp2((-8.0 / n_heads) * (tile * hpt + rr // blk + 1).astype(F32))
    lane_head = lax.broadcasted_iota(jnp.int32, (blk, LANES), 1) // dh
    n_pat = len(DIL_PATTERNS)

    def per_lane(x):
        out = jnp.broadcast_to(x[0:blk], (blk, LANES))
        for e in range(1, hpt):
            out = jnp.where(lane_head == e, jnp.broadcast_to(x[e * blk:(e + 1) * blk], (blk, LANES)), out)
        return out

    ones = jnp.ones((ncol, LANES), BF16)
    order = sorted(range(n_pat), key=lambda p: -DIL_PATTERNS[p][1])
    for step, p in enumerate(order):
        dil = DIL_PATTERNS[p][1]
        alibi = (-LOG2E) * slope * (delta * dil).astype(F32)
        bias_sc[0] = jnp.where(in_band & (cc >= blk), alibi, NEG)
        bias_sc[1] = jnp.where(in_band, alibi, NEG)
        nblk = T // (blk * dil)

        def body(n, carry, first=step == 0, last=step == n_pat - 1, dil=dil, nblk=nblk):
            kp, vp = carry
            i = n % nblk
            start = n // nblk + i * (blk * dil)
            idx = pl.ds(start, blk, stride=dil) if dil > 1 else pl.ds(start, blk)
            q2 = q_ref[0, idx, :]
            kc = k_ref[0, idx, :].astype(BF16)
            vc = v_ref[0, idx, :].astype(BF16)
            qs = jnp.concatenate([jnp.where(lane_head == e, q2, 0.0) for e in range(hpt)], axis=0).astype(BF16)
            s = _dot_nt(qs, jnp.concatenate([kp, kc], axis=0)) + bias_sc[min(i, 1)]
            m = jnp.max(s, axis=-1, keepdims=True)
            pr = jnp.exp2(s - m).astype(BF16)
            pv = _dot(pr, jnp.concatenate([jnp.concatenate([vp, vc], axis=0), ones], axis=1))
            m_t, acc_t, den_t = per_lane(m), per_lane(pv[:, :LANES]), per_lane(pv[:, LANES:])
            if not first:
                m_old = m_sc[idx, :]
                m_new = jnp.maximum(m_old, m_t)
                a_old, a_t = jnp.exp2(m_old - m_new), jnp.exp2(m_t - m_new)
                den_t = a_old * den_sc[idx, :] + a_t * den_t
                acc_t = a_old * acc_sc[idx, :] + a_t * acc_t
                m_t = m_new
            if last:
                acc_sc[idx, :] = acc_t / den_t
            else:
                m_sc[idx, :] = m_t
                den_sc[idx, :] = den_t
                acc_sc[idx, :] = acc_t
            return kc, vc

        carry = (jnp.zeros((blk, LANES), BF16),) * 2
        for n in range(T // blk):
            carry = body(n, carry)

    ch = 4 * blk
    same_head = (lax.broadcasted_iota(jnp.int32, (LANES, LANES), 0) // dh
                 == lax.broadcasted_iota(jnp.int32, (LANES, LANES), 1) // dh).astype(BF16)
    for c in range(T // ch):
        a = acc_sc[c * ch:(c + 1) * ch, :]
        mean_sq = _dot((a * a).astype(BF16), same_head) * (1.0 / dh)
        o_ref[0, c * ch:(c + 1) * ch, :] = (a * lax.rsqrt(mean_sq + EPS) * g_ref[...]).astype(BF16)


def _dil_attn(q, k, v, g, *, n_heads, dh):
    B, T, W = q.shape
    blk = ATT_BLOCK
    seq = pl.BlockSpec((1, T, LANES), lambda b, t: (b, 0, t))
    return pl.pallas_call(
        functools.partial(_dil_attn_kernel, n_heads=n_heads, dh=dh),
        grid=(B, W // LANES),
        in_specs=[seq, seq, seq, pl.BlockSpec((1, LANES), lambda b, t: (0, t))],
        out_specs=seq,
        out_shape=jax.ShapeDtypeStruct((B, T, W), BF16),
        scratch_shapes=[pltpu.VMEM((T, LANES), F32)] * 3
        + [pltpu.VMEM((2, (LANES // dh) * blk, 2 * blk), F32)],
        compiler_params=_params(("parallel", "arbitrary")),
        name="dil_attn",
    )(q, k, v, g)


def _dil_attn_step_kernel(q_ref, kn_ref, vn_ref, kt_ref, vt_ref, g_ref, o_ref, cnt_sc, bias_sc,
                          *, n_heads, dh, n_q):
    past = kt_ref.shape[2]
    n_new = LANES
    W = q_ref.shape[-1]
    rows = n_q * n_heads

    def new_rows(ref):
        sub8 = lax.broadcasted_iota(jnp.int32, (SUBLANES, W), 0)
        tile = jnp.zeros((SUBLANES, W), F32)
        for i in range(n_q):
            tile = jnp.where(sub8 == i, jnp.broadcast_to(ref[0, i:i + 1, :], (SUBLANES, W)), tile)
        return jnp.concatenate([tile, jnp.zeros((n_new - SUBLANES, W), F32)], axis=0).astype(BF16)
    n_cols = past + n_new

    @pl.when(pl.program_id(0) == 0)
    def _():
        r = lax.broadcasted_iota(jnp.int32, (rows, n_cols), 0)
        c = lax.broadcasted_iota(jnp.int32, (rows, n_cols), 1)
        delta = past + r // n_heads - c
        cnt = jnp.zeros((rows, n_cols), F32)
        for win, dil in DIL_PATTERNS:
            cnt = cnt + ((delta >= 0) & (delta <= win) & ((delta & (dil - 1)) == 0)).astype(F32)
        slope = jnp.exp2((-8.0 / n_heads) * ((r % n_heads) + 1).astype(F32))
        cnt_sc[...] = cnt
        bias_sc[...] = jnp.where(cnt > 0.0, (-LOG2E) * slope * delta.astype(F32), NEG)

    sub = lax.broadcasted_iota(jnp.int32, (n_heads, W), 0)
    lane_head = lax.broadcasted_iota(jnp.int32, (n_heads, W), 1) // dh
    qbd = jnp.concatenate(
        [jnp.where(sub == lane_head, jnp.broadcast_to(q_ref[0, i:i + 1, :], (n_heads, W)), 0.0) for i in range(n_q)],
        axis=0).astype(BF16)
    s = jnp.concatenate([_dot(qbd, kt_ref[0].astype(BF16)), _dot_nt(qbd, new_rows(kn_ref))], axis=1) + bias_sc[...]
    p = jnp.exp2(s - jnp.max(s, axis=-1, keepdims=True)) * cnt_sc[...]
    den = jnp.sum(p, axis=-1, keepdims=True)
    p = p.astype(BF16)
    out = (_dot_nt(p[:, :past], vt_ref[0].astype(BF16)) + _dot(p[:, past:], new_rows(vn_ref))) / den
    r = lax.broadcasted_iota(jnp.int32, (rows, W), 0)
    c = lax.broadcasted_iota(jnp.int32, (rows, W), 1)
    out = jnp.where((r % n_heads) == (c // dh), out, 0.0)
    out = out * lax.rsqrt(jnp.sum(out * out, axis=-1, keepdims=True) * (1.0 / dh) + EPS)
    o_ref[0] = jnp.sum(out.reshape(n_q, n_heads, W), axis=1) * g_ref[...]


def _lower_bound(lbl_ref, layer):
    logits = lbl_ref[...]
    e = jnp.exp(logits - jnp.max(logits, axis=0, keepdims=True))
    return jnp.sum(e[:layer + 1], axis=0, keepdims=True) / jnp.sum(e, axis=0, keepdims=True)


def _cumsum_rows(tril_bf, x):
    x1 = x.astype(BF16)
    r1 = x - x1.astype(F32)
    x2 = r1.astype(BF16)
    x3 = (r1 - x2.astype(F32)).astype(BF16)
    return _dot(tril_bf, x1) + _dot(tril_bf, x2) + _dot(tril_bf, x3)


def _hgrn_rows(hz_ref, o_ref, lb, gn_ref, st_ref, b_sc, c_sc, *, n_heads, dk):
    hw = n_heads * dk
    tc = hz_ref.shape[0]
    ch, sub = HG_CHUNK, HG_SUB
    r_c = lax.broadcasted_iota(jnp.int32, (ch, ch), 0)
    c_c = lax.broadcasted_iota(jnp.int32, (ch, ch), 1)
    tril = (r_c >= c_c).astype(BF16)
    r_s = lax.broadcasted_iota(jnp.int32, (sub, ch), 0)
    c_s = lax.broadcasted_iota(jnp.int32, (sub, ch), 1)

    local = [[None] * n_heads for _ in range(tc // ch)]
    for c in range(tc // ch):
        rows = slice(c * ch, (c + 1) * ch)
        hq = hz_ref[rows, 0:hw]
        f = lb + (1.0 - lb) * _sigmoid(hz_ref[rows, hw:2 * hw])
        qh = hq * _sigmoid(hq) * dk ** -0.5
        kk = 1.0 - f
        b = _cumsum_rows(tril, jnp.log(f) * LOG2E)
        b_rows, c_rows = b_sc.at[c], c_sc.at[c]
        b_rows[...] = b
        c_rows[...] = b - jnp.log(kk) * LOG2E
        for h in range(n_heads):
            hs = slice(h * dk, (h + 1) * dk)
            bh = b[:, hs]
            kh = kk[:, hs]
            qhh = qh[:, hs]
            vh = hz_ref[rows, 2 * hw + h * dk:2 * hw + (h + 1) * dk].astype(BF16)
            blocks = []
            for j in range(ch // sub):
                r0 = j * sub
                bj = bh[r0:r0 + sub]
                qj = qhh[r0:r0 + sub]
                a_j = jnp.zeros((sub, ch), F32)
                for s in range(sub):
                    a = jnp.sum(qj * jnp.exp2(bj - c_rows[r0 + s:r0 + s + 1, hs]), axis=-1, keepdims=True)
                    a_j = jnp.where(c_s == r0 + s, a, a_j)
                if j > 0:
                    beta = b_rows[r0 - 1:r0, hs]
                    qt = qj * jnp.exp2(bj - beta)
                    kt = kh[0:r0] * jnp.exp2(beta - bh[0:r0])
                    kt = jnp.concatenate([kt, jnp.zeros((ch - r0, dk), F32)], axis=0)
                    a_j = a_j + _dot_nt(qt.astype(BF16), kt.astype(BF16))
                blocks.append(jnp.where(c_s - r0 <= r_s, a_j, 0.0))
            b_last = b_rows[ch - 1:ch, hs]
            hg = hz_ref[rows, 3 * hw + h * dk:3 * hw + (h + 1) * dk]
            local[c][h] = dict(
                q=(qhh * jnp.exp2(bh)).astype(BF16), a=jnp.concatenate(blocks, axis=0).astype(BF16), v=vh,
                k=(kh * jnp.exp2(b_last - bh)).astype(BF16), decay=jnp.exp2(b_last), gate=hg * _sigmoid(hg))

    state = [st_ref[h] for h in range(n_heads)]
    for c in range(tc // ch):
        rows = slice(c * ch, (c + 1) * ch)
        for h in range(n_heads):
            hs = slice(h * dk, (h + 1) * dk)
            t = local[c][h]
            o = _dot_nt(t["q"], state[h].astype(BF16)) + _dot(t["a"], t["v"])
            state[h] = state[h] * t["decay"] + lax.dot_general(t["v"], t["k"], TN_DIMS, preferred_element_type=F32)
            o_ref[rows, hs] = (_rms(o, gn_ref[:, hs]) * t["gate"]).astype(BF16)
    for h in range(n_heads):
        st_ref[h] = state[h]


def _in_proj_hgrn_kernel(x_ref, g_ref, w_ref, lbl_ref, gn_ref, q_ref, k_ref, v_ref, kf_ref, vf_ref, o_ref, s_ref,
                         hz_sc, st_ref, b_sc, c_sc, *, att_w, q_scale, n_heads, dk, layer, tiles_per_seq):
    g = pl.program_id(0)
    t_rec = jnp.maximum(g - 1, 0)

    @pl.when(g == 0)
    def _():
        hz_sc[...] = jnp.zeros_like(hz_sc)

    @pl.when(t_rec % tiles_per_seq == 0)
    def _():
        st_ref[...] = jnp.zeros_like(st_ref)

    lb = _lower_bound(lbl_ref, layer)
    slot_w = g % 2
    hz = hz_sc.at[1 - slot_w]

    h = _rms(x_ref[...], g_ref[...]).astype(BF16)

    def proj(lo, hi):
        return _dot(h, w_ref[:, lo:hi])

    q_ref[...] = proj(0, att_w) * q_scale
    zk = proj(att_w, 2 * att_w)
    k_ref[...] = zk
    kf_ref[0] = zk
    zv = proj(2 * att_w, 3 * att_w)
    v_ref[...] = zv
    vf_ref[0] = zv
    base = 3 * att_w
    step = 4 * LANES
    for c in range(0, hz_sc.shape[-1], step):
        hz_sc[slot_w, :, c:c + step] = proj(base + c, base + c + step)

    _hgrn_rows(hz, o_ref, lb, gn_ref, st_ref, b_sc, c_sc, n_heads=n_heads, dk=dk)

    @pl.when((g > 0) & (t_rec % tiles_per_seq == tiles_per_seq - 1))
    def _():
        for h in range(n_heads):
            s_ref[0, h] = st_ref[h].T


def _in_proj_hgrn(x, g, w_bf, lb_logits, gn, *, att_w, q_scale, n_heads, dk, tm, keep, layer):
    B, T, D = x.shape
    hw = n_heads * dk
    n_t = T // tm
    n_tiles = B * n_t
    n_keep = keep // tm
    first = n_t - n_keep
    proj_tile = lambda s: jnp.minimum(s, n_tiles - 1)
    rec_tile = lambda s: jnp.maximum(s - 1, 0)
    rows = lambda s: (proj_tile(s), 0)
    keep_rows = lambda s: (proj_tile(s) // n_t, jnp.maximum(proj_tile(s) % n_t - first, 0), 0)
    f32 = lambda n, w: jax.ShapeDtypeStruct((n, w), F32)
    return pl.pallas_call(
        functools.partial(_in_proj_hgrn_kernel, att_w=att_w, q_scale=q_scale, n_heads=n_heads, dk=dk, layer=layer,
                          tiles_per_seq=n_t),
        grid=(n_tiles + 1,),
        in_specs=[pl.BlockSpec((tm, D), rows), _const_spec((1, D)), _const_spec(w_bf.shape),
                  _const_spec(lb_logits.shape), _const_spec((1, hw))],
        out_specs=[pl.BlockSpec((tm, att_w), rows)] * 3 + [pl.BlockSpec((1, tm, att_w), keep_rows)] * 2
        + [pl.BlockSpec((tm, hw), lambda s: (rec_tile(s), 0)),
           pl.BlockSpec((1, n_heads, dk, dk), lambda s: (rec_tile(s) // n_t, 0, 0, 0))],
        out_shape=[f32(B * T, att_w)] * 3 + [jax.ShapeDtypeStruct((B, keep, att_w), F32)] * 2
        + [jax.ShapeDtypeStruct((B * T, hw), BF16), jax.ShapeDtypeStruct((B, n_heads, dk, dk), F32)],
        scratch_shapes=[pltpu.VMEM((2, tm, 4 * hw), F32), pltpu.VMEM((n_heads, dk, dk), F32)]
        + [pltpu.VMEM((tm // HG_CHUNK, HG_CHUNK, hw), F32)] * 2,
        compiler_params=_params(("arbitrary",)),
        name="in_proj_hgrn",
    )(x.reshape(B * T, D), g, w_bf, lb_logits, gn)


def _hgrn_step_kernel(hz_ref, s0_ref, lbl_ref, gn_ref, o_ref, s_ref, *, n_heads, dk, layer):
    hw = n_heads * dk
    n_t = hz_ref.shape[1]
    lb = _lower_bound(lbl_ref, layer)
    q, k, v, b, gate = [], [], [], [], []
    acc = jnp.zeros((1, hw), F32)
    for t in range(n_t):
        hq = hz_ref[0, t:t + 1, 0:hw]
        f = lb + (1.0 - lb) * _sigmoid(hz_ref[0, t:t + 1, hw:2 * hw])
        acc = acc + jnp.log(f)
        q.append(hq * _sigmoid(hq) * dk ** -0.5)
        k.append(1.0 - f)
        v.append(hz_ref[0, t:t + 1, 2 * hw:3 * hw])
        b.append(acc)
        hg = hz_ref[0, t:t + 1, 3 * hw:4 * hw]
        gate.append(hg * _sigmoid(hg))
    rr = lax.broadcasted_iota(jnp.int32, (dk, dk), 0)
    cc = lax.broadcasted_iota(jnp.int32, (dk, dk), 1)

    def tile_of_rows(rows):
        tile = jnp.zeros((dk, dk), F32)
        for t, r in enumerate(rows):
            tile = jnp.where(rr == t, jnp.broadcast_to(r, (dk, dk)), tile)
        return tile

    for h in range(n_heads):
        hs = slice(h * dk, (h + 1) * dk)
        s0 = s0_ref[0, h]
        qe = tile_of_rows([q[t][:, hs] * jnp.exp(b[t][:, hs]) for t in range(n_t)])
        inter = jnp.dot(qe, s0, precision=HIGHEST, preferred_element_type=F32)
        for t in range(n_t):
            o = inter[t:t + 1]
            for s in range(t + 1):
                a = jnp.sum(q[t][:, hs] * jnp.exp(b[t][:, hs] - b[s][:, hs]) * k[s][:, hs], axis=-1, keepdims=True)
                o = o + a * v[s][:, hs]
            o_ref[0, t:t + 1, hs] = _rms(o, gn_ref[:, hs]) * gate[t][:, hs]
        b_last = b[n_t - 1][:, hs]
        khat = tile_of_rows([k[t][:, hs] * jnp.exp(b_last - b[t][:, hs]) for t in range(n_t)])
        vpad = tile_of_rows([v[t][:, hs] for t in range(n_t)])
        decay = jnp.where(rr == cc, jnp.broadcast_to(jnp.exp(b_last), (dk, dk)), 0.0)
        s_ref[0, h] = (jnp.dot(decay, s0, precision=HIGHEST, preferred_element_type=F32)
                       + jnp.dot(khat.T, vpad, precision=HIGHEST, preferred_element_type=F32))


def _step_mixers_kernel(q_ref, kn_ref, vn_ref, kt_ref, vt_ref, g_ref, hz_ref, s0_ref, lbl_ref, gn_ref,
                        o_ref, oh_ref, s_ref, cnt_sc, bias_sc, *, att_heads, dh, n_q, hg_heads, dk, layer):
    _dil_attn_step_kernel(q_ref, kn_ref, vn_ref, kt_ref, vt_ref, g_ref, o_ref, cnt_sc, bias_sc,
                          n_heads=att_heads, dh=dh, n_q=n_q)
    _hgrn_step_kernel(hz_ref, s0_ref, lbl_ref, gn_ref, oh_ref, s_ref, n_heads=hg_heads, dk=dk, layer=layer)


def _step_mixers(q, k_new, v_new, cache_kt, cache_vt, g_att, hz, s0, lb_logits, g_hg,
                 *, att_heads, dh, layer):
    B, n_q, W = q.shape
    past = cache_kt.shape[2]
    _, hg_heads, dk, dv = s0.shape
    hw = hg_heads * dk
    bmap = lambda b: (b, 0, 0)
    smap = lambda b: (b, 0, 0, 0)
    return pl.pallas_call(
        functools.partial(_step_mixers_kernel, att_heads=att_heads, dh=dh, n_q=n_q, hg_heads=hg_heads, dk=dk,
                          layer=layer),
        grid=(B,),
        in_specs=[pl.BlockSpec((1, n_q, W), bmap)] * 3
        + [pl.BlockSpec((1, W, past), bmap), pl.BlockSpec((1, W, past), bmap), _const_spec((1, W)),
                  pl.BlockSpec((1, n_q, 4 * hw), bmap), pl.BlockSpec((1, hg_heads, dk, dv), smap),
                  _const_spec(lb_logits.shape), _const_spec((1, hw))],
        out_specs=[pl.BlockSpec((1, n_q, W), bmap), pl.BlockSpec((1, n_q, hw), bmap),
                   pl.BlockSpec((1, hg_heads, dk, dv), smap)],
        out_shape=[jax.ShapeDtypeStruct((B, n_q, W), F32), jax.ShapeDtypeStruct((B, n_q, hw), F32),
                   jax.ShapeDtypeStruct((B, hg_heads, dk, dv), F32)],
        scratch_shapes=[pltpu.VMEM((n_q * att_heads, past + LANES), F32)] * 2,
        compiler_params=_params(("arbitrary",)),
        name="step_mixers",
    )(q, k_new, v_new, cache_kt, cache_vt, g_att, hz, s0, lb_logits, g_hg)


def _mix_out(x, att_bf, ohg_bf, wo_ref):
    aw = att_bf.shape[-1]
    return x + _dot(att_bf, wo_ref[0:aw, :]) + _dot(ohg_bf, wo_ref[aw:, :])


def _cross_attend(cq, head_k, head_v, n_heads, dh):
    heads = range(n_heads)
    scale = dh ** -0.5 * LOG2E
    scores = [_dot_nt((cq[:, h * dh:(h + 1) * dh] * scale).astype(BF16), head_k(h)) for h in heads]
    probs = [jnp.exp2(s - jnp.max(s, axis=-1, keepdims=True)).astype(BF16) for s in scores]
    ones = jnp.ones((probs[0].shape[1], dh), BF16)
    pvs = [_dot(probs[h], jnp.concatenate([head_v(h), ones], axis=1)) for h in heads]
    return jnp.concatenate([pv[:, :dh] / pv[:, dh:] for pv in pvs], axis=-1)


def _ffn(u_bf, taps_fn, wg_ref, wu_ref, wd_ref, cw_ref, cb_ref, ff_chunk):
    dff = wg_ref.shape[1]
    n_taps = cw_ref.shape[0]
    acc = jnp.zeros((u_bf.shape[0], wd_ref.shape[1]), F32)
    for n in range(dff // ff_chunk):
        cs = slice(n * ff_chunk, (n + 1) * ff_chunk)
        ug = _dot(u_bf, wg_ref[:, cs])
        taps = taps_fn(ug, cs, n)
        conv = cb_ref[:, cs] + cw_ref[n_taps - 1:n_taps, cs] * taps[n_taps - 1]
        for j in range(n_taps - 1):
            conv = conv + cw_ref[j:j + 1, cs] * taps[j]
        act = conv * _sigmoid(conv) * _dot(u_bf, wu_ref[:, cs])
        acc = acc + _dot(act.astype(BF16), wd_ref[cs, :])
    return acc


def _post_kernel(x_ref, att_ref, ohg_ref, mk_ref, mv_ref, cinit_ref, gc_ref, gf_ref, gl_ref, wo_ref, wq_ref,
                 wc_ref, wg_ref, wu_ref, wd_ref, cw_ref, cb_ref, y_ref, cst_ref, buf_ref, carry_ref,
                 *, x_heads, x_dh, ff_chunk, n_sub):
    i = pl.program_id(1)
    tm = x_ref.shape[1]
    pad = SUBLANES
    n_taps = cw_ref.shape[0]

    @pl.when(i == 0)
    def _():
        carry_ref[...] = cinit_ref[0]

    ts = tm // n_sub
    for t in range(n_sub):
        rows = slice(t * ts, (t + 1) * ts)
        x1 = _mix_out(x_ref[0, rows], att_ref[0, rows], ohg_ref[0, rows], wo_ref)
        cq = _dot(_rms(x1, gc_ref[...]).astype(BF16), wq_ref[...])
        co = _cross_attend(cq, lambda h: mk_ref[0, :, h * x_dh:(h + 1) * x_dh],
                           lambda h: mv_ref[0, :, h * x_dh:(h + 1) * x_dh], x_heads, x_dh)
        x2 = x1 + _dot(co.astype(BF16), wc_ref[...])

        def taps_fn(ug, cs, n, t=t):
            buf = buf_ref.at[t, n % buf_ref.shape[1]]
            buf[0:pad, :] = carry_ref[:, cs]
            buf[pad:pad + ts, :] = ug
            carry_ref[:, cs] = ug[ts - pad:ts]
            return [buf[pad - (n_taps - 1 - j):pad - (n_taps - 1 - j) + ts, :] for j in range(n_taps - 1)] + [ug]

        u_bf = _rms(x2, gf_ref[...]).astype(BF16)
        x3 = x2 + _ffn(u_bf, taps_fn, wg_ref, wu_ref, wd_ref, cw_ref, cb_ref, ff_chunk)
        y_ref[0, rows] = _rms(x3, gl_ref[...])
    cst_ref[0] = carry_ref[...]


def _post(x, att, ohg, mk_bf, mv_bf, cinit, gains, weights, conv_w, conv_b, *, tm, n_sub, x_heads, ff_chunk):
    B, T, D = x.shape
    aw, hw = att.shape[-1], ohg.shape[-1]
    n_mem, xw = mk_bf.shape[1:]
    dff = conv_w.shape[1]
    row = lambda b, i: (b, i, 0)
    bat = lambda b, i: (b, 0, 0)
    in_specs = ([pl.BlockSpec((1, tm, D), row), pl.BlockSpec((1, tm, aw), row), pl.BlockSpec((1, tm, hw), row)]
                + [pl.BlockSpec((1, n_mem, xw), bat)] * 2 + [pl.BlockSpec((1, SUBLANES, dff), bat)]
                + [_const_spec(a.shape) for a in tuple(gains) + tuple(weights) + (conv_w, conv_b)])
    return pl.pallas_call(
        functools.partial(_post_kernel, x_heads=x_heads, x_dh=xw // x_heads, ff_chunk=ff_chunk, n_sub=n_sub),
        grid=(B, T // tm),
        in_specs=in_specs,
        out_specs=[pl.BlockSpec((1, tm, D), row), pl.BlockSpec((1, SUBLANES, dff), bat)],
        out_shape=[jax.ShapeDtypeStruct((B, T, D), F32), jax.ShapeDtypeStruct((B, SUBLANES, dff), F32)],
        scratch_shapes=[pltpu.VMEM((n_sub, min(2, dff // ff_chunk), tm // n_sub + SUBLANES, ff_chunk), F32),
                        pltpu.VMEM((SUBLANES, dff), F32)],
        compiler_params=_params(("parallel", "arbitrary")),
        name="post_mixer",
    )(x, att, ohg, mk_bf, mv_bf, cinit, *gains, *weights, conv_w, conv_b)


def _step_mix_kernel(x_ref, att_ref, ohg_ref, gc_ref, wo_ref, wq_ref, x1_ref, cq_ref):
    x1 = _mix_out(x_ref[...], att_ref[...].astype(BF16), ohg_ref[...].astype(BF16), wo_ref)
    x1_ref[...] = x1
    cq_ref[...] = _dot(_rms(x1, gc_ref[...]).astype(BF16), wq_ref[...])


def _step_cross_kernel(cq_ref, mk_ref, mv_ref, co_ref, *, n_q):
    n_b, n_mem, n_heads, dh = mk_ref.shape
    row = lax.broadcasted_iota(jnp.int32, (cq_ref.shape[1], n_mem * n_heads), 0)
    col = lax.broadcasted_iota(jnp.int32, (cq_ref.shape[1], n_mem * n_heads), 1)
    own_head = (col % n_heads) == (row // n_q)
    for b in range(n_b):
        mk = mk_ref[b].reshape(n_mem * n_heads, dh).astype(BF16)
        mv = mv_ref[b].reshape(n_mem * n_heads, dh).astype(BF16)
        s = jnp.where(own_head, _dot_nt((cq_ref[b] * dh ** -0.5).astype(BF16), mk), NEG)
        p = jnp.exp(s - jnp.max(s, axis=-1, keepdims=True))
        co_ref[b] = _dot(p.astype(BF16), mv) / jnp.sum(p, axis=-1, keepdims=True)


def _step_ffn_kernel(x1_ref, co_ref, prev_ref, gf_ref, gl_ref, wc_ref, wg_ref, wu_ref, wd_ref,
                     cw_ref, cb_ref, y_ref, ug_ref, buf_ref, *, ff_chunk, n_t):
    n = x1_ref.shape[0]
    pad = SUBLANES
    n_taps = cw_ref.shape[0]
    x2 = x1_ref[...] + _dot(co_ref[...].astype(BF16), wc_ref[...])
    t_of_row = lax.broadcasted_iota(jnp.int32, (n, ff_chunk), 0) % n_t

    def taps_fn(ug, cs, _):
        ug_ref[:, cs] = ug
        buf_ref[0:pad, cs] = jnp.zeros((pad, ff_chunk), F32)
        buf_ref[pad:pad + n, cs] = ug
        taps = []
        for j in range(n_taps - 1):
            shift = n_taps - 1 - j
            shifted = buf_ref[pad - shift:pad - shift + n, cs]
            taps.append(jnp.where(t_of_row < shift, prev_ref[j, :, cs], shifted))
        return taps + [ug]

    u_bf = _rms(x2, gf_ref[...]).astype(BF16)
    x3 = x2 + _ffn(u_bf, taps_fn, wg_ref, wu_ref, wd_ref, cw_ref, cb_ref, ff_chunk)
    y_ref[...] = _rms(x3, gl_ref[...])


def _single_call(kernel, args, out_shape, name, scratch_shapes=()):
    return pl.pallas_call(
        kernel,
        grid=(1,),
        in_specs=[_const_spec(a.shape) for a in args],
        out_specs=[pl.BlockSpec(o.shape, lambda *_, nd=len(o.shape): (0,) * nd) for o in out_shape],
        out_shape=out_shape,
        scratch_shapes=list(scratch_shapes),
        compiler_params=_params(("arbitrary",)),
        name=name,
    )(*args)


def _step_cross(cq, mem_k, mem_v, *, n_q):
    B, n_rows, dh = cq.shape
    _, n_mem, x_heads, _ = mem_k.shape
    n_b = 4 if B % 4 == 0 else 1
    bmap = lambda b: (b, 0, 0)
    mem = pl.BlockSpec((n_b, n_mem, x_heads, dh), lambda b: (b, 0, 0, 0))
    return pl.pallas_call(
        functools.partial(_step_cross_kernel, n_q=n_q),
        grid=(B // n_b,),
        in_specs=[pl.BlockSpec((n_b, n_rows, dh), bmap), mem, mem],
        out_specs=pl.BlockSpec((n_b, n_rows, dh), bmap),
        out_shape=jax.ShapeDtypeStruct((B, n_rows, dh), F32),
        compiler_params=_params(("parallel",)),
        name="step_cross",
    )(cq, mem_k, mem_v)


def kernel(x_prompt, x_sample, cache_win_k, cache_win_v, state_hgrn, state_ffn_conv, cache_mem_k, cache_mem_v,
           mem_prompt, hg_lb_logits, norm_mix, w_in, att_out_norm, hg_out_norm, w_out, norm_cross, norm_mem,
           w_cq, w_ck, w_cv, w_co, norm_ffn, w_gate, w_up, conv_w, conv_b, w_down, norm_final):
    Bp, T, D = x_prompt.shape
    Bs, Ts, _ = x_sample.shape
    depth, _, past, att_h, att_dh = cache_win_k.shape
    _, _, hg_h, hg_dk, hg_dv = state_hgrn.shape
    _, _, n_mem, x_h, x_dh = cache_mem_k.shape
    n_taps, dff = conv_w.shape[1:]
    att_w = att_h * att_dh
    hg_w = hg_h * hg_dk
    xw = x_h * x_dh
    keep = min(max(w for w, _ in DIL_PATTERNS), T)
    assert depth == 1 and hg_dk == hg_dv == LANES and x_dh == LANES and LANES % att_dh == 0
    assert all(w // d == ATT_BLOCK for w, d in DIL_PATTERNS) and past >= max(w for w, _ in DIL_PATTERNS)
    d_max = max(d for _, d in DIL_PATTERNS)
    assert T % (ATT_BLOCK * d_max) == 0 and past % d_max == 0 and past % (4 * ATT_BLOCK) == 0
    assert n_taps - 1 <= min(Ts, SUBLANES) and Ts <= SUBLANES
    layer = 0
    ff_chunk = dff
    q_scale = att_dh ** -0.5 * LOG2E
    row2 = lambda a: a.reshape(1, -1)

    w_in_bf = w_in[layer].astype(BF16)
    w_ckv_bf = jnp.concatenate([w_ck[layer], w_cv[layer]], axis=1).astype(BF16)
    weights = tuple(w[layer].astype(BF16) for w in (w_out, w_cq, w_co, w_gate, w_up, w_down))
    g_mix, g_att, g_hg = row2(norm_mix[layer]), row2(att_out_norm[layer]), row2(hg_out_norm[layer])
    g_cross, g_mem, g_ffn, g_final = (row2(norm_cross[layer]), row2(norm_mem[layer]), row2(norm_ffn[layer]),
                                      row2(norm_final))
    cw, cb = conv_w[layer], row2(conv_b[layer])

    q, k, v, k_keep, v_keep, ohg, s_prompt = _in_proj_hgrn(
        x_prompt, g_mix, w_in_bf, hg_lb_logits, g_hg, att_w=att_w, q_scale=q_scale, n_heads=hg_h, dk=hg_dk,
        tm=512, keep=keep, layer=layer)
    q, k, v = (a.reshape(Bp, T, att_w) for a in (q, k, v))
    ohg = ohg.reshape(Bp, T, hg_w)
    mk, mv, mk_bf, mv_bf = _mem_kv(mem_prompt.reshape(Bp * n_mem, D), g_mem, w_ckv_bf, tm=256)
    att = _dil_attn(q, k, v, g_att, n_heads=att_h, dh=att_dh)
    y_prompt, cst = _post(
        x_prompt, att, ohg, mk_bf.reshape(Bp, n_mem, xw), mv_bf.reshape(Bp, n_mem, xw),
        jnp.zeros((Bp, SUBLANES, dff), F32), (g_cross, g_ffn, g_final), weights, cw, cb,
        tm=512, n_sub=1, x_heads=x_h, ff_chunk=ff_chunk)

    n_s = Bs * Ts
    qs, ks, vs, ks_f, vs_f, hzs = _in_proj(x_sample.reshape(1, n_s, D), g_mix, w_in_bf, att_w=att_w, q_scale=q_scale,
                                           tm=n_s, keep=n_s)
    per_seq = lambda a: a.reshape(Bs, Ts, att_w)
    feature_major = lambda c: jnp.transpose(c, (0, 2, 3, 1)).reshape(Bs, att_w, past)
    att_s, ohg_s, s_sample = _step_mixers(
        per_seq(qs), per_seq(ks), per_seq(vs), feature_major(cache_win_k[layer]),
        feature_major(cache_win_v[layer]), g_att, hzs.reshape(Bs, Ts, 4 * hg_w), state_hgrn[layer], hg_lb_logits,
        g_hg, att_heads=att_h, dh=att_dh, layer=layer)
    x1_s, cq_s = _single_call(
        _step_mix_kernel,
        (x_sample.reshape(n_s, D), att_s.reshape(n_s, att_w), ohg_s.reshape(n_s, hg_w), g_cross, weights[0],
         weights[1]),
        [jax.ShapeDtypeStruct((n_s, D), F32), jax.ShapeDtypeStruct((n_s, xw), F32)], "step_mix")
    cq_heads = cq_s.reshape(Bs, Ts, x_h, x_dh).transpose(0, 2, 1, 3).reshape(Bs, x_h * Ts, x_dh)
    co_s = _step_cross(cq_heads, cache_mem_k[layer], cache_mem_v[layer], n_q=Ts)
    co_s = co_s.reshape(Bs, x_h, Ts, x_dh).transpose(0, 2, 1, 3)
    conv_state = state_ffn_conv[layer]
    t_idx = jnp.arange(Ts)
    prev = jnp.stack([jnp.take(conv_state, jnp.clip(j + t_idx, 0, n_taps - 2), axis=1).reshape(n_s, dff)
                      for j in range(n_taps - 1)])
    y_s, ug_s = _single_call(
        functools.partial(_step_ffn_kernel, ff_chunk=ff_chunk, n_t=Ts),
        (x1_s, co_s.reshape(n_s, xw), prev, g_ffn, g_final, weights[2], weights[3], weights[4], weights[5], cw, cb),
        [jax.ShapeDtypeStruct((n_s, D), F32), jax.ShapeDtypeStruct((n_s, dff), F32)], "step_ffn",
        scratch_shapes=[pltpu.VMEM((n_s + SUBLANES, dff), F32)])

    stack = lambda a: a[None]
    return (y_prompt, y_s.reshape(Bs, Ts, D),
            stack(k_keep.reshape(Bp, keep, att_h, att_dh)), stack(v_keep.reshape(Bp, keep, att_h, att_dh)),
            stack(s_prompt), stack(cst[:, SUBLANES - (n_taps - 1):]),
            stack(mk.reshape(Bp, n_mem, x_h, x_dh)), stack(mv.reshape(Bp, n_mem, x_h, x_dh)),
            stack(ks_f.reshape(Bs, Ts, att_h, att_dh)), stack(vs_f.reshape(Bs, Ts, att_h, att_dh)),
            stack(s_sample), stack(ug_s.reshape(Bs, Ts, dff)[:, Ts - (n_taps - 1):]))
```

```python
import functools

import jax
import jax.numpy as jnp
from jax import lax
from jax.experimental import pallas as pl
from jax.experimental.pallas import tpu as pltpu

F32 = jnp.float32
BF16 = jnp.bfloat16
EPS = 1e-6
DIL_PATTERNS = ((128, 1), (512, 4), (2048, 16))
ATT_BLOCK = 128
LANES = 128
SUBLANES = 8
ROW_TILE = 512
MEM_TILE = 256
HG_CHUNK = 64
HG_SUB = 8
NEG = -1e30
LOG2E = 1.4426950408889634
HIGHEST = lax.Precision.HIGHEST
NT_DIMS = (((1,), (1,)), ((), ()))
TN_DIMS = (((0,), (0,)), ((), ()))
VMEM_LIMIT = 56 * 1024 * 1024


def _dot(a, b):
    return jnp.dot(a, b, preferred_element_type=F32)


def _dot_nt(a, b):
    return lax.dot_general(a, b, NT_DIMS, preferred_element_type=F32)


def _sigmoid(x):
    return 1.0 / (1.0 + jnp.exp(-x))


def _rms(x, g):
    return x * lax.rsqrt(jnp.mean(x * x, axis=-1, keepdims=True) + EPS) * g


def _const_spec(shape):
    nd = len(shape)
    return pl.BlockSpec(shape, lambda *_: (0,) * nd, pipeline_mode=pl.Buffered(1))


def _params(sem):
    return pltpu.CompilerParams(dimension_semantics=sem, vmem_limit_bytes=VMEM_LIMIT)


def _in_proj_kernel(x_ref, g_ref, w_ref, q_ref, k_ref, v_ref, kf_ref, vf_ref, hz_ref, *, att_w, q_scale):
    h = _rms(x_ref[0], g_ref[...]).astype(BF16)

    def proj(lo, hi):
        return _dot(h, w_ref[:, lo:hi])

    q_ref[0] = proj(0, att_w) * q_scale
    zk = proj(att_w, 2 * att_w)
    k_ref[0] = zk
    kf_ref[0] = zk
    zv = proj(2 * att_w, 3 * att_w)
    v_ref[0] = zv
    vf_ref[0] = zv
    base = 3 * att_w
    step = 4 * LANES
    for c in range(0, hz_ref.shape[-1], step):
        hz_ref[0, :, c:c + step] = proj(base + c, base + c + step)


def _in_proj(x, g, w_bf, *, att_w, q_scale, tm, keep):
    B, T, D = x.shape
    ncols = w_bf.shape[1]
    hzw = ncols - 3 * att_w
    n_t = T // tm
    first = n_t - keep // tm
    row = lambda b, i: (b, i, 0)
    keep_map = lambda b, i: (b, jnp.maximum(i - first, 0), 0)
    return pl.pallas_call(
        functools.partial(_in_proj_kernel, att_w=att_w, q_scale=q_scale),
        grid=(B, n_t),
        in_specs=[pl.BlockSpec((1, tm, D), row), _const_spec((1, D)), _const_spec((D, ncols))],
        out_specs=[pl.BlockSpec((1, tm, att_w), row)] * 3
        + [pl.BlockSpec((1, tm, att_w), keep_map)] * 2
        + [pl.BlockSpec((1, tm, hzw), row)],
        out_shape=[jax.ShapeDtypeStruct((B, T, att_w), F32)] * 3
        + [jax.ShapeDtypeStruct((B, keep, att_w), F32)] * 2
        + [jax.ShapeDtypeStruct((B, T, hzw), F32)],
        compiler_params=_params(("parallel", "arbitrary")),
        name="in_proj",
    )(x, g, w_bf)


def _mem_kv_kernel(m_ref, g_ref, w_ref, kf_ref, vf_ref, kb_ref, vb_ref):
    h = _rms(m_ref[...], g_ref[...]).astype(BF16)
    xw = w_ref.shape[1] // 2
    k = _dot(h, w_ref[:, :xw])
    v = _dot(h, w_ref[:, xw:])
    kf_ref[...] = k
    vf_ref[...] = v
    kb_ref[...] = k.astype(BF16)
    vb_ref[...] = v.astype(BF16)


def _mem_kv(mem2d, g, w_ckv_bf, *, tm):
    n, D = mem2d.shape
    xw = w_ckv_bf.shape[1] // 2
    row = lambda i: (i, 0)
    return pl.pallas_call(
        _mem_kv_kernel,
        grid=(n // tm,),
        in_specs=[pl.BlockSpec((tm, D), row), _const_spec((1, D)), _const_spec((D, 2 * xw))],
        out_specs=[pl.BlockSpec((tm, xw), row)] * 4,
        out_shape=[jax.ShapeDtypeStruct((n, xw), F32)] * 2 + [jax.ShapeDtypeStruct((n, xw), BF16)] * 2,
        compiler_params=_params(("parallel",)),
        name="mem_kv",
    )(mem2d, g, w_ckv_bf)


def _dil_attn_kernel(q_ref, k_ref, v_ref, g_ref, o_ref, acc_sc, m_sc, den_sc, bias_sc, *, n_heads, dh):
    tile = pl.program_id(1)
    T = q_ref.shape[1]
    blk = ATT_BLOCK
    hpt = LANES // dh
    rows, ncol = hpt * blk, 2 * blk
    rr = lax.broadcasted_iota(jnp.int32, (rows, ncol), 0)
    cc = lax.broadcasted_iota(jnp.int32, (rows, ncol), 1)
    delta = (rr % blk) + blk - cc
    in_band = (delta >= 0) & (delta <= blk)
    slope = jnp.exp2((-8.0 / n_heads) * (tile * hpt + rr // blk + 1).astype(F32))
    lane_head = lax.broadcasted_iota(jnp.int32, (blk, LANES), 1) // dh
    n_pat = len(DIL_PATTERNS)

    def per_lane(x):
        out = jnp.broadcast_to(x[0:blk], (blk, LANES))
        for e in range(1, hpt):
            out = jnp.where(lane_head == e, jnp.broadcast_to(x[e * blk:(e + 1) * blk], (blk, LANES)), out)
        return out

    ones = jnp.ones((ncol, LANES), BF16)
    order = sorted(range(n_pat), key=lambda p: -DIL_PATTERNS[p][1])
    for step, p in enumerate(order):
        dil = DIL_PATTERNS[p][1]
        alibi = (-LOG2E) * slope * (delta * dil).astype(F32)
        bias_sc[0] = jnp.where(in_band & (cc >= blk), alibi, NEG)
        bias_sc[1] = jnp.where(in_band, alibi, NEG)
        nblk = T // (blk * dil)

        def body(n, carry, first=step == 0, last=step == n_pat - 1, dil=dil, nblk=nblk):
            kp, vp = carry
            i = n % nblk
            start = n // nblk + i * (blk * dil)
            idx = pl.ds(start, blk, stride=dil) if dil > 1 else pl.ds(start, blk)
            q2 = q_ref[0, idx, :]
            kc = k_ref[0, idx, :].astype(BF16)
            vc = v_ref[0, idx, :].astype(BF16)
            qs = jnp.concatenate([jnp.where(lane_head == e, q2, 0.0) for e in range(hpt)], axis=0).astype(BF16)
            s = _dot_nt(qs, jnp.concatenate([kp, kc], axis=0)) + bias_sc[min(i, 1)]
            m = jnp.max(s, axis=-1, keepdims=True)
            pr = jnp.exp2(s - m).astype(BF16)
            pv = _dot(pr, jnp.concatenate([jnp.concatenate([vp, vc], axis=0), ones], axis=1))
            m_t, acc_t, den_t = per_lane(m), per_lane(pv[:, :LANES]), per_lane(pv[:, LANES:])
            if not first:
                m_old = m_sc[idx, :]
                m_new = jnp.maximum(m_old, m_t)
                a_old, a_t = jnp.exp2(m_old - m_new), jnp.exp2(m_t - m_new)
                den_t = a_old * den_sc[idx, :] + a_t * den_t
                acc_t = a_old * acc_sc[idx, :] + a_t * acc_t
                m_t = m_new
            if last:
                acc_sc[idx, :] = acc_t / den_t
            else:
                m_sc[idx, :] = m_t
                den_sc[idx, :] = den_t
                acc_sc[idx, :] = acc_t
            return kc, vc

        carry = (jnp.zeros((blk, LANES), BF16),) * 2
        for n in range(T // blk):
            carry = body(n, carry)

    ch = 4 * blk
    same_head = (lax.broadcasted_iota(jnp.int32, (LANES, LANES), 0) // dh
                 == lax.broadcasted_iota(jnp.int32, (LANES, LANES), 1) // dh).astype(BF16)
    for c in range(T // ch):
        a = acc_sc[c * ch:(c + 1) * ch, :]
        mean_sq = _dot((a * a).astype(BF16), same_head) * (1.0 / dh)
        o_ref[0, c * ch:(c + 1) * ch, :] = (a * lax.rsqrt(mean_sq + EPS) * g_ref[...]).astype(BF16)


def _dil_attn(q, k, v, g, *, n_heads, dh):
    B, T, W = q.shape
    blk = ATT_BLOCK
    seq = pl.BlockSpec((1, T, LANES), lambda b, t: (b, 0, t))
    return pl.pallas_call(
        functools.partial(_dil_attn_kernel, n_heads=n_heads, dh=dh),
        grid=(B, W // LANES),
        in_specs=[seq, seq, seq, pl.BlockSpec((1, LANES), lambda b, t: (0, t))],
        out_specs=seq,
        out_shape=jax.ShapeDtypeStruct((B, T, W), BF16),
        scratch_shapes=[pltpu.VMEM((T, LANES), F32)] * 3
        + [pltpu.VMEM((2, (LANES // dh) * blk, 2 * blk), F32)],
        compiler_params=_params(("parallel", "arbitrary")),
        name="dil_attn",
    )(q, k, v, g)


def _dil_attn_step_kernel(q_ref, kn_ref, vn_ref, kt_ref, vt_ref, g_ref, o_ref, cnt_sc, bias_sc,
                          *, n_heads, dh, n_q):
    past = kt_ref.shape[2]
    n_new = LANES
    W = q_ref.shape[-1]
    rows = n_q * n_heads

    def new_rows(ref):
        sub8 = lax.broadcasted_iota(jnp.int32, (SUBLANES, W), 0)
        tile = jnp.zeros((SUBLANES, W), F32)
        for i in range(n_q):
            tile = jnp.where(sub8 == i, jnp.broadcast_to(ref[0, i:i + 1, :], (SUBLANES, W)), tile)
        return jnp.concatenate([tile, jnp.zeros((n_new - SUBLANES, W), F32)], axis=0).astype(BF16)
    n_cols = past + n_new

    @pl.when(pl.program_id(0) == 0)
    def _():
        r = lax.broadcasted_iota(jnp.int32, (rows, n_cols), 0)
        c = lax.broadcasted_iota(jnp.int32, (rows, n_cols), 1)
        delta = past + r // n_heads - c
        cnt = jnp.zeros((rows, n_cols), F32)
        for win, dil in DIL_PATTERNS:
            cnt = cnt + ((delta >= 0) & (delta <= win) & ((delta & (dil - 1)) == 0)).astype(F32)
        slope = jnp.exp2((-8.0 / n_heads) * ((r % n_heads) + 1).astype(F32))
        cnt_sc[...] = cnt
        bias_sc[...] = jnp.where(cnt > 0.0, (-LOG2E) * slope * delta.astype(F32), NEG)

    sub = lax.broadcasted_iota(jnp.int32, (n_heads, W), 0)
    lane_head = lax.broadcasted_iota(jnp.int32, (n_heads, W), 1) // dh
    qbd = jnp.concatenate(
        [jnp.where(sub == lane_head, jnp.broadcast_to(q_ref[0, i:i + 1, :], (n_heads, W)), 0.0) for i in range(n_q)],
        axis=0).astype(BF16)
    s = jnp.concatenate([_dot(qbd, kt_ref[0].astype(BF16)), _dot_nt(qbd, new_rows(kn_ref))], axis=1) + bias_sc[...]
    p = jnp.exp2(s - jnp.max(s, axis=-1, keepdims=True)) * cnt_sc[...]
    den = jnp.sum(p, axis=-1, keepdims=True)
    p = p.astype(BF16)
    out = (_dot_nt(p[:, :past], vt_ref[0].astype(BF16)) + _dot(p[:, past:], new_rows(vn_ref))) / den
    r = lax.broadcasted_iota(jnp.int32, (rows, W), 0)
    c = lax.broadcasted_iota(jnp.int32, (rows, W), 1)
    out = jnp.where((r % n_heads) == (c // dh), out, 0.0)
    out = out * lax.rsqrt(jnp.sum(out * out, axis=-1, keepdims=True) * (1.0 / dh) + EPS)
    o_ref[0] = jnp.sum(out.reshape(n_q, n_heads, W), axis=1) * g_ref[...]


def _lower_bound(lbl_ref, layer):
    logits = lbl_ref[...]
    e = jnp.exp(logits - jnp.max(logits, axis=0, keepdims=True))
    return jnp.sum(e[:layer + 1], axis=0, keepdims=True) / jnp.sum(e, axis=0, keepdims=True)


def _cumsum_rows(tril_bf, x):
    x1 = x.astype(BF16)
    r1 = x - x1.astype(F32)
    x2 = r1.astype(BF16)
    x3 = (r1 - x2.astype(F32)).astype(BF16)
    return _dot(tril_bf, x1) + _dot(tril_bf, x2) + _dot(tril_bf, x3)


def _hgrn_rows(hz_ref, o_ref, lb, gn_ref, st_ref, b_sc, c_sc, *, n_heads, dk):
    hw = n_heads * dk
    tc = hz_ref.shape[0]
    ch, sub = HG_CHUNK, HG_SUB
    r_c = lax.broadcasted_iota(jnp.int32, (ch, ch), 0)
    c_c = lax.broadcasted_iota(jnp.int32, (ch, ch), 1)
    tril = (r_c >= c_c).astype(BF16)
    r_s = lax.broadcasted_iota(jnp.int32, (sub, ch), 0)
    c_s = lax.broadcasted_iota(jnp.int32, (sub, ch), 1)

    local = [[None] * n_heads for _ in range(tc // ch)]
    for c in range(tc // ch):
        rows = slice(c * ch, (c + 1) * ch)
        hq = hz_ref[rows, 0:hw]
        f = lb + (1.0 - lb) * _sigmoid(hz_ref[rows, hw:2 * hw])
        qh = hq * _sigmoid(hq) * dk ** -0.5
        kk = 1.0 - f
        b = _cumsum_rows(tril, jnp.log(f) * LOG2E)
        b_rows, c_rows = b_sc.at[c], c_sc.at[c]
        b_rows[...] = b
        c_rows[...] = b - jnp.log(kk) * LOG2E
        for h in range(n_heads):
            hs = slice(h * dk, (h + 1) * dk)
            bh = b[:, hs]
            kh = kk[:, hs]
            qhh = qh[:, hs]
            vh = hz_ref[rows, 2 * hw + h * dk:2 * hw + (h + 1) * dk].astype(BF16)
            blocks = []
            for j in range(ch // sub):
                r0 = j * sub
                bj = bh[r0:r0 + sub]
                qj = qhh[r0:r0 + sub]
                a_j = jnp.zeros((sub, ch), F32)
                for s in range(sub):
                    a = jnp.sum(qj * jnp.exp2(bj - c_rows[r0 + s:r0 + s + 1, hs]), axis=-1, keepdims=True)
                    a_j = jnp.where(c_s == r0 + s, a, a_j)
                if j > 0:
                    beta = b_rows[r0 - 1:r0, hs]
                    qt = qj * jnp.exp2(bj - beta)
                    kt = kh[0:r0] * jnp.exp2(beta - bh[0:r0])
                    kt = jnp.concatenate([kt, jnp.zeros((ch - r0, dk), F32)], axis=0)
                    a_j = a_j + _dot_nt(qt.astype(BF16), kt.astype(BF16))
                blocks.append(jnp.where(c_s - r0 <= r_s, a_j, 0.0))
            b_last = b_rows[ch - 1:ch, hs]
            hg = hz_ref[rows, 3 * hw + h * dk:3 * hw + (h + 1) * dk]
            local[c][h] = dict(
                q=(qhh * jnp.exp2(bh)).astype(BF16), a=jnp.concatenate(blocks, axis=0).astype(BF16), v=vh,
                k=(kh * jnp.exp2(b_last - bh)).astype(BF16), decay=jnp.exp2(b_last), gate=hg * _sigmoid(hg))

    state = [st_ref[h] for h in range(n_heads)]
    for c in range(tc // ch):
        rows = slice(c * ch, (c + 1) * ch)
        for h in range(n_heads):
            hs = slice(h * dk, (h + 1) * dk)
            t = local[c][h]
            o = _dot_nt(t["q"], state[h].astype(BF16)) + _dot(t["a"], t["v"])
            state[h] = state[h] * t["decay"] + lax.dot_general(t["v"], t["k"], TN_DIMS, preferred_element_type=F32)
            o_ref[rows, hs] = (_rms(o, gn_ref[:, hs]) * t["gate"]).astype(BF16)
    for h in range(n_heads):
        st_ref[h] = state[h]


def _in_proj_hgrn_kernel(x_ref, g_ref, w_ref, lbl_ref, gn_ref, q_ref, k_ref, v_ref, kf_ref, vf_ref, o_ref, s_ref,
                         hz_sc, st_ref, b_sc, c_sc, *, att_w, q_scale, n_heads, dk, layer, tiles_per_seq):
    g = pl.program_id(0)
    t_rec = jnp.maximum(g - 1, 0)

    @pl.when(g == 0)
    def _():
        hz_sc[...] = jnp.zeros_like(hz_sc)

    @pl.when(t_rec % tiles_per_seq == 0)
    def _():
        st_ref[...] = jnp.zeros_like(st_ref)

    lb = _lower_bound(lbl_ref, layer)
    slot_w = g % 2
    hz = hz_sc.at[1 - slot_w]

    h = _rms(x_ref[...], g_ref[...]).astype(BF16)

    def proj(lo, hi):
        return _dot(h, w_ref[:, lo:hi])

    q_ref[...] = proj(0, att_w) * q_scale
    zk = proj(att_w, 2 * att_w)
    k_ref[...] = zk
    kf_ref[0] = zk
    zv = proj(2 * att_w, 3 * att_w)
    v_ref[...] = zv
    vf_ref[0] = zv
    base = 3 * att_w
    step = 4 * LANES
    for c in range(0, hz_sc.shape[-1], step):
        hz_sc[slot_w, :, c:c + step] = proj(base + c, base + c + step)

    _hgrn_rows(hz, o_ref, lb, gn_ref, st_ref, b_sc, c_sc, n_heads=n_heads, dk=dk)

    @pl.when((g > 0) & (t_rec % tiles_per_seq == tiles_per_seq - 1))
    def _():
        for h in range(n_heads):
            s_ref[0, h] = st_ref[h].T


def _in_proj_hgrn(x, g, w_bf, lb_logits, gn, *, att_w, q_scale, n_heads, dk, tm, keep, layer):
    B, T, D = x.shape
    hw = n_heads * dk
    n_t = T // tm
    n_tiles = B * n_t
    n_keep = keep // tm
    first = n_t - n_keep
    proj_tile = lambda s: jnp.minimum(s, n_tiles - 1)
    rec_tile = lambda s: jnp.maximum(s - 1, 0)
    rows = lambda s: (proj_tile(s), 0)
    keep_rows = lambda s: (proj_tile(s) // n_t, jnp.maximum(proj_tile(s) % n_t - first, 0), 0)
    f32 = lambda n, w: jax.ShapeDtypeStruct((n, w), F32)
    return pl.pallas_call(
        functools.partial(_in_proj_hgrn_kernel, att_w=att_w, q_scale=q_scale, n_heads=n_heads, dk=dk, layer=layer,
                          tiles_per_seq=n_t),
        grid=(n_tiles + 1,),
        in_specs=[pl.BlockSpec((tm, D), rows), _const_spec((1, D)), _const_spec(w_bf.shape),
                  _const_spec(lb_logits.shape), _const_spec((1, hw))],
        out_specs=[pl.BlockSpec((tm, att_w), rows)] * 3 + [pl.BlockSpec((1, tm, att_w), keep_rows)] * 2
        + [pl.BlockSpec((tm, hw), lambda s: (rec_tile(s), 0)),
           pl.BlockSpec((1, n_heads, dk, dk), lambda s: (rec_tile(s) // n_t, 0, 0, 0))],
        out_shape=[f32(B * T, att_w)] * 3 + [jax.ShapeDtypeStruct((B, keep, att_w), F32)] * 2
        + [jax.ShapeDtypeStruct((B * T, hw), BF16), jax.ShapeDtypeStruct((B, n_heads, dk, dk), F32)],
        scratch_shapes=[pltpu.VMEM((2, tm, 4 * hw), F32), pltpu.VMEM((n_heads, dk, dk), F32)]
        + [pltpu.VMEM((tm // HG_CHUNK, HG_CHUNK, hw), F32)] * 2,
        compiler_params=_params(("arbitrary",)),
        name="in_proj_hgrn",
    )(x.reshape(B * T, D), g, w_bf, lb_logits, gn)


def _hgrn_step_kernel(hz_ref, s0_ref, lbl_ref, gn_ref, o_ref, s_ref, *, n_heads, dk, layer):
    hw = n_heads * dk
    n_t = hz_ref.shape[1]
    lb = _lower_bound(lbl_ref, layer)
    q, k, v, b, gate = [], [], [], [], []
    acc = jnp.zeros((1, hw), F32)
    for t in range(n_t):
        hq = hz_ref[0, t:t + 1, 0:hw]
        f = lb + (1.0 - lb) * _sigmoid(hz_ref[0, t:t + 1, hw:2 * hw])
        acc = acc + jnp.log(f)
        q.append(hq * _sigmoid(hq) * dk ** -0.5)
        k.append(1.0 - f)
        v.append(hz_ref[0, t:t + 1, 2 * hw:3 * hw])
        b.append(acc)
        hg = hz_ref[0, t:t + 1, 3 * hw:4 * hw]
        gate.append(hg * _sigmoid(hg))
    rr = lax.broadcasted_iota(jnp.int32, (dk, dk), 0)
    cc = lax.broadcasted_iota(jnp.int32, (dk, dk), 1)

    def tile_of_rows(rows):
        tile = jnp.zeros((dk, dk), F32)
        for t, r in enumerate(rows):
            tile = jnp.where(rr == t, jnp.broadcast_to(r, (dk, dk)), tile)
        return tile

    for h in range(n_heads):
        hs = slice(h * dk, (h + 1) * dk)
        s0 = s0_ref[0, h]
        qe = tile_of_rows([q[t][:, hs] * jnp.exp(b[t][:, hs]) for t in range(n_t)])
        inter = jnp.dot(qe, s0, precision=HIGHEST, preferred_element_type=F32)
        for t in range(n_t):
            o = inter[t:t + 1]
            for s in range(t + 1):
                a = jnp.sum(q[t][:, hs] * jnp.exp(b[t][:, hs] - b[s][:, hs]) * k[s][:, hs], axis=-1, keepdims=True)
                o = o + a * v[s][:, hs]
            o_ref[0, t:t + 1, hs] = _rms(o, gn_ref[:, hs]) * gate[t][:, hs]
        b_last = b[n_t - 1][:, hs]
        khat = tile_of_rows([k[t][:, hs] * jnp.exp(b_last - b[t][:, hs]) for t in range(n_t)])
        vpad = tile_of_rows([v[t][:, hs] for t in range(n_t)])
        decay = jnp.where(rr == cc, jnp.broadcast_to(jnp.exp(b_last), (dk, dk)), 0.0)
        s_ref[0, h] = (jnp.dot(decay, s0, precision=HIGHEST, preferred_element_type=F32)
                       + jnp.dot(khat.T, vpad, precision=HIGHEST, preferred_element_type=F32))


def _step_mixers_kernel(q_ref, kn_ref, vn_ref, kt_ref, vt_ref, g_ref, hz_ref, s0_ref, lbl_ref, gn_ref,
                        o_ref, oh_ref, s_ref, cnt_sc, bias_sc, *, att_heads, dh, n_q, hg_heads, dk, layer):
    _dil_attn_step_kernel(q_ref, kn_ref, vn_ref, kt_ref, vt_ref, g_ref, o_ref, cnt_sc, bias_sc,
                          n_heads=att_heads, dh=dh, n_q=n_q)
    _hgrn_step_kernel(hz_ref, s0_ref, lbl_ref, gn_ref, oh_ref, s_ref, n_heads=hg_heads, dk=dk, layer=layer)


def _step_mixers(q, k_new, v_new, cache_kt, cache_vt, g_att, hz, s0, lb_logits, g_hg,
                 *, att_heads, dh, layer):
    B, n_q, W = q.shape
    past = cache_kt.shape[2]
    _, hg_heads, dk, dv = s0.shape
    hw = hg_heads * dk
    bmap = lambda b: (b, 0, 0)
    smap = lambda b: (b, 0, 0, 0)
    return pl.pallas_call(
        functools.partial(_step_mixers_kernel, att_heads=att_heads, dh=dh, n_q=n_q, hg_heads=hg_heads, dk=dk,
                          layer=layer),
        grid=(B,),
        in_specs=[pl.BlockSpec((1, n_q, W), bmap)] * 3
        + [pl.BlockSpec((1, W, past), bmap), pl.BlockSpec((1, W, past), bmap), _const_spec((1, W)),
                  pl.BlockSpec((1, n_q, 4 * hw), bmap), pl.BlockSpec((1, hg_heads, dk, dv), smap),
                  _const_spec(lb_logits.shape), _const_spec((1, hw))],
        out_specs=[pl.BlockSpec((1, n_q, W), bmap), pl.BlockSpec((1, n_q, hw), bmap),
                   pl.BlockSpec((1, hg_heads, dk, dv), smap)],
        out_shape=[jax.ShapeDtypeStruct((B, n_q, W), F32), jax.ShapeDtypeStruct((B, n_q, hw), F32),
                   jax.ShapeDtypeStruct((B, hg_heads, dk, dv), F32)],
        scratch_shapes=[pltpu.VMEM((n_q * att_heads, past + LANES), F32)] * 2,
        compiler_params=_params(("arbitrary",)),
        name="step_mixers",
    )(q, k_new, v_new, cache_kt, cache_vt, g_att, hz, s0, lb_logits, g_hg)


def _mix_out(x, att_bf, ohg_bf, wo_ref):
    aw = att_bf.shape[-1]
    return x + _dot(att_bf, wo_ref[0:aw, :]) + _dot(ohg_bf, wo_ref[aw:, :])


def _cross_attend(cq, head_k, head_v, n_heads, dh):
    heads = range(n_heads)
    scale = dh ** -0.5 * LOG2E
    scores = [_dot_nt((cq[:, h * dh:(h + 1) * dh] * scale).astype(BF16), head_k(h)) for h in heads]
    probs = [jnp.exp2(s - jnp.max(s, axis=-1, keepdims=True)).astype(BF16) for s in scores]
    ones = jnp.ones((probs[0].shape[1], dh), BF16)
    pvs = [_dot(probs[h], jnp.concatenate([head_v(h), ones], axis=1)) for h in heads]
    return jnp.concatenate([pv[:, :dh] / pv[:, dh:] for pv in pvs], axis=-1)


def _ffn(u_bf, taps_fn, wg_ref, wu_ref, wd_ref, cw_ref, cb_ref):
    n_taps = cw_ref.shape[0]
    ug = _dot(u_bf, wg_ref[...])
    taps = taps_fn(ug)
    conv = cb_ref[...] + cw_ref[n_taps - 1:n_taps, :] * taps[n_taps - 1]
    for j in range(n_taps - 1):
        conv = conv + cw_ref[j:j + 1, :] * taps[j]
    act = conv * _sigmoid(conv) * _dot(u_bf, wu_ref[...])
    return _dot(act.astype(BF16), wd_ref[...])


def _post_kernel(x_ref, att_ref, ohg_ref, mk_ref, mv_ref, cinit_ref, gc_ref, gf_ref, gl_ref, wo_ref, wq_ref,
                 wc_ref, wg_ref, wu_ref, wd_ref, cw_ref, cb_ref, y_ref, cst_ref, buf_ref, carry_ref,
                 *, x_heads, x_dh):
    i = pl.program_id(1)
    tm = x_ref.shape[1]
    pad = SUBLANES
    n_taps = cw_ref.shape[0]

    @pl.when(i == 0)
    def _():
        carry_ref[...] = cinit_ref[0]

    x1 = _mix_out(x_ref[0], att_ref[0], ohg_ref[0], wo_ref)
    cq = _dot(_rms(x1, gc_ref[...]).astype(BF16), wq_ref[...])
    co = _cross_attend(cq, lambda h: mk_ref[0, :, h * x_dh:(h + 1) * x_dh],
                       lambda h: mv_ref[0, :, h * x_dh:(h + 1) * x_dh], x_heads, x_dh)
    x2 = x1 + _dot(co.astype(BF16), wc_ref[...])

    def taps_fn(ug):
        buf_ref[0:pad, :] = carry_ref[...]
        buf_ref[pad:pad + tm, :] = ug
        carry_ref[...] = ug[tm - pad:tm]
        return [buf_ref[pad - (n_taps - 1 - j):pad - (n_taps - 1 - j) + tm, :] for j in range(n_taps - 1)] + [ug]

    u_bf = _rms(x2, gf_ref[...]).astype(BF16)
    x3 = x2 + _ffn(u_bf, taps_fn, wg_ref, wu_ref, wd_ref, cw_ref, cb_ref)
    y_ref[0] = _rms(x3, gl_ref[...])
    cst_ref[0] = carry_ref[...]


def _post(x, att, ohg, mk_bf, mv_bf, cinit, gains, weights, conv_w, conv_b, *, tm, x_heads):
    B, T, D = x.shape
    aw, hw = att.shape[-1], ohg.shape[-1]
    n_mem, xw = mk_bf.shape[1:]
    dff = conv_w.shape[1]
    row = lambda b, i: (b, i, 0)
    bat = lambda b, i: (b, 0, 0)
    in_specs = ([pl.BlockSpec((1, tm, D), row), pl.BlockSpec((1, tm, aw), row), pl.BlockSpec((1, tm, hw), row)]
                + [pl.BlockSpec((1, n_mem, xw), bat)] * 2 + [pl.BlockSpec((1, SUBLANES, dff), bat)]
                + [_const_spec(a.shape) for a in tuple(gains) + tuple(weights) + (conv_w, conv_b)])
    return pl.pallas_call(
        functools.partial(_post_kernel, x_heads=x_heads, x_dh=xw // x_heads),
        grid=(B, T // tm),
        in_specs=in_specs,
        out_specs=[pl.BlockSpec((1, tm, D), row), pl.BlockSpec((1, SUBLANES, dff), bat)],
        out_shape=[jax.ShapeDtypeStruct((B, T, D), F32), jax.ShapeDtypeStruct((B, SUBLANES, dff), F32)],
        scratch_shapes=[pltpu.VMEM((tm + SUBLANES, dff), F32), pltpu.VMEM((SUBLANES, dff), F32)],
        compiler_params=_params(("parallel", "arbitrary")),
        name="post_mixer",
    )(x, att, ohg, mk_bf, mv_bf, cinit, *gains, *weights, conv_w, conv_b)


def _step_mix_kernel(x_ref, att_ref, ohg_ref, gc_ref, wo_ref, wq_ref, x1_ref, cq_ref):
    x1 = _mix_out(x_ref[...], att_ref[...].astype(BF16), ohg_ref[...].astype(BF16), wo_ref)
    x1_ref[...] = x1
    cq_ref[...] = _dot(_rms(x1, gc_ref[...]).astype(BF16), wq_ref[...])


def _step_cross_kernel(cq_ref, mk_ref, mv_ref, co_ref, *, n_q):
    n_b, n_mem, n_heads, dh = mk_ref.shape
    row = lax.broadcasted_iota(jnp.int32, (cq_ref.shape[1], n_mem * n_heads), 0)
    col = lax.broadcasted_iota(jnp.int32, (cq_ref.shape[1], n_mem * n_heads), 1)
    own_head = (col % n_heads) == (row // n_q)
    for b in range(n_b):
        mk = mk_ref[b].reshape(n_mem * n_heads, dh).astype(BF16)
        mv = mv_ref[b].reshape(n_mem * n_heads, dh).astype(BF16)
        s = jnp.where(own_head, _dot_nt((cq_ref[b] * dh ** -0.5).astype(BF16), mk), NEG)
        p = jnp.exp(s - jnp.max(s, axis=-1, keepdims=True))
        co_ref[b] = _dot(p.astype(BF16), mv) / jnp.sum(p, axis=-1, keepdims=True)


def _step_ffn_kernel(x1_ref, co_ref, prev_ref, gf_ref, gl_ref, wc_ref, wg_ref, wu_ref, wd_ref,
                     cw_ref, cb_ref, y_ref, ug_ref, buf_ref, *, n_t):
    n = x1_ref.shape[0]
    pad = SUBLANES
    n_taps = cw_ref.shape[0]
    x2 = x1_ref[...] + _dot(co_ref[...].astype(BF16), wc_ref[...])
    t_of_row = lax.broadcasted_iota(jnp.int32, (n, wg_ref.shape[1]), 0) % n_t

    def taps_fn(ug):
        ug_ref[...] = ug
        buf_ref[0:pad, :] = jnp.zeros((pad, ug.shape[1]), F32)
        buf_ref[pad:pad + n, :] = ug
        taps = []
        for j in range(n_taps - 1):
            shift = n_taps - 1 - j
            shifted = buf_ref[pad - shift:pad - shift + n, :]
            taps.append(jnp.where(t_of_row < shift, prev_ref[j], shifted))
        return taps + [ug]

    u_bf = _rms(x2, gf_ref[...]).astype(BF16)
    x3 = x2 + _ffn(u_bf, taps_fn, wg_ref, wu_ref, wd_ref, cw_ref, cb_ref)
    y_ref[...] = _rms(x3, gl_ref[...])


def _single_call(kernel, args, out_shape, name, scratch_shapes=()):
    return pl.pallas_call(
        kernel,
        grid=(1,),
        in_specs=[_const_spec(a.shape) for a in args],
        out_specs=[pl.BlockSpec(o.shape, lambda *_, nd=len(o.shape): (0,) * nd) for o in out_shape],
        out_shape=out_shape,
        scratch_shapes=list(scratch_shapes),
        compiler_params=_params(("arbitrary",)),
        name=name,
    )(*args)


def _step_cross(cq, mem_k, mem_v, *, n_q):
    B, n_rows, dh = cq.shape
    _, n_mem, x_heads, _ = mem_k.shape
    n_b = 4 if B % 4 == 0 else 1
    bmap = lambda b: (b, 0, 0)
    mem = pl.BlockSpec((n_b, n_mem, x_heads, dh), lambda b: (b, 0, 0, 0))
    return pl.pallas_call(
        functools.partial(_step_cross_kernel, n_q=n_q),
        grid=(B // n_b,),
        in_specs=[pl.BlockSpec((n_b, n_rows, dh), bmap), mem, mem],
        out_specs=pl.BlockSpec((n_b, n_rows, dh), bmap),
        out_shape=jax.ShapeDtypeStruct((B, n_rows, dh), F32),
        compiler_params=_params(("parallel",)),
        name="step_cross",
    )(cq, mem_k, mem_v)


def kernel(x_prompt, x_sample, cache_win_k, cache_win_v, state_hgrn, state_ffn_conv, cache_mem_k, cache_mem_v,
           mem_prompt, hg_lb_logits, norm_mix, w_in, att_out_norm, hg_out_norm, w_out, norm_cross, norm_mem,
           w_cq, w_ck, w_cv, w_co, norm_ffn, w_gate, w_up, conv_w, conv_b, w_down, norm_final):
    Bp, T, D = x_prompt.shape
    Bs, Ts, _ = x_sample.shape
    depth, _, past, att_h, att_dh = cache_win_k.shape
    _, _, hg_h, hg_dk, hg_dv = state_hgrn.shape
    _, _, n_mem, x_h, x_dh = cache_mem_k.shape
    n_taps, dff = conv_w.shape[1:]
    att_w = att_h * att_dh
    hg_w = hg_h * hg_dk
    xw = x_h * x_dh
    keep = min(max(w for w, _ in DIL_PATTERNS), T)
    assert depth == 1 and hg_dk == hg_dv == LANES and x_dh == LANES and LANES % att_dh == 0
    assert all(w // d == ATT_BLOCK for w, d in DIL_PATTERNS) and past >= max(w for w, _ in DIL_PATTERNS)
    d_max = max(d for _, d in DIL_PATTERNS)
    assert T % (ATT_BLOCK * d_max) == 0 and past % d_max == 0 and past % (4 * ATT_BLOCK) == 0
    assert n_taps - 1 <= min(Ts, SUBLANES) and Ts <= SUBLANES
    assert T % ROW_TILE == 0 and keep % ROW_TILE == 0 and (Bp * n_mem) % MEM_TILE == 0
    layer = 0
    q_scale = att_dh ** -0.5 * LOG2E
    row2 = lambda a: a.reshape(1, -1)

    w_in_bf = w_in[layer].astype(BF16)
    w_ckv_bf = jnp.concatenate([w_ck[layer], w_cv[layer]], axis=1).astype(BF16)
    weights = tuple(w[layer].astype(BF16) for w in (w_out, w_cq, w_co, w_gate, w_up, w_down))
    g_mix, g_att, g_hg = row2(norm_mix[layer]), row2(att_out_norm[layer]), row2(hg_out_norm[layer])
    g_cross, g_mem, g_ffn, g_final = (row2(norm_cross[layer]), row2(norm_mem[layer]), row2(norm_ffn[layer]),
                                      row2(norm_final))
    cw, cb = conv_w[layer], row2(conv_b[layer])

    q, k, v, k_keep, v_keep, ohg, s_prompt = _in_proj_hgrn(
        x_prompt, g_mix, w_in_bf, hg_lb_logits, g_hg, att_w=att_w, q_scale=q_scale, n_heads=hg_h, dk=hg_dk,
        tm=ROW_TILE, keep=keep, layer=layer)
    q, k, v = (a.reshape(Bp, T, att_w) for a in (q, k, v))
    ohg = ohg.reshape(Bp, T, hg_w)
    mk, mv, mk_bf, mv_bf = _mem_kv(mem_prompt.reshape(Bp * n_mem, D), g_mem, w_ckv_bf, tm=MEM_TILE)
    att = _dil_attn(q, k, v, g_att, n_heads=att_h, dh=att_dh)
    y_prompt, cst = _post(
        x_prompt, att, ohg, mk_bf.reshape(Bp, n_mem, xw), mv_bf.reshape(Bp, n_mem, xw),
        jnp.zeros((Bp, SUBLANES, dff), F32), (g_cross, g_ffn, g_final), weights, cw, cb,
        tm=ROW_TILE, x_heads=x_h)

    n_s = Bs * Ts
    qs, ks, vs, ks_f, vs_f, hzs = _in_proj(x_sample.reshape(1, n_s, D), g_mix, w_in_bf, att_w=att_w, q_scale=q_scale,
                                           tm=n_s, keep=n_s)
    per_seq = lambda a: a.reshape(Bs, Ts, att_w)
    feature_major = lambda c: jnp.transpose(c, (0, 2, 3, 1)).reshape(Bs, att_w, past)
    att_s, ohg_s, s_sample = _step_mixers(
        per_seq(qs), per_seq(ks), per_seq(vs), feature_major(cache_win_k[layer]),
        feature_major(cache_win_v[layer]), g_att, hzs.reshape(Bs, Ts, 4 * hg_w), state_hgrn[layer], hg_lb_logits,
        g_hg, att_heads=att_h, dh=att_dh, layer=layer)
    x1_s, cq_s = _single_call(
        _step_mix_kernel,
        (x_sample.reshape(n_s, D), att_s.reshape(n_s, att_w), ohg_s.reshape(n_s, hg_w), g_cross, weights[0],
         weights[1]),
        [jax.ShapeDtypeStruct((n_s, D), F32), jax.ShapeDtypeStruct((n_s, xw), F32)], "step_mix")
    cq_heads = cq_s.reshape(Bs, Ts, x_h, x_dh).transpose(0, 2, 1, 3).reshape(Bs, x_h * Ts, x_dh)
    co_s = _step_cross(cq_heads, cache_mem_k[layer], cache_mem_v[layer], n_q=Ts)
    co_s = co_s.reshape(Bs, x_h, Ts, x_dh).transpose(0, 2, 1, 3)
    conv_state = state_ffn_conv[layer]
    t_idx = jnp.arange(Ts)
    prev = jnp.stack([jnp.take(conv_state, jnp.clip(j + t_idx, 0, n_taps - 2), axis=1).reshape(n_s, dff)
                      for j in range(n_taps - 1)])
    y_s, ug_s = _single_call(
        functools.partial(_step_ffn_kernel, n_t=Ts),
        (x1_s, co_s.reshape(n_s, xw), prev, g_ffn, g_final, weights[2], weights[3], weights[4], weights[5], cw, cb),
        [jax.ShapeDtypeStruct((n_s, D), F32), jax.ShapeDtypeStruct((n_s, dff), F32)], "step_ffn",
        scratch_shapes=[pltpu.VMEM((n_s + SUBLANES, dff), F32)])

    stack = lambda a: a[None]
    return (y_prompt, y_s.reshape(Bs, Ts, D),
            stack(k_keep.reshape(Bp, keep, att_h, att_dh)), stack(v_keep.reshape(Bp, keep, att_h, att_dh)),
            stack(s_prompt), stack(cst[:, SUBLANES - (n_taps - 1):]),
            stack(mk.reshape(Bp, n_mem, x_h, x_dh)), stack(mv.reshape(Bp, n_mem, x_h, x_dh)),
            stack(ks_f.reshape(Bs, Ts, att_h, att_dh)), stack(vs_f.reshape(Bs, Ts, att_h, att_dh)),
            stack(s_sample), stack(ug_s.reshape(Bs, Ts, dff)[:, Ts - (n_taps - 1):]))
```

```python
import functools

import jax
import jax.numpy as jnp
from jax import lax
from jax.experimental import pallas as pl
from jax.experimental.pallas import tpu as pltpu

F32 = jnp.float32
BF16 = jnp.bfloat16
EPS = 1e-6
DIL_PATTERNS = ((128, 1), (512, 4), (2048, 16))
ATT_BLOCK = 128
LANES = 128
SUBLANES = 8
ROW_TILE = 512
MEM_TILE = 256
HG_CHUNK = 64
HG_SUB = 8
NEG = -1e30
LOG2E = 1.4426950408889634
HIGHEST = lax.Precision.HIGHEST
NT_DIMS = (((1,), (1,)), ((), ()))
TN_DIMS = (((0,), (0,)), ((), ()))
VMEM_LIMIT = 56 * 1024 * 1024


def _dot(a, b):
    return jnp.dot(a, b, preferred_element_type=F32)


def _dot_nt(a, b):
    return lax.dot_general(a, b, NT_DIMS, preferred_element_type=F32)


def _sigmoid(x):
    return 1.0 / (1.0 + jnp.exp(-x))


def _rms(x, g):
    return x * lax.rsqrt(jnp.mean(x * x, axis=-1, keepdims=True) + EPS) * g


def _const_spec(shape):
    nd = len(shape)
    return pl.BlockSpec(shape, lambda *_: (0,) * nd, pipeline_mode=pl.Buffered(1))


def _params(sem):
    return pltpu.CompilerParams(dimension_semantics=sem, vmem_limit_bytes=VMEM_LIMIT)


def _in_proj_kernel(x_ref, g_ref, w_ref, q_ref, k_ref, v_ref, kf_ref, vf_ref, hz_ref, *, att_w, q_scale):
    h = _rms(x_ref[0], g_ref[...]).astype(BF16)

    def proj(lo, hi):
        return _dot(h, w_ref[:, lo:hi])

    q_ref[0] = proj(0, att_w) * q_scale
    zk = proj(att_w, 2 * att_w)
    k_ref[0] = zk
    kf_ref[0] = zk
    zv = proj(2 * att_w, 3 * att_w)
    v_ref[0] = zv
    vf_ref[0] = zv
    base = 3 * att_w
    step = 4 * LANES
    for c in range(0, hz_ref.shape[-1], step):
        hz_ref[0, :, c:c + step] = proj(base + c, base + c + step)


def _in_proj(x, g, w_bf, *, att_w, q_scale, tm, keep):
    B, T, D = x.shape
    ncols = w_bf.shape[1]
    hzw = ncols - 3 * att_w
    n_t = T // tm
    first = n_t - keep // tm
    row = lambda b, i: (b, i, 0)
    keep_map = lambda b, i: (b, jnp.maximum(i - first, 0), 0)
    return pl.pallas_call(
        functools.partial(_in_proj_kernel, att_w=att_w, q_scale=q_scale),
        grid=(B, n_t),
        in_specs=[pl.BlockSpec((1, tm, D), row), _const_spec((1, D)), _const_spec((D, ncols))],
        out_specs=[pl.BlockSpec((1, tm, att_w), row)] * 3
        + [pl.BlockSpec((1, tm, att_w), keep_map)] * 2
        + [pl.BlockSpec((1, tm, hzw), row)],
        out_shape=[jax.ShapeDtypeStruct((B, T, att_w), F32)] * 3
        + [jax.ShapeDtypeStruct((B, keep, att_w), F32)] * 2
        + [jax.ShapeDtypeStruct((B, T, hzw), F32)],
        compiler_params=_params(("parallel", "arbitrary")),
        name="in_proj",
    )(x, g, w_bf)


def _mem_kv_kernel(m_ref, g_ref, w_ref, kf_ref, vf_ref, kb_ref, vb_ref):
    h = _rms(m_ref[...], g_ref[...]).astype(BF16)
    xw = w_ref.shape[1] // 2
    k = _dot(h, w_ref[:, :xw])
    v = _dot(h, w_ref[:, xw:])
    kf_ref[...] = k
    vf_ref[...] = v
    kb_ref[...] = k.astype(BF16)
    vb_ref[...] = v.astype(BF16)


def _mem_kv(mem2d, g, w_ckv_bf, *, tm):
    n, D = mem2d.shape
    xw = w_ckv_bf.shape[1] // 2
    row = lambda i: (i, 0)
    return pl.pallas_call(
        _mem_kv_kernel,
        grid=(n // tm,),
        in_specs=[pl.BlockSpec((tm, D), row), _const_spec((1, D)), _const_spec((D, 2 * xw))],
        out_specs=[pl.BlockSpec((tm, xw), row)] * 4,
        out_shape=[jax.ShapeDtypeStruct((n, xw), F32)] * 2 + [jax.ShapeDtypeStruct((n, xw), BF16)] * 2,
        compiler_params=_params(("parallel",)),
        name="mem_kv",
    )(mem2d, g, w_ckv_bf)


def _dil_attn_kernel(q_ref, k_ref, v_ref, g_ref, o_ref, acc_sc, m_sc, den_sc, bias_sc, *, n_heads, dh):
    tile = pl.program_id(1)
    T = q_ref.shape[1]
    blk = ATT_BLOCK
    hpt = LANES // dh
    rows, ncol = hpt * blk, 2 * blk
    rr = lax.broadcasted_iota(jnp.int32, (rows, ncol), 0)
    cc = lax.broadcasted_iota(jnp.int32, (rows, ncol), 1)
    delta = (rr % blk) + blk - cc
    in_band = (delta >= 0) & (delta <= blk)
    slope = jnp.exp2((-8.0 / n_heads) * (tile * hpt + rr // blk + 1).astype(F32))
    lane_head = lax.broadcasted_iota(jnp.int32, (blk, LANES), 1) // dh
    n_pat = len(DIL_PATTERNS)

    def per_lane(x):
        out = jnp.broadcast_to(x[0:blk], (blk, LANES))
        for e in range(1, hpt):
            out = jnp.where(lane_head == e, jnp.broadcast_to(x[e * blk:(e + 1) * blk], (blk, LANES)), out)
        return out

    ones = jnp.ones((ncol, LANES), BF16)
    order = sorted(range(n_pat), key=lambda p: -DIL_PATTERNS[p][1])
    for step, p in enumerate(order):
        dil = DIL_PATTERNS[p][1]
        alibi = (-LOG2E) * slope * (delta * dil).astype(F32)
        bias_sc[0] = jnp.where(in_band & (cc >= blk), alibi, NEG)
        bias_sc[1] = jnp.where(in_band, alibi, NEG)
        nblk = T // (blk * dil)

        def body(n, carry, first=step == 0, last=step == n_pat - 1, dil=dil, nblk=nblk):
            kp, vp = carry
            i = n % nblk
            start = n // nblk + i * (blk * dil)
            idx = pl.ds(start, blk, stride=dil) if dil > 1 else pl.ds(start, blk)
            q2 = q_ref[0, idx, :]
            kc = k_ref[0, idx, :].astype(BF16)
            vc = v_ref[0, idx, :].astype(BF16)
            qs = jnp.concatenate([jnp.where(lane_head == e, q2, 0.0) for e in range(hpt)], axis=0).astype(BF16)
            s = _dot_nt(qs, jnp.concatenate([kp, kc], axis=0)) + bias_sc[min(i, 1)]
            m = jnp.max(s, axis=-1, keepdims=True)
            pr = jnp.exp2(s - m).astype(BF16)
            pv = _dot(pr, jnp.concatenate([jnp.concatenate([vp, vc], axis=0), ones], axis=1))
            m_t, acc_t, den_t = per_lane(m), per_lane(pv[:, :LANES]), per_lane(pv[:, LANES:])
            if not first:
                m_old = m_sc[idx, :]
                m_new = jnp.maximum(m_old, m_t)
                a_old, a_t = jnp.exp2(m_old - m_new), jnp.exp2(m_t - m_new)
                den_t = a_old * den_sc[idx, :] + a_t * den_t
                acc_t = a_old * acc_sc[idx, :] + a_t * acc_t
                m_t = m_new
            if last:
                acc_sc[idx, :] = acc_t / den_t
            else:
                m_sc[idx, :] = m_t
                den_sc[idx, :] = den_t
                acc_sc[idx, :] = acc_t
            return kc, vc

        carry = (jnp.zeros((blk, LANES), BF16),) * 2
        for n in range(T // blk):
            carry = body(n, carry)

    ch = 4 * blk
    same_head = (lax.broadcasted_iota(jnp.int32, (LANES, LANES), 0) // dh
                 == lax.broadcasted_iota(jnp.int32, (LANES, LANES), 1) // dh).astype(BF16)
    for c in range(T // ch):
        a = acc_sc[c * ch:(c + 1) * ch, :]
        mean_sq = _dot((a * a).astype(BF16), same_head) * (1.0 / dh)
        o_ref[0, c * ch:(c + 1) * ch, :] = (a * lax.rsqrt(mean_sq + EPS) * g_ref[...]).astype(BF16)


def _dil_attn(q, k, v, g, *, n_heads, dh):
    B, T, W = q.shape
    blk = ATT_BLOCK
    seq = pl.BlockSpec((1, T, LANES), lambda b, t: (b, 0, t))
    return pl.pallas_call(
        functools.partial(_dil_attn_kernel, n_heads=n_heads, dh=dh),
        grid=(B, W // LANES),
        in_specs=[seq, seq, seq, pl.BlockSpec((1, LANES), lambda b, t: (0, t))],
        out_specs=seq,
        out_shape=jax.ShapeDtypeStruct((B, T, W), BF16),
        scratch_shapes=[pltpu.VMEM((T, LANES), F32)] * 3
        + [pltpu.VMEM((2, (LANES // dh) * blk, 2 * blk), F32)],
        compiler_params=_params(("parallel", "arbitrary")),
        name="dil_attn",
    )(q, k, v, g)


def _dil_attn_step_kernel(q_ref, kn_ref, vn_ref, kt_ref, vt_ref, g_ref, o_ref, cnt_sc, bias_sc,
                          *, n_heads, dh, n_q):
    past = kt_ref.shape[2]
    n_new = LANES
    W = q_ref.shape[-1]
    rows = n_q * n_heads

    def new_rows(ref):
        sub8 = lax.broadcasted_iota(jnp.int32, (SUBLANES, W), 0)
        tile = jnp.zeros((SUBLANES, W), F32)
        for i in range(n_q):
            tile = jnp.where(sub8 == i, jnp.broadcast_to(ref[0, i:i + 1, :], (SUBLANES, W)), tile)
        return jnp.concatenate([tile, jnp.zeros((n_new - SUBLANES, W), F32)], axis=0).astype(BF16)
    n_cols = past + n_new

    @pl.when(pl.program_id(0) == 0)
    def _():
        r = lax.broadcasted_iota(jnp.int32, (rows, n_cols), 0)
        c = lax.broadcasted_iota(jnp.int32, (rows, n_cols), 1)
        delta = past + r // n_heads - c
        cnt = jnp.zeros((rows, n_cols), F32)
        for win, dil in DIL_PATTERNS:
            cnt = cnt + ((delta >= 0) & (delta <= win) & ((delta & (dil - 1)) == 0)).astype(F32)
        slope = jnp.exp2((-8.0 / n_heads) * ((r % n_heads) + 1).astype(F32))
        cnt_sc[...] = cnt
        bias_sc[...] = jnp.where(cnt > 0.0, (-LOG2E) * slope * delta.astype(F32), NEG)

    sub = lax.broadcasted_iota(jnp.int32, (n_heads, W), 0)
    lane_head = lax.broadcasted_iota(jnp.int32, (n_heads, W), 1) // dh
    qbd = jnp.concatenate(
        [jnp.where(sub == lane_head, jnp.broadcast_to(q_ref[0, i:i + 1, :], (n_heads, W)), 0.0) for i in range(n_q)],
        axis=0).astype(BF16)
    s = jnp.concatenate([_dot(qbd, kt_ref[0].astype(BF16)), _dot_nt(qbd, new_rows(kn_ref))], axis=1) + bias_sc[...]
    p = jnp.exp2(s - jnp.max(s, axis=-1, keepdims=True)) * cnt_sc[...]
    den = jnp.sum(p, axis=-1, keepdims=True)
    p = p.astype(BF16)
    out = (_dot_nt(p[:, :past], vt_ref[0].astype(BF16)) + _dot(p[:, past:], new_rows(vn_ref))) / den
    r = lax.broadcasted_iota(jnp.int32, (rows, W), 0)
    c = lax.broadcasted_iota(jnp.int32, (rows, W), 1)
    out = jnp.where((r % n_heads) == (c // dh), out, 0.0)
    out = out * lax.rsqrt(jnp.sum(out * out, axis=-1, keepdims=True) * (1.0 / dh) + EPS)
    o_ref[0] = jnp.sum(out.reshape(n_q, n_heads, W), axis=1) * g_ref[...]


def _lower_bound(lbl_ref, layer):
    logits = lbl_ref[...]
    e = jnp.exp(logits - jnp.max(logits, axis=0, keepdims=True))
    return jnp.sum(e[:layer + 1], axis=0, keepdims=True) / jnp.sum(e, axis=0, keepdims=True)


def _cumsum_rows(tril_bf, x):
    x1 = x.astype(BF16)
    r1 = x - x1.astype(F32)
    x2 = r1.astype(BF16)
    x3 = (r1 - x2.astype(F32)).astype(BF16)
    return _dot(tril_bf, x1) + _dot(tril_bf, x2) + _dot(tril_bf, x3)


def _hgrn_rows(hz_ref, o_ref, lb, gn_ref, st_ref, b_sc, c_sc, *, n_heads, dk):
    hw = n_heads * dk
    tc = hz_ref.shape[0]
    ch, sub = HG_CHUNK, HG_SUB
    r_c = lax.broadcasted_iota(jnp.int32, (ch, ch), 0)
    c_c = lax.broadcasted_iota(jnp.int32, (ch, ch), 1)
    tril = (r_c >= c_c).astype(BF16)
    r_s = lax.broadcasted_iota(jnp.int32, (sub, ch), 0)
    c_s = lax.broadcasted_iota(jnp.int32, (sub, ch), 1)

    local = [[None] * n_heads for _ in range(tc // ch)]
    for c in range(tc // ch):
        rows = slice(c * ch, (c + 1) * ch)
        hq = hz_ref[rows, 0:hw]
        f = lb + (1.0 - lb) * _sigmoid(hz_ref[rows, hw:2 * hw])
        qh = hq * _sigmoid(hq) * dk ** -0.5
        kk = 1.0 - f
        b = _cumsum_rows(tril, jnp.log(f) * LOG2E)
        b_rows, c_rows = b_sc.at[c], c_sc.at[c]
        b_rows[...] = b
        c_rows[...] = b - jnp.log(kk) * LOG2E
        for h in range(n_heads):
            hs = slice(h * dk, (h + 1) * dk)
            bh = b[:, hs]
            kh = kk[:, hs]
            qhh = qh[:, hs]
            vh = hz_ref[rows, 2 * hw + h * dk:2 * hw + (h + 1) * dk].astype(BF16)
            blocks = []
            for j in range(ch // sub):
                r0 = j * sub
                bj = bh[r0:r0 + sub]
                qj = qhh[r0:r0 + sub]
                a_j = jnp.zeros((sub, ch), F32)
                for s in range(sub):
                    a = jnp.sum(qj * jnp.exp2(bj - c_rows[r0 + s:r0 + s + 1, hs]), axis=-1, keepdims=True)
                    a_j = jnp.where(c_s == r0 + s, a, a_j)
                if j > 0:
                    beta = b_rows[r0 - 1:r0, hs]
                    qt = qj * jnp.exp2(bj - beta)
                    kt = kh[0:r0] * jnp.exp2(beta - bh[0:r0])
                    kt = jnp.concatenate([kt, jnp.zeros((ch - r0, dk), F32)], axis=0)
                    a_j = a_j + _dot_nt(qt.astype(BF16), kt.astype(BF16))
                blocks.append(jnp.where(c_s - r0 <= r_s, a_j, 0.0))
            b_last = b_rows[ch - 1:ch, hs]
            hg = hz_ref[rows, 3 * hw + h * dk:3 * hw + (h + 1) * dk]
            local[c][h] = dict(
                q=(qhh * jnp.exp2(bh)).astype(BF16), a=jnp.concatenate(blocks, axis=0).astype(BF16), v=vh,
                k=(kh * jnp.exp2(b_last - bh)).astype(BF16), decay=jnp.exp2(b_last), gate=hg * _sigmoid(hg))

    state = [st_ref[h] for h in range(n_heads)]
    for c in range(tc // ch):
        rows = slice(c * ch, (c + 1) * ch)
        for h in range(n_heads):
            hs = slice(h * dk, (h + 1) * dk)
            t = local[c][h]
            o = _dot_nt(t["q"], state[h].astype(BF16)) + _dot(t["a"], t["v"])
            state[h] = state[h] * t["decay"] + lax.dot_general(t["v"], t["k"], TN_DIMS, preferred_element_type=F32)
            o_ref[rows, hs] = (_rms(o, gn_ref[:, hs]) * t["gate"]).astype(BF16)
    for h in range(n_heads):
        st_ref[h] = state[h]


def _in_proj_hgrn_kernel(x_ref, g_ref, w_ref, lbl_ref, gn_ref, q_ref, k_ref, v_ref, kf_ref, vf_ref, o_ref, s_ref,
                         hz_sc, st_ref, b_sc, c_sc, *, att_w, q_scale, n_heads, dk, layer, tiles_per_seq):
    g = pl.program_id(0)
    t_rec = jnp.maximum(g - 1, 0)

    @pl.when(g == 0)
    def _():
        hz_sc[...] = jnp.zeros_like(hz_sc)

    @pl.when(t_rec % tiles_per_seq == 0)
    def _():
        st_ref[...] = jnp.zeros_like(st_ref)

    lb = _lower_bound(lbl_ref, layer)
    slot_w = g % 2
    hz = hz_sc.at[1 - slot_w]

    h = _rms(x_ref[...], g_ref[...]).astype(BF16)

    def proj(lo, hi):
        return _dot(h, w_ref[:, lo:hi])

    q_ref[...] = proj(0, att_w) * q_scale
    zk = proj(att_w, 2 * att_w)
    k_ref[...] = zk
    kf_ref[0] = zk
    zv = proj(2 * att_w, 3 * att_w)
    v_ref[...] = zv
    vf_ref[0] = zv
    base = 3 * att_w
    step = 4 * LANES
    for c in range(0, hz_sc.shape[-1], step):
        hz_sc[slot_w, :, c:c + step] = proj(base + c, base + c + step)

    _hgrn_rows(hz, o_ref, lb, gn_ref, st_ref, b_sc, c_sc, n_heads=n_heads, dk=dk)

    @pl.when((g > 0) & (t_rec % tiles_per_seq == tiles_per_seq - 1))
    def _():
        for h in range(n_heads):
            s_ref[0, h] = st_ref[h].T


def _in_proj_hgrn(x, g, w_bf, lb_logits, gn, *, att_w, q_scale, n_heads, dk, tm, keep, layer):
    B, T, D = x.shape
    hw = n_heads * dk
    n_t = T // tm
    n_tiles = B * n_t
    n_keep = keep // tm
    first = n_t - n_keep
    proj_tile = lambda s: jnp.minimum(s, n_tiles - 1)
    rec_tile = lambda s: jnp.maximum(s - 1, 0)
    rows = lambda s: (proj_tile(s), 0)
    keep_rows = lambda s: (proj_tile(s) // n_t, jnp.maximum(proj_tile(s) % n_t - first, 0), 0)
    f32 = lambda n, w: jax.ShapeDtypeStruct((n, w), F32)
    return pl.pallas_call(
        functools.partial(_in_proj_hgrn_kernel, att_w=att_w, q_scale=q_scale, n_heads=n_heads, dk=dk, layer=layer,
                          tiles_per_seq=n_t),
        grid=(n_tiles + 1,),
        in_specs=[pl.BlockSpec((tm, D), rows), _const_spec((1, D)), _const_spec(w_bf.shape),
                  _const_spec(lb_logits.shape), _const_spec((1, hw))],
        out_specs=[pl.BlockSpec((tm, att_w), rows)] * 3 + [pl.BlockSpec((1, tm, att_w), keep_rows)] * 2
        + [pl.BlockSpec((tm, hw), lambda s: (rec_tile(s), 0)),
           pl.BlockSpec((1, n_heads, dk, dk), lambda s: (rec_tile(s) // n_t, 0, 0, 0))],
        out_shape=[f32(B * T, att_w)] * 3 + [jax.ShapeDtypeStruct((B, keep, att_w), F32)] * 2
        + [jax.ShapeDtypeStruct((B * T, hw), BF16), jax.ShapeDtypeStruct((B, n_heads, dk, dk), F32)],
        scratch_shapes=[pltpu.VMEM((2, tm, 4 * hw), F32), pltpu.VMEM((n_heads, dk, dk), F32)]
        + [pltpu.VMEM((tm // HG_CHUNK, HG_CHUNK, hw), F32)] * 2,
        compiler_params=_params(("arbitrary",)),
        name="in_proj_hgrn",
    )(x.reshape(B * T, D), g, w_bf, lb_logits, gn)


def _hgrn_step_kernel(hz_ref, s0_ref, lbl_ref, gn_ref, o_ref, s_ref, *, n_heads, dk, layer):
    hw = n_heads * dk
    n_t = hz_ref.shape[1]
    lb = _lower_bound(lbl_ref, layer)
    q, k, v, b, gate = [], [], [], [], []
    acc = jnp.zeros((1, hw), F32)
    for t in range(n_t):
        hq = hz_ref[0, t:t + 1, 0:hw]
        f = lb + (1.0 - lb) * _sigmoid(hz_ref[0, t:t + 1, hw:2 * hw])
        acc = acc + jnp.log(f)
        q.append(hq * _sigmoid(hq) * dk ** -0.5)
        k.append(1.0 - f)
        v.append(hz_ref[0, t:t + 1, 2 * hw:3 * hw])
        b.append(acc)
        hg = hz_ref[0, t:t + 1, 3 * hw:4 * hw]
        gate.append(hg * _sigmoid(hg))
    rr = lax.broadcasted_iota(jnp.int32, (dk, dk), 0)
    cc = lax.broadcasted_iota(jnp.int32, (dk, dk), 1)

    def tile_of_rows(rows):
        tile = jnp.zeros((dk, dk), F32)
        for t, r in enumerate(rows):
            tile = jnp.where(rr == t, jnp.broadcast_to(r, (dk, dk)), tile)
        return tile

    for h in range(n_heads):
        hs = slice(h * dk, (h + 1) * dk)
        s0 = s0_ref[0, h]
        qe = tile_of_rows([q[t][:, hs] * jnp.exp(b[t][:, hs]) for t in range(n_t)])
        inter = jnp.dot(qe, s0, precision=HIGHEST, preferred_element_type=F32)
        for t in range(n_t):
            o = inter[t:t + 1]
            for s in range(t + 1):
                a = jnp.sum(q[t][:, hs] * jnp.exp(b[t][:, hs] - b[s][:, hs]) * k[s][:, hs], axis=-1, keepdims=True)
                o = o + a * v[s][:, hs]
            o_ref[0, t:t + 1, hs] = _rms(o, gn_ref[:, hs]) * gate[t][:, hs]
        b_last = b[n_t - 1][:, hs]
        khat = tile_of_rows([k[t][:, hs] * jnp.exp(b_last - b[t][:, hs]) for t in range(n_t)])
        vpad = tile_of_rows([v[t][:, hs] for t in range(n_t)])
        decay = jnp.where(rr == cc, jnp.broadcast_to(jnp.exp(b_last), (dk, dk)), 0.0)
        s_ref[0, h] = (jnp.dot(decay, s0, precision=HIGHEST, preferred_element_type=F32)
                       + jnp.dot(khat.T, vpad, precision=HIGHEST, preferred_element_type=F32))


def _step_mixers_kernel(q_ref, kn_ref, vn_ref, kt_ref, vt_ref, g_ref, hz_ref, s0_ref, lbl_ref, gn_ref,
                        o_ref, oh_ref, s_ref, cnt_sc, bias_sc, *, att_heads, dh, n_q, hg_heads, dk, layer):
    for b in range(q_ref.shape[0]):
        one = lambda ref: ref.at[pl.ds(b, 1)]
        _dil_attn_step_kernel(one(q_ref), one(kn_ref), one(vn_ref), one(kt_ref), one(vt_ref), g_ref, one(o_ref),
                              cnt_sc, bias_sc, n_heads=att_heads, dh=dh, n_q=n_q)
        _hgrn_step_kernel(one(hz_ref), one(s0_ref), lbl_ref, gn_ref, one(oh_ref), one(s_ref), n_heads=hg_heads,
                          dk=dk, layer=layer)


def _step_mixers(q, k_new, v_new, cache_kt, cache_vt, g_att, hz, s0, lb_logits, g_hg,
                 *, att_heads, dh, layer):
    B, n_q, W = q.shape
    past = cache_kt.shape[2]
    _, hg_heads, dk, dv = s0.shape
    hw = hg_heads * dk
    n_b = 2 if B % 2 == 0 else 1
    bmap = lambda b: (b, 0, 0)
    smap = lambda b: (b, 0, 0, 0)
    return pl.pallas_call(
        functools.partial(_step_mixers_kernel, att_heads=att_heads, dh=dh, n_q=n_q, hg_heads=hg_heads, dk=dk,
                          layer=layer),
        grid=(B // n_b,),
        in_specs=[pl.BlockSpec((n_b, n_q, W), bmap)] * 3
        + [pl.BlockSpec((n_b, W, past), bmap), pl.BlockSpec((n_b, W, past), bmap), _const_spec((1, W)),
           pl.BlockSpec((n_b, n_q, 4 * hw), bmap), pl.BlockSpec((n_b, hg_heads, dk, dv), smap),
           _const_spec(lb_logits.shape), _const_spec((1, hw))],
        out_specs=[pl.BlockSpec((n_b, n_q, W), bmap), pl.BlockSpec((n_b, n_q, hw), bmap),
                   pl.BlockSpec((n_b, hg_heads, dk, dv), smap)],
        out_shape=[jax.ShapeDtypeStruct((B, n_q, W), F32), jax.ShapeDtypeStruct((B, n_q, hw), F32),
                   jax.ShapeDtypeStruct((B, hg_heads, dk, dv), F32)],
        scratch_shapes=[pltpu.VMEM((n_q * att_heads, past + LANES), F32)] * 2,
        compiler_params=_params(("arbitrary",)),
        name="step_mixers",
    )(q, k_new, v_new, cache_kt, cache_vt, g_att, hz, s0, lb_logits, g_hg)


def _mix_out(x, att_bf, ohg_bf, wo_ref):
    aw = att_bf.shape[-1]
    return x + _dot(att_bf, wo_ref[0:aw, :]) + _dot(ohg_bf, wo_ref[aw:, :])


def _cross_attend(cq, head_k, head_v, n_heads, dh):
    heads = range(n_heads)
    scale = dh ** -0.5 * LOG2E
    scores = [_dot_nt((cq[:, h * dh:(h + 1) * dh] * scale).astype(BF16), head_k(h)) for h in heads]
    probs = [jnp.exp2(s - jnp.max(s, axis=-1, keepdims=True)).astype(BF16) for s in scores]
    ones = jnp.ones((probs[0].shape[1], dh), BF16)
    pvs = [_dot(probs[h], jnp.concatenate([head_v(h), ones], axis=1)) for h in heads]
    return jnp.concatenate([pv[:, :dh] / pv[:, dh:] for pv in pvs], axis=-1)


def _ffn(u_bf, taps_fn, wg_ref, wu_ref, wd_ref, cw_ref, cb_ref):
    n_taps = cw_ref.shape[0]
    ug = _dot(u_bf, wg_ref[...])
    taps = taps_fn(ug)
    conv = cb_ref[...] + cw_ref[n_taps - 1:n_taps, :] * taps[n_taps - 1]
    for j in range(n_taps - 1):
        conv = conv + cw_ref[j:j + 1, :] * taps[j]
    act = conv * _sigmoid(conv) * _dot(u_bf, wu_ref[...])
    return _dot(act.astype(BF16), wd_ref[...])


def _post_kernel(x_ref, att_ref, ohg_ref, mk_ref, mv_ref, cinit_ref, gc_ref, gf_ref, gl_ref, wo_ref, wq_ref,
                 wc_ref, wg_ref, wu_ref, wd_ref, cw_ref, cb_ref, y_ref, cst_ref, buf_ref, carry_ref,
                 *, x_heads, x_dh):
    i = pl.program_id(1)
    tm = x_ref.shape[1]
    pad = SUBLANES
    n_taps = cw_ref.shape[0]

    @pl.when(i == 0)
    def _():
        carry_ref[...] = cinit_ref[0]

    x1 = _mix_out(x_ref[0], att_ref[0], ohg_ref[0], wo_ref)
    cq = _dot(_rms(x1, gc_ref[...]).astype(BF16), wq_ref[...])
    co = _cross_attend(cq, lambda h: mk_ref[0, :, h * x_dh:(h + 1) * x_dh],
                       lambda h: mv_ref[0, :, h * x_dh:(h + 1) * x_dh], x_heads, x_dh)
    x2 = x1 + _dot(co.astype(BF16), wc_ref[...])

    def taps_fn(ug):
        buf_ref[0:pad, :] = carry_ref[...]
        buf_ref[pad:pad + tm, :] = ug
        carry_ref[...] = ug[tm - pad:tm]
        return [buf_ref[pad - (n_taps - 1 - j):pad - (n_taps - 1 - j) + tm, :] for j in range(n_taps - 1)] + [ug]

    u_bf = _rms(x2, gf_ref[...]).astype(BF16)
    x3 = x2 + _ffn(u_bf, taps_fn, wg_ref, wu_ref, wd_ref, cw_ref, cb_ref)
    y_ref[0] = _rms(x3, gl_ref[...])
    cst_ref[0] = carry_ref[...]


def _post(x, att, ohg, mk_bf, mv_bf, cinit, gains, weights, conv_w, conv_b, *, tm, x_heads):
    B, T, D = x.shape
    aw, hw = att.shape[-1], ohg.shape[-1]
    n_mem, xw = mk_bf.shape[1:]
    dff = conv_w.shape[1]
    row = lambda b, i: (b, i, 0)
    bat = lambda b, i: (b, 0, 0)
    in_specs = ([pl.BlockSpec((1, tm, D), row), pl.BlockSpec((1, tm, aw), row), pl.BlockSpec((1, tm, hw), row)]
                + [pl.BlockSpec((1, n_mem, xw), bat)] * 2 + [pl.BlockSpec((1, SUBLANES, dff), bat)]
                + [_const_spec(a.shape) for a in tuple(gains) + tuple(weights) + (conv_w, conv_b)])
    return pl.pallas_call(
        functools.partial(_post_kernel, x_heads=x_heads, x_dh=xw // x_heads),
        grid=(B, T // tm),
        in_specs=in_specs,
        out_specs=[pl.BlockSpec((1, tm, D), row), pl.BlockSpec((1, SUBLANES, dff), bat)],
        out_shape=[jax.ShapeDtypeStruct((B, T, D), F32), jax.ShapeDtypeStruct((B, SUBLANES, dff), F32)],
        scratch_shapes=[pltpu.VMEM((tm + SUBLANES, dff), F32), pltpu.VMEM((SUBLANES, dff), F32)],
        compiler_params=_params(("parallel", "arbitrary")),
        name="post_mixer",
    )(x, att, ohg, mk_bf, mv_bf, cinit, *gains, *weights, conv_w, conv_b)


def _step_mix_kernel(x_ref, att_ref, ohg_ref, gc_ref, wo_ref, wq_ref, x1_ref, cq_ref):
    x1 = _mix_out(x_ref[...], att_ref[...].astype(BF16), ohg_ref[...].astype(BF16), wo_ref)
    x1_ref[...] = x1
    cq_ref[...] = _dot(_rms(x1, gc_ref[...]).astype(BF16), wq_ref[...])


def _step_cross_kernel(cq_ref, mk_ref, mv_ref, co_ref, *, n_q):
    n_b, n_mem, n_heads, dh = mk_ref.shape
    row = lax.broadcasted_iota(jnp.int32, (cq_ref.shape[1], n_mem * n_heads), 0)
    col = lax.broadcasted_iota(jnp.int32, (cq_ref.shape[1], n_mem * n_heads), 1)
    own_head = (col % n_heads) == (row // n_q)
    for b in range(n_b):
        mk = mk_ref[b].reshape(n_mem * n_heads, dh).astype(BF16)
        mv = mv_ref[b].reshape(n_mem * n_heads, dh).astype(BF16)
        s = jnp.where(own_head, _dot_nt((cq_ref[b] * dh ** -0.5).astype(BF16), mk), NEG)
        p = jnp.exp(s - jnp.max(s, axis=-1, keepdims=True))
        co_ref[b] = _dot(p.astype(BF16), mv) / jnp.sum(p, axis=-1, keepdims=True)


def _step_ffn_kernel(x1_ref, co_ref, prev_ref, gf_ref, gl_ref, wc_ref, wg_ref, wu_ref, wd_ref,
                     cw_ref, cb_ref, y_ref, ug_ref, buf_ref, *, n_t):
    n = x1_ref.shape[0]
    pad = SUBLANES
    n_taps = cw_ref.shape[0]
    x2 = x1_ref[...] + _dot(co_ref[...].astype(BF16), wc_ref[...])
    t_of_row = lax.broadcasted_iota(jnp.int32, (n, wg_ref.shape[1]), 0) % n_t

    def taps_fn(ug):
        ug_ref[...] = ug
        buf_ref[0:pad, :] = jnp.zeros((pad, ug.shape[1]), F32)
        buf_ref[pad:pad + n, :] = ug
        taps = []
        for j in range(n_taps - 1):
            shift = n_taps - 1 - j
            shifted = buf_ref[pad - shift:pad - shift + n, :]
            taps.append(jnp.where(t_of_row < shift, prev_ref[j], shifted))
        return taps + [ug]

    u_bf = _rms(x2, gf_ref[...]).astype(BF16)
    x3 = x2 + _ffn(u_bf, taps_fn, wg_ref, wu_ref, wd_ref, cw_ref, cb_ref)
    y_ref[...] = _rms(x3, gl_ref[...])


def _single_call(kernel, args, out_shape, name, scratch_shapes=()):
    return pl.pallas_call(
        kernel,
        grid=(1,),
        in_specs=[_const_spec(a.shape) for a in args],
        out_specs=[pl.BlockSpec(o.shape, lambda *_, nd=len(o.shape): (0,) * nd) for o in out_shape],
        out_shape=out_shape,
        scratch_shapes=list(scratch_shapes),
        compiler_params=_params(("arbitrary",)),
        name=name,
    )(*args)


def _step_cross(cq, mem_k, mem_v, *, n_q):
    B, n_rows, dh = cq.shape
    _, n_mem, x_heads, _ = mem_k.shape
    n_b = 4 if B % 4 == 0 else 1
    bmap = lambda b: (b, 0, 0)
    mem = pl.BlockSpec((n_b, n_mem, x_heads, dh), lambda b: (b, 0, 0, 0))
    return pl.pallas_call(
        functools.partial(_step_cross_kernel, n_q=n_q),
        grid=(B // n_b,),
        in_specs=[pl.BlockSpec((n_b, n_rows, dh), bmap), mem, mem],
        out_specs=pl.BlockSpec((n_b, n_rows, dh), bmap),
        out_shape=jax.ShapeDtypeStruct((B, n_rows, dh), F32),
        compiler_params=_params(("parallel",)),
        name="step_cross",
    )(cq, mem_k, mem_v)


def kernel(x_prompt, x_sample, cache_win_k, cache_win_v, state_hgrn, state_ffn_conv, cache_mem_k, cache_mem_v,
           mem_prompt, hg_lb_logits, norm_mix, w_in, att_out_norm, hg_out_norm, w_out, norm_cross, norm_mem,
           w_cq, w_ck, w_cv, w_co, norm_ffn, w_gate, w_up, conv_w, conv_b, w_down, norm_final):
    Bp, T, D = x_prompt.shape
    Bs, Ts, _ = x_sample.shape
    depth, _, past, att_h, att_dh = cache_win_k.shape
    _, _, hg_h, hg_dk, hg_dv = state_hgrn.shape
    _, _, n_mem, x_h, x_dh = cache_mem_k.shape
    n_taps, dff = conv_w.shape[1:]
    att_w = att_h * att_dh
    hg_w = hg_h * hg_dk
    xw = x_h * x_dh
    keep = min(max(w for w, _ in DIL_PATTERNS), T)
    assert depth == 1 and hg_dk == hg_dv == LANES and x_dh == LANES and LANES % att_dh == 0
    assert all(w // d == ATT_BLOCK for w, d in DIL_PATTERNS) and past >= max(w for w, _ in DIL_PATTERNS)
    d_max = max(d for _, d in DIL_PATTERNS)
    assert T % (ATT_BLOCK * d_max) == 0 and past % d_max == 0 and past % (4 * ATT_BLOCK) == 0
    assert n_taps - 1 <= min(Ts, SUBLANES) and Ts <= SUBLANES
    assert T % ROW_TILE == 0 and keep % ROW_TILE == 0 and (Bp * n_mem) % MEM_TILE == 0
    layer = 0
    q_scale = att_dh ** -0.5 * LOG2E
    row2 = lambda a: a.reshape(1, -1)

    w_in_bf = w_in[layer].astype(BF16)
    w_ckv_bf = jnp.concatenate([w_ck[layer], w_cv[layer]], axis=1).astype(BF16)
    weights = tuple(w[layer].astype(BF16) for w in (w_out, w_cq, w_co, w_gate, w_up, w_down))
    g_mix, g_att, g_hg = row2(norm_mix[layer]), row2(att_out_norm[layer]), row2(hg_out_norm[layer])
    g_cross, g_mem, g_ffn, g_final = (row2(norm_cross[layer]), row2(norm_mem[layer]), row2(norm_ffn[layer]),
                                      row2(norm_final))
    cw, cb = conv_w[layer], row2(conv_b[layer])

    q, k, v, k_keep, v_keep, ohg, s_prompt = _in_proj_hgrn(
        x_prompt, g_mix, w_in_bf, hg_lb_logits, g_hg, att_w=att_w, q_scale=q_scale, n_heads=hg_h, dk=hg_dk,
        tm=ROW_TILE, keep=keep, layer=layer)
    q, k, v = (a.reshape(Bp, T, att_w) for a in (q, k, v))
    ohg = ohg.reshape(Bp, T, hg_w)
    mk, mv, mk_bf, mv_bf = _mem_kv(mem_prompt.reshape(Bp * n_mem, D), g_mem, w_ckv_bf, tm=MEM_TILE)
    att = _dil_attn(q, k, v, g_att, n_heads=att_h, dh=att_dh)
    y_prompt, cst = _post(
        x_prompt, att, ohg, mk_bf.reshape(Bp, n_mem, xw), mv_bf.reshape(Bp, n_mem, xw),
        jnp.zeros((Bp, SUBLANES, dff), F32), (g_cross, g_ffn, g_final), weights, cw, cb,
        tm=ROW_TILE, x_heads=x_h)

    n_s = Bs * Ts
    qs, ks, vs, ks_f, vs_f, hzs = _in_proj(x_sample.reshape(1, n_s, D), g_mix, w_in_bf, att_w=att_w, q_scale=q_scale,
                                           tm=n_s, keep=n_s)
    per_seq = lambda a: a.reshape(Bs, Ts, att_w)
    feature_major = lambda c: jnp.transpose(c, (0, 2, 3, 1)).reshape(Bs, att_w, past)
    att_s, ohg_s, s_sample = _step_mixers(
        per_seq(qs), per_seq(ks), per_seq(vs), feature_major(cache_win_k[layer]),
        feature_major(cache_win_v[layer]), g_att, hzs.reshape(Bs, Ts, 4 * hg_w), state_hgrn[layer], hg_lb_logits,
        g_hg, att_heads=att_h, dh=att_dh, layer=layer)
    x1_s, cq_s = _single_call(
        _step_mix_kernel,
        (x_sample.reshape(n_s, D), att_s.reshape(n_s, att_w), ohg_s.reshape(n_s, hg_w), g_cross, weights[0],
         weights[1]),
        [jax.ShapeDtypeStruct((n_s, D), F32), jax.ShapeDtypeStruct((n_s, xw), F32)], "step_mix")
    cq_heads = cq_s.reshape(Bs, Ts, x_h, x_dh).transpose(0, 2, 1, 3).reshape(Bs, x_h * Ts, x_dh)
    co_s = _step_cross(cq_heads, cache_mem_k[layer], cache_mem_v[layer], n_q=Ts)
    co_s = co_s.reshape(Bs, x_h, Ts, x_dh).transpose(0, 2, 1, 3)
    conv_state = state_ffn_conv[layer]
    t_idx = jnp.arange(Ts)
    prev = jnp.stack([jnp.take(conv_state, jnp.clip(j + t_idx, 0, n_taps - 2), axis=1).reshape(n_s, dff)
                      for j in range(n_taps - 1)])
    y_s, ug_s = _single_call(
        functools.partial(_step_ffn_kernel, n_t=Ts),
        (x1_s, co_s.reshape(n_s, xw), prev, g_ffn, g_final, weights[2], weights[3], weights[4], weights[5], cw, cb),
        [jax.ShapeDtypeStruct((n_s, D), F32), jax.ShapeDtypeStruct((n_s, dff), F32)], "step_ffn",
        scratch_shapes=[pltpu.VMEM((n_s + SUBLANES, dff), F32)])

    stack = lambda a: a[None]
    return (y_prompt, y_s.reshape(Bs, Ts, D),
            stack(k_keep.reshape(Bp, keep, att_h, att_dh)), stack(v_keep.reshape(Bp, keep, att_h, att_dh)),
            stack(s_prompt), stack(cst[:, SUBLANES - (n_taps - 1):]),
            stack(mk.reshape(Bp, n_mem, x_h, x_dh)), stack(mv.reshape(Bp, n_mem, x_h, x_dh)),
            stack(ks_f.reshape(Bs, Ts, att_h, att_dh)), stack(vs_f.reshape(Bs, Ts, att_h, att_dh)),
            stack(s_sample), stack(ug_s.reshape(Bs, Ts, dff)[:, Ts - (n_taps - 1):]))
```

```python
import functools

import jax
import jax.numpy as jnp
from jax import lax
from jax.experimental import pallas as pl
from jax.experimental.pallas import tpu as pltpu

F32 = jnp.float32
BF16 = jnp.bfloat16
EPS = 1e-6
DIL_PATTERNS = ((128, 1), (512, 4), (2048, 16))
ATT_BLOCK = 128
LANES = 128
SUBLANES = 8
ROW_TILE = 512
MEM_TILE = 256
HG_CHUNK = 64
HG_SUB = 8
NEG = -1e30
LOG2E = 1.4426950408889634
HIGHEST = lax.Precision.HIGHEST
NT_DIMS = (((1,), (1,)), ((), ()))
TN_DIMS = (((0,), (0,)), ((), ()))
VMEM_LIMIT = 56 * 1024 * 1024


def _dot(a, b):
    return jnp.dot(a, b, preferred_element_type=F32)


def _dot_nt(a, b):
    return lax.dot_general(a, b, NT_DIMS, preferred_element_type=F32)


def _sigmoid(x):
    return 1.0 / (1.0 + jnp.exp(-x))


def _rms(x, g):
    return x * lax.rsqrt(jnp.mean(x * x, axis=-1, keepdims=True) + EPS) * g


def _const_spec(shape):
    nd = len(shape)
    return pl.BlockSpec(shape, lambda *_: (0,) * nd, pipeline_mode=pl.Buffered(1))


def _params(sem):
    return pltpu.CompilerParams(dimension_semantics=sem, vmem_limit_bytes=VMEM_LIMIT)


def _in_proj_kernel(x_ref, g_ref, w_ref, q_ref, k_ref, v_ref, kf_ref, vf_ref, hz_ref, *, att_w, q_scale):
    h = _rms(x_ref[0], g_ref[...]).astype(BF16)

    def proj(lo, hi):
        return _dot(h, w_ref[:, lo:hi])

    q_ref[0] = proj(0, att_w) * q_scale
    zk = proj(att_w, 2 * att_w)
    k_ref[0] = zk
    kf_ref[0] = zk
    zv = proj(2 * att_w, 3 * att_w)
    v_ref[0] = zv
    vf_ref[0] = zv
    base = 3 * att_w
    step = 4 * LANES
    for c in range(0, hz_ref.shape[-1], step):
        hz_ref[0, :, c:c + step] = proj(base + c, base + c + step)


def _in_proj(x, g, w_bf, *, att_w, q_scale, tm, keep):
    B, T, D = x.shape
    ncols = w_bf.shape[1]
    hzw = ncols - 3 * att_w
    n_t = T // tm
    first = n_t - keep // tm
    row = lambda b, i: (b, i, 0)
    keep_map = lambda b, i: (b, jnp.maximum(i - first, 0), 0)
    return pl.pallas_call(
        functools.partial(_in_proj_kernel, att_w=att_w, q_scale=q_scale),
        grid=(B, n_t),
        in_specs=[pl.BlockSpec((1, tm, D), row), _const_spec((1, D)), _const_spec((D, ncols))],
        out_specs=[pl.BlockSpec((1, tm, att_w), row)] * 3
        + [pl.BlockSpec((1, tm, att_w), keep_map)] * 2
        + [pl.BlockSpec((1, tm, hzw), row)],
        out_shape=[jax.ShapeDtypeStruct((B, T, att_w), F32)] * 3
        + [jax.ShapeDtypeStruct((B, keep, att_w), F32)] * 2
        + [jax.ShapeDtypeStruct((B, T, hzw), F32)],
        compiler_params=_params(("parallel", "arbitrary")),
        name="in_proj",
    )(x, g, w_bf)


def _mem_kv_kernel(m_ref, g_ref, w_ref, kf_ref, vf_ref, kb_ref, vb_ref):
    h = _rms(m_ref[...], g_ref[...]).astype(BF16)
    xw = w_ref.shape[1] // 2
    k = _dot(h, w_ref[:, :xw])
    v = _dot(h, w_ref[:, xw:])
    kf_ref[...] = k
    vf_ref[...] = v
    kb_ref[...] = k.astype(BF16)
    vb_ref[...] = v.astype(BF16)


def _mem_kv(mem2d, g, w_ckv_bf, *, tm):
    n, D = mem2d.shape
    xw = w_ckv_bf.shape[1] // 2
    row = lambda i: (i, 0)
    return pl.pallas_call(
        _mem_kv_kernel,
        grid=(n // tm,),
        in_specs=[pl.BlockSpec((tm, D), row), _const_spec((1, D)), _const_spec((D, 2 * xw))],
        out_specs=[pl.BlockSpec((tm, xw), row)] * 4,
        out_shape=[jax.ShapeDtypeStruct((n, xw), F32)] * 2 + [jax.ShapeDtypeStruct((n, xw), BF16)] * 2,
        compiler_params=_params(("parallel",)),
        name="mem_kv",
    )(mem2d, g, w_ckv_bf)


def _dil_attn_kernel(q_ref, k_ref, v_ref, g_ref, o_ref, acc_sc, m_sc, den_sc, bias_sc, *, n_heads, dh):
    tile = pl.program_id(1)
    T = q_ref.shape[1]
    blk = ATT_BLOCK
    hpt = LANES // dh
    rows, ncol = hpt * blk, 2 * blk
    rr = lax.broadcasted_iota(jnp.int32, (rows, ncol), 0)
    cc = lax.broadcasted_iota(jnp.int32, (rows, ncol), 1)
    delta = (rr % blk) + blk - cc
    in_band = (delta >= 0) & (delta <= blk)
    slope = jnp.exp2((-8.0 / n_heads) * (tile * hpt + rr // blk + 1).astype(F32))
    lane_head = lax.broadcasted_iota(jnp.int32, (blk, LANES), 1) // dh
    n_pat = len(DIL_PATTERNS)

    def per_lane(x):
        out = jnp.broadcast_to(x[0:blk], (blk, LANES))
        for e in range(1, hpt):
            out = jnp.where(lane_head == e, jnp.broadcast_to(x[e * blk:(e + 1) * blk], (blk, LANES)), out)
        return out

    ones = jnp.ones((ncol, LANES), BF16)
    order = sorted(range(n_pat), key=lambda p: -DIL_PATTERNS[p][1])
    for step, p in enumerate(order):
        dil = DIL_PATTERNS[p][1]
        alibi = (-LOG2E) * slope * (delta * dil).astype(F32)
        bias_sc[0] = jnp.where(in_band & (cc >= blk), alibi, NEG)
        bias_sc[1] = jnp.where(in_band, alibi, NEG)
        nblk = T // (blk * dil)

        def body(n, carry, first=step == 0, last=step == n_pat - 1, dil=dil, nblk=nblk):
            kp, vp = carry
            i = n % nblk
            start = n // nblk + i * (blk * dil)
            idx = pl.ds(start, blk, stride=dil) if dil > 1 else pl.ds(start, blk)
            q2 = q_ref[0, idx, :]
            kc = k_ref[0, idx, :].astype(BF16)
            vc = v_ref[0, idx, :].astype(BF16)
            qs = jnp.concatenate([jnp.where(lane_head == e, q2, 0.0) for e in range(hpt)], axis=0).astype(BF16)
            s = _dot_nt(qs, jnp.concatenate([kp, kc], axis=0)) + bias_sc[min(i, 1)]
            m = jnp.max(s, axis=-1, keepdims=True)
            pr = jnp.exp2(s - m).astype(BF16)
            pv = _dot(pr, jnp.concatenate([jnp.concatenate([vp, vc], axis=0), ones], axis=1))
            m_t, acc_t, den_t = per_lane(m), per_lane(pv[:, :LANES]), per_lane(pv[:, LANES:])
            if not first:
                m_old = m_sc[idx, :]
                m_new = jnp.maximum(m_old, m_t)
                a_old, a_t = jnp.exp2(m_old - m_new), jnp.exp2(m_t - m_new)
                den_t = a_old * den_sc[idx, :] + a_t * den_t
                acc_t = a_old * acc_sc[idx, :] + a_t * acc_t
                m_t = m_new
            if last:
                acc_sc[idx, :] = acc_t / den_t
            else:
                m_sc[idx, :] = m_t
                den_sc[idx, :] = den_t
                acc_sc[idx, :] = acc_t
            return kc, vc

        carry = (jnp.zeros((blk, LANES), BF16),) * 2
        for n in range(T // blk):
            carry = body(n, carry)

    ch = 4 * blk
    same_head = (lax.broadcasted_iota(jnp.int32, (LANES, LANES), 0) // dh
                 == lax.broadcasted_iota(jnp.int32, (LANES, LANES), 1) // dh).astype(BF16)
    for c in range(T // ch):
        a = acc_sc[c * ch:(c + 1) * ch, :]
        mean_sq = _dot((a * a).astype(BF16), same_head) * (1.0 / dh)
        o_ref[0, c * ch:(c + 1) * ch, :] = (a * lax.rsqrt(mean_sq + EPS) * g_ref[...]).astype(BF16)


def _dil_attn(q, k, v, g, *, n_heads, dh):
    B, T, W = q.shape
    blk = ATT_BLOCK
    seq = pl.BlockSpec((1, T, LANES), lambda b, t: (b, 0, t))
    return pl.pallas_call(
        functools.partial(_dil_attn_kernel, n_heads=n_heads, dh=dh),
        grid=(B, W // LANES),
        in_specs=[seq, seq, seq, pl.BlockSpec((1, LANES), lambda b, t: (0, t))],
        out_specs=seq,
        out_shape=jax.ShapeDtypeStruct((B, T, W), BF16),
        scratch_shapes=[pltpu.VMEM((T, LANES), F32)] * 3
        + [pltpu.VMEM((2, (LANES // dh) * blk, 2 * blk), F32)],
        compiler_params=_params(("parallel", "arbitrary")),
        name="dil_attn",
    )(q, k, v, g)


def _dil_attn_step_kernel(q_ref, kn_ref, vn_ref, kt_ref, vt_ref, g_ref, o_ref, cnt_sc, bias_sc,
                          *, n_heads, dh, n_q):
    past = kt_ref.shape[2]
    n_new = LANES
    W = q_ref.shape[-1]
    rows = n_q * n_heads

    def new_rows(ref):
        sub8 = lax.broadcasted_iota(jnp.int32, (SUBLANES, W), 0)
        tile = jnp.zeros((SUBLANES, W), F32)
        for i in range(n_q):
            tile = jnp.where(sub8 == i, jnp.broadcast_to(ref[0, i:i + 1, :], (SUBLANES, W)), tile)
        return jnp.concatenate([tile, jnp.zeros((n_new - SUBLANES, W), F32)], axis=0).astype(BF16)
    n_cols = past + n_new

    @pl.when(pl.program_id(0) == 0)
    def _():
        r = lax.broadcasted_iota(jnp.int32, (rows, n_cols), 0)
        c = lax.broadcasted_iota(jnp.int32, (rows, n_cols), 1)
        delta = past + r // n_heads - c
        cnt = jnp.zeros((rows, n_cols), F32)
        for win, dil in DIL_PATTERNS:
            cnt = cnt + ((delta >= 0) & (delta <= win) & ((delta & (dil - 1)) == 0)).astype(F32)
        slope = jnp.exp2((-8.0 / n_heads) * ((r % n_heads) + 1).astype(F32))
        cnt_sc[...] = cnt
        bias_sc[...] = jnp.where(cnt > 0.0, (-LOG2E) * slope * delta.astype(F32), NEG)

    sub = lax.broadcasted_iota(jnp.int32, (n_heads, W), 0)
    lane_head = lax.broadcasted_iota(jnp.int32, (n_heads, W), 1) // dh
    qbd = jnp.concatenate(
        [jnp.where(sub == lane_head, jnp.broadcast_to(q_ref[0, i:i + 1, :], (n_heads, W)), 0.0) for i in range(n_q)],
        axis=0).astype(BF16)
    s = jnp.concatenate([_dot(qbd, kt_ref[0].astype(BF16)), _dot_nt(qbd, new_rows(kn_ref))], axis=1) + bias_sc[...]
    p = jnp.exp2(s - jnp.max(s, axis=-1, keepdims=True)) * cnt_sc[...]
    den = jnp.sum(p, axis=-1, keepdims=True)
    p = p.astype(BF16)
    out = (_dot_nt(p[:, :past], vt_ref[0].astype(BF16)) + _dot(p[:, past:], new_rows(vn_ref))) / den
    r = lax.broadcasted_iota(jnp.int32, (rows, W), 0)
    c = lax.broadcasted_iota(jnp.int32, (rows, W), 1)
    out = jnp.where((r % n_heads) == (c // dh), out, 0.0)
    out = out * lax.rsqrt(jnp.sum(out * out, axis=-1, keepdims=True) * (1.0 / dh) + EPS)
    o_ref[0] = jnp.sum(out.reshape(n_q, n_heads, W), axis=1) * g_ref[...]


def _lower_bound(lbl_ref, layer):
    logits = lbl_ref[...]
    e = jnp.exp(logits - jnp.max(logits, axis=0, keepdims=True))
    return jnp.sum(e[:layer + 1], axis=0, keepdims=True) / jnp.sum(e, axis=0, keepdims=True)


def _cumsum_rows(tril_bf, x):
    x1 = x.astype(BF16)
    r1 = x - x1.astype(F32)
    x2 = r1.astype(BF16)
    x3 = (r1 - x2.astype(F32)).astype(BF16)
    return _dot(tril_bf, x1) + _dot(tril_bf, x2) + _dot(tril_bf, x3)


def _hgrn_rows(hz_ref, o_ref, lb, gn_ref, st_ref, b_sc, c_sc, *, n_heads, dk):
    hw = n_heads * dk
    tc = hz_ref.shape[0]
    ch, sub = HG_CHUNK, HG_SUB
    r_c = lax.broadcasted_iota(jnp.int32, (ch, ch), 0)
    c_c = lax.broadcasted_iota(jnp.int32, (ch, ch), 1)
    tril = (r_c >= c_c).astype(BF16)
    r_s = lax.broadcasted_iota(jnp.int32, (sub, ch), 0)
    c_s = lax.broadcasted_iota(jnp.int32, (sub, ch), 1)

    local = [[None] * n_heads for _ in range(tc // ch)]
    for c in range(tc // ch):
        rows = slice(c * ch, (c + 1) * ch)
        hq = hz_ref[rows, 0:hw]
        f = lb + (1.0 - lb) * _sigmoid(hz_ref[rows, hw:2 * hw])
        qh = hq * _sigmoid(hq) * dk ** -0.5
        kk = 1.0 - f
        b = _cumsum_rows(tril, jnp.log(f) * LOG2E)
        b_rows, c_rows = b_sc.at[c], c_sc.at[c]
        b_rows[...] = b
        c_rows[...] = b - jnp.log(kk) * LOG2E
        for h in range(n_heads):
            hs = slice(h * dk, (h + 1) * dk)
            bh = b[:, hs]
            kh = kk[:, hs]
            qhh = qh[:, hs]
            vh = hz_ref[rows, 2 * hw + h * dk:2 * hw + (h + 1) * dk].astype(BF16)
            blocks = []
            for j in range(ch // sub):
                r0 = j * sub
                bj = bh[r0:r0 + sub]
                qj = qhh[r0:r0 + sub]
                a_j = jnp.zeros((sub, ch), F32)
                for s in range(sub):
                    a = jnp.sum(qj * jnp.exp2(bj - c_rows[r0 + s:r0 + s + 1, hs]), axis=-1, keepdims=True)
                    a_j = jnp.where(c_s == r0 + s, a, a_j)
                if j > 0:
                    beta = b_rows[r0 - 1:r0, hs]
                    qt = qj * jnp.exp2(bj - beta)
                    kt = kh[0:r0] * jnp.exp2(beta - bh[0:r0])
                    kt = jnp.concatenate([kt, jnp.zeros((ch - r0, dk), F32)], axis=0)
                    a_j = a_j + _dot_nt(qt.astype(BF16), kt.astype(BF16))
                blocks.append(jnp.where(c_s - r0 <= r_s, a_j, 0.0))
            b_last = b_rows[ch - 1:ch, hs]
            hg = hz_ref[rows, 3 * hw + h * dk:3 * hw + (h + 1) * dk]
            local[c][h] = dict(
                q=(qhh * jnp.exp2(bh)).astype(BF16), a=jnp.concatenate(blocks, axis=0).astype(BF16), v=vh,
                k=(kh * jnp.exp2(b_last - bh)).astype(BF16), decay=jnp.exp2(b_last), gate=hg * _sigmoid(hg))

    state = [st_ref[h] for h in range(n_heads)]
    for c in range(tc // ch):
        rows = slice(c * ch, (c + 1) * ch)
        for h in range(n_heads):
            hs = slice(h * dk, (h + 1) * dk)
            t = local[c][h]
            o = _dot_nt(t["q"], state[h].astype(BF16)) + _dot(t["a"], t["v"])
            state[h] = state[h] * t["decay"] + lax.dot_general(t["v"], t["k"], TN_DIMS, preferred_element_type=F32)
            o_ref[rows, hs] = (_rms(o, gn_ref[:, hs]) * t["gate"]).astype(BF16)
    for h in range(n_heads):
        st_ref[h] = state[h]


def _in_proj_hgrn_kernel(x_ref, g_ref, w_ref, lbl_ref, gn_ref, q_ref, k_ref, v_ref, kf_ref, vf_ref, o_ref, s_ref,
                         hz_sc, st_ref, b_sc, c_sc, *, att_w, q_scale, n_heads, dk, layer, tiles_per_seq):
    g = pl.program_id(0)
    t_rec = jnp.maximum(g - 1, 0)

    @pl.when(t_rec % tiles_per_seq == 0)
    def _():
        st_ref[...] = jnp.zeros_like(st_ref)

    slot_w = g % 2

    @pl.when(g < pl.num_programs(0) - 1)
    def _():
        h = _rms(x_ref[...], g_ref[...]).astype(BF16)

        def proj(lo, hi):
            return _dot(h, w_ref[:, lo:hi])

        q_ref[...] = proj(0, att_w) * q_scale
        zk = proj(att_w, 2 * att_w)
        k_ref[...] = zk
        kf_ref[0] = zk
        zv = proj(2 * att_w, 3 * att_w)
        v_ref[...] = zv
        vf_ref[0] = zv
        base = 3 * att_w
        step = 4 * LANES
        for c in range(0, hz_sc.shape[-1], step):
            hz_sc[slot_w, :, c:c + step] = proj(base + c, base + c + step)

    @pl.when(g > 0)
    def _():
        _hgrn_rows(hz_sc.at[1 - slot_w], o_ref, _lower_bound(lbl_ref, layer), gn_ref, st_ref, b_sc, c_sc,
                   n_heads=n_heads, dk=dk)

    @pl.when((g > 0) & (t_rec % tiles_per_seq == tiles_per_seq - 1))
    def _():
        for h in range(n_heads):
            s_ref[0, h] = st_ref[h].T


def _in_proj_hgrn(x, g, w_bf, lb_logits, gn, *, att_w, q_scale, n_heads, dk, tm, keep, layer):
    B, T, D = x.shape
    hw = n_heads * dk
    n_t = T // tm
    n_tiles = B * n_t
    n_keep = keep // tm
    first = n_t - n_keep
    proj_tile = lambda s: jnp.minimum(s, n_tiles - 1)
    rec_tile = lambda s: jnp.maximum(s - 1, 0)
    rows = lambda s: (proj_tile(s), 0)
    keep_rows = lambda s: (proj_tile(s) // n_t, jnp.maximum(proj_tile(s) % n_t - first, 0), 0)
    f32 = lambda n, w: jax.ShapeDtypeStruct((n, w), F32)
    return pl.pallas_call(
        functools.partial(_in_proj_hgrn_kernel, att_w=att_w, q_scale=q_scale, n_heads=n_heads, dk=dk, layer=layer,
                          tiles_per_seq=n_t),
        grid=(n_tiles + 1,),
        in_specs=[pl.BlockSpec((tm, D), rows), _const_spec((1, D)), _const_spec(w_bf.shape),
                  _const_spec(lb_logits.shape), _const_spec((1, hw))],
        out_specs=[pl.BlockSpec((tm, att_w), rows)] * 3 + [pl.BlockSpec((1, tm, att_w), keep_rows)] * 2
        + [pl.BlockSpec((tm, hw), lambda s: (rec_tile(s), 0)),
           pl.BlockSpec((1, n_heads, dk, dk), lambda s: (rec_tile(s) // n_t, 0, 0, 0))],
        out_shape=[f32(B * T, att_w)] * 3 + [jax.ShapeDtypeStruct((B, keep, att_w), F32)] * 2
        + [jax.ShapeDtypeStruct((B * T, hw), BF16), jax.ShapeDtypeStruct((B, n_heads, dk, dk), F32)],
        scratch_shapes=[pltpu.VMEM((2, tm, 4 * hw), F32), pltpu.VMEM((n_heads, dk, dk), F32)]
        + [pltpu.VMEM((tm // HG_CHUNK, HG_CHUNK, hw), F32)] * 2,
        compiler_params=_params(("arbitrary",)),
        name="in_proj_hgrn",
    )(x.reshape(B * T, D), g, w_bf, lb_logits, gn)


def _hgrn_step_kernel(hz_ref, s0_ref, lbl_ref, gn_ref, o_ref, s_ref, *, n_heads, dk, layer):
    hw = n_heads * dk
    n_t = hz_ref.shape[1]
    lb = _lower_bound(lbl_ref, layer)
    q, k, v, b, gate = [], [], [], [], []
    acc = jnp.zeros((1, hw), F32)
    for t in range(n_t):
        hq = hz_ref[0, t:t + 1, 0:hw]
        f = lb + (1.0 - lb) * _sigmoid(hz_ref[0, t:t + 1, hw:2 * hw])
        acc = acc + jnp.log(f)
        q.append(hq * _sigmoid(hq) * dk ** -0.5)
        k.append(1.0 - f)
        v.append(hz_ref[0, t:t + 1, 2 * hw:3 * hw])
        b.append(acc)
        hg = hz_ref[0, t:t + 1, 3 * hw:4 * hw]
        gate.append(hg * _sigmoid(hg))
    rr = lax.broadcasted_iota(jnp.int32, (dk, dk), 0)
    cc = lax.broadcasted_iota(jnp.int32, (dk, dk), 1)

    def tile_of_rows(rows):
        tile = jnp.zeros((dk, dk), F32)
        for t, r in enumerate(rows):
            tile = jnp.where(rr == t, jnp.broadcast_to(r, (dk, dk)), tile)
        return tile

    for h in range(n_heads):
        hs = slice(h * dk, (h + 1) * dk)
        s0 = s0_ref[0, h]
        qe = tile_of_rows([q[t][:, hs] * jnp.exp(b[t][:, hs]) for t in range(n_t)])
        inter = jnp.dot(qe, s0, precision=HIGHEST, preferred_element_type=F32)
        for t in range(n_t):
            o = inter[t:t + 1]
            for s in range(t + 1):
                a = jnp.sum(q[t][:, hs] * jnp.exp(b[t][:, hs] - b[s][:, hs]) * k[s][:, hs], axis=-1, keepdims=True)
                o = o + a * v[s][:, hs]
            o_ref[0, t:t + 1, hs] = _rms(o, gn_ref[:, hs]) * gate[t][:, hs]
        b_last = b[n_t - 1][:, hs]
        khat = tile_of_rows([k[t][:, hs] * jnp.exp(b_last - b[t][:, hs]) for t in range(n_t)])
        vpad = tile_of_rows([v[t][:, hs] for t in range(n_t)])
        decay = jnp.where(rr == cc, jnp.broadcast_to(jnp.exp(b_last), (dk, dk)), 0.0)
        s_ref[0, h] = (jnp.dot(decay, s0, precision=HIGHEST, preferred_element_type=F32)
                       + jnp.dot(khat.T, vpad, precision=HIGHEST, preferred_element_type=F32))


def _step_mixers_kernel(q_ref, kn_ref, vn_ref, kt_ref, vt_ref, g_ref, hz_ref, s0_ref, lbl_ref, gn_ref,
                        o_ref, oh_ref, s_ref, cnt_sc, bias_sc, *, att_heads, dh, n_q, hg_heads, dk, layer):
    for b in range(q_ref.shape[0]):
        one = lambda ref: ref.at[pl.ds(b, 1)]
        _dil_attn_step_kernel(one(q_ref), one(kn_ref), one(vn_ref), one(kt_ref), one(vt_ref), g_ref, one(o_ref),
                              cnt_sc, bias_sc, n_heads=att_heads, dh=dh, n_q=n_q)
        _hgrn_step_kernel(one(hz_ref), one(s0_ref), lbl_ref, gn_ref, one(oh_ref), one(s_ref), n_heads=hg_heads,
                          dk=dk, layer=layer)


def _step_mixers(q, k_new, v_new, cache_kt, cache_vt, g_att, hz, s0, lb_logits, g_hg,
                 *, att_heads, dh, layer):
    B, n_q, W = q.shape
    past = cache_kt.shape[2]
    _, hg_heads, dk, dv = s0.shape
    hw = hg_heads * dk
    n_b = 2 if B % 2 == 0 else 1
    bmap = lambda b: (b, 0, 0)
    smap = lambda b: (b, 0, 0, 0)
    return pl.pallas_call(
        functools.partial(_step_mixers_kernel, att_heads=att_heads, dh=dh, n_q=n_q, hg_heads=hg_heads, dk=dk,
                          layer=layer),
        grid=(B // n_b,),
        in_specs=[pl.BlockSpec((n_b, n_q, W), bmap)] * 3
        + [pl.BlockSpec((n_b, W, past), bmap), pl.BlockSpec((n_b, W, past), bmap), _const_spec((1, W)),
           pl.BlockSpec((n_b, n_q, 4 * hw), bmap), pl.BlockSpec((n_b, hg_heads, dk, dv), smap),
           _const_spec(lb_logits.shape), _const_spec((1, hw))],
        out_specs=[pl.BlockSpec((n_b, n_q, W), bmap), pl.BlockSpec((n_b, n_q, hw), bmap),
                   pl.BlockSpec((n_b, hg_heads, dk, dv), smap)],
        out_shape=[jax.ShapeDtypeStruct((B, n_q, W), F32), jax.ShapeDtypeStruct((B, n_q, hw), F32),
                   jax.ShapeDtypeStruct((B, hg_heads, dk, dv), F32)],
        scratch_shapes=[pltpu.VMEM((n_q * att_heads, past + LANES), F32)] * 2,
        compiler_params=_params(("arbitrary",)),
        name="step_mixers",
    )(q, k_new, v_new, cache_kt, cache_vt, g_att, hz, s0, lb_logits, g_hg)


def _mix_out(x, att_bf, ohg_bf, wo_ref):
    aw = att_bf.shape[-1]
    return x + _dot(att_bf, wo_ref[0:aw, :]) + _dot(ohg_bf, wo_ref[aw:, :])


def _cross_attend(cq, head_k, head_v, n_heads, dh):
    heads = range(n_heads)
    scale = dh ** -0.5 * LOG2E
    scores = [_dot_nt((cq[:, h * dh:(h + 1) * dh] * scale).astype(BF16), head_k(h)) for h in heads]
    probs = [jnp.exp2(s - jnp.max(s, axis=-1, keepdims=True)).astype(BF16) for s in scores]
    ones = jnp.ones((probs[0].shape[1], dh), BF16)
    pvs = [_dot(probs[h], jnp.concatenate([head_v(h), ones], axis=1)) for h in heads]
    return jnp.concatenate([pv[:, :dh] / pv[:, dh:] for pv in pvs], axis=-1)


def _ffn(u_bf, taps_fn, wg_ref, wu_ref, wd_ref, cw_ref, cb_ref):
    n_taps = cw_ref.shape[0]
    ug = _dot(u_bf, wg_ref[...])
    taps = taps_fn(ug)
    conv = cb_ref[...] + cw_ref[n_taps - 1:n_taps, :] * taps[n_taps - 1]
    for j in range(n_taps - 1):
        conv = conv + cw_ref[j:j + 1, :] * taps[j]
    act = conv * _sigmoid(conv) * _dot(u_bf, wu_ref[...])
    return _dot(act.astype(BF16), wd_ref[...])


def _post_kernel(x_ref, att_ref, ohg_ref, mk_ref, mv_ref, cinit_ref, gc_ref, gf_ref, gl_ref, wo_ref, wq_ref,
                 wc_ref, wg_ref, wu_ref, wd_ref, cw_ref, cb_ref, y_ref, cst_ref, buf_ref, carry_ref,
                 *, x_heads, x_dh):
    i = pl.program_id(1)
    tm = x_ref.shape[1]
    pad = SUBLANES
    n_taps = cw_ref.shape[0]

    @pl.when(i == 0)
    def _():
        carry_ref[...] = cinit_ref[0]

    x1 = _mix_out(x_ref[0], att_ref[0], ohg_ref[0], wo_ref)
    cq = _dot(_rms(x1, gc_ref[...]).astype(BF16), wq_ref[...])
    co = _cross_attend(cq, lambda h: mk_ref[0, :, h * x_dh:(h + 1) * x_dh],
                       lambda h: mv_ref[0, :, h * x_dh:(h + 1) * x_dh], x_heads, x_dh)
    x2 = x1 + _dot(co.astype(BF16), wc_ref[...])

    def taps_fn(ug):
        buf_ref[0:pad, :] = carry_ref[...]
        buf_ref[pad:pad + tm, :] = ug
        carry_ref[...] = ug[tm - pad:tm]
        return [buf_ref[pad - (n_taps - 1 - j):pad - (n_taps - 1 - j) + tm, :] for j in range(n_taps - 1)] + [ug]

    u_bf = _rms(x2, gf_ref[...]).astype(BF16)
    x3 = x2 + _ffn(u_bf, taps_fn, wg_ref, wu_ref, wd_ref, cw_ref, cb_ref)
    y_ref[0] = _rms(x3, gl_ref[...])
    cst_ref[0] = carry_ref[...]


def _post(x, att, ohg, mk_bf, mv_bf, cinit, gains, weights, conv_w, conv_b, *, tm, x_heads):
    B, T, D = x.shape
    aw, hw = att.shape[-1], ohg.shape[-1]
    n_mem, xw = mk_bf.shape[1:]
    dff = conv_w.shape[1]
    row = lambda b, i: (b, i, 0)
    bat = lambda b, i: (b, 0, 0)
    in_specs = ([pl.BlockSpec((1, tm, D), row), pl.BlockSpec((1, tm, aw), row), pl.BlockSpec((1, tm, hw), row)]
                + [pl.BlockSpec((1, n_mem, xw), bat)] * 2 + [pl.BlockSpec((1, SUBLANES, dff), bat)]
                + [_const_spec(a.shape) for a in tuple(gains) + tuple(weights) + (conv_w, conv_b)])
    return pl.pallas_call(
        functools.partial(_post_kernel, x_heads=x_heads, x_dh=xw // x_heads),
        grid=(B, T // tm),
        in_specs=in_specs,
        out_specs=[pl.BlockSpec((1, tm, D), row), pl.BlockSpec((1, SUBLANES, dff), bat)],
        out_shape=[jax.ShapeDtypeStruct((B, T, D), F32), jax.ShapeDtypeStruct((B, SUBLANES, dff), F32)],
        scratch_shapes=[pltpu.VMEM((tm + SUBLANES, dff), F32), pltpu.VMEM((SUBLANES, dff), F32)],
        compiler_params=_params(("parallel", "arbitrary")),
        name="post_mixer",
    )(x, att, ohg, mk_bf, mv_bf, cinit, *gains, *weights, conv_w, conv_b)


def _step_mix_kernel(x_ref, att_ref, ohg_ref, gc_ref, wo_ref, wq_ref, x1_ref, cq_ref):
    x1 = _mix_out(x_ref[...], att_ref[...].astype(BF16), ohg_ref[...].astype(BF16), wo_ref)
    x1_ref[...] = x1
    cq_ref[...] = _dot(_rms(x1, gc_ref[...]).astype(BF16), wq_ref[...])


def _step_cross_kernel(cq_ref, mk_ref, mv_ref, co_ref, *, n_q):
    n_b, n_mem, n_heads, dh = mk_ref.shape
    row = lax.broadcasted_iota(jnp.int32, (cq_ref.shape[1], n_mem * n_heads), 0)
    col = lax.broadcasted_iota(jnp.int32, (cq_ref.shape[1], n_mem * n_heads), 1)
    own_head = (col % n_heads) == (row // n_q)
    for b in range(n_b):
        mk = mk_ref[b].reshape(n_mem * n_heads, dh).astype(BF16)
        mv = mv_ref[b].reshape(n_mem * n_heads, dh).astype(BF16)
        s = jnp.where(own_head, _dot_nt((cq_ref[b] * dh ** -0.5).astype(BF16), mk), NEG)
        p = jnp.exp(s - jnp.max(s, axis=-1, keepdims=True))
        co_ref[b] = _dot(p.astype(BF16), mv) / jnp.sum(p, axis=-1, keepdims=True)


def _step_ffn_kernel(x1_ref, co_ref, prev_ref, gf_ref, gl_ref, wc_ref, wg_ref, wu_ref, wd_ref,
                     cw_ref, cb_ref, y_ref, ug_ref, buf_ref, *, n_t):
    n = x1_ref.shape[0]
    pad = SUBLANES
    n_taps = cw_ref.shape[0]
    x2 = x1_ref[...] + _dot(co_ref[...].astype(BF16), wc_ref[...])
    t_of_row = lax.broadcasted_iota(jnp.int32, (n, wg_ref.shape[1]), 0) % n_t

    def taps_fn(ug):
        ug_ref[...] = ug
        buf_ref[0:pad, :] = jnp.zeros((pad, ug.shape[1]), F32)
        buf_ref[pad:pad + n, :] = ug
        taps = []
        for j in range(n_taps - 1):
            shift = n_taps - 1 - j
            shifted = buf_ref[pad - shift:pad - shift + n, :]
            taps.append(jnp.where(t_of_row < shift, prev_ref[j], shifted))
        return taps + [ug]

    u_bf = _rms(x2, gf_ref[...]).astype(BF16)
    x3 = x2 + _ffn(u_bf, taps_fn, wg_ref, wu_ref, wd_ref, cw_ref, cb_ref)
    y_ref[...] = _rms(x3, gl_ref[...])


def _single_call(kernel, args, out_shape, name, scratch_shapes=()):
    return pl.pallas_call(
        kernel,
        grid=(1,),
        in_specs=[_const_spec(a.shape) for a in args],
        out_specs=[pl.BlockSpec(o.shape, lambda *_, nd=len(o.shape): (0,) * nd) for o in out_shape],
        out_shape=out_shape,
        scratch_shapes=list(scratch_shapes),
        compiler_params=_params(("arbitrary",)),
        name=name,
    )(*args)


def _step_cross(cq, mem_k, mem_v, *, n_q):
    B, n_rows, dh = cq.shape
    _, n_mem, x_heads, _ = mem_k.shape
    n_b = 4 if B % 4 == 0 else 1
    bmap = lambda b: (b, 0, 0)
    mem = pl.BlockSpec((n_b, n_mem, x_heads, dh), lambda b: (b, 0, 0, 0))
    return pl.pallas_call(
        functools.partial(_step_cross_kernel, n_q=n_q),
        grid=(B // n_b,),
        in_specs=[pl.BlockSpec((n_b, n_rows, dh), bmap), mem, mem],
        out_specs=pl.BlockSpec((n_b, n_rows, dh), bmap),
        out_shape=jax.ShapeDtypeStruct((B, n_rows, dh), F32),
        compiler_params=_params(("parallel",)),
        name="step_cross",
    )(cq, mem_k, mem_v)


def kernel(x_prompt, x_sample, cache_win_k, cache_win_v, state_hgrn, state_ffn_conv, cache_mem_k, cache_mem_v,
           mem_prompt, hg_lb_logits, norm_mix, w_in, att_out_norm, hg_out_norm, w_out, norm_cross, norm_mem,
           w_cq, w_ck, w_cv, w_co, norm_ffn, w_gate, w_up, conv_w, conv_b, w_down, norm_final):
    Bp, T, D = x_prompt.shape
    Bs, Ts, _ = x_sample.shape
    depth, _, past, att_h, att_dh = cache_win_k.shape
    _, _, hg_h, hg_dk, hg_dv = state_hgrn.shape
    _, _, n_mem, x_h, x_dh = cache_mem_k.shape
    n_taps, dff = conv_w.shape[1:]
    att_w = att_h * att_dh
    hg_w = hg_h * hg_dk
    xw = x_h * x_dh
    keep = min(max(w for w, _ in DIL_PATTERNS), T)
    assert depth == 1 and hg_dk == hg_dv == LANES and x_dh == LANES and LANES % att_dh == 0
    assert all(w // d == ATT_BLOCK for w, d in DIL_PATTERNS) and past >= max(w for w, _ in DIL_PATTERNS)
    d_max = max(d for _, d in DIL_PATTERNS)
    assert T % (ATT_BLOCK * d_max) == 0 and past % d_max == 0 and past % (4 * ATT_BLOCK) == 0
    assert n_taps - 1 <= min(Ts, SUBLANES) and Ts <= SUBLANES
    assert T % ROW_TILE == 0 and keep % ROW_TILE == 0 and (Bp * n_mem) % MEM_TILE == 0
    layer = 0
    q_scale = att_dh ** -0.5 * LOG2E
    row2 = lambda a: a.reshape(1, -1)

    w_in_bf = w_in[layer].astype(BF16)
    w_ckv_bf = jnp.concatenate([w_ck[layer], w_cv[layer]], axis=1).astype(BF16)
    weights = tuple(w[layer].astype(BF16) for w in (w_out, w_cq, w_co, w_gate, w_up, w_down))
    g_mix, g_att, g_hg = row2(norm_mix[layer]), row2(att_out_norm[layer]), row2(hg_out_norm[layer])
    g_cross, g_mem, g_ffn, g_final = (row2(norm_cross[layer]), row2(norm_mem[layer]), row2(norm_ffn[layer]),
                                      row2(norm_final))
    cw, cb = conv_w[layer], row2(conv_b[layer])

    q, k, v, k_keep, v_keep, ohg, s_prompt = _in_proj_hgrn(
        x_prompt, g_mix, w_in_bf, hg_lb_logits, g_hg, att_w=att_w, q_scale=q_scale, n_heads=hg_h, dk=hg_dk,
        tm=ROW_TILE, keep=keep, layer=layer)
    q, k, v = (a.reshape(Bp, T, att_w) for a in (q, k, v))
    ohg = ohg.reshape(Bp, T, hg_w)
    mk, mv, mk_bf, mv_bf = _mem_kv(mem_prompt.reshape(Bp * n_mem, D), g_mem, w_ckv_bf, tm=MEM_TILE)
    att = _dil_attn(q, k, v, g_att, n_heads=att_h, dh=att_dh)
    y_prompt, cst = _post(
        x_prompt, att, ohg, mk_bf.reshape(Bp, n_mem, xw), mv_bf.reshape(Bp, n_mem, xw),
        jnp.zeros((Bp, SUBLANES, dff), F32), (g_cross, g_ffn, g_final), weights, cw, cb,
        tm=ROW_TILE, x_heads=x_h)

    n_s = Bs * Ts
    qs, ks, vs, ks_f, vs_f, hzs = _in_proj(x_sample.reshape(1, n_s, D), g_mix, w_in_bf, att_w=att_w, q_scale=q_scale,
                                           tm=n_s, keep=n_s)
    per_seq = lambda a: a.reshape(Bs, Ts, att_w)
    feature_major = lambda c: jnp.transpose(c, (0, 2, 3, 1)).reshape(Bs, att_w, past)
    att_s, ohg_s, s_sample = _step_mixers(
        per_seq(qs), per_seq(ks), per_seq(vs), feature_major(cache_win_k[layer]),
        feature_major(cache_win_v[layer]), g_att, hzs.reshape(Bs, Ts, 4 * hg_w), state_hgrn[layer], hg_lb_logits,
        g_hg, att_heads=att_h, dh=att_dh, layer=layer)
    x1_s, cq_s = _single_call(
        _step_mix_kernel,
        (x_sample.reshape(n_s, D), att_s.reshape(n_s, att_w), ohg_s.reshape(n_s, hg_w), g_cross, weights[0],
         weights[1]),
        [jax.ShapeDtypeStruct((n_s, D), F32), jax.ShapeDtypeStruct((n_s, xw), F32)], "step_mix")
    cq_heads = cq_s.reshape(Bs, Ts, x_h, x_dh).transpose(0, 2, 1, 3).reshape(Bs, x_h * Ts, x_dh)
    co_s = _step_cross(cq_heads, cache_mem_k[layer], cache_mem_v[layer], n_q=Ts)
    co_s = co_s.reshape(Bs, x_h, Ts, x_dh).transpose(0, 2, 1, 3)
    conv_state = state_ffn_conv[layer]
    t_idx = jnp.arange(Ts)
    prev = jnp.stack([jnp.take(conv_state, jnp.clip(j + t_idx, 0, n_taps - 2), axis=1).reshape(n_s, dff)
                      for j in range(n_taps - 1)])
    y_s, ug_s = _single_call(
        functools.partial(_step_ffn_kernel, n_t=Ts),
        (x1_s, co_s.reshape(n_s, xw), prev, g_ffn, g_final, weights[2], weights[3], weights[4], weights[5], cw, cb),
        [jax.ShapeDtypeStruct((n_s, D), F32), jax.ShapeDtypeStruct((n_s, dff), F32)], "step_ffn",
        scratch_shapes=[pltpu.VMEM((n_s + SUBLANES, dff), F32)])

    stack = lambda a: a[None]
    return (y_prompt, y_s.reshape(Bs, Ts, D),
            stack(k_keep.reshape(Bp, keep, att_h, att_dh)), stack(v_keep.reshape(Bp, keep, att_h, att_dh)),
            stack(s_prompt), stack(cst[:, SUBLANES - (n_taps - 1):]),
            stack(mk.reshape(Bp, n_mem, x_h, x_dh)), stack(mv.reshape(Bp, n_mem, x_h, x_dh)),
            stack(ks_f.reshape(Bs, Ts, att_h, att_dh)), stack(vs_f.reshape(Bs, Ts, att_h, att_dh)),
            stack(s_sample), stack(ug_s.reshape(Bs, Ts, dff)[:, Ts - (n_taps - 1):]))
```
